```python
import math
import jax
import jax.numpy as jnp
from jax import lax
import numpy as np

D_MODEL = 2048
BATCH = 16
SEQ = 256
DEPTH = 4
DEC_BATCH = 4
DEC_SEQ = 1024
PAST_LEN = 512

GRID_W = 64
EPS = 1e-6
CHUNK = 64
SHORT_W = 3
N_BRANCH = 3
HY_WIDTH = 1024
HY_ORDER = 2
HY_EMB = 33
HY_BANDS = (HY_EMB - 1) // 2
HY_FFN = 64
HY_DECAY_TARGET = 1e-2
HY_FAST_PCT = 0.3
HY_SLOW_PCT = 1.5
GLA_HEADS = 4
GLA_DK = 128
GLA_DV = 256
GLA_RANK = 16
GLA_NORMALIZER = 16.0
GLA_QK = GLA_HEADS * GLA_DK
GLA_VW = GLA_HEADS * GLA_DV
ML_HEADS = 4
ML_DH = 256
ML_W = ML_HEADS * ML_DH
BRANCH_W = HY_WIDTH
PEER_HEADS = 8
PEER_NKEYS = 128
PEER_EXPERTS = PEER_NKEYS * PEER_NKEYS
PEER_DKEY = 256
PEER_TOPK = 16
PEER_BLOCK = 128
IN_SPLITS = (HY_WIDTH, HY_WIDTH, HY_WIDTH,
             GLA_QK, GLA_QK, GLA_VW, GLA_VW, GLA_RANK, GLA_RANK,
             ML_W, ML_W, ML_W, ML_W, 2 * ML_HEADS, 2 * ML_HEADS,
             D_MODEL, D_MODEL, D_MODEL)
IN_COLS = sum(IN_SPLITS)

kernel_name = 'hybrid_flow_trunk_step'


def _f32(a):
    return a.astype(jnp.float32)


def rmsnorm(x, g):
    xf = _f32(x)
    y = xf * lax.rsqrt(jnp.mean(xf * xf, axis=-1, keepdims=True) + EPS)
    return (y * _f32(g)).astype(x.dtype)


def split_cols(z):
    parts, start = [], 0
    for width in IN_SPLITS:
        parts.append(z[..., start:start + width])
        start += width
    return parts


def short_conv(x, w, grid):
    b, l, ch = x.shape
    xr = x.reshape(b * (l // GRID_W), GRID_W, ch) if grid else x
    xp = jnp.pad(xr, ((0, 0), (1, 1), (0, 0)))
    y = xp[:, :-2] * w[0] + xp[:, 1:-1] * w[1] + xp[:, 2:] * w[2]
    return y.reshape(b, l, ch)


def to_heads(a, n_heads):
    b, l, _ = a.shape
    return a.reshape(b, l, n_heads, -1).transpose(0, 2, 1, 3)


def from_heads_norm(o, g):
    o = o * lax.rsqrt(jnp.mean(o * o, axis=-1, keepdims=True) + EPS) * _f32(g)
    b, h, l, d = o.shape
    return o.transpose(0, 2, 1, 3).reshape(b, l, h * d)


def to_chunks(a):
    b, h, l = a.shape[:3]
    a = a.reshape((b, h, l // CHUNK, CHUNK) + a.shape[3:])
    return jnp.moveaxis(a, 2, 0)


def from_chunks(a):
    a = jnp.moveaxis(a, 0, 2)
    return a.reshape(a.shape[:2] + (-1,) + a.shape[4:])


def flip_t(a):
    return jnp.flip(a, axis=2)


def hyena_filter_spectrum(length, w1, b1, w2, b2, w3, freq):
    t = jnp.arange(length, dtype=jnp.float32)
    t_norm = t / (length - 1)
    bands = jnp.linspace(1e-4, HY_BANDS - 1, HY_BANDS, dtype=jnp.float32)
    ang = (2.0 * math.pi / length) * t[:, None] * bands[None, :]
    z = jnp.concatenate([t_norm[:, None], jnp.cos(ang), -jnp.sin(ang)], axis=-1)
    hid = jnp.sin(_f32(freq[0]) * (z @ _f32(w1) + _f32(b1)))
    hid = jnp.sin(_f32(freq[1]) * (hid @ _f32(w2) + _f32(b2)))
    filt = (hid @ _f32(w3)).reshape(length, HY_ORDER, 2, HY_WIDTH)
    max_decay = math.log(HY_DECAY_TARGET) / HY_FAST_PCT
    min_decay = math.log(HY_DECAY_TARGET) / HY_SLOW_PCT
    deltas = jnp.abs(jnp.linspace(min_decay, max_decay, HY_WIDTH, dtype=jnp.float32))
    filt = filt * jnp.exp(-t_norm[:, None] * deltas[None, :])[:, None, None, :]
    fwd, bwd = filt[:, :, 0], filt[:, :, 1]
    circ = jnp.concatenate([fwd, jnp.zeros((1, HY_ORDER, HY_WIDTH), jnp.float32), bwd[:0:-1]], axis=0)
    return jnp.fft.rfft(circ, axis=0)


def long_conv(u, spec, bias):
    l = u.shape[1]
    uf = jnp.fft.rfft(u, n=2 * l, axis=1)
    y = jnp.fft.irfft(uf * spec[None], n=2 * l, axis=1)[:, :l]
    return y + u * bias


def hyena_operator(x1, x2, v, spec, bias):
    z = _f32(x1) * long_conv(_f32(v), spec[:, 0], _f32(bias[0]))
    return _f32(x2) * long_conv(z, spec[:, 1], _f32(bias[1]))


def gla_scan(q, k, v, lg, s0):
    tri = jnp.tril(jnp.ones((CHUNK, CHUNK), bool))

    def step(s, inp):
        qc, kc, vc, gc = inp
        bc = jnp.cumsum(gc, axis=2)
        inter = jnp.einsum('bhtd,bhdv->bhtv', qc * jnp.exp(bc), s)
        decay = jnp.exp(jnp.where(tri[:, :, None], bc[:, :, :, None, :] - bc[:, :, None, :, :], -jnp.inf))
        att = jnp.einsum('bhtd,bhsd,bhtsd->bhts', qc, kc, decay)
        o = inter + jnp.einsum('bhts,bhsv->bhtv', att, vc)
        b_end = bc[:, :, -1]
        s_new = jnp.exp(b_end)[..., None] * s + jnp.einsum(
            'bhsd,bhsv->bhdv', kc * jnp.exp(b_end[:, :, None] - bc), vc)
        return s_new, o

    s_fin, o = lax.scan(step, s0, (to_chunks(q), to_chunks(k), to_chunks(v), to_chunks(lg)))
    return from_chunks(o), s_fin


def mlstm_scan(q, k, v, li, lf, state0):
    tri = jnp.tril(jnp.ones((CHUNK, CHUNK), bool))

    def step(carry, inp):
        cm, nv, m = carry
        qc, kc, vc, ic, fc = inp
        b = jnp.cumsum(fc, axis=-1)
        dmat = jnp.where(tri, b[..., :, None] - b[..., None, :] + ic[..., None, :], -jnp.inf)
        m_t = jnp.maximum(b + m[..., None], jnp.max(dmat, axis=-1))
        w_inter = jnp.exp(b + m[..., None] - m_t)
        sc = jnp.einsum('bhtd,bhsd->bhts', qc, kc) * jnp.exp(dmat - m_t[..., None])
        num = w_inter[..., None] * jnp.einsum('bhtd,bhdv->bhtv', qc, cm) + jnp.einsum('bhts,bhsv->bhtv', sc, vc)
        den = w_inter * jnp.einsum('bhtd,bhd->bht', qc, nv) + jnp.sum(sc, axis=-1)
        hc = num / jnp.maximum(jnp.abs(den), jnp.exp(-m_t))[..., None]
        g = b[..., -1:] - b + ic
        m_new = jnp.maximum(b[..., -1] + m, jnp.max(g, axis=-1))
        w_c = jnp.exp(b[..., -1] + m - m_new)
        w_k = jnp.exp(g - m_new[..., None])
        c_new = w_c[..., None, None] * cm + jnp.einsum('bhs,bhsd,bhsv->bhdv', w_k, kc, vc)
        n_new = w_c[..., None] * nv + jnp.einsum('bhs,bhsd->bhd', w_k, kc)
        return (c_new, n_new, m_new), hc

    carry, h = lax.scan(step, state0, (to_chunks(q), to_chunks(k), to_chunks(v), to_chunks(li), to_chunks(lf)))
    return from_chunks(h), carry


def token_mixers(h, w_in, b_in, hy_conv, hy_spec, hy_bias, gla_a2_w, gla_a2_b, gla_norm_g,
                 ml_conv, ml_gate_b, ml_norm_g, w_branch, w_out, gla_init, ml_init, grid):
    b, l, _ = h.shape
    dt = h.dtype
    (hx1, hx2, hv, gq, gk, gv, gr, glr_f, glr_b, mq, mk, mv, mo, mi, mf, ga, gb, gc) = split_cols(h @ w_in + b_in)
    y_hy = hyena_operator(short_conv(hx1, hy_conv[0], grid), short_conv(hx2, hy_conv[1], grid),
                          short_conv(hv, hy_conv[2], grid), hy_spec, hy_bias).astype(dt)
    q = to_heads(_f32(gq), GLA_HEADS) * GLA_DK ** -0.5
    k = to_heads(_f32(gk), GLA_HEADS)
    v = to_heads(_f32(gv), GLA_HEADS)
    lg_f = to_heads(jax.nn.log_sigmoid(_f32(glr_f @ gla_a2_w[0] + gla_a2_b[0])) / GLA_NORMALIZER, GLA_HEADS)
    lg_b = to_heads(jax.nn.log_sigmoid(_f32(glr_b @ gla_a2_w[1] + gla_a2_b[1])) / GLA_NORMALIZER, GLA_HEADS)
    o_f, s_f = gla_scan(q, k, v, lg_f, gla_init[0])
    o_b, s_b = gla_scan(flip_t(q), flip_t(k), flip_t(v), flip_t(lg_b), gla_init[1])
    y_gla = from_heads_norm(o_f + flip_t(o_b), gla_norm_g).astype(dt) * jax.nn.silu(gr)
    q = to_heads(_f32(short_conv(mq, ml_conv[0], grid)), ML_HEADS)
    k = to_heads(_f32(short_conv(mk, ml_conv[1], grid)), ML_HEADS) * ML_DH ** -0.5
    v = to_heads(_f32(mv), ML_HEADS)
    li = (_f32(mi).reshape(b, l, 2, ML_HEADS) + _f32(ml_gate_b[:, 0])).transpose(2, 0, 3, 1)
    lf = jax.nn.log_sigmoid(_f32(mf).reshape(b, l, 2, ML_HEADS) + _f32(ml_gate_b[:, 1])).transpose(2, 0, 3, 1)
    h_f, st_f = mlstm_scan(q, k, v, li[0], lf[0], ml_init[0])
    h_b, st_b = mlstm_scan(flip_t(q), flip_t(k), flip_t(v), flip_t(li[1]), flip_t(lf[1]), ml_init[1])
    y_ml = from_heads_norm(h_f + flip_t(h_b), ml_norm_g).astype(dt) * jax.nn.sigmoid(mo)
    merged = (jax.nn.sigmoid(ga) * (y_hy @ w_branch[0])
              + jax.nn.sigmoid(gb) * (y_gla @ w_branch[1])
              + jax.nn.sigmoid(gc) * (y_ml @ w_branch[2]))
    return merged @ w_out, (s_f, s_b), (st_f, st_b)


def peer(h, wq, keys, u_tab, v_tab):
    b, l, d = h.shape

    def block(x):
        q = _f32(x @ wq).reshape(PEER_BLOCK, PEER_HEADS, 2, PEER_DKEY // 2)
        s = jnp.einsum('thpd,hpkd->thpk', q, _f32(keys))
        sv, si = lax.top_k(s, PEER_TOPK)
        cand = (sv[:, :, 0, :, None] + sv[:, :, 1, None, :]).reshape(PEER_BLOCK, PEER_HEADS, PEER_TOPK * PEER_TOPK)
        best, bi = lax.top_k(cand, PEER_TOPK)
        i1 = jnp.take_along_axis(si[:, :, 0], bi // PEER_TOPK, axis=-1)
        i2 = jnp.take_along_axis(si[:, :, 1], bi % PEER_TOPK, axis=-1)
        idx = i1 * PEER_NKEYS + i2
        g = jax.nn.softmax(best, axis=-1).astype(x.dtype)
        act = jnp.einsum('td,thkd->thk', x, u_tab[idx])
        return jnp.einsum('thk,thkd->td', g * jax.nn.gelu(act), v_tab[idx])

    return lax.map(block, h.reshape(-1, PEER_BLOCK, d)).reshape(b, l, d)


def setup_inputs(seed: int = 0) -> dict:
    key = jax.random.key(seed)
    ks = list(jax.random.split(key, 48))

    def nrm(i, shape, scale):
        return scale * jax.random.normal(ks[i], shape, jnp.float32)

    ml_gate_b = jnp.concatenate([
        nrm(30, (DEPTH, 2, 1, ML_HEADS), 0.1),
        jnp.linspace(3.0, 6.0, ML_HEADS, dtype=jnp.float32) + nrm(31, (DEPTH, 2, 1, ML_HEADS), 0.1)], axis=2)
    return {
        'x_prompt': nrm(0, (BATCH, SEQ, D_MODEL), 1.0),
        'x_sample': nrm(1, (DEC_BATCH, DEC_SEQ, D_MODEL), 1.0),
        'c': nrm(2, (DEC_BATCH, D_MODEL), 1.0),
        'state_gla': nrm(3, (DEC_BATCH, DEPTH, 2, GLA_HEADS, GLA_DK, GLA_DV), 0.3),
        'state_mlstm_C': nrm(4, (DEC_BATCH, DEPTH, 2, ML_HEADS, ML_DH, ML_DH), 0.3),
        'state_mlstm_n': jnp.abs(nrm(5, (DEC_BATCH, DEPTH, 2, ML_HEADS, ML_DH), 0.5)),
        'state_mlstm_m': nrm(6, (DEC_BATCH, DEPTH, 2, ML_HEADS), 0.5),
        'c_ctx': nrm(7, (D_MODEL,), 1.0),
        'mod_w': nrm(8, (DEPTH, D_MODEL, 6 * D_MODEL), 0.5 * D_MODEL ** -0.5),
        'mod_b': nrm(9, (DEPTH, 6 * D_MODEL), 0.01),
        'norm1_g': 1.0 + nrm(10, (DEPTH, D_MODEL), 0.02),
        'norm2_g': 1.0 + nrm(11, (DEPTH, D_MODEL), 0.02),
        'final_g': 1.0 + nrm(12, (D_MODEL,), 0.02),
        'w_in': nrm(13, (DEPTH, D_MODEL, IN_COLS), D_MODEL ** -0.5),
        'b_in': nrm(14, (DEPTH, IN_COLS), 0.01),
        'hy_conv': nrm(15, (DEPTH, 3, SHORT_W, HY_WIDTH), SHORT_W ** -0.5),
        'hy_w1': nrm(16, (DEPTH, HY_EMB, HY_FFN), HY_EMB ** -0.5),
        'hy_b1': nrm(17, (DEPTH, HY_FFN), 0.1),
        'hy_w2': nrm(18, (DEPTH, HY_FFN, HY_FFN), HY_FFN ** -0.5),
        'hy_b2': nrm(19, (DEPTH, HY_FFN), 0.1),
        'hy_w3': nrm(20, (DEPTH, HY_FFN, HY_ORDER * 2 * HY_WIDTH), 0.05 * HY_FFN ** -0.5),
        'hy_freq': 1.0 + nrm(21, (DEPTH, 2, HY_FFN), 0.02),
        'hy_bias': nrm(22, (DEPTH, HY_ORDER, HY_WIDTH), 0.1),
        'gla_a2_w': nrm(23, (DEPTH, 2, GLA_RANK, GLA_QK), GLA_RANK ** -0.5),
        'gla_a2_b': nrm(24, (DEPTH, 2, GLA_QK), 0.1),
        'gla_norm_g': 1.0 + nrm(25, (DEPTH, GLA_DV), 0.02),
        'ml_conv': nrm(26, (DEPTH, 2, SHORT_W, ML_W), SHORT_W ** -0.5),
        'ml_gate_b': ml_gate_b,
        'ml_norm_g': 1.0 + nrm(27, (DEPTH, ML_DH), 0.02),
        'w_branch': nrm(28, (DEPTH, N_BRANCH, BRANCH_W, D_MODEL), BRANCH_W ** -0.5),
        'w_out': nrm(29, (DEPTH, D_MODEL, D_MODEL), D_MODEL ** -0.5),
        'peer_wq': nrm(32, (DEPTH, D_MODEL, PEER_HEADS * PEER_DKEY), D_MODEL ** -0.5),
        'peer_keys': nrm(33, (DEPTH, PEER_HEADS, 2, PEER_NKEYS, PEER_DKEY // 2), (PEER_DKEY // 2) ** -0.5),
        'peer_u': nrm(34, (DEPTH, PEER_EXPERTS, D_MODEL), D_MODEL ** -0.5),
        'peer_v': nrm(35, (DEPTH, PEER_EXPERTS, D_MODEL), PEER_HEADS ** -0.5),
    }


def reference(x_prompt, x_sample, c, state_gla, state_mlstm_C, state_mlstm_n, state_mlstm_m, c_ctx,
              mod_w, mod_b, norm1_g, norm2_g, final_g, w_in, b_in, hy_conv, hy_w1, hy_b1, hy_w2, hy_b2,
              hy_w3, hy_freq, hy_bias, gla_a2_w, gla_a2_b, gla_norm_g, ml_conv, ml_gate_b, ml_norm_g,
              w_branch, w_out, peer_wq, peer_keys, peer_u, peer_v):

    def run_layer(x, mod, l, gla_init, ml_init, grid):
        sh1, sc1, gt1, sh2, sc2, gt2 = jnp.split(mod[:, None, :], 6, axis=-1)
        h = rmsnorm(x, norm1_g[l]) * (1 + sc1) + sh1
        spec = hyena_filter_spectrum(x.shape[1], hy_w1[l], hy_b1[l], hy_w2[l], hy_b2[l], hy_w3[l], hy_freq[l])
        mix, gla_fin, ml_fin = token_mixers(h, w_in[l], b_in[l], hy_conv[l], spec, hy_bias[l], gla_a2_w[l],
                                            gla_a2_b[l], gla_norm_g[l], ml_conv[l], ml_gate_b[l], ml_norm_g[l],
                                            w_branch[l], w_out[l], gla_init, ml_init, grid)
        x = x + gt1 * mix
        h = rmsnorm(x, norm2_g[l]) * (1 + sc2) + sh2
        x = x + gt2 * peer(h, peer_wq[l], peer_keys[l], peer_u[l], peer_v[l])
        return x, gla_fin, ml_fin

    bp = x_prompt.shape[0]
    z_gla = jnp.zeros((bp, GLA_HEADS, GLA_DK, GLA_DV), jnp.float32)
    z_ml = (jnp.zeros((bp, ML_HEADS, ML_DH, ML_DH), jnp.float32),
            jnp.zeros((bp, ML_HEADS, ML_DH), jnp.float32),
            jnp.zeros((bp, ML_HEADS), jnp.float32))
    xp = x_prompt
    new_gla, new_c, new_n, new_m = [], [], [], []
    for l in range(DEPTH):
        mod_ctx = jax.nn.silu(c_ctx)[None] @ mod_w[l] + mod_b[l]
        xp, gla_fin, ml_fin = run_layer(xp, mod_ctx, l, (z_gla, z_gla), (z_ml, z_ml), False)
        new_gla.append(jnp.stack(gla_fin, axis=1))
        new_c.append(jnp.stack([ml_fin[0][0], ml_fin[1][0]], axis=1))
        new_n.append(jnp.stack([ml_fin[0][1], ml_fin[1][1]], axis=1))
        new_m.append(jnp.stack([ml_fin[0][2], ml_fin[1][2]], axis=1))
    new_state_gla = jnp.stack(new_gla, axis=1)
    new_state_mlstm_C = jnp.stack(new_c, axis=1)
    new_state_mlstm_n = jnp.stack(new_n, axis=1)
    new_state_mlstm_m = jnp.stack(new_m, axis=1)

    xs = x_sample
    for l in range(DEPTH):
        mod = jax.nn.silu(c) @ mod_w[l] + mod_b[l]
        gla_init = (_f32(state_gla[:, l, 0]), _f32(state_gla[:, l, 1]))
        ml_init = tuple((_f32(state_mlstm_C[:, l, dr]), _f32(state_mlstm_n[:, l, dr]), _f32(state_mlstm_m[:, l, dr]))
                        for dr in range(2))
        xs, _, _ = run_layer(xs, mod, l, gla_init, ml_init, True)

    y_prompt = rmsnorm(xp, final_g)
    y_sample = rmsnorm(xs, final_g)
    return (y_prompt, y_sample, new_state_gla, new_state_mlstm_C, new_state_mlstm_n, new_state_mlstm_m)
```

```python
import functools
import math

import jax
import jax.numpy as jnp
import numpy as np
from jax import lax
from jax.experimental import pallas as pl
from jax.experimental.pallas import tpu as pltpu

F32 = jnp.float32
BF16 = jnp.bfloat16

D_MODEL = 2048
BATCH = 16
SEQ = 256
DEPTH = 4
DEC_BATCH = 4
DEC_SEQ = 1024
GRID_W = 64
EPS = 1e-6
CHUNK = 64
HY_WIDTH = 1024
HY_EMB = 33
HY_BANDS = (HY_EMB - 1) // 2
HY_FFN = 64
HY_DECAY_TARGET = 1e-2
HY_FAST_PCT = 0.3
HY_SLOW_PCT = 1.5
GLA_HEADS = 4
GLA_DK = 128
GLA_DV = 256
GLA_RANK = 16
GLA_NORMALIZER = 16.0
ML_HEADS = 4
ML_DH = 256
PEER_HEADS = 8
PEER_NKEYS = 128
PEER_EXPERTS = PEER_NKEYS * PEER_NKEYS
PEER_TOPK = 16

N_PROMPT = BATCH * SEQ
N_SAMPLE = DEC_BATCH * DEC_SEQ
N_ROWS = N_PROMPT + N_SAMPLE
CTX_ROW = DEC_BATCH
MOD_ROWS = 8

C_HX1, C_HX2, C_HV = 0, 1024, 2048
C_GQ, C_GK, C_GV, C_GR = 3072, 3584, 4096, 5120
C_MQ, C_MK, C_MV, C_MO = 6144, 7168, 8192, 9216
C_GA, C_GB, C_GC = 10240, 12288, 14336
N_MAIN = 16384
N_SMALL = 128

VMEM_LIMIT = 56 * 1024 * 1024


def _cp(sem):
    return pltpu.CompilerParams(dimension_semantics=sem, vmem_limit_bytes=VMEM_LIMIT)


def _dot(a, b):
    return jnp.dot(a, b, preferred_element_type=F32)


def _dot_nt(a, b):
    return lax.dot_general(a, b, (((1,), (1,)), ((), ())), preferred_element_type=F32)


def _dot_tn(a, b):
    return lax.dot_general(a, b, (((0,), (0,)), ((), ())), preferred_element_type=F32)


def _split2(a):
    hi = a.astype(BF16)
    lo = (a - hi.astype(F32)).astype(BF16)
    return hi, lo


def _split3(a):
    a1 = a.astype(BF16)
    r1 = a - a1.astype(F32)
    a2 = r1.astype(BF16)
    a3 = (r1 - a2.astype(F32)).astype(BF16)
    return a1, a2, a3


def _dot3(a, b, dot=_dot):
    ah, al = _split2(a)
    bh, bl = _split2(b)
    return dot(ah, bh) + (dot(ah, bl) + dot(al, bh))


def _log_sigmoid(x):
    return jnp.minimum(x, 0.0) - jnp.log(1.0 + jnp.exp(-jnp.abs(x)))


def _mod_row(i, tm):
    n_p = N_PROMPT // tm
    return jnp.where(i < n_p, CTX_ROW, (i - n_p) // (DEC_SEQ // tm))


def _mod_kernel(c_ref, w_ref, b_ref, o_ref):
    a = c_ref[...]
    a = a * jax.nn.sigmoid(a)
    o_ref[0] = _dot3(a, w_ref[0]) + b_ref[0]


def mod_table(cvec, mod_w, mod_b):
    tn = 1024
    n = mod_w.shape[-1]
    return pl.pallas_call(
        _mod_kernel,
        grid=(DEPTH, n // tn),
        in_specs=[pl.BlockSpec((MOD_ROWS, D_MODEL), lambda l, j: (0, 0)),
                  pl.BlockSpec((1, D_MODEL, tn), lambda l, j: (l, 0, j)),
                  pl.BlockSpec((1, 1, tn), lambda l, j: (l, 0, j))],
        out_specs=pl.BlockSpec((1, MOD_ROWS, tn), lambda l, j: (l, 0, j)),
        out_shape=jax.ShapeDtypeStruct((DEPTH, MOD_ROWS, n), F32),
        compiler_params=_cp(("parallel", "parallel")),
        name="mod_table",
    )(cvec, mod_w, mod_b.reshape(DEPTH, 1, n))


def _normmod_kernel(x_ref, g_ref, sc_ref, sh_ref, hi_ref, lo_ref, *t_ref):
    x = x_ref[...]
    y = x * lax.rsqrt(jnp.mean(x * x, axis=-1, keepdims=True) + EPS) * g_ref[...]
    h = y * (1.0 + sc_ref[...]) + sh_ref[...]
    hi = h.astype(BF16)
    hi_ref[...] = hi
    lo_ref[...] = (h - hi.astype(F32)).astype(BF16)
    if t_ref:
        t_ref[0][...] = h.T.astype(BF16)


def normmod(x, g, mod3, sh_chunk, sc_chunk, transposed=False):
    tm = 256
    out_shape = [jax.ShapeDtypeStruct((N_ROWS, D_MODEL), BF16)] * 2
    out_specs = [pl.BlockSpec((tm, D_MODEL), lambda i: (i, 0))] * 2
    if transposed:
        out_shape = out_shape + [jax.ShapeDtypeStruct((D_MODEL, N_ROWS), BF16)]
        out_specs = out_specs + [pl.BlockSpec((D_MODEL, tm), lambda i: (0, i))]
    return pl.pallas_call(
        _normmod_kernel,
        grid=(N_ROWS // tm,),
        in_specs=[pl.BlockSpec((tm, D_MODEL), lambda i: (i, 0)),
                  pl.BlockSpec((1, D_MODEL), lambda i: (0, 0)),
                  pl.BlockSpec((None, 1, D_MODEL), lambda i: (_mod_row(i, tm), 0, sc_chunk)),
                  pl.BlockSpec((None, 1, D_MODEL), lambda i: (_mod_row(i, tm), 0, sh_chunk))],
        out_specs=out_specs,
        out_shape=out_shape,
        compiler_params=_cp(("parallel",)),
        name="normmod",
    )(x, g.reshape(1, D_MODEL), mod3, mod3)


def _final_norm_kernel(x_ref, g_ref, o_ref):
    x = x_ref[...]
    o_ref[...] = x * lax.rsqrt(jnp.mean(x * x, axis=-1, keepdims=True) + EPS) * g_ref[...]


def final_norm(x, g):
    tm = 256
    return pl.pallas_call(
        _final_norm_kernel,
        grid=(N_ROWS // tm,),
        in_specs=[pl.BlockSpec((tm, D_MODEL), lambda i: (i, 0)),
                  pl.BlockSpec((1, D_MODEL), lambda i: (0, 0))],
        out_specs=pl.BlockSpec((tm, D_MODEL), lambda i: (i, 0)),
        out_shape=jax.ShapeDtypeStruct((N_ROWS, D_MODEL), F32),
        compiler_params=_cp(("parallel",)),
        name="final_norm",
    )(x, g.reshape(1, D_MODEL))


def _mm_bias_kernel(x_ref, w_ref, b_ref, o_ref):
    o_ref[...] = (_dot(x_ref[...], w_ref[...]) + b_ref[...]).astype(o_ref.dtype)


def mm_bias(x, w, b, out_dtype=F32, tm=512, tn=1024):
    m, k = x.shape
    n = w.shape[1]
    return pl.pallas_call(
        _mm_bias_kernel,
        grid=(n // tn, m // tm),
        in_specs=[pl.BlockSpec((tm, k), lambda j, i: (i, 0)),
                  pl.BlockSpec((k, tn), lambda j, i: (0, j)),
                  pl.BlockSpec((1, tn), lambda j, i: (0, j))],
        out_specs=pl.BlockSpec((tm, tn), lambda j, i: (i, j)),
        out_shape=jax.ShapeDtypeStruct((m, n), out_dtype),
        compiler_params=_cp(("parallel", "parallel")),
        name="mm_bias",
    )(x, w, b)


def _mm3_bias_kernel(xh_ref, xl_ref, wh_ref, wl_ref, b_ref, o_ref):
    xh = xh_ref[...]
    acc = _dot(xh, wh_ref[...]) + (_dot(xh, wl_ref[...]) + _dot(xl_ref[...], wh_ref[...]))
    o_ref[...] = acc + b_ref[...]


def mm3_bias(xh, xl, wh, wl, b, tm=512, tn=1024):
    m, k = xh.shape
    n = wh.shape[1]
    tn = min(tn, n)
    return pl.pallas_call(
        _mm3_bias_kernel,
        grid=(n // tn, m // tm),
        in_specs=[pl.BlockSpec((tm, k), lambda j, i: (i, 0)),
                  pl.BlockSpec((tm, k), lambda j, i: (i, 0)),
                  pl.BlockSpec((k, tn), lambda j, i: (0, j)),
                  pl.BlockSpec((k, tn), lambda j, i: (0, j)),
                  pl.BlockSpec((1, tn), lambda j, i: (0, j))],
        out_specs=pl.BlockSpec((tm, tn), lambda j, i: (i, j)),
        out_shape=jax.ShapeDtypeStruct((m, n), F32),
        compiler_params=_cp(("parallel", "parallel")),
        name="mm3_bias",
    )(xh, xl, wh, wl, b)


def _merge_kernel(yh_ref, yg_ref, ym_ref, w_ref, ga_ref, gb_ref, gc_ref, o_ref):
    acc = jax.nn.sigmoid(ga_ref[...]) * _dot(yh_ref[...], w_ref[0])
    acc += jax.nn.sigmoid(gb_ref[...]) * _dot(yg_ref[...], w_ref[1])
    acc += jax.nn.sigmoid(gc_ref[...]) * _dot(ym_ref[...], w_ref[2])
    o_ref[...] = acc.astype(o_ref.dtype)


def merge_branches(y_hy, y_gla, y_ml, w_branch, zmain, tm=512, tn=1024):
    kb = HY_WIDTH
    y_spec = pl.BlockSpec((tm, kb), lambda j, i: (i, 0))

    def gate_spec(col):
        return pl.BlockSpec((tm, tn), lambda j, i, c=col // tn: (i, c + j))

    return pl.pallas_call(
        _merge_kernel,
        grid=(D_MODEL // tn, N_ROWS // tm),
        in_specs=[y_spec, y_spec, y_spec,
                  pl.BlockSpec((3, kb, tn), lambda j, i: (0, 0, j)),
                  gate_spec(C_GA), gate_spec(C_GB), gate_spec(C_GC)],
        out_specs=pl.BlockSpec((tm, tn), lambda j, i: (i, j)),
        out_shape=jax.ShapeDtypeStruct((N_ROWS, D_MODEL), BF16),
        compiler_params=_cp(("parallel", "parallel")),
        name="merge_branches",
    )(y_hy, y_gla, y_ml, w_branch, zmain, zmain, zmain)


def _mm_resid_kernel(m_ref, w_ref, x_ref, gt_ref, o_ref):
    o_ref[...] = x_ref[...] + gt_ref[...] * _dot(m_ref[...], w_ref[...])


def mm_residual(merged, w, x, mod3, gt_chunk, tm=512, tn=1024):
    k = merged.shape[1]
    return pl.pallas_call(
        _mm_resid_kernel,
        grid=(D_MODEL // tn, N_ROWS // tm),
        in_specs=[pl.BlockSpec((tm, k), lambda j, i: (i, 0)),
                  pl.BlockSpec((k, tn), lambda j, i: (0, j)),
                  pl.BlockSpec((tm, tn), lambda j, i: (i, j)),
                  pl.BlockSpec((None, 1, tn),
                               lambda j, i: (_mod_row(i, tm), 0, gt_chunk * (D_MODEL // tn) + j))],
        out_specs=pl.BlockSpec((tm, tn), lambda j, i: (i, j)),
        out_shape=jax.ShapeDtypeStruct((N_ROWS, D_MODEL), F32),
        compiler_params=_cp(("parallel", "parallel")),
        name="mm_residual",
    )(merged, w, x, mod3)


def _dft_tables(length):
    k = jnp.arange(length, dtype=jnp.int32)
    m = (k[:, None] * k[None, :]) % (2 * length)
    ang = m.astype(F32) * (math.pi / length)
    cos = jnp.cos(ang)
    sin = jnp.sin(ang)
    sgn = jnp.where(k % 2 == 0, 1.0, -1.0).astype(F32)
    fre = cos
    fim = jnp.where(k[:, None] == 0, sgn[None, :], -sin)
    wk = jnp.where(k == 0, 1.0, 2.0).astype(F32) / (2.0 * length)
    g_re = cos.T * wk[None, :]
    g_im = jnp.where(k[None, :] == 0, sgn[:, None] / (2.0 * length), -sin.T / length)
    return fre, fim, jnp.concatenate([g_re, g_im], axis=1), sgn


def _hyfilt_kernel(z_ref, w1_ref, b1_ref, w2_ref, b2_ref, fq_ref, w3f_ref, w3b_ref, tn_ref, dl_ref,
                   sgn_ref, freh_ref, frel_ref, fimh_ref, fiml_ref, sre_ref, sim_ref):
    hid = jnp.sin(fq_ref[0:1] * (_dot3(z_ref[...], w1_ref[...]) + b1_ref[...]))
    hid = jnp.sin(fq_ref[1:2] * (_dot3(hid, w2_ref[...]) + b2_ref[...]))
    decay = jnp.exp(-tn_ref[...] * dl_ref[...])
    fwd = _dot3(hid, w3f_ref[...]) * decay
    bwd = _dot3(hid, w3b_ref[...]) * decay
    row = lax.broadcasted_iota(jnp.int32, fwd.shape, 0)
    bwd = jnp.where(row == 0, 0.0, bwd)
    a = fwd + bwd
    d = fwd - bwd
    ah, al = _split2(a)
    dh, dl2 = _split2(d)
    re = _dot(freh_ref[...], ah) + (_dot(freh_ref[...], al) + _dot(frel_ref[...], ah))
    im = _dot(fimh_ref[...], dh) + (_dot(fimh_ref[...], dl2) + _dot(fiml_ref[...], dh))
    nyq = jnp.sum(sgn_ref[...] * a, axis=0, keepdims=True)
    sre_ref[0] = re
    sim_ref[0] = jnp.where(row == 0, nyq, im)


def hyena_spectrum(length, w1, b1, w2, b2, w3, freq, tabs):
    fre_h, fre_l, fim_h, fim_l, sgn = tabs
    t = jnp.arange(length, dtype=F32)
    t_norm = t / (length - 1)
    bands = jnp.linspace(1e-4, HY_BANDS - 1, HY_BANDS, dtype=F32)
    ang = (2.0 * math.pi / length) * t[:, None] * bands[None, :]
    z = jnp.concatenate([t_norm[:, None], jnp.cos(ang), -jnp.sin(ang),
                         jnp.zeros((length, HY_FFN - HY_EMB), F32)], axis=-1)
    w1p = jnp.pad(w1, ((0, HY_FFN - HY_EMB), (0, 0)))
    max_decay = math.log(HY_DECAY_TARGET) / HY_FAST_PCT
    min_decay = math.log(HY_DECAY_TARGET) / HY_SLOW_PCT
    deltas = jnp.abs(jnp.linspace(min_decay, max_decay, HY_WIDTH, dtype=F32)).reshape(1, HY_WIDTH)
    ct = 256
    nct = HY_WIDTH // ct
    full = lambda shape: pl.BlockSpec(shape, lambda o, j: (0,) * len(shape))
    out_spec = pl.BlockSpec((1, length, ct), lambda o, j: (o, 0, j))
    return pl.pallas_call(
        _hyfilt_kernel,
        grid=(2, nct),
        in_specs=[full((length, HY_FFN)), full((HY_FFN, HY_FFN)), full((1, HY_FFN)),
                  full((HY_FFN, HY_FFN)), full((1, HY_FFN)), full((2, HY_FFN)),
                  pl.BlockSpec((HY_FFN, ct), lambda o, j: (0, o * 2 * nct + j)),
                  pl.BlockSpec((HY_FFN, ct), lambda o, j: (0, o * 2 * nct + nct + j)),
                  full((length, 1)),
                  pl.BlockSpec((1, ct), lambda o, j: (0, j)),
                  full((length, 1)),
                  full((length, length)), full((length, length)),
                  full((length, length)), full((length, length))],
        out_specs=[out_spec, out_spec],
        out_shape=[jax.ShapeDtypeStruct((2, length, HY_WIDTH), F32)] * 2,
        compiler_params=_cp(("parallel", "parallel")),
        name="hyena_spectrum",
    )(z, w1p, b1.reshape(1, HY_FFN), w2, b2.reshape(1, HY_FFN), freq, w3, w3,
      t_norm.reshape(length, 1), deltas, sgn.reshape(length, 1), fre_h, fre_l, fim_h, fim_l)


def _short_conv(x, w, pos, seg, length):
    prev = jnp.where(pos == 0, 0.0, pltpu.roll(x, 1, 0))
    nxt = jnp.where(pos == seg - 1, 0.0, pltpu.roll(x, length - 1, 0))
    return prev * w[0:1] + x * w[1:2] + nxt * w[2:3]


def _hyena_kernel(x1_ref, x2_ref, v_ref, cw_ref, sre_ref, sim_ref, hb_ref, f_ref, g_ref, o_ref,
                  *, length, seg):
    row = lax.broadcasted_iota(jnp.int32, (length, 1), 0)
    pos = row % seg
    row0 = row == 0

    def long_conv(u, o):
        spec = _dot(f_ref[...], u.astype(BF16))
        ur, ui = spec[:length], spec[length:]
        hre, him = sre_ref[o], sim_ref[o]
        uihi = ui * him
        yr = ur * hre - jnp.where(row0, 0.0, uihi)
        yi = jnp.where(row0, uihi, ur * him + ui * hre)
        y = _dot(g_ref[...], jnp.concatenate([yr, yi], axis=0).astype(BF16))
        return y + u * hb_ref[o:o + 1]

    x1 = _short_conv(x1_ref[...], cw_ref[0], pos, seg, length)
    x2 = _short_conv(x2_ref[...], cw_ref[1], pos, seg, length)
    v = _short_conv(v_ref[...], cw_ref[2], pos, seg, length)
    z = x1 * long_conv(v, 0)
    o_ref[...] = (x2 * long_conv(z, 1)).astype(o_ref.dtype)


def hyena(zmain, conv_w, spec_re, spec_im, bias, f_mat, g_mat, length, n_seq, row_blk0, seg):
    ct = 256
    nct = HY_WIDTH // ct

    def zspec(col):
        return pl.BlockSpec((length, ct), lambda b, j, c=col // ct: (row_blk0 + b, c + j))

    return pl.pallas_call(
        functools.partial(_hyena_kernel, length=length, seg=seg),
        grid=(n_seq, nct),
        in_specs=[zspec(C_HX1), zspec(C_HX2), zspec(C_HV),
                  pl.BlockSpec((3, 3, ct), lambda b, j: (0, 0, j)),
                  pl.BlockSpec((2, length, ct), lambda b, j: (0, 0, j)),
                  pl.BlockSpec((2, length, ct), lambda b, j: (0, 0, j)),
                  pl.BlockSpec((2, ct), lambda b, j: (0, j)),
                  pl.BlockSpec((2 * length, length), lambda b, j: (0, 0)),
                  pl.BlockSpec((length, 2 * length), lambda b, j: (0, 0))],
        out_specs=pl.BlockSpec((length, ct), lambda b, j: (b, j)),
        out_shape=jax.ShapeDtypeStruct((n_seq * length, HY_WIDTH), BF16),
        compiler_params=_cp(("parallel", "parallel")),
        name="hyena",
    )(zmain, zmain, zmain, conv_w, spec_re, spec_im, bias, f_mat, g_mat)


def _tri_masks():
    t = lax.broadcasted_iota(jnp.int32, (CHUNK, CHUNK), 0)
    s = lax.broadcasted_iota(jnp.int32, (CHUNK, CHUNK), 1)
    return s <= t, s >= t


def _gla_chunk(q, k, v, g, state, mask, rev):
    tm = jnp.where(mask, 1.0, 0.0).astype(BF16)
    g1, g2, g3 = _split3(g)
    bc = _dot(tm, g1) + (_dot(tm, g2) + _dot(tm, g3))
    b_end = bc[0:1] if rev else bc[CHUNK - 1:CHUNK]
    ref = bc[CHUNK // 2:CHUNK // 2 + 1]
    inter = _dot((q * jnp.exp(bc)).astype(BF16), state.astype(BF16))
    qh = (q * jnp.exp(bc - ref)).astype(BF16)
    kh = (k * jnp.exp(ref - bc)).astype(BF16)
    att = jnp.where(mask, _dot_nt(qh, kh), 0.0)
    vb = v.astype(BF16)
    o = inter + _dot(att.astype(BF16), vb)
    ones = jnp.ones((CHUNK, GLA_DK), BF16)
    tot = _dot_tn(g1, ones) + (_dot_tn(g2, ones) + _dot_tn(g3, ones))
    e = jnp.exp(tot)
    kd = (k * jnp.exp(b_end - bc)).astype(BF16)
    new_state = jnp.concatenate([e, e], axis=1) * state + _dot_tn(kd, vb)
    return o, new_state


def _gla_kernel(*refs, length, has_init):
    if has_init:
        (q_ref, k_ref, v_ref, gr_ref, zs_ref, wf_ref, wb_ref, ab_ref, ng_ref, s0_ref,
         y_ref, of_ref, ob_ref, lg_ref, st_ref) = refs
    else:
        (q_ref, k_ref, v_ref, gr_ref, zs_ref, wf_ref, wb_ref, ab_ref, ng_ref,
         y_ref, sout_ref, of_ref, ob_ref, lg_ref, st_ref) = refs
    zs = zs_ref[...]
    lg_ref[0] = _log_sigmoid(_dot3(zs, wf_ref[...]) + ab_ref[0]) * (1.0 / GLA_NORMALIZER)
    lg_ref[1] = _log_sigmoid(_dot3(zs, wb_ref[...]) + ab_ref[1]) * (1.0 / GLA_NORMALIZER)
    if has_init:
        st_ref[...] = s0_ref[...]
    else:
        st_ref[...] = jnp.zeros_like(st_ref)
    n = length // CHUNK
    mask_f, mask_b = _tri_masks()
    scale = GLA_DK ** -0.5

    def body(i, carry):
        for d, (mask, o_acc) in enumerate(((mask_f, of_ref), (mask_b, ob_ref))):
            c = i if d == 0 else n - 1 - i
            rows = pl.ds(pl.multiple_of(c * CHUNK, CHUNK), CHUNK)
            o, s_new = _gla_chunk(q_ref[rows, :] * scale, k_ref[rows, :], v_ref[rows, :],
                                  lg_ref[d, rows, :], st_ref[d], mask, d == 1)
            o_acc[rows, :] = o
            st_ref[d] = s_new
        return carry

    lax.fori_loop(0, n, body, 0)
    o = of_ref[...] + ob_ref[...]
    o = o * lax.rsqrt(jnp.mean(o * o, axis=-1, keepdims=True) + EPS) * ng_ref[...]
    gr = gr_ref[...]
    y_ref[...] = (o * (gr * jax.nn.sigmoid(gr))).astype(y_ref.dtype)
    if not has_init:
        sout_ref[...] = st_ref[...]


def gla(zmain, zsmall, wf, wb, ab, norm_g, length, n_seq, row_blk0, state0, layer):
    has_init = state0 is not None

    def zspec(col, width):
        return pl.BlockSpec((length, width), lambda b, h, c=col // width: (row_blk0 + b, c + h))

    in_specs = [zspec(C_GQ, GLA_DK), zspec(C_GK, GLA_DK), zspec(C_GV, GLA_DV), zspec(C_GR, GLA_DV),
                pl.BlockSpec((length, N_SMALL), lambda b, h: (row_blk0 + b, 0)),
                pl.BlockSpec((N_SMALL, GLA_DK), lambda b, h: (0, h)),
                pl.BlockSpec((N_SMALL, GLA_DK), lambda b, h: (0, h)),
                pl.BlockSpec((2, 1, GLA_DK), lambda b, h: (0, 0, h)),
                pl.BlockSpec((1, GLA_DV), lambda b, h: (0, 0))]
    args = [zmain, zmain, zmain, zmain, zsmall, wf, wb, ab, norm_g]
    y_shape = jax.ShapeDtypeStruct((n_seq * length, GLA_HEADS * GLA_DV), BF16)
    y_spec = pl.BlockSpec((length, GLA_DV), lambda b, h: (b, h))
    if has_init:
        in_specs.append(pl.BlockSpec((None, None, 2, None, GLA_DK, GLA_DV),
                                     lambda b, h: (b, layer, 0, h, 0, 0)))
        args.append(state0)
        out_shape, out_specs = y_shape, y_spec
    else:
        out_shape = [y_shape, jax.ShapeDtypeStruct((n_seq, 2, GLA_HEADS, GLA_DK, GLA_DV), F32)]
        out_specs = [y_spec, pl.BlockSpec((None, 2, None, GLA_DK, GLA_DV), lambda b, h: (b, 0, h, 0, 0))]
    return pl.pallas_call(
        functools.partial(_gla_kernel, length=length, has_init=has_init),
        grid=(n_seq, GLA_HEADS),
        in_specs=in_specs,
        out_specs=out_specs,
        out_shape=out_shape,
        scratch_shapes=[pltpu.VMEM((length, GLA_DV), F32), pltpu.VMEM((length, GLA_DV), F32),
                        pltpu.VMEM((2, length, GLA_DK), F32), pltpu.VMEM((2, GLA_DK, GLA_DV), F32)],
        compiler_params=_cp(("parallel", "parallel")),
        name="gla",
    )(*args)


def _mlstm_chunk(q, ks, v, lf_c, li_c, lf_r, li_r, cm, nv, m_prev, mask, mask_t, rev):
    t_n = CHUNK
    tm = jnp.where(mask, 1.0, 0.0).astype(BF16)
    tmt = jnp.where(mask_t, 1.0, 0.0).astype(BF16)
    c1, c2, c3 = _split3(jnp.broadcast_to(lf_c, (t_n, t_n)))
    b_colb = _dot(tm, c1) + (_dot(tm, c2) + _dot(tm, c3))
    r1, r2, r3 = _split3(jnp.broadcast_to(lf_r, (t_n, t_n)))
    b_rowb = _dot(r1, tmt) + (_dot(r2, tmt) + _dot(r3, tmt))
    b_col = b_colb[:, 0:1]
    b_row = b_rowb[0:1, :]
    b_end = b_colb[0:1, 0:1] if rev else b_colb[t_n - 1:t_n, 0:1]
    dmat = jnp.where(mask, b_colb - b_rowb + li_r, -jnp.inf)
    m_t = jnp.maximum(b_col + m_prev, jnp.max(dmat, axis=-1, keepdims=True))
    w_inter = jnp.exp(b_col + m_prev - m_t)
    qb = q.astype(BF16)
    vb = v.astype(BF16)
    sc = _dot_nt(qb, ks.astype(BF16)) * jnp.exp(dmat - m_t)
    num = w_inter * _dot(qb, cm.astype(BF16)) + _dot(sc.astype(BF16), vb)
    den = w_inter * jnp.sum(q * nv, axis=-1, keepdims=True) + jnp.sum(sc, axis=-1, keepdims=True)
    h = num / jnp.maximum(jnp.abs(den), jnp.exp(-m_t))
    g_r = b_end - b_row + li_r
    g_c = b_end - b_col + li_c
    m_new = jnp.maximum(b_end + m_prev, jnp.max(g_r, axis=-1, keepdims=True))
    w_c = jnp.exp(b_end + m_prev - m_new)
    kw = ks * jnp.exp(g_c - m_new)
    cm_new = w_c * cm + _dot_tn(kw.astype(BF16), vb)
    nv_new = w_c * nv + jnp.sum(kw, axis=0, keepdims=True)
    return h, cm_new, nv_new, m_new


def _mlstm_kernel(*refs, length, seg, has_init):
    if has_init:
        (gb_ref, m0_ref, q_ref, k_ref, v_ref, mo_ref, gc_ref, gr_ref, cw_ref, ng_ref, c0_ref, n0_ref,
         y_ref, qc_ref, kc_ref, hf_ref, hb_ref, c_ref, n_ref, m_ref) = refs
    else:
        (gb_ref, q_ref, k_ref, v_ref, mo_ref, gc_ref, gr_ref, cw_ref, ng_ref,
         y_ref, cout_ref, nout_ref, mout_ref, qc_ref, kc_ref, hf_ref, hb_ref, c_ref, n_ref, m_ref) = refs
    b_idx = pl.program_id(0)
    h_idx = pl.program_id(1)
    row = lax.broadcasted_iota(jnp.int32, (length, 1), 0)
    pos = row % seg
    qc_ref[...] = _short_conv(q_ref[...], cw_ref[0], pos, seg, length)
    kc_ref[...] = _short_conv(k_ref[...], cw_ref[1], pos, seg, length) * (ML_DH ** -0.5)
    if has_init:
        c_ref[...] = c0_ref[...]
        n_ref[...] = n0_ref[...]
        for d in range(2):
            m_ref[d] = jnp.full((1, 128), m0_ref[b_idx * 2 * ML_HEADS + d * ML_HEADS + h_idx], F32)
    else:
        c_ref[...] = jnp.zeros_like(c_ref)
        n_ref[...] = jnp.zeros_like(n_ref)
        m_ref[...] = jnp.zeros_like(m_ref)
    n = length // CHUNK
    mask_f, mask_b = _tri_masks()

    def body(i, carry):
        for d, (mask, mask_t, h_acc) in enumerate(((mask_f, mask_b, hf_ref), (mask_b, mask_f, hb_ref))):
            c = i if d == 0 else n - 1 - i
            rows = pl.ds(pl.multiple_of(c * CHUNK, CHUNK), CHUNK)
            bi = gb_ref[d * 2 * ML_HEADS + h_idx]
            bf = gb_ref[d * 2 * ML_HEADS + ML_HEADS + h_idx]
            gcol = gc_ref[rows, :]
            grow = gr_ref[c]
            li_c = gcol[:, d:d + 1] + bi
            lf_c = _log_sigmoid(gcol[:, 2 + d:3 + d] + bf)
            li_r = grow[d:d + 1, :] + bi
            lf_r = _log_sigmoid(grow[2 + d:3 + d, :] + bf)
            h, cm, nv, m_new = _mlstm_chunk(qc_ref[rows, :], kc_ref[rows, :], v_ref[rows, :],
                                            lf_c, li_c, lf_r, li_r, c_ref[d], n_ref[d],
                                            m_ref[d][:, 0:1], mask, mask_t, d == 1)
            h_acc[rows, :] = h
            c_ref[d] = cm
            n_ref[d] = nv
            m_ref[d] = jnp.broadcast_to(m_new, (1, 128))
        return carry

    lax.fori_loop(0, n, body, 0)
    o = hf_ref[...] + hb_ref[...]
    o = o * lax.rsqrt(jnp.mean(o * o, axis=-1, keepdims=True) + EPS) * ng_ref[...]
    y_ref[...] = (o * jax.nn.sigmoid(mo_ref[...])).astype(y_ref.dtype)
    if not has_init:
        cout_ref[...] = c_ref[...]
        nout_ref[...] = n_ref[...]
        mout_ref[...] = m_ref[...]


def mlstm(zmain, gates_col, gates_row, gate_b, conv_w, norm_g, length, n_seq, row_blk0, seg, init, layer):
    has_init = init is not None
    dh = ML_DH

    def zspec(col):
        return pl.BlockSpec((length, dh), lambda b, h, c=col // dh: (row_blk0 + b, c + h))

    smem = pl.BlockSpec(memory_space=pltpu.SMEM)
    in_specs = [smem]
    args = [gate_b]
    if has_init:
        c0, n0, m0 = init
        in_specs.append(smem)
        args.append(m0)
    in_specs += [zspec(C_MQ), zspec(C_MK), zspec(C_MV), zspec(C_MO),
                 pl.BlockSpec((None, length, 8), lambda b, h: (h, row_blk0 + b, 0)),
                 pl.BlockSpec((None, length // CHUNK, 8, CHUNK), lambda b, h: (h, row_blk0 + b, 0, 0)),
                 pl.BlockSpec((2, 3, dh), lambda b, h: (0, 0, h)),
                 pl.BlockSpec((1, dh), lambda b, h: (0, 0))]
    args += [zmain, zmain, zmain, zmain, gates_col, gates_row, conv_w, norm_g]
    y_shape = jax.ShapeDtypeStruct((n_seq * length, ML_HEADS * dh), BF16)
    y_spec = pl.BlockSpec((length, dh), lambda b, h: (b, h))
    if has_init:
        in_specs += [pl.BlockSpec((None, None, 2, None, dh, dh), lambda b, h: (b, layer, 0, h, 0, 0)),
                     pl.BlockSpec((None, None, 2, None, 1, dh), lambda b, h: (b, layer, 0, h, 0, 0))]
        args += [c0, n0]
        out_shape, out_specs = y_shape, y_spec
    else:
        out_shape = [y_shape,
                     jax.ShapeDtypeStruct((n_seq, 2, ML_HEADS, dh, dh), F32),
                     jax.ShapeDtypeStruct((n_seq, 2, ML_HEADS, 1, dh), F32),
                     jax.ShapeDtypeStruct((n_seq, 2, ML_HEADS, 1, 128), F32)]
        out_specs = [y_spec,
                     pl.BlockSpec((None, 2, None, dh, dh), lambda b, h: (b, 0, h, 0, 0)),
                     pl.BlockSpec((None, 2, None, 1, dh), lambda b, h: (b, 0, h, 0, 0)),
                     pl.BlockSpec((None, 2, None, 1, 128), lambda b, h: (b, 0, h, 0, 0))]
    return pl.pallas_call(
        functools.partial(_mlstm_kernel, length=length, seg=seg, has_init=has_init),
        grid=(n_seq, ML_HEADS),
        in_specs=in_specs,
        out_specs=out_specs,
        out_shape=out_shape,
        scratch_shapes=[pltpu.VMEM((length, dh), F32), pltpu.VMEM((length, dh), F32),
                        pltpu.VMEM((length, dh), F32), pltpu.VMEM((length, dh), F32),
                        pltpu.VMEM((2, dh, dh), F32), pltpu.VMEM((2, 1, dh), F32),
                        pltpu.VMEM((2, 1, 128), F32)],
        compiler_params=_cp(("parallel", "parallel")),
        name="mlstm",
    )(*args)


def _top_values(x, count):
    vals = []
    cur = x
    for r in range(count):
        m = jnp.max(cur, axis=0, keepdims=True)
        vals.append(m)
        if r + 1 < count:
            cur = jnp.where(cur == m, -jnp.inf, cur)
    return vals


def _peer_topk_kernel(q_ref, k_ref, s1_ref, s2_ref, e1_ref, e2_ref, th_ref):
    dk = PEER_NKEYS
    s1 = _dot3(k_ref[0], q_ref[:, 0:dk], dot=_dot_nt)
    s2 = _dot3(k_ref[1], q_ref[:, dk:2 * dk], dot=_dot_nt)
    v1 = _top_values(s1, PEER_TOPK)
    v2 = _top_values(s2, PEER_TOPK)
    sv2 = jnp.concatenate(v2, axis=0)
    cand = jnp.concatenate([a + sv2 for a in v1], axis=0)
    best = _top_values(cand, PEER_TOPK)
    zsum = jnp.exp(best[0] - best[0])
    for r in range(1, PEER_TOPK):
        zsum = zsum + jnp.exp(best[r] - best[0])
    s1_ref[...] = s1
    s2_ref[...] = s2
    e1_ref[...] = jnp.exp(s1 - v1[0]) / zsum
    e2_ref[...] = jnp.exp(s2 - v2[0])
    th_ref[...] = jnp.broadcast_to(best[PEER_TOPK - 1], th_ref.shape)


def peer_topk(q, keys, tt=256):
    n = q.shape[0]
    nk = PEER_NKEYS
    sds = jax.ShapeDtypeStruct((PEER_HEADS, nk, n), F32)
    spec = pl.BlockSpec((None, nk, tt), lambda i, h: (h, 0, i))
    return pl.pallas_call(
        _peer_topk_kernel,
        grid=(n // tt, PEER_HEADS),
        in_specs=[pl.BlockSpec((tt, 2 * nk), lambda i, h: (i, h)),
                  pl.BlockSpec((None, 2, nk, nk), lambda i, h: (h, 0, 0, 0))],
        out_specs=[spec, spec, spec, spec, pl.BlockSpec((None, 8, tt), lambda i, h: (h, 0, i))],
        out_shape=[sds, sds, sds, sds, jax.ShapeDtypeStruct((PEER_HEADS, 8, n), F32)],
        compiler_params=_cp(("parallel", "parallel")),
        name="peer_topk",
    )(q, keys)


def _gelu_tanh(x):
    return 0.5 * x * (1.0 + jnp.tanh(math.sqrt(2.0 / math.pi) * (x + 0.044715 * (x * x * x))))


def _peer_expert_kernel(ht_ref, u_ref, vt_ref, s1_ref, s2_ref, e1_ref, e2_ref, th_ref, x_ref, gt_ref,
                        o_ref, acc_ref, w_ref, *, n_i1, tt):
    j = pl.program_id(1)

    @pl.when(j == 0)
    def _():
        acc_ref[...] = jnp.zeros_like(acc_ref)

    act = _dot(u_ref[...], ht_ref[...])
    nk = PEER_NKEYS
    for ts in range(tt // 128):
        lanes = slice(ts * 128, (ts + 1) * 128)
        for r in range(n_i1):
            w = None
            for h in range(PEER_HEADS):
                s1r = s1_ref[h, r:r + 1, lanes]
                e1r = e1_ref[h, r:r + 1, lanes]
                hit = (s1r + s2_ref[h, :, lanes]) >= th_ref[h, 0:1, lanes]
                term = jnp.where(hit, e2_ref[h, :, lanes] * e1r, 0.0)
                w = term if w is None else w + term
            w_ref[r * nk:(r + 1) * nk, lanes] = w
    p = (w_ref[...] * _gelu_tanh(act)).astype(BF16)
    acc_ref[...] += _dot(vt_ref[...], p)

    @pl.when(j == pl.num_programs(1) - 1)
    def _():
        o_ref[...] = x_ref[...] + gt_ref[...] * acc_ref[...].T


def peer_experts(h2t, u_tab, vt_tab, s1, s2, e1, e2, th, x, mod3, gt_chunk, tt=512, ec=1024):
    n = x.shape[0]
    nk = PEER_NKEYS
    n_i1 = ec // nk
    once = pl.Buffered(1)
    sspec = pl.BlockSpec((PEER_HEADS, nk, tt), lambda i, j: (0, 0, i), pipeline_mode=once)
    rspec = pl.BlockSpec((PEER_HEADS, n_i1, tt), lambda i, j: (0, j, i))
    return pl.pallas_call(
        functools.partial(_peer_expert_kernel, n_i1=n_i1, tt=tt),
        grid=(n // tt, PEER_EXPERTS // ec),
        in_specs=[pl.BlockSpec((D_MODEL, tt), lambda i, j: (0, i)),
                  pl.BlockSpec((ec, D_MODEL), lambda i, j: (j, 0)),
                  pl.BlockSpec((D_MODEL, ec), lambda i, j: (0, j)),
                  rspec, sspec, rspec, sspec,
                  pl.BlockSpec((PEER_HEADS, 8, tt), lambda i, j: (0, 0, i), pipeline_mode=once),
                  pl.BlockSpec((tt, D_MODEL), lambda i, j: (i, 0), pipeline_mode=once),
                  pl.BlockSpec((None, 1, D_MODEL), lambda i, j: (_mod_row(i, tt), 0, gt_chunk))],
        out_specs=pl.BlockSpec((tt, D_MODEL), lambda i, j: (i, 0)),
        out_shape=jax.ShapeDtypeStruct((n, D_MODEL), F32),
        scratch_shapes=[pltpu.VMEM((D_MODEL, tt), F32), pltpu.VMEM((ec, tt), F32)],
        compiler_params=_cp(("parallel", "arbitrary")),
        name="peer_experts",
    )(h2t, u_tab, vt_tab, s1, s2, e1, e2, th, x, mod3)


def _reorder_in_proj(w_in, b_in):
    splits = (1024, 1024, 1024, 512, 512, 1024, 1024, 16, 16, 1024, 1024, 1024, 1024, 8, 8, 2048, 2048, 2048)
    offs = np.concatenate([[0], np.cumsum(splits)])
    seg = lambda a, i: a[..., offs[i]:offs[i + 1]]
    main_ids = (0, 1, 2, 3, 4, 5, 6, 9, 10, 11, 12, 15, 16, 17)
    small_ids = (7, 8, 13, 14)
    w_main = jnp.concatenate([seg(w_in, i) for i in main_ids], axis=-1)
    b_main = jnp.concatenate([seg(b_in, i) for i in main_ids], axis=-1)
    w_small = jnp.concatenate([seg(w_in, i) for i in small_ids], axis=-1)
    b_small = jnp.concatenate([seg(b_in, i) for i in small_ids], axis=-1)
    pad = N_SMALL - w_small.shape[-1]
    w_small = jnp.pad(w_small, ((0, 0), (0, 0), (0, pad)))
    b_small = jnp.pad(b_small, ((0, 0), (0, pad)))
    return w_main, b_main, w_small, b_small


def kernel(x_prompt, x_sample, c, state_gla, state_mlstm_C, state_mlstm_n, state_mlstm_m, c_ctx,
           mod_w, mod_b, norm1_g, norm2_g, final_g, w_in, b_in, hy_conv, hy_w1, hy_b1, hy_w2, hy_b2,
           hy_w3, hy_freq, hy_bias, gla_a2_w, gla_a2_b, gla_norm_g, ml_conv, ml_gate_b, ml_norm_g,
           w_branch, w_out, peer_wq, peer_keys, peer_u, peer_v):
    w_main, b_main, w_small, b_small = _reorder_in_proj(w_in, b_in)
    w_main = w_main.astype(BF16)
    w_small_hi = w_small.astype(BF16)
    w_small_lo = (w_small - w_small_hi.astype(F32)).astype(BF16)
    w_branch_b = w_branch.astype(BF16)
    w_out_b = w_out.astype(BF16)
    wq_hi = peer_wq.astype(BF16)
    wq_lo = (peer_wq - wq_hi.astype(F32)).astype(BF16)
    u_b = peer_u.astype(BF16)
    vt_b = jnp.swapaxes(peer_v, 1, 2).astype(BF16)
    zero_b = jnp.zeros((1, D_MODEL), F32)
    a2f = jnp.pad(gla_a2_w[:, 0], ((0, 0), (0, N_SMALL - GLA_RANK), (0, 0)))
    a2b = jnp.pad(gla_a2_w[:, 1], ((0, 0), (GLA_RANK, N_SMALL - 2 * GLA_RANK), (0, 0)))
    a2bias = gla_a2_b.reshape(DEPTH, 2, 1, GLA_HEADS * GLA_DK)
    n0_all = state_mlstm_n.reshape(DEC_BATCH, DEPTH, 2, ML_HEADS, 1, ML_DH)

    tabs = {}
    for length in (SEQ, DEC_SEQ):
        fre, fim, g_mat, sgn = _dft_tables(length)
        fre_h, fre_l = _split2(fre)
        fim_h, fim_l = _split2(fim)
        tabs[length] = dict(spec=(fre_h, fre_l, fim_h, fim_l, sgn),
                            f=jnp.concatenate([fre_h, fim_h], axis=0), g=g_mat.astype(BF16))

    cvec = jnp.zeros((MOD_ROWS, D_MODEL), F32).at[:DEC_BATCH].set(c).at[CTX_ROW].set(c_ctx)
    mod_all = mod_table(cvec, mod_w, mod_b)

    x = jnp.concatenate([x_prompt.reshape(N_PROMPT, D_MODEL), x_sample.reshape(N_SAMPLE, D_MODEL)], axis=0)
    groups = ((SEQ, BATCH, 0, SEQ), (DEC_SEQ, DEC_BATCH, N_PROMPT // DEC_SEQ, GRID_W))
    new_gla, new_c, new_n, new_m = [], [], [], []
    for l in range(DEPTH):
        mod3 = mod_all[l].reshape(MOD_ROWS, 1, 6 * D_MODEL)
        h_hi, h_lo = normmod(x, norm1_g[l], mod3, 0, 1)
        zmain = mm_bias(h_hi, w_main[l], b_main[l].reshape(1, N_MAIN))
        zsmall = mm3_bias(h_hi, h_lo, w_small_hi[l], w_small_lo[l], b_small[l].reshape(1, N_SMALL))
        mi = zsmall[:, 32:40].reshape(N_ROWS, 2, ML_HEADS)
        mf = zsmall[:, 40:48].reshape(N_ROWS, 2, ML_HEADS)
        gcol = jnp.concatenate([mi, mf, jnp.zeros((N_ROWS, 4, ML_HEADS), F32)], axis=1)
        gates_col = jnp.transpose(gcol, (2, 0, 1))
        gates_row = jnp.transpose(gcol.reshape(N_ROWS // CHUNK, CHUNK, 8, ML_HEADS), (3, 0, 2, 1))
        gate_b = ml_gate_b[l].reshape(-1)
        y_hy, y_gla, y_ml = [], [], []
        for gi, (length, n_seq, blk0, seg) in enumerate(groups):
            t = tabs[length]
            sre, sim = hyena_spectrum(length, hy_w1[l], hy_b1[l], hy_w2[l], hy_b2[l], hy_w3[l],
                                      hy_freq[l], t["spec"])
            y_hy.append(hyena(zmain, hy_conv[l], sre, sim, hy_bias[l], t["f"], t["g"],
                              length, n_seq, blk0, seg))
            gla_args = (zmain, zsmall, a2f[l], a2b[l], a2bias[l], gla_norm_g[l].reshape(1, GLA_DV),
                        length, n_seq, blk0)
            ml_args = (zmain, gates_col, gates_row, gate_b, ml_conv[l], ml_norm_g[l].reshape(1, ML_DH),
                       length, n_seq, blk0, seg)
            if gi == 0:
                yg, s_fin = gla(*gla_args, None, l)
                ym, c_fin, n_fin, m_fin = mlstm(*ml_args, None, l)
                new_gla.append(s_fin)
                new_c.append(c_fin)
                new_n.append(n_fin[:, :, :, 0, :])
                new_m.append(m_fin[:, :, :, 0, 0])
            else:
                yg = gla(*gla_args, state_gla, l)
                ym = mlstm(*ml_args, (state_mlstm_C, n0_all, state_mlstm_m[:, l].reshape(-1)), l)
            y_gla.append(yg)
            y_ml.append(ym)
        merged = merge_branches(jnp.concatenate(y_hy, 0), jnp.concatenate(y_gla, 0),
                                jnp.concatenate(y_ml, 0), w_branch_b[l], zmain)
        x = mm_residual(merged, w_out_b[l], x, mod3, 2)
        h2, h2_lo, h2t = normmod(x, norm2_g[l], mod3, 3, 4, transposed=True)
        q = mm3_bias(h2, h2_lo, wq_hi[l], wq_lo[l], zero_b)
        s1, s2, e1, e2, th = peer_topk(q, peer_keys[l])
        x = peer_experts(h2t, u_b[l], vt_b[l], s1, s2, e1, e2, th, x, mod3, 5)

    y = final_norm(x, final_g)
    y_prompt = y[:N_PROMPT].reshape(BATCH, SEQ, D_MODEL)
    y_sample = y[N_PROMPT:].reshape(DEC_BATCH, DEC_SEQ, D_MODEL)
    return (y_prompt, y_sample, jnp.stack(new_gla, axis=1), jnp.stack(new_c, axis=1),
            jnp.stack(new_n, axis=1), jnp.stack(new_m, axis=1))
```

```python
import functools
import math

import jax
import jax.numpy as jnp
import numpy as np
from jax import lax
from jax.experimental import pallas as pl
from jax.experimental.pallas import tpu as pltpu

F32 = jnp.float32
BF16 = jnp.bfloat16

D_MODEL = 2048
BATCH = 16
SEQ = 256
DEPTH = 4
DEC_BATCH = 4
DEC_SEQ = 1024
GRID_W = 64
EPS = 1e-6
CHUNK = 64
HY_WIDTH = 1024
HY_EMB = 33
HY_BANDS = (HY_EMB - 1) // 2
HY_FFN = 64
HY_DECAY_TARGET = 1e-2
HY_FAST_PCT = 0.3
HY_SLOW_PCT = 1.5
GLA_HEADS = 4
GLA_DK = 128
GLA_DV = 256
GLA_RANK = 16
GLA_NORMALIZER = 16.0
ML_HEADS = 4
ML_DH = 256
PEER_HEADS = 8
PEER_NKEYS = 128
PEER_EXPERTS = PEER_NKEYS * PEER_NKEYS
PEER_TOPK = 16

N_PROMPT = BATCH * SEQ
N_SAMPLE = DEC_BATCH * DEC_SEQ
N_ROWS = N_PROMPT + N_SAMPLE
CTX_ROW = DEC_BATCH
MOD_ROWS = 8

C_HX1, C_HX2, C_HV = 0, 1024, 2048
C_GQ, C_GK, C_GV, C_GR = 3072, 3584, 4096, 5120
C_MQ, C_MK, C_MV, C_MO = 6144, 7168, 8192, 9216
C_GA, C_GB, C_GC = 10240, 12288, 14336
N_MAIN = 16384
N_SMALL = 128

VMEM_LIMIT = 56 * 1024 * 1024


def _cp(sem):
    return pltpu.CompilerParams(dimension_semantics=sem, vmem_limit_bytes=VMEM_LIMIT)


def _dot(a, b):
    return jnp.dot(a, b, preferred_element_type=F32)


def _dot_nt(a, b):
    return lax.dot_general(a, b, (((1,), (1,)), ((), ())), preferred_element_type=F32)


def _dot_tn(a, b):
    return lax.dot_general(a, b, (((0,), (0,)), ((), ())), preferred_element_type=F32)


def _split2(a):
    hi = a.astype(BF16)
    lo = (a - hi.astype(F32)).astype(BF16)
    return hi, lo


def _split3(a):
    a1 = a.astype(BF16)
    r1 = a - a1.astype(F32)
    a2 = r1.astype(BF16)
    a3 = (r1 - a2.astype(F32)).astype(BF16)
    return a1, a2, a3


def _dot3(a, b, dot=_dot):
    ah, al = _split2(a)
    bh, bl = _split2(b)
    return dot(ah, bh) + (dot(ah, bl) + dot(al, bh))


def _log_sigmoid(x):
    return jnp.minimum(x, 0.0) - jnp.log(1.0 + jnp.exp(-jnp.abs(x)))


def _mod_row(i, tm):
    n_p = N_PROMPT // tm
    return jnp.where(i < n_p, CTX_ROW, (i - n_p) // (DEC_SEQ // tm))


def _mod_kernel(c_ref, w_ref, b_ref, o_ref):
    a = c_ref[...]
    a = a * jax.nn.sigmoid(a)
    o_ref[0] = _dot3(a, w_ref[0]) + b_ref[0]


def mod_table(cvec, mod_w, mod_b):
    tn = 1024
    n = mod_w.shape[-1]
    return pl.pallas_call(
        _mod_kernel,
        grid=(DEPTH, n // tn),
        in_specs=[pl.BlockSpec((MOD_ROWS, D_MODEL), lambda l, j: (0, 0)),
                  pl.BlockSpec((1, D_MODEL, tn), lambda l, j: (l, 0, j)),
                  pl.BlockSpec((1, 1, tn), lambda l, j: (l, 0, j))],
        out_specs=pl.BlockSpec((1, MOD_ROWS, tn), lambda l, j: (l, 0, j)),
        out_shape=jax.ShapeDtypeStruct((DEPTH, MOD_ROWS, n), F32),
        compiler_params=_cp(("parallel", "parallel")),
        name="mod_table",
    )(cvec, mod_w, mod_b.reshape(DEPTH, 1, n))


def _normmod_kernel(x_ref, g_ref, sc_ref, sh_ref, hi_ref, lo_ref, *t_ref):
    x = x_ref[...]
    y = x * lax.rsqrt(jnp.mean(x * x, axis=-1, keepdims=True) + EPS) * g_ref[...]
    h = y * (1.0 + sc_ref[...]) + sh_ref[...]
    hi = h.astype(BF16)
    hi_ref[...] = hi
    lo_ref[...] = (h - hi.astype(F32)).astype(BF16)
    if t_ref:
        t_ref[0][...] = h.T.astype(BF16)


def normmod(x, g, mod3, sh_chunk, sc_chunk, transposed=False):
    tm = 256
    out_shape = [jax.ShapeDtypeStruct((N_ROWS, D_MODEL), BF16)] * 2
    out_specs = [pl.BlockSpec((tm, D_MODEL), lambda i: (i, 0))] * 2
    if transposed:
        out_shape = out_shape + [jax.ShapeDtypeStruct((D_MODEL, N_ROWS), BF16)]
        out_specs = out_specs + [pl.BlockSpec((D_MODEL, tm), lambda i: (0, i))]
    return pl.pallas_call(
        _normmod_kernel,
        grid=(N_ROWS // tm,),
        in_specs=[pl.BlockSpec((tm, D_MODEL), lambda i: (i, 0)),
                  pl.BlockSpec((1, D_MODEL), lambda i: (0, 0)),
                  pl.BlockSpec((None, 1, D_MODEL), lambda i: (_mod_row(i, tm), 0, sc_chunk)),
                  pl.BlockSpec((None, 1, D_MODEL), lambda i: (_mod_row(i, tm), 0, sh_chunk))],
        out_specs=out_specs,
        out_shape=out_shape,
        compiler_params=_cp(("parallel",)),
        name="normmod",
    )(x, g.reshape(1, D_MODEL), mod3, mod3)


def _final_norm_kernel(x_ref, g_ref, o_ref):
    x = x_ref[...]
    o_ref[...] = x * lax.rsqrt(jnp.mean(x * x, axis=-1, keepdims=True) + EPS) * g_ref[...]


def final_norm(x, g):
    tm = 256
    return pl.pallas_call(
        _final_norm_kernel,
        grid=(N_ROWS // tm,),
        in_specs=[pl.BlockSpec((tm, D_MODEL), lambda i: (i, 0)),
                  pl.BlockSpec((1, D_MODEL), lambda i: (0, 0))],
        out_specs=pl.BlockSpec((tm, D_MODEL), lambda i: (i, 0)),
        out_shape=jax.ShapeDtypeStruct((N_ROWS, D_MODEL), F32),
        compiler_params=_cp(("parallel",)),
        name="final_norm",
    )(x, g.reshape(1, D_MODEL))


def _mm_bias_kernel(x_ref, w_ref, b_ref, o_ref):
    o_ref[...] = (_dot(x_ref[...], w_ref[...]) + b_ref[...]).astype(o_ref.dtype)


def mm_bias(x, w, b, out_dtype=F32, tm=512, tn=1024):
    m, k = x.shape
    n = w.shape[1]
    return pl.pallas_call(
        _mm_bias_kernel,
        grid=(n // tn, m // tm),
        in_specs=[pl.BlockSpec((tm, k), lambda j, i: (i, 0)),
                  pl.BlockSpec((k, tn), lambda j, i: (0, j)),
                  pl.BlockSpec((1, tn), lambda j, i: (0, j))],
        out_specs=pl.BlockSpec((tm, tn), lambda j, i: (i, j)),
        out_shape=jax.ShapeDtypeStruct((m, n), out_dtype),
        compiler_params=_cp(("parallel", "parallel")),
        name="mm_bias",
    )(x, w, b)


def _mm3_bias_kernel(xh_ref, xl_ref, wh_ref, wl_ref, b_ref, o_ref):
    xh = xh_ref[...]
    acc = _dot(xh, wh_ref[...]) + (_dot(xh, wl_ref[...]) + _dot(xl_ref[...], wh_ref[...]))
    o_ref[...] = acc + b_ref[...]


def mm3_bias(xh, xl, wh, wl, b, tm=512, tn=1024):
    m, k = xh.shape
    n = wh.shape[1]
    tn = min(tn, n)
    return pl.pallas_call(
        _mm3_bias_kernel,
        grid=(n // tn, m // tm),
        in_specs=[pl.BlockSpec((tm, k), lambda j, i: (i, 0)),
                  pl.BlockSpec((tm, k), lambda j, i: (i, 0)),
                  pl.BlockSpec((k, tn), lambda j, i: (0, j)),
                  pl.BlockSpec((k, tn), lambda j, i: (0, j)),
                  pl.BlockSpec((1, tn), lambda j, i: (0, j))],
        out_specs=pl.BlockSpec((tm, tn), lambda j, i: (i, j)),
        out_shape=jax.ShapeDtypeStruct((m, n), F32),
        compiler_params=_cp(("parallel", "parallel")),
        name="mm3_bias",
    )(xh, xl, wh, wl, b)


def _merge_kernel(yh_ref, yg_ref, ym_ref, w_ref, ga_ref, gb_ref, gc_ref, o_ref):
    acc = jax.nn.sigmoid(ga_ref[...]) * _dot(yh_ref[...], w_ref[0])
    acc += jax.nn.sigmoid(gb_ref[...]) * _dot(yg_ref[...], w_ref[1])
    acc += jax.nn.sigmoid(gc_ref[...]) * _dot(ym_ref[...], w_ref[2])
    o_ref[...] = acc.astype(o_ref.dtype)


def merge_branches(y_hy, y_gla, y_ml, w_branch, zmain, tm=512, tn=1024):
    kb = HY_WIDTH
    y_spec = pl.BlockSpec((tm, kb), lambda j, i: (i, 0))

    def gate_spec(col):
        return pl.BlockSpec((tm, tn), lambda j, i, c=col // tn: (i, c + j))

    return pl.pallas_call(
        _merge_kernel,
        grid=(D_MODEL // tn, N_ROWS // tm),
        in_specs=[y_spec, y_spec, y_spec,
                  pl.BlockSpec((3, kb, tn), lambda j, i: (0, 0, j)),
                  gate_spec(C_GA), gate_spec(C_GB), gate_spec(C_GC)],
        out_specs=pl.BlockSpec((tm, tn), lambda j, i: (i, j)),
        out_shape=jax.ShapeDtypeStruct((N_ROWS, D_MODEL), BF16),
        compiler_params=_cp(("parallel", "parallel")),
        name="merge_branches",
    )(y_hy, y_gla, y_ml, w_branch, zmain, zmain, zmain)


def _mm_resid_kernel(m_ref, w_ref, x_ref, gt_ref, o_ref):
    o_ref[...] = x_ref[...] + gt_ref[...] * _dot(m_ref[...], w_ref[...])


def mm_residual(merged, w, x, mod3, gt_chunk, tm=512, tn=1024):
    k = merged.shape[1]
    return pl.pallas_call(
        _mm_resid_kernel,
        grid=(D_MODEL // tn, N_ROWS // tm),
        in_specs=[pl.BlockSpec((tm, k), lambda j, i: (i, 0)),
                  pl.BlockSpec((k, tn), lambda j, i: (0, j)),
                  pl.BlockSpec((tm, tn), lambda j, i: (i, j)),
                  pl.BlockSpec((None, 1, tn),
                               lambda j, i: (_mod_row(i, tm), 0, gt_chunk * (D_MODEL // tn) + j))],
        out_specs=pl.BlockSpec((tm, tn), lambda j, i: (i, j)),
        out_shape=jax.ShapeDtypeStruct((N_ROWS, D_MODEL), F32),
        compiler_params=_cp(("parallel", "parallel")),
        name="mm_residual",
    )(merged, w, x, mod3)


def _dft_tables(length):
    k = jnp.arange(length, dtype=jnp.int32)
    m = (k[:, None] * k[None, :]) % (2 * length)
    ang = m.astype(F32) * (math.pi / length)
    cos = jnp.cos(ang)
    sin = jnp.sin(ang)
    sgn = jnp.where(k % 2 == 0, 1.0, -1.0).astype(F32)
    fre = cos
    fim = jnp.where(k[:, None] == 0, sgn[None, :], -sin)
    wk = jnp.where(k == 0, 1.0, 2.0).astype(F32) / (2.0 * length)
    g_re = cos.T * wk[None, :]
    g_im = jnp.where(k[None, :] == 0, sgn[:, None] / (2.0 * length), -sin.T / length)
    return fre, fim, jnp.concatenate([g_re, g_im], axis=1), sgn


def _hyfilt_kernel(z_ref, w1_ref, b1_ref, w2_ref, b2_ref, fq_ref, w3f_ref, w3b_ref, tn_ref, dl_ref,
                   sgn_ref, freh_ref, frel_ref, fimh_ref, fiml_ref, sre_ref, sim_ref):
    hid = jnp.sin(fq_ref[0:1] * (_dot3(z_ref[...], w1_ref[...]) + b1_ref[...]))
    hid = jnp.sin(fq_ref[1:2] * (_dot3(hid, w2_ref[...]) + b2_ref[...]))
    decay = jnp.exp(-tn_ref[...] * dl_ref[...])
    fwd = _dot3(hid, w3f_ref[...]) * decay
    bwd = _dot3(hid, w3b_ref[...]) * decay
    row = lax.broadcasted_iota(jnp.int32, fwd.shape, 0)
    bwd = jnp.where(row == 0, 0.0, bwd)
    a = fwd + bwd
    d = fwd - bwd
    ah, al = _split2(a)
    dh, dl2 = _split2(d)
    re = _dot(freh_ref[...], ah) + (_dot(freh_ref[...], al) + _dot(frel_ref[...], ah))
    im = _dot(fimh_ref[...], dh) + (_dot(fimh_ref[...], dl2) + _dot(fiml_ref[...], dh))
    nyq = jnp.sum(sgn_ref[...] * a, axis=0, keepdims=True)
    sre_ref[0] = re
    sim_ref[0] = jnp.where(row == 0, nyq, im)


def hyena_spectrum(length, w1, b1, w2, b2, w3, freq, tabs):
    fre_h, fre_l, fim_h, fim_l, sgn = tabs
    t = jnp.arange(length, dtype=F32)
    t_norm = t / (length - 1)
    bands = jnp.linspace(1e-4, HY_BANDS - 1, HY_BANDS, dtype=F32)
    ang = (2.0 * math.pi / length) * t[:, None] * bands[None, :]
    z = jnp.concatenate([t_norm[:, None], jnp.cos(ang), -jnp.sin(ang),
                         jnp.zeros((length, HY_FFN - HY_EMB), F32)], axis=-1)
    w1p = jnp.pad(w1, ((0, HY_FFN - HY_EMB), (0, 0)))
    max_decay = math.log(HY_DECAY_TARGET) / HY_FAST_PCT
    min_decay = math.log(HY_DECAY_TARGET) / HY_SLOW_PCT
    deltas = jnp.abs(jnp.linspace(min_decay, max_decay, HY_WIDTH, dtype=F32)).reshape(1, HY_WIDTH)
    ct = 256
    nct = HY_WIDTH // ct
    full = lambda shape: pl.BlockSpec(shape, lambda o, j: (0,) * len(shape))
    out_spec = pl.BlockSpec((1, length, ct), lambda o, j: (o, 0, j))
    return pl.pallas_call(
        _hyfilt_kernel,
        grid=(2, nct),
        in_specs=[full((length, HY_FFN)), full((HY_FFN, HY_FFN)), full((1, HY_FFN)),
                  full((HY_FFN, HY_FFN)), full((1, HY_FFN)), full((2, HY_FFN)),
                  pl.BlockSpec((HY_FFN, ct), lambda o, j: (0, o * 2 * nct + j)),
                  pl.BlockSpec((HY_FFN, ct), lambda o, j: (0, o * 2 * nct + nct + j)),
                  full((length, 1)),
                  pl.BlockSpec((1, ct), lambda o, j: (0, j)),
                  full((length, 1)),
                  full((length, length)), full((length, length)),
                  full((length, length)), full((length, length))],
        out_specs=[out_spec, out_spec],
        out_shape=[jax.ShapeDtypeStruct((2, length, HY_WIDTH), F32)] * 2,
        compiler_params=_cp(("parallel", "parallel")),
        name="hyena_spectrum",
    )(z, w1p, b1.reshape(1, HY_FFN), w2, b2.reshape(1, HY_FFN), freq, w3, w3,
      t_norm.reshape(length, 1), deltas, sgn.reshape(length, 1), fre_h, fre_l, fim_h, fim_l)


def _short_conv(x, w, pos, seg, length):
    prev = jnp.where(pos == 0, 0.0, pltpu.roll(x, 1, 0))
    nxt = jnp.where(pos == seg - 1, 0.0, pltpu.roll(x, length - 1, 0))
    return prev * w[0:1] + x * w[1:2] + nxt * w[2:3]


def _hyena_kernel(x1_ref, x2_ref, v_ref, cw_ref, sre_ref, sim_ref, hb_ref, f_ref, g_ref, o_ref,
                  *, length, seg):
    row = lax.broadcasted_iota(jnp.int32, (length, 1), 0)
    pos = row % seg
    row0 = row == 0

    def long_conv(u, o):
        spec = _dot(f_ref[...], u.astype(BF16))
        ur, ui = spec[:length], spec[length:]
        hre, him = sre_ref[o], sim_ref[o]
        uihi = ui * him
        yr = ur * hre - jnp.where(row0, 0.0, uihi)
        yi = jnp.where(row0, uihi, ur * him + ui * hre)
        y = _dot(g_ref[...], jnp.concatenate([yr, yi], axis=0).astype(BF16))
        return y + u * hb_ref[o:o + 1]

    x1 = _short_conv(x1_ref[...], cw_ref[0], pos, seg, length)
    x2 = _short_conv(x2_ref[...], cw_ref[1], pos, seg, length)
    v = _short_conv(v_ref[...], cw_ref[2], pos, seg, length)
    z = x1 * long_conv(v, 0)
    o_ref[...] = (x2 * long_conv(z, 1)).astype(o_ref.dtype)


def hyena(zmain, conv_w, spec_re, spec_im, bias, f_mat, g_mat, length, n_seq, row_blk0, seg):
    ct = 256
    nct = HY_WIDTH // ct

    def zspec(col):
        return pl.BlockSpec((length, ct), lambda b, j, c=col // ct: (row_blk0 + b, c + j))

    return pl.pallas_call(
        functools.partial(_hyena_kernel, length=length, seg=seg),
        grid=(n_seq, nct),
        in_specs=[zspec(C_HX1), zspec(C_HX2), zspec(C_HV),
                  pl.BlockSpec((3, 3, ct), lambda b, j: (0, 0, j)),
                  pl.BlockSpec((2, length, ct), lambda b, j: (0, 0, j)),
                  pl.BlockSpec((2, length, ct), lambda b, j: (0, 0, j)),
                  pl.BlockSpec((2, ct), lambda b, j: (0, j)),
                  pl.BlockSpec((2 * length, length), lambda b, j: (0, 0)),
                  pl.BlockSpec((length, 2 * length), lambda b, j: (0, 0))],
        out_specs=pl.BlockSpec((length, ct), lambda b, j: (b, j)),
        out_shape=jax.ShapeDtypeStruct((n_seq * length, HY_WIDTH), BF16),
        compiler_params=_cp(("parallel", "parallel")),
        name="hyena",
    )(zmain, zmain, zmain, conv_w, spec_re, spec_im, bias, f_mat, g_mat)


def _tri_masks():
    t = lax.broadcasted_iota(jnp.int32, (CHUNK, CHUNK), 0)
    s = lax.broadcasted_iota(jnp.int32, (CHUNK, CHUNK), 1)
    return s <= t, s >= t


def _gla_chunk(q, k, v, g, state, mask, rev):
    tm = jnp.where(mask, 1.0, 0.0).astype(BF16)
    g1, g2, g3 = _split3(g)
    bc = _dot(tm, g1) + (_dot(tm, g2) + _dot(tm, g3))
    b_end = bc[0:1] if rev else bc[CHUNK - 1:CHUNK]
    ref = bc[CHUNK // 2:CHUNK // 2 + 1]
    inter = _dot((q * jnp.exp(bc)).astype(BF16), state.astype(BF16))
    qh = (q * jnp.exp(bc - ref)).astype(BF16)
    kh = (k * jnp.exp(ref - bc)).astype(BF16)
    att = jnp.where(mask, _dot_nt(qh, kh), 0.0)
    vb = v.astype(BF16)
    o = inter + _dot(att.astype(BF16), vb)
    ones = jnp.ones((CHUNK, GLA_DK), BF16)
    tot = _dot_tn(g1, ones) + (_dot_tn(g2, ones) + _dot_tn(g3, ones))
    e = jnp.exp(tot)
    kd = (k * jnp.exp(b_end - bc)).astype(BF16)
    new_state = jnp.concatenate([e, e], axis=1) * state + _dot_tn(kd, vb)
    return o, new_state


def _gla_kernel(*refs, length, has_init):
    if has_init:
        (q_ref, k_ref, v_ref, gr_ref, zs_ref, wf_ref, wb_ref, ab_ref, ng_ref, s0_ref,
         y_ref, o_ref, lg_ref, st_ref) = refs
    else:
        (q_ref, k_ref, v_ref, gr_ref, zs_ref, wf_ref, wb_ref, ab_ref, ng_ref,
         y_ref, sout_ref, o_ref, lg_ref, st_ref) = refs
    zs = zs_ref[...]
    lg_ref[0] = _log_sigmoid(_dot3(zs, wf_ref[...]) + ab_ref[0]) * (1.0 / GLA_NORMALIZER)
    lg_ref[1] = _log_sigmoid(_dot3(zs, wb_ref[...]) + ab_ref[1]) * (1.0 / GLA_NORMALIZER)
    if has_init:
        st_ref[...] = s0_ref[...]
    else:
        st_ref[...] = jnp.zeros_like(st_ref)
    o_ref[...] = jnp.zeros_like(o_ref)
    n = length // CHUNK
    mask_f, mask_b = _tri_masks()
    scale = GLA_DK ** -0.5

    def body(i, carry):
        for d, mask in enumerate((mask_f, mask_b)):
            c = i if d == 0 else n - 1 - i
            rows = pl.ds(pl.multiple_of(c * CHUNK, CHUNK), CHUNK)
            for h in range(GLA_HEADS):
                kc = slice(h * GLA_DK, (h + 1) * GLA_DK)
                vc = slice(h * GLA_DV, (h + 1) * GLA_DV)
                o, s_new = _gla_chunk(q_ref[rows, kc] * scale, k_ref[rows, kc], v_ref[rows, vc],
                                      lg_ref[d, rows, kc], st_ref[d, h], mask, d == 1)
                o_ref[rows, vc] += o
                st_ref[d, h] = s_new
        return carry

    lax.fori_loop(0, n, body, 0)
    for h in range(GLA_HEADS):
        vc = slice(h * GLA_DV, (h + 1) * GLA_DV)
        o = o_ref[:, vc]
        o = o * lax.rsqrt(jnp.mean(o * o, axis=-1, keepdims=True) + EPS) * ng_ref[...]
        gr = gr_ref[:, vc]
        y_ref[:, vc] = (o * (gr * jax.nn.sigmoid(gr))).astype(y_ref.dtype)
    if not has_init:
        sout_ref[...] = st_ref[...]


def gla(zmain, zsmall, wf, wb, ab, norm_g, length, n_seq, row_blk0, state0, layer):
    has_init = state0 is not None
    qk_w = GLA_HEADS * GLA_DK
    v_w = GLA_HEADS * GLA_DV
    once = pl.Buffered(1)

    def zspec(col, width):
        return pl.BlockSpec((length, width), lambda b, c=col // width: (row_blk0 + b, c), pipeline_mode=once)

    in_specs = [zspec(C_GQ, qk_w), zspec(C_GK, qk_w), zspec(C_GV, v_w), zspec(C_GR, v_w),
                pl.BlockSpec((length, N_SMALL), lambda b: (row_blk0 + b, 0)),
                pl.BlockSpec((N_SMALL, qk_w), lambda b: (0, 0)),
                pl.BlockSpec((N_SMALL, qk_w), lambda b: (0, 0)),
                pl.BlockSpec((2, 1, qk_w), lambda b: (0, 0, 0)),
                pl.BlockSpec((1, GLA_DV), lambda b: (0, 0))]
    args = [zmain, zmain, zmain, zmain, zsmall, wf, wb, ab, norm_g]
    y_shape = jax.ShapeDtypeStruct((n_seq * length, v_w), BF16)
    y_spec = pl.BlockSpec((length, v_w), lambda b: (b, 0))
    if has_init:
        in_specs.append(pl.BlockSpec((None, None, 2, GLA_HEADS, GLA_DK, GLA_DV),
                                     lambda b: (b, layer, 0, 0, 0, 0)))
        args.append(state0)
        out_shape, out_specs = y_shape, y_spec
    else:
        out_shape = [y_shape, jax.ShapeDtypeStruct((n_seq, 2, GLA_HEADS, GLA_DK, GLA_DV), F32)]
        out_specs = [y_spec, pl.BlockSpec((None, 2, GLA_HEADS, GLA_DK, GLA_DV), lambda b: (b, 0, 0, 0, 0))]
    return pl.pallas_call(
        functools.partial(_gla_kernel, length=length, has_init=has_init),
        grid=(n_seq,),
        in_specs=in_specs,
        out_specs=out_specs,
        out_shape=out_shape,
        scratch_shapes=[pltpu.VMEM((length, v_w), F32), pltpu.VMEM((2, length, qk_w), F32),
                        pltpu.VMEM((2, GLA_HEADS, GLA_DK, GLA_DV), F32)],
        compiler_params=_cp(("parallel",)),
        name="gla",
    )(*args)


def _mlstm_chunk(q, ks, v, lf_c, li_c, lf_r, li_r, cm, nv, m_prev, mask, mask_t, rev):
    t_n = CHUNK
    tm = jnp.where(mask, 1.0, 0.0).astype(BF16)
    tmt = jnp.where(mask_t, 1.0, 0.0).astype(BF16)
    c1, c2, c3 = _split3(jnp.broadcast_to(lf_c, (t_n, t_n)))
    b_colb = _dot(tm, c1) + (_dot(tm, c2) + _dot(tm, c3))
    r1, r2, r3 = _split3(jnp.broadcast_to(lf_r, (t_n, t_n)))
    b_rowb = _dot(r1, tmt) + (_dot(r2, tmt) + _dot(r3, tmt))
    b_col = b_colb[:, 0:1]
    b_row = b_rowb[0:1, :]
    b_end = b_colb[0:1, 0:1] if rev else b_colb[t_n - 1:t_n, 0:1]
    dmat = jnp.where(mask, b_colb - b_rowb + li_r, -jnp.inf)
    m_t = jnp.maximum(b_col + m_prev, jnp.max(dmat, axis=-1, keepdims=True))
    w_inter = jnp.exp(b_col + m_prev - m_t)
    qb = q.astype(BF16)
    vb = v.astype(BF16)
    sc = _dot_nt(qb, ks.astype(BF16)) * jnp.exp(dmat - m_t)
    num = w_inter * _dot(qb, cm.astype(BF16)) + _dot(sc.astype(BF16), vb)
    den = w_inter * jnp.sum(q * nv, axis=-1, keepdims=True) + jnp.sum(sc, axis=-1, keepdims=True)
    h = num / jnp.maximum(jnp.abs(den), jnp.exp(-m_t))
    g_r = b_end - b_row + li_r
    g_c = b_end - b_col + li_c
    m_new = jnp.maximum(b_end + m_prev, jnp.max(g_r, axis=-1, keepdims=True))
    w_c = jnp.exp(b_end + m_prev - m_new)
    kw = ks * jnp.exp(g_c - m_new)
    cm_new = w_c * cm + _dot_tn(kw.astype(BF16), vb)
    nv_new = w_c * nv + jnp.sum(kw, axis=0, keepdims=True)
    return h, cm_new, nv_new, m_new


def _mlstm_kernel(*refs, length, seg, has_init):
    if has_init:
        (gb_ref, m0_ref, q_ref, k_ref, v_ref, mo_ref, gc_ref, gr_ref, cw_ref, ng_ref, c0_ref, n0_ref,
         y_ref, qc_ref, kc_ref, h_ref, c_ref, n_ref, m_ref) = refs
    else:
        (gb_ref, q_ref, k_ref, v_ref, mo_ref, gc_ref, gr_ref, cw_ref, ng_ref,
         y_ref, cout_ref, nout_ref, mout_ref, qc_ref, kc_ref, h_ref, c_ref, n_ref, m_ref) = refs
    b_idx = pl.program_id(0)
    dh = ML_DH
    row = lax.broadcasted_iota(jnp.int32, (length, 1), 0)
    pos = row % seg
    qc_ref[...] = _short_conv(q_ref[...], cw_ref[0], pos, seg, length)
    kc_ref[...] = _short_conv(k_ref[...], cw_ref[1], pos, seg, length) * (dh ** -0.5)
    h_ref[...] = jnp.zeros_like(h_ref)
    if has_init:
        c_ref[...] = c0_ref[...]
        n_ref[...] = n0_ref[...]
        for d in range(2):
            for h in range(ML_HEADS):
                m_ref[d, h] = jnp.full((1, 128), m0_ref[b_idx * 2 * ML_HEADS + d * ML_HEADS + h], F32)
    else:
        c_ref[...] = jnp.zeros_like(c_ref)
        n_ref[...] = jnp.zeros_like(n_ref)
        m_ref[...] = jnp.zeros_like(m_ref)
    n = length // CHUNK
    mask_f, mask_b = _tri_masks()

    def body(i, carry):
        for d, (mask, mask_t) in enumerate(((mask_f, mask_b), (mask_b, mask_f))):
            c = i if d == 0 else n - 1 - i
            rows = pl.ds(pl.multiple_of(c * CHUNK, CHUNK), CHUNK)
            for h in range(ML_HEADS):
                cols = slice(h * dh, (h + 1) * dh)
                bi = gb_ref[d * 2 * ML_HEADS + h]
                bf = gb_ref[d * 2 * ML_HEADS + ML_HEADS + h]
                gcol = gc_ref[h, rows, :]
                grow = gr_ref[h, c]
                li_c = gcol[:, d:d + 1] + bi
                lf_c = _log_sigmoid(gcol[:, 2 + d:3 + d] + bf)
                li_r = grow[d:d + 1, :] + bi
                lf_r = _log_sigmoid(grow[2 + d:3 + d, :] + bf)
                hc, cm, nv, m_new = _mlstm_chunk(qc_ref[rows, cols], kc_ref[rows, cols], v_ref[rows, cols],
                                                 lf_c, li_c, lf_r, li_r, c_ref[d, h], n_ref[d, h],
                                                 m_ref[d, h][:, 0:1], mask, mask_t, d == 1)
                h_ref[rows, cols] += hc
                c_ref[d, h] = cm
                n_ref[d, h] = nv
                m_ref[d, h] = jnp.broadcast_to(m_new, (1, 128))
        return carry

    lax.fori_loop(0, n, body, 0)
    for h in range(ML_HEADS):
        cols = slice(h * dh, (h + 1) * dh)
        o = h_ref[:, cols]
        o = o * lax.rsqrt(jnp.mean(o * o, axis=-1, keepdims=True) + EPS) * ng_ref[...]
        y_ref[:, cols] = (o * jax.nn.sigmoid(mo_ref[:, cols])).astype(y_ref.dtype)
    if not has_init:
        cout_ref[...] = c_ref[...]
        nout_ref[...] = n_ref[...]
        mout_ref[...] = m_ref[...]


def mlstm(zmain, gates_col, gates_row, gate_b, conv_w, norm_g, length, n_seq, row_blk0, seg, init, layer):
    has_init = init is not None
    dh = ML_DH
    nh = ML_HEADS
    width = nh * dh
    once = pl.Buffered(1)

    def zspec(col):
        return pl.BlockSpec((length, width), lambda b, c=col // width: (row_blk0 + b, c), pipeline_mode=once)

    smem = pl.BlockSpec(memory_space=pltpu.SMEM)
    in_specs = [smem]
    args = [gate_b]
    if has_init:
        c0, n0, m0 = init
        in_specs.append(smem)
        args.append(m0)
    in_specs += [zspec(C_MQ), zspec(C_MK), zspec(C_MV), zspec(C_MO),
                 pl.BlockSpec((nh, length, 8), lambda b: (0, row_blk0 + b, 0)),
                 pl.BlockSpec((nh, length // CHUNK, 8, CHUNK), lambda b: (0, row_blk0 + b, 0, 0)),
                 pl.BlockSpec((2, 3, width), lambda b: (0, 0, 0)),
                 pl.BlockSpec((1, dh), lambda b: (0, 0))]
    args += [zmain, zmain, zmain, zmain, gates_col, gates_row, conv_w, norm_g]
    y_shape = jax.ShapeDtypeStruct((n_seq * length, width), BF16)
    y_spec = pl.BlockSpec((length, width), lambda b: (b, 0))
    if has_init:
        in_specs += [pl.BlockSpec((None, None, 2, nh, dh, dh), lambda b: (b, layer, 0, 0, 0, 0)),
                     pl.BlockSpec((None, None, 2, nh, 1, dh), lambda b: (b, layer, 0, 0, 0, 0))]
        args += [c0, n0]
        out_shape, out_specs = y_shape, y_spec
    else:
        out_shape = [y_shape,
                     jax.ShapeDtypeStruct((n_seq, 2, nh, dh, dh), F32),
                     jax.ShapeDtypeStruct((n_seq, 2, nh, 1, dh), F32),
                     jax.ShapeDtypeStruct((n_seq, 2, nh, 1, 128), F32)]
        out_specs = [y_spec,
                     pl.BlockSpec((None, 2, nh, dh, dh), lambda b: (b, 0, 0, 0, 0)),
                     pl.BlockSpec((None, 2, nh, 1, dh), lambda b: (b, 0, 0, 0, 0)),
                     pl.BlockSpec((None, 2, nh, 1, 128), lambda b: (b, 0, 0, 0, 0))]
    return pl.pallas_call(
        functools.partial(_mlstm_kernel, length=length, seg=seg, has_init=has_init),
        grid=(n_seq,),
        in_specs=in_specs,
        out_specs=out_specs,
        out_shape=out_shape,
        scratch_shapes=[pltpu.VMEM((length, width), F32), pltpu.VMEM((length, width), F32),
                        pltpu.VMEM((length, width), F32),
                        pltpu.VMEM((2, nh, dh, dh), F32), pltpu.VMEM((2, nh, 1, dh), F32),
                        pltpu.VMEM((2, nh, 1, 128), F32)],
        compiler_params=_cp(("parallel",)),
        name="mlstm",
    )(*args)


def _top_values(x, count):
    vals = []
    cur = x
    for r in range(count):
        m = jnp.max(cur, axis=0, keepdims=True)
        vals.append(m)
        if r + 1 < count:
            cur = jnp.where(cur == m, -jnp.inf, cur)
    return vals


def _peer_topk_kernel(q_ref, k_ref, s1_ref, s2_ref, e1_ref, e2_ref, th_ref):
    dk = PEER_NKEYS
    s1 = _dot3(k_ref[0], q_ref[:, 0:dk], dot=_dot_nt)
    s2 = _dot3(k_ref[1], q_ref[:, dk:2 * dk], dot=_dot_nt)
    v1 = _top_values(s1, PEER_TOPK)
    v2 = _top_values(s2, PEER_TOPK)
    rows = [v1[a] + v2[b] for a in range(PEER_TOPK) for b in range(PEER_TOPK // (a + 1))]
    rows += [jnp.full_like(rows[0], -jnp.inf)] * (-len(rows) % 8)
    cand = jnp.concatenate(rows, axis=0)
    best = _top_values(cand, PEER_TOPK)
    zsum = jnp.exp(best[0] - best[0])
    for r in range(1, PEER_TOPK):
        zsum = zsum + jnp.exp(best[r] - best[0])
    s1_ref[...] = s1
    s2_ref[...] = s2
    e1_ref[...] = jnp.exp(s1 - v1[0]) / zsum
    e2_ref[...] = jnp.exp(s2 - v2[0])
    th_ref[...] = jnp.broadcast_to(best[PEER_TOPK - 1], th_ref.shape)


def peer_topk(q, keys, tt=256):
    n = q.shape[0]
    nk = PEER_NKEYS
    sds = jax.ShapeDtypeStruct((PEER_HEADS, nk, n), F32)
    spec = pl.BlockSpec((None, nk, tt), lambda i, h: (h, 0, i))
    return pl.pallas_call(
        _peer_topk_kernel,
        grid=(n // tt, PEER_HEADS),
        in_specs=[pl.BlockSpec((tt, 2 * nk), lambda i, h: (i, h)),
                  pl.BlockSpec((None, 2, nk, nk), lambda i, h: (h, 0, 0, 0))],
        out_specs=[spec, spec, spec, spec, pl.BlockSpec((None, 8, tt), lambda i, h: (h, 0, i))],
        out_shape=[sds, sds, sds, sds, jax.ShapeDtypeStruct((PEER_HEADS, 8, n), F32)],
        compiler_params=_cp(("parallel", "parallel")),
        name="peer_topk",
    )(q, keys)


def _gelu_tanh(x):
    return 0.5 * x * (1.0 + jnp.tanh(math.sqrt(2.0 / math.pi) * (x + 0.044715 * (x * x * x))))


def _peer_expert_kernel(ht_ref, u_ref, vt_ref, s1_ref, s2_ref, e1_ref, e2_ref, th_ref, x_ref, gt_ref,
                        o_ref, acc_ref, act0_ref, act1_ref, p0_ref, p1_ref, *, n_i1, tt):
    j = pl.program_id(1)
    nk = PEER_NKEYS

    @pl.when(j == 0)
    def _():
        acc_ref[...] = jnp.zeros_like(acc_ref)
        act1_ref[...] = jnp.zeros_like(act1_ref)
        p0_ref[...] = jnp.zeros_like(p0_ref)
        p1_ref[...] = jnp.zeros_like(p1_ref)

    def step(act_new, act_old, p_new, p_old):
        act_new[...] = _dot(u_ref[...], ht_ref[...])
        acc_ref[...] += _dot(vt_ref[...], p_old[...])
        live = j >= 1
        for ts in range(tt // 128):
            lanes = slice(ts * 128, (ts + 1) * 128)
            for r in range(n_i1):
                w = None
                for h in range(PEER_HEADS):
                    s1r = s1_ref[h, r:r + 1, lanes]
                    e1r = e1_ref[h, r:r + 1, lanes]
                    hit = (s1r + s2_ref[h, :, lanes]) >= th_ref[h, 0:1, lanes]
                    term = jnp.where(hit, e2_ref[h, :, lanes] * e1r, 0.0)
                    w = term if w is None else w + term
                rows = slice(r * nk, (r + 1) * nk)
                p = w * _gelu_tanh(act_old[rows, lanes])
                p_new[rows, lanes] = jnp.where(live, p, 0.0).astype(BF16)

    @pl.when(j % 2 == 0)
    def _():
        step(act0_ref, act1_ref, p1_ref, p0_ref)

    @pl.when(j % 2 == 1)
    def _():
        step(act1_ref, act0_ref, p0_ref, p1_ref)

    @pl.when(j == pl.num_programs(1) - 1)
    def _():
        o_ref[...] = x_ref[...] + gt_ref[...] * acc_ref[...].T


def peer_experts(h2t, u_tab, vt_tab, s1, s2, e1, e2, th, x, mod3, gt_chunk, tt=256, ec=1024):
    n = x.shape[0]
    nk = PEER_NKEYS
    n_i1 = ec // nk
    nch = PEER_EXPERTS // ec
    once = pl.Buffered(1)
    chunk_a = lambda j: jnp.minimum(j, nch - 1)
    chunk_b = lambda j: jnp.clip(j - 1, 0, nch - 1)
    chunk_c = lambda j: jnp.clip(j - 2, 0, nch - 1)
    sspec = pl.BlockSpec((PEER_HEADS, nk, tt), lambda i, j: (0, 0, i), pipeline_mode=once)
    rspec = pl.BlockSpec((PEER_HEADS, n_i1, tt), lambda i, j: (0, chunk_b(j), i))
    return pl.pallas_call(
        functools.partial(_peer_expert_kernel, n_i1=n_i1, tt=tt),
        grid=(n // tt, nch + 2),
        in_specs=[pl.BlockSpec((D_MODEL, tt), lambda i, j: (0, i)),
                  pl.BlockSpec((ec, D_MODEL), lambda i, j: (chunk_a(j), 0)),
                  pl.BlockSpec((D_MODEL, ec), lambda i, j: (0, chunk_c(j))),
                  rspec, sspec, rspec, sspec,
                  pl.BlockSpec((PEER_HEADS, 8, tt), lambda i, j: (0, 0, i), pipeline_mode=once),
                  pl.BlockSpec((tt, D_MODEL), lambda i, j: (i, 0), pipeline_mode=once),
                  pl.BlockSpec((None, 1, D_MODEL), lambda i, j: (_mod_row(i, tt), 0, gt_chunk))],
        out_specs=pl.BlockSpec((tt, D_MODEL), lambda i, j: (i, 0)),
        out_shape=jax.ShapeDtypeStruct((n, D_MODEL), F32),
        scratch_shapes=[pltpu.VMEM((D_MODEL, tt), F32),
                        pltpu.VMEM((ec, tt), F32), pltpu.VMEM((ec, tt), F32),
                        pltpu.VMEM((ec, tt), BF16), pltpu.VMEM((ec, tt), BF16)],
        compiler_params=_cp(("parallel", "arbitrary")),
        name="peer_experts",
    )(h2t, u_tab, vt_tab, s1, s2, e1, e2, th, x, mod3)


def _reorder_in_proj(w_in, b_in):
    splits = (1024, 1024, 1024, 512, 512, 1024, 1024, 16, 16, 1024, 1024, 1024, 1024, 8, 8, 2048, 2048, 2048)
    offs = np.concatenate([[0], np.cumsum(splits)])
    seg = lambda a, i: a[..., offs[i]:offs[i + 1]]
    main_ids = (0, 1, 2, 3, 4, 5, 6, 9, 10, 11, 12, 15, 16, 17)
    small_ids = (7, 8, 13, 14)
    w_main = jnp.concatenate([seg(w_in, i).astype(BF16) for i in main_ids], axis=-1)
    b_main = jnp.concatenate([seg(b_in, i) for i in main_ids], axis=-1)
    w_small = jnp.concatenate([seg(w_in, i) for i in small_ids], axis=-1)
    b_small = jnp.concatenate([seg(b_in, i) for i in small_ids], axis=-1)
    pad = N_SMALL - w_small.shape[-1]
    w_small = jnp.pad(w_small, ((0, 0), (0, 0), (0, pad)))
    b_small = jnp.pad(b_small, ((0, 0), (0, pad)))
    return w_main, b_main, w_small, b_small


def kernel(x_prompt, x_sample, c, state_gla, state_mlstm_C, state_mlstm_n, state_mlstm_m, c_ctx,
           mod_w, mod_b, norm1_g, norm2_g, final_g, w_in, b_in, hy_conv, hy_w1, hy_b1, hy_w2, hy_b2,
           hy_w3, hy_freq, hy_bias, gla_a2_w, gla_a2_b, gla_norm_g, ml_conv, ml_gate_b, ml_norm_g,
           w_branch, w_out, peer_wq, peer_keys, peer_u, peer_v):
    w_main, b_main, w_small, b_small = _reorder_in_proj(w_in, b_in)
    w_small_hi = w_small.astype(BF16)
    w_small_lo = (w_small - w_small_hi.astype(F32)).astype(BF16)
    w_branch_b = w_branch.astype(BF16)
    w_out_b = w_out.astype(BF16)
    wq_hi = peer_wq.astype(BF16)
    wq_lo = (peer_wq - wq_hi.astype(F32)).astype(BF16)
    u_b = peer_u.astype(BF16)
    vt_b = jnp.swapaxes(peer_v.astype(BF16), 1, 2)
    zero_b = jnp.zeros((1, D_MODEL), F32)
    a2f = jnp.pad(gla_a2_w[:, 0], ((0, 0), (0, N_SMALL - GLA_RANK), (0, 0)))
    a2b = jnp.pad(gla_a2_w[:, 1], ((0, 0), (GLA_RANK, N_SMALL - 2 * GLA_RANK), (0, 0)))
    a2bias = gla_a2_b.reshape(DEPTH, 2, 1, GLA_HEADS * GLA_DK)
    n0_all = state_mlstm_n.reshape(DEC_BATCH, DEPTH, 2, ML_HEADS, 1, ML_DH)

    tabs = {}
    for length in (SEQ, DEC_SEQ):
        fre, fim, g_mat, sgn = _dft_tables(length)
        fre_h, fre_l = _split2(fre)
        fim_h, fim_l = _split2(fim)
        tabs[length] = dict(spec=(fre_h, fre_l, fim_h, fim_l, sgn),
                            f=jnp.concatenate([fre_h, fim_h], axis=0), g=g_mat.astype(BF16))

    cvec = jnp.zeros((MOD_ROWS, D_MODEL), F32).at[:DEC_BATCH].set(c).at[CTX_ROW].set(c_ctx)
    mod_all = mod_table(cvec, mod_w, mod_b)

    x = jnp.concatenate([x_prompt.reshape(N_PROMPT, D_MODEL), x_sample.reshape(N_SAMPLE, D_MODEL)], axis=0)
    groups = ((SEQ, BATCH, 0, SEQ), (DEC_SEQ, DEC_BATCH, N_PROMPT // DEC_SEQ, GRID_W))
    new_gla, new_c, new_n, new_m = [], [], [], []
    for l in range(DEPTH):
        mod3 = mod_all[l].reshape(MOD_ROWS, 1, 6 * D_MODEL)
        h_hi, h_lo = normmod(x, norm1_g[l], mod3, 0, 1)
        zmain = mm_bias(h_hi, w_main[l], b_main[l].reshape(1, N_MAIN))
        zsmall = mm3_bias(h_hi, h_lo, w_small_hi[l], w_small_lo[l], b_small[l].reshape(1, N_SMALL))
        mi = zsmall[:, 32:40].reshape(N_ROWS, 2, ML_HEADS)
        mf = zsmall[:, 40:48].reshape(N_ROWS, 2, ML_HEADS)
        gcol = jnp.concatenate([mi, mf, jnp.zeros((N_ROWS, 4, ML_HEADS), F32)], axis=1)
        gates_col = jnp.transpose(gcol, (2, 0, 1))
        gates_row = jnp.transpose(gcol.reshape(N_ROWS // CHUNK, CHUNK, 8, ML_HEADS), (3, 0, 2, 1))
        gate_b = ml_gate_b[l].reshape(-1)
        y_hy, y_gla, y_ml = [], [], []
        for gi, (length, n_seq, blk0, seg) in enumerate(groups):
            t = tabs[length]
            sre, sim = hyena_spectrum(length, hy_w1[l], hy_b1[l], hy_w2[l], hy_b2[l], hy_w3[l],
                                      hy_freq[l], t["spec"])
            y_hy.append(hyena(zmain, hy_conv[l], sre, sim, hy_bias[l], t["f"], t["g"],
                              length, n_seq, blk0, seg))
            gla_args = (zmain, zsmall, a2f[l], a2b[l], a2bias[l], gla_norm_g[l].reshape(1, GLA_DV),
                        length, n_seq, blk0)
            ml_args = (zmain, gates_col, gates_row, gate_b, ml_conv[l], ml_norm_g[l].reshape(1, ML_DH),
                       length, n_seq, blk0, seg)
            if gi == 0:
                yg, s_fin = gla(*gla_args, None, l)
                ym, c_fin, n_fin, m_fin = mlstm(*ml_args, None, l)
                new_gla.append(s_fin)
                new_c.append(c_fin)
                new_n.append(n_fin[:, :, :, 0, :])
                new_m.append(m_fin[:, :, :, 0, 0])
            else:
                yg = gla(*gla_args, state_gla, l)
                ym = mlstm(*ml_args, (state_mlstm_C, n0_all, state_mlstm_m[:, l].reshape(-1)), l)
            y_gla.append(yg)
            y_ml.append(ym)
        merged = merge_branches(jnp.concatenate(y_hy, 0), jnp.concatenate(y_gla, 0),
                                jnp.concatenate(y_ml, 0), w_branch_b[l], zmain)
        x = mm_residual(merged, w_out_b[l], x, mod3, 2)
        h2, h2_lo, h2t = normmod(x, norm2_g[l], mod3, 3, 4, transposed=True)
        q = mm3_bias(h2, h2_lo, wq_hi[l], wq_lo[l], zero_b)
        s1, s2, e1, e2, th = peer_topk(q, peer_keys[l])
        x = peer_experts(h2t, u_b[l], vt_b[l], s1, s2, e1, e2, th, x, mod3, 5)

    y = final_norm(x, final_g)
    y_prompt = y[:N_PROMPT].reshape(BATCH, SEQ, D_MODEL)
    y_sample = y[N_PROMPT:].reshape(DEC_BATCH, DEC_SEQ, D_MODEL)
    return (y_prompt, y_sample, jnp.stack(new_gla, axis=1), jnp.stack(new_c, axis=1),
            jnp.stack(new_n, axis=1), jnp.stack(new_m, axis=1))
```

```python
import functools
import math

import jax
import jax.numpy as jnp
import numpy as np
from jax import lax
from jax.experimental import pallas as pl
from jax.experimental.pallas import tpu as pltpu

F32 = jnp.float32
BF16 = jnp.bfloat16

D_MODEL = 2048
BATCH = 16
SEQ = 256
DEPTH = 4
DEC_BATCH = 4
DEC_SEQ = 1024
GRID_W = 64
EPS = 1e-6
CHUNK = 64
HY_WIDTH = 1024
HY_EMB = 33
HY_BANDS = (HY_EMB - 1) // 2
HY_FFN = 64
HY_DECAY_TARGET = 1e-2
HY_FAST_PCT = 0.3
HY_SLOW_PCT = 1.5
GLA_HEADS = 4
GLA_DK = 128
GLA_DV = 256
GLA_RANK = 16
GLA_NORMALIZER = 16.0
ML_HEADS = 4
ML_DH = 256
PEER_HEADS = 8
PEER_NKEYS = 128
PEER_EXPERTS = PEER_NKEYS * PEER_NKEYS
PEER_TOPK = 16

N_PROMPT = BATCH * SEQ
N_SAMPLE = DEC_BATCH * DEC_SEQ
N_ROWS = N_PROMPT + N_SAMPLE
CTX_ROW = DEC_BATCH
MOD_ROWS = 8

C_HX1, C_HX2, C_HV = 0, 1024, 2048
C_GQ, C_GK, C_GV, C_GR = 3072, 3584, 4096, 5120
C_MQ, C_MK, C_MV, C_MO = 6144, 7168, 8192, 9216
C_GA, C_GB, C_GC = 10240, 12288, 14336
N_MAIN = 16384
N_SMALL = 128

VMEM_LIMIT = 56 * 1024 * 1024


def _cp(sem):
    return pltpu.CompilerParams(dimension_semantics=sem, vmem_limit_bytes=VMEM_LIMIT)


def _dot(a, b):
    return jnp.dot(a, b, preferred_element_type=F32)


def _dot_nt(a, b):
    return lax.dot_general(a, b, (((1,), (1,)), ((), ())), preferred_element_type=F32)


def _dot_tn(a, b):
    return lax.dot_general(a, b, (((0,), (0,)), ((), ())), preferred_element_type=F32)


def _split2(a):
    hi = a.astype(BF16)
    lo = (a - hi.astype(F32)).astype(BF16)
    return hi, lo


def _split3(a):
    a1 = a.astype(BF16)
    r1 = a - a1.astype(F32)
    a2 = r1.astype(BF16)
    a3 = (r1 - a2.astype(F32)).astype(BF16)
    return a1, a2, a3


def _dot3(a, b, dot=_dot):
    ah, al = _split2(a)
    bh, bl = _split2(b)
    return dot(ah, bh) + (dot(ah, bl) + dot(al, bh))


def _log_sigmoid(x):
    return jnp.minimum(x, 0.0) - jnp.log(1.0 + jnp.exp(-jnp.abs(x)))


def _mod_row(i, tm):
    n_p = N_PROMPT // tm
    return jnp.where(i < n_p, CTX_ROW, (i - n_p) // (DEC_SEQ // tm))


def _mod_kernel(c_ref, w_ref, b_ref, o_ref):
    a = c_ref[...]
    a = a * jax.nn.sigmoid(a)
    o_ref[0] = _dot3(a, w_ref[0]) + b_ref[0]


def mod_table(cvec, mod_w, mod_b):
    tn = 1024
    n = mod_w.shape[-1]
    return pl.pallas_call(
        _mod_kernel,
        grid=(DEPTH, n // tn),
        in_specs=[pl.BlockSpec((MOD_ROWS, D_MODEL), lambda l, j: (0, 0)),
                  pl.BlockSpec((1, D_MODEL, tn), lambda l, j: (l, 0, j)),
                  pl.BlockSpec((1, 1, tn), lambda l, j: (l, 0, j))],
        out_specs=pl.BlockSpec((1, MOD_ROWS, tn), lambda l, j: (l, 0, j)),
        out_shape=jax.ShapeDtypeStruct((DEPTH, MOD_ROWS, n), F32),
        compiler_params=_cp(("parallel", "parallel")),
        name="mod_table",
    )(cvec, mod_w, mod_b.reshape(DEPTH, 1, n))


def _normmod_kernel(x_ref, g_ref, sc_ref, sh_ref, hi_ref, lo_ref, *t_ref):
    x = x_ref[...]
    y = x * lax.rsqrt(jnp.mean(x * x, axis=-1, keepdims=True) + EPS) * g_ref[...]
    h = y * (1.0 + sc_ref[...]) + sh_ref[...]
    hi = h.astype(BF16)
    hi_ref[...] = hi
    lo_ref[...] = (h - hi.astype(F32)).astype(BF16)
    if t_ref:
        t_ref[0][...] = h.T.astype(BF16)


def normmod(x, g, mod3, sh_chunk, sc_chunk, transposed=False):
    tm = 256
    out_shape = [jax.ShapeDtypeStruct((N_ROWS, D_MODEL), BF16)] * 2
    out_specs = [pl.BlockSpec((tm, D_MODEL), lambda i: (i, 0))] * 2
    if transposed:
        out_shape = out_shape + [jax.ShapeDtypeStruct((D_MODEL, N_ROWS), BF16)]
        out_specs = out_specs + [pl.BlockSpec((D_MODEL, tm), lambda i: (0, i))]
    return pl.pallas_call(
        _normmod_kernel,
        grid=(N_ROWS // tm,),
        in_specs=[pl.BlockSpec((tm, D_MODEL), lambda i: (i, 0)),
                  pl.BlockSpec((1, D_MODEL), lambda i: (0, 0)),
                  pl.BlockSpec((None, 1, D_MODEL), lambda i: (_mod_row(i, tm), 0, sc_chunk)),
                  pl.BlockSpec((None, 1, D_MODEL), lambda i: (_mod_row(i, tm), 0, sh_chunk))],
        out_specs=out_specs,
        out_shape=out_shape,
        compiler_params=_cp(("parallel",)),
        name="normmod",
    )(x, g.reshape(1, D_MODEL), mod3, mod3)


def _final_norm_kernel(x_ref, g_ref, o_ref):
    x = x_ref[...]
    o_ref[...] = x * lax.rsqrt(jnp.mean(x * x, axis=-1, keepdims=True) + EPS) * g_ref[...]


def final_norm(x, g):
    tm = 256
    return pl.pallas_call(
        _final_norm_kernel,
        grid=(N_ROWS // tm,),
        in_specs=[pl.BlockSpec((tm, D_MODEL), lambda i: (i, 0)),
                  pl.BlockSpec((1, D_MODEL), lambda i: (0, 0))],
        out_specs=pl.BlockSpec((tm, D_MODEL), lambda i: (i, 0)),
        out_shape=jax.ShapeDtypeStruct((N_ROWS, D_MODEL), F32),
        compiler_params=_cp(("parallel",)),
        name="final_norm",
    )(x, g.reshape(1, D_MODEL))


def _mm_bias_kernel(x_ref, w_ref, b_ref, o_ref):
    o_ref[...] = (_dot(x_ref[...], w_ref[...]) + b_ref[...]).astype(o_ref.dtype)


def mm_bias(x, w, layer, b, out_dtype=F32, tm=512, tn=1024):
    m, k = x.shape
    n = w.shape[-1]
    return pl.pallas_call(
        _mm_bias_kernel,
        grid=(n // tn, m // tm),
        in_specs=[pl.BlockSpec((tm, k), lambda j, i: (i, 0)),
                  pl.BlockSpec((None, k, tn), lambda j, i: (layer, 0, j)),
                  pl.BlockSpec((1, tn), lambda j, i: (0, j))],
        out_specs=pl.BlockSpec((tm, tn), lambda j, i: (i, j)),
        out_shape=jax.ShapeDtypeStruct((m, n), out_dtype),
        compiler_params=_cp(("parallel", "parallel")),
        name="mm_bias",
    )(x, w, b)


def _mm3_bias_kernel(xh_ref, xl_ref, wh_ref, wl_ref, b_ref, o_ref):
    xh = xh_ref[...]
    acc = _dot(xh, wh_ref[...]) + (_dot(xh, wl_ref[...]) + _dot(xl_ref[...], wh_ref[...]))
    o_ref[...] = acc + b_ref[...]


def mm3_bias(xh, xl, wh, wl, layer, b, tm=512, tn=1024):
    m, k = xh.shape
    n = wh.shape[-1]
    tn = min(tn, n)
    return pl.pallas_call(
        _mm3_bias_kernel,
        grid=(n // tn, m // tm),
        in_specs=[pl.BlockSpec((tm, k), lambda j, i: (i, 0)),
                  pl.BlockSpec((tm, k), lambda j, i: (i, 0)),
                  pl.BlockSpec((None, k, tn), lambda j, i: (layer, 0, j)),
                  pl.BlockSpec((None, k, tn), lambda j, i: (layer, 0, j)),
                  pl.BlockSpec((1, tn), lambda j, i: (0, j))],
        out_specs=pl.BlockSpec((tm, tn), lambda j, i: (i, j)),
        out_shape=jax.ShapeDtypeStruct((m, n), F32),
        compiler_params=_cp(("parallel", "parallel")),
        name="mm3_bias",
    )(xh, xl, wh, wl, b)


def _merge_kernel(yh_ref, yg_ref, ym_ref, w_ref, ga_ref, gb_ref, gc_ref, o_ref):
    acc = jax.nn.sigmoid(ga_ref[...]) * _dot(yh_ref[...], w_ref[0])
    acc += jax.nn.sigmoid(gb_ref[...]) * _dot(yg_ref[...], w_ref[1])
    acc += jax.nn.sigmoid(gc_ref[...]) * _dot(ym_ref[...], w_ref[2])
    o_ref[...] = acc.astype(o_ref.dtype)


def merge_branches(y_hy, y_gla, y_ml, w_branch, layer, zmain, tm=512, tn=1024):
    kb = HY_WIDTH
    y_spec = pl.BlockSpec((tm, kb), lambda j, i: (i, 0))

    def gate_spec(col):
        return pl.BlockSpec((tm, tn), lambda j, i, c=col // tn: (i, c + j))

    return pl.pallas_call(
        _merge_kernel,
        grid=(D_MODEL // tn, N_ROWS // tm),
        in_specs=[y_spec, y_spec, y_spec,
                  pl.BlockSpec((None, 3, kb, tn), lambda j, i: (layer, 0, 0, j)),
                  gate_spec(C_GA), gate_spec(C_GB), gate_spec(C_GC)],
        out_specs=pl.BlockSpec((tm, tn), lambda j, i: (i, j)),
        out_shape=jax.ShapeDtypeStruct((N_ROWS, D_MODEL), BF16),
        compiler_params=_cp(("parallel", "parallel")),
        name="merge_branches",
    )(y_hy, y_gla, y_ml, w_branch, zmain, zmain, zmain)


def _mm_resid_kernel(m_ref, w_ref, x_ref, gt_ref, o_ref):
    o_ref[...] = x_ref[...] + gt_ref[...] * _dot(m_ref[...], w_ref[...])


def mm_residual(merged, w, layer, x, mod3, gt_chunk, tm=512, tn=1024):
    k = merged.shape[1]
    return pl.pallas_call(
        _mm_resid_kernel,
        grid=(D_MODEL // tn, N_ROWS // tm),
        in_specs=[pl.BlockSpec((tm, k), lambda j, i: (i, 0)),
                  pl.BlockSpec((None, k, tn), lambda j, i: (layer, 0, j)),
                  pl.BlockSpec((tm, tn), lambda j, i: (i, j)),
                  pl.BlockSpec((None, 1, tn),
                               lambda j, i: (_mod_row(i, tm), 0, gt_chunk * (D_MODEL // tn) + j))],
        out_specs=pl.BlockSpec((tm, tn), lambda j, i: (i, j)),
        out_shape=jax.ShapeDtypeStruct((N_ROWS, D_MODEL), F32),
        compiler_params=_cp(("parallel", "parallel")),
        name="mm_residual",
    )(merged, w, x, mod3)


def _dft_tables(length):
    k = jnp.arange(length, dtype=jnp.int32)
    m = (k[:, None] * k[None, :]) % (2 * length)
    ang = m.astype(F32) * (math.pi / length)
    cos = jnp.cos(ang)
    sin = jnp.sin(ang)
    sgn = jnp.where(k % 2 == 0, 1.0, -1.0).astype(F32)
    fre = cos
    fim = jnp.where(k[:, None] == 0, sgn[None, :], -sin)
    wk = jnp.where(k == 0, 1.0, 2.0).astype(F32) / (2.0 * length)
    g_re = cos.T * wk[None, :]
    g_im = jnp.where(k[None, :] == 0, sgn[:, None] / (2.0 * length), -sin.T / length)
    return fre, fim, jnp.concatenate([g_re, g_im], axis=1), sgn


def _hyfilt_kernel(z_ref, w1_ref, b1_ref, w2_ref, b2_ref, fq_ref, w3f_ref, w3b_ref, tn_ref, dl_ref,
                   sgn_ref, freh_ref, frel_ref, fimh_ref, fiml_ref, sre_ref, sim_ref):
    hid = jnp.sin(fq_ref[0:1] * (_dot3(z_ref[...], w1_ref[...]) + b1_ref[...]))
    hid = jnp.sin(fq_ref[1:2] * (_dot3(hid, w2_ref[...]) + b2_ref[...]))
    decay = jnp.exp(-tn_ref[...] * dl_ref[...])
    fwd = _dot3(hid, w3f_ref[...]) * decay
    bwd = _dot3(hid, w3b_ref[...]) * decay
    row = lax.broadcasted_iota(jnp.int32, fwd.shape, 0)
    bwd = jnp.where(row == 0, 0.0, bwd)
    a = fwd + bwd
    d = fwd - bwd
    ah, al = _split2(a)
    dh, dl2 = _split2(d)
    re = _dot(freh_ref[...], ah) + (_dot(freh_ref[...], al) + _dot(frel_ref[...], ah))
    im = _dot(fimh_ref[...], dh) + (_dot(fimh_ref[...], dl2) + _dot(fiml_ref[...], dh))
    nyq = jnp.sum(sgn_ref[...] * a, axis=0, keepdims=True)
    sre_ref[0] = re
    sim_ref[0] = jnp.where(row == 0, nyq, im)


def hyena_spectrum(length, w1, b1, w2, b2, w3, freq, tabs):
    fre_h, fre_l, fim_h, fim_l, sgn = tabs
    t = jnp.arange(length, dtype=F32)
    t_norm = t / (length - 1)
    bands = jnp.linspace(1e-4, HY_BANDS - 1, HY_BANDS, dtype=F32)
    ang = (2.0 * math.pi / length) * t[:, None] * bands[None, :]
    z = jnp.concatenate([t_norm[:, None], jnp.cos(ang), -jnp.sin(ang),
                         jnp.zeros((length, HY_FFN - HY_EMB), F32)], axis=-1)
    w1p = jnp.pad(w1, ((0, HY_FFN - HY_EMB), (0, 0)))
    max_decay = math.log(HY_DECAY_TARGET) / HY_FAST_PCT
    min_decay = math.log(HY_DECAY_TARGET) / HY_SLOW_PCT
    deltas = jnp.abs(jnp.linspace(min_decay, max_decay, HY_WIDTH, dtype=F32)).reshape(1, HY_WIDTH)
    ct = 256
    nct = HY_WIDTH // ct
    full = lambda shape: pl.BlockSpec(shape, lambda o, j: (0,) * len(shape))
    out_spec = pl.BlockSpec((1, length, ct), lambda o, j: (o, 0, j))
    return pl.pallas_call(
        _hyfilt_kernel,
        grid=(2, nct),
        in_specs=[full((length, HY_FFN)), full((HY_FFN, HY_FFN)), full((1, HY_FFN)),
                  full((HY_FFN, HY_FFN)), full((1, HY_FFN)), full((2, HY_FFN)),
                  pl.BlockSpec((HY_FFN, ct), lambda o, j: (0, o * 2 * nct + j)),
                  pl.BlockSpec((HY_FFN, ct), lambda o, j: (0, o * 2 * nct + nct + j)),
                  full((length, 1)),
                  pl.BlockSpec((1, ct), lambda o, j: (0, j)),
                  full((length, 1)),
                  full((length, length)), full((length, length)),
                  full((length, length)), full((length, length))],
        out_specs=[out_spec, out_spec],
        out_shape=[jax.ShapeDtypeStruct((2, length, HY_WIDTH), F32)] * 2,
        compiler_params=_cp(("parallel", "parallel")),
        name="hyena_spectrum",
    )(z, w1p, b1.reshape(1, HY_FFN), w2, b2.reshape(1, HY_FFN), freq, w3, w3,
      t_norm.reshape(length, 1), deltas, sgn.reshape(length, 1), fre_h, fre_l, fim_h, fim_l)


def _short_conv(x, w, pos, seg, length):
    prev = jnp.where(pos == 0, 0.0, pltpu.roll(x, 1, 0))
    nxt = jnp.where(pos == seg - 1, 0.0, pltpu.roll(x, length - 1, 0))
    return prev * w[0:1] + x * w[1:2] + nxt * w[2:3]


def _hyena_kernel(x1_ref, x2_ref, v_ref, cw_ref, sre_ref, sim_ref, hb_ref, f_ref, g_ref, o_ref,
                  *, length, seg):
    row = lax.broadcasted_iota(jnp.int32, (length, 1), 0)
    pos = row % seg
    row0 = row == 0

    def long_conv(u, o):
        spec = _dot(f_ref[...], u.astype(BF16))
        ur, ui = spec[:length], spec[length:]
        hre, him = sre_ref[o], sim_ref[o]
        uihi = ui * him
        yr = ur * hre - jnp.where(row0, 0.0, uihi)
        yi = jnp.where(row0, uihi, ur * him + ui * hre)
        y = _dot(g_ref[...], jnp.concatenate([yr, yi], axis=0).astype(BF16))
        return y + u * hb_ref[o:o + 1]

    x1 = _short_conv(x1_ref[...], cw_ref[0], pos, seg, length)
    x2 = _short_conv(x2_ref[...], cw_ref[1], pos, seg, length)
    v = _short_conv(v_ref[...], cw_ref[2], pos, seg, length)
    z = x1 * long_conv(v, 0)
    o_ref[...] = (x2 * long_conv(z, 1)).astype(o_ref.dtype)


def hyena(zmain, conv_w, spec_re, spec_im, bias, f_mat, g_mat, length, n_seq, row_blk0, seg):
    ct = 256
    nct = HY_WIDTH // ct

    def zspec(col):
        return pl.BlockSpec((length, ct), lambda b, j, c=col // ct: (row_blk0 + b, c + j))

    return pl.pallas_call(
        functools.partial(_hyena_kernel, length=length, seg=seg),
        grid=(n_seq, nct),
        in_specs=[zspec(C_HX1), zspec(C_HX2), zspec(C_HV),
                  pl.BlockSpec((3, 3, ct), lambda b, j: (0, 0, j)),
                  pl.BlockSpec((2, length, ct), lambda b, j: (0, 0, j)),
                  pl.BlockSpec((2, length, ct), lambda b, j: (0, 0, j)),
                  pl.BlockSpec((2, ct), lambda b, j: (0, j)),
                  pl.BlockSpec((2 * length, length), lambda b, j: (0, 0)),
                  pl.BlockSpec((length, 2 * length), lambda b, j: (0, 0))],
        out_specs=pl.BlockSpec((length, ct), lambda b, j: (b, j)),
        out_shape=jax.ShapeDtypeStruct((n_seq * length, HY_WIDTH), BF16),
        compiler_params=_cp(("parallel", "parallel")),
        name="hyena",
    )(zmain, zmain, zmain, conv_w, spec_re, spec_im, bias, f_mat, g_mat)


def _tri_masks():
    t = lax.broadcasted_iota(jnp.int32, (CHUNK, CHUNK), 0)
    s = lax.broadcasted_iota(jnp.int32, (CHUNK, CHUNK), 1)
    return s <= t, s >= t


def _gla_chunk(q, k, v, g, state, mask, rev):
    tm = jnp.where(mask, 1.0, 0.0).astype(BF16)
    g1, g2, g3 = _split3(g)
    bc = _dot(tm, g1) + (_dot(tm, g2) + _dot(tm, g3))
    b_end = bc[0:1] if rev else bc[CHUNK - 1:CHUNK]
    ref = bc[CHUNK // 2:CHUNK // 2 + 1]
    inter = _dot((q * jnp.exp(bc)).astype(BF16), state.astype(BF16))
    qh = (q * jnp.exp(bc - ref)).astype(BF16)
    kh = (k * jnp.exp(ref - bc)).astype(BF16)
    att = jnp.where(mask, _dot_nt(qh, kh), 0.0)
    vb = v.astype(BF16)
    o = inter + _dot(att.astype(BF16), vb)
    ones = jnp.ones((CHUNK, GLA_DK), BF16)
    tot = _dot_tn(g1, ones) + (_dot_tn(g2, ones) + _dot_tn(g3, ones))
    e = jnp.exp(tot)
    kd = (k * jnp.exp(b_end - bc)).astype(BF16)
    new_state = jnp.concatenate([e, e], axis=1) * state + _dot_tn(kd, vb)
    return o, new_state


def _gla_kernel(*refs, length, has_init):
    if has_init:
        (q_ref, k_ref, v_ref, gr_ref, zs_ref, wf_ref, wb_ref, ab_ref, ng_ref, s0_ref,
         y_ref, o_ref, lg_ref, st_ref) = refs
    else:
        (q_ref, k_ref, v_ref, gr_ref, zs_ref, wf_ref, wb_ref, ab_ref, ng_ref,
         y_ref, sout_ref, o_ref, lg_ref, st_ref) = refs
    zs = zs_ref[...]
    lg_ref[0] = _log_sigmoid(_dot3(zs, wf_ref[...]) + ab_ref[0]) * (1.0 / GLA_NORMALIZER)
    lg_ref[1] = _log_sigmoid(_dot3(zs, wb_ref[...]) + ab_ref[1]) * (1.0 / GLA_NORMALIZER)
    if has_init:
        st_ref[...] = s0_ref[...]
    else:
        st_ref[...] = jnp.zeros_like(st_ref)
    o_ref[...] = jnp.zeros_like(o_ref)
    n = length // CHUNK
    mask_f, mask_b = _tri_masks()
    scale = GLA_DK ** -0.5

    def body(i, carry):
        for d, mask in enumerate((mask_f, mask_b)):
            c = i if d == 0 else n - 1 - i
            rows = pl.ds(pl.multiple_of(c * CHUNK, CHUNK), CHUNK)
            for h in range(GLA_HEADS):
                kc = slice(h * GLA_DK, (h + 1) * GLA_DK)
                vc = slice(h * GLA_DV, (h + 1) * GLA_DV)
                o, s_new = _gla_chunk(q_ref[rows, kc] * scale, k_ref[rows, kc], v_ref[rows, vc],
                                      lg_ref[d, rows, kc], st_ref[d, h], mask, d == 1)
                o_ref[rows, vc] += o
                st_ref[d, h] = s_new
        return carry

    lax.fori_loop(0, n, body, 0)
    for h in range(GLA_HEADS):
        vc = slice(h * GLA_DV, (h + 1) * GLA_DV)
        o = o_ref[:, vc]
        o = o * lax.rsqrt(jnp.mean(o * o, axis=-1, keepdims=True) + EPS) * ng_ref[...]
        gr = gr_ref[:, vc]
        y_ref[:, vc] = (o * (gr * jax.nn.sigmoid(gr))).astype(y_ref.dtype)
    if not has_init:
        sout_ref[...] = st_ref[...]


def gla(zmain, zsmall, wf, wb, ab, norm_g, length, n_seq, row_blk0, state0, layer):
    has_init = state0 is not None
    qk_w = GLA_HEADS * GLA_DK
    v_w = GLA_HEADS * GLA_DV
    once = pl.Buffered(1)

    def zspec(col, width):
        return pl.BlockSpec((length, width), lambda b, c=col // width: (row_blk0 + b, c), pipeline_mode=once)

    in_specs = [zspec(C_GQ, qk_w), zspec(C_GK, qk_w), zspec(C_GV, v_w), zspec(C_GR, v_w),
                pl.BlockSpec((length, N_SMALL), lambda b: (row_blk0 + b, 0)),
                pl.BlockSpec((N_SMALL, qk_w), lambda b: (0, 0)),
                pl.BlockSpec((N_SMALL, qk_w), lambda b: (0, 0)),
                pl.BlockSpec((2, 1, qk_w), lambda b: (0, 0, 0)),
                pl.BlockSpec((1, GLA_DV), lambda b: (0, 0))]
    args = [zmain, zmain, zmain, zmain, zsmall, wf, wb, ab, norm_g]
    y_shape = jax.ShapeDtypeStruct((n_seq * length, v_w), BF16)
    y_spec = pl.BlockSpec((length, v_w), lambda b: (b, 0))
    if has_init:
        in_specs.append(pl.BlockSpec((None, None, 2, GLA_HEADS, GLA_DK, GLA_DV),
                                     lambda b: (b, layer, 0, 0, 0, 0)))
        args.append(state0)
        out_shape, out_specs = y_shape, y_spec
    else:
        out_shape = [y_shape, jax.ShapeDtypeStruct((n_seq, 2, GLA_HEADS, GLA_DK, GLA_DV), F32)]
        out_specs = [y_spec, pl.BlockSpec((None, 2, GLA_HEADS, GLA_DK, GLA_DV), lambda b: (b, 0, 0, 0, 0))]
    return pl.pallas_call(
        functools.partial(_gla_kernel, length=length, has_init=has_init),
        grid=(n_seq,),
        in_specs=in_specs,
        out_specs=out_specs,
        out_shape=out_shape,
        scratch_shapes=[pltpu.VMEM((length, v_w), F32), pltpu.VMEM((2, length, qk_w), F32),
                        pltpu.VMEM((2, GLA_HEADS, GLA_DK, GLA_DV), F32)],
        compiler_params=_cp(("parallel",)),
        name="gla",
    )(*args)


def _mlstm_chunk(q, ks, v, lf_c, li_c, lf_r, li_r, cm, nv, m_prev, mask, mask_t, rev):
    t_n = CHUNK
    tm = jnp.where(mask, 1.0, 0.0).astype(BF16)
    tmt = jnp.where(mask_t, 1.0, 0.0).astype(BF16)
    c1, c2, c3 = _split3(jnp.broadcast_to(lf_c, (t_n, t_n)))
    b_colb = _dot(tm, c1) + (_dot(tm, c2) + _dot(tm, c3))
    r1, r2, r3 = _split3(jnp.broadcast_to(lf_r, (t_n, t_n)))
    b_rowb = _dot(r1, tmt) + (_dot(r2, tmt) + _dot(r3, tmt))
    b_col = b_colb[:, 0:1]
    b_row = b_rowb[0:1, :]
    b_end = b_colb[0:1, 0:1] if rev else b_colb[t_n - 1:t_n, 0:1]
    dmat = jnp.where(mask, b_colb - b_rowb + li_r, -jnp.inf)
    m_t = jnp.maximum(b_col + m_prev, jnp.max(dmat, axis=-1, keepdims=True))
    w_inter = jnp.exp(b_col + m_prev - m_t)
    qb = q.astype(BF16)
    vb = v.astype(BF16)
    sc = _dot_nt(qb, ks.astype(BF16)) * jnp.exp(dmat - m_t)
    num = w_inter * _dot(qb, cm.astype(BF16)) + _dot(sc.astype(BF16), vb)
    den = w_inter * jnp.sum(q * nv, axis=-1, keepdims=True) + jnp.sum(sc, axis=-1, keepdims=True)
    h = num / jnp.maximum(jnp.abs(den), jnp.exp(-m_t))
    g_r = b_end - b_row + li_r
    g_c = b_end - b_col + li_c
    m_new = jnp.maximum(b_end + m_prev, jnp.max(g_r, axis=-1, keepdims=True))
    w_c = jnp.exp(b_end + m_prev - m_new)
    kw = ks * jnp.exp(g_c - m_new)
    cm_new = w_c * cm + _dot_tn(kw.astype(BF16), vb)
    nv_new = w_c * nv + jnp.sum(kw, axis=0, keepdims=True)
    return h, cm_new, nv_new, m_new


def _mlstm_kernel(*refs, length, seg, has_init):
    if has_init:
        (gb_ref, m0_ref, q_ref, k_ref, v_ref, mo_ref, gc_ref, gr_ref, cw_ref, ng_ref, c0_ref, n0_ref,
         y_ref, qc_ref, kc_ref, h_ref, c_ref, n_ref, m_ref) = refs
    else:
        (gb_ref, q_ref, k_ref, v_ref, mo_ref, gc_ref, gr_ref, cw_ref, ng_ref,
         y_ref, cout_ref, nout_ref, mout_ref, qc_ref, kc_ref, h_ref, c_ref, n_ref, m_ref) = refs
    b_idx = pl.program_id(0)
    dh = ML_DH
    row = lax.broadcasted_iota(jnp.int32, (length, 1), 0)
    pos = row % seg
    qc_ref[...] = _short_conv(q_ref[...], cw_ref[0], pos, seg, length)
    kc_ref[...] = _short_conv(k_ref[...], cw_ref[1], pos, seg, length) * (dh ** -0.5)
    h_ref[...] = jnp.zeros_like(h_ref)
    if has_init:
        c_ref[...] = c0_ref[...]
        n_ref[...] = n0_ref[...]
        for d in range(2):
            for h in range(ML_HEADS):
                m_ref[d, h] = jnp.full((1, 128), m0_ref[b_idx * 2 * ML_HEADS + d * ML_HEADS + h], F32)
    else:
        c_ref[...] = jnp.zeros_like(c_ref)
        n_ref[...] = jnp.zeros_like(n_ref)
        m_ref[...] = jnp.zeros_like(m_ref)
    n = length // CHUNK
    mask_f, mask_b = _tri_masks()

    def body(i, carry):
        for d, (mask, mask_t) in enumerate(((mask_f, mask_b), (mask_b, mask_f))):
            c = i if d == 0 else n - 1 - i
            rows = pl.ds(pl.multiple_of(c * CHUNK, CHUNK), CHUNK)
            for h in range(ML_HEADS):
                cols = slice(h * dh, (h + 1) * dh)
                bi = gb_ref[d * 2 * ML_HEADS + h]
                bf = gb_ref[d * 2 * ML_HEADS + ML_HEADS + h]
                gcol = gc_ref[h, rows, :]
                grow = gr_ref[h, c]
                li_c = gcol[:, d:d + 1] + bi
                lf_c = _log_sigmoid(gcol[:, 2 + d:3 + d] + bf)
                li_r = grow[d:d + 1, :] + bi
                lf_r = _log_sigmoid(grow[2 + d:3 + d, :] + bf)
                hc, cm, nv, m_new = _mlstm_chunk(qc_ref[rows, cols], kc_ref[rows, cols], v_ref[rows, cols],
                                                 lf_c, li_c, lf_r, li_r, c_ref[d, h], n_ref[d, h],
                                                 m_ref[d, h][:, 0:1], mask, mask_t, d == 1)
                h_ref[rows, cols] += hc
                c_ref[d, h] = cm
                n_ref[d, h] = nv
                m_ref[d, h] = jnp.broadcast_to(m_new, (1, 128))
        return carry

    lax.fori_loop(0, n, body, 0)
    for h in range(ML_HEADS):
        cols = slice(h * dh, (h + 1) * dh)
        o = h_ref[:, cols]
        o = o * lax.rsqrt(jnp.mean(o * o, axis=-1, keepdims=True) + EPS) * ng_ref[...]
        y_ref[:, cols] = (o * jax.nn.sigmoid(mo_ref[:, cols])).astype(y_ref.dtype)
    if not has_init:
        cout_ref[...] = c_ref[...]
        nout_ref[...] = n_ref[...]
        mout_ref[...] = m_ref[...]


def mlstm(zmain, gates_col, gates_row, gate_b, conv_w, norm_g, length, n_seq, row_blk0, seg, init, layer):
    has_init = init is not None
    dh = ML_DH
    nh = ML_HEADS
    width = nh * dh
    once = pl.Buffered(1)

    def zspec(col):
        return pl.BlockSpec((length, width), lambda b, c=col // width: (row_blk0 + b, c), pipeline_mode=once)

    smem = pl.BlockSpec(memory_space=pltpu.SMEM)
    in_specs = [smem]
    args = [gate_b]
    if has_init:
        c0, n0, m0 = init
        in_specs.append(smem)
        args.append(m0)
    in_specs += [zspec(C_MQ), zspec(C_MK), zspec(C_MV), zspec(C_MO),
                 pl.BlockSpec((nh, length, 8), lambda b: (0, row_blk0 + b, 0)),
                 pl.BlockSpec((nh, length // CHUNK, 8, CHUNK), lambda b: (0, row_blk0 + b, 0, 0)),
                 pl.BlockSpec((2, 3, width), lambda b: (0, 0, 0)),
                 pl.BlockSpec((1, dh), lambda b: (0, 0))]
    args += [zmain, zmain, zmain, zmain, gates_col, gates_row, conv_w, norm_g]
    y_shape = jax.ShapeDtypeStruct((n_seq * length, width), BF16)
    y_spec = pl.BlockSpec((length, width), lambda b: (b, 0))
    if has_init:
        in_specs += [pl.BlockSpec((None, None, 2, nh, dh, dh), lambda b: (b, layer, 0, 0, 0, 0)),
                     pl.BlockSpec((None, None, 2, nh, 1, dh), lambda b: (b, layer, 0, 0, 0, 0))]
        args += [c0, n0]
        out_shape, out_specs = y_shape, y_spec
    else:
        out_shape = [y_shape,
                     jax.ShapeDtypeStruct((n_seq, 2, nh, dh, dh), F32),
                     jax.ShapeDtypeStruct((n_seq, 2, nh, 1, dh), F32),
                     jax.ShapeDtypeStruct((n_seq, 2, nh, 1, 128), F32)]
        out_specs = [y_spec,
                     pl.BlockSpec((None, 2, nh, dh, dh), lambda b: (b, 0, 0, 0, 0)),
                     pl.BlockSpec((None, 2, nh, 1, dh), lambda b: (b, 0, 0, 0, 0)),
                     pl.BlockSpec((None, 2, nh, 1, 128), lambda b: (b, 0, 0, 0, 0))]
    return pl.pallas_call(
        functools.partial(_mlstm_kernel, length=length, seg=seg, has_init=has_init),
        grid=(n_seq,),
        in_specs=in_specs,
        out_specs=out_specs,
        out_shape=out_shape,
        scratch_shapes=[pltpu.VMEM((length, width), F32), pltpu.VMEM((length, width), F32),
                        pltpu.VMEM((length, width), F32),
                        pltpu.VMEM((2, nh, dh, dh), F32), pltpu.VMEM((2, nh, 1, dh), F32),
                        pltpu.VMEM((2, nh, 1, 128), F32)],
        compiler_params=_cp(("parallel",)),
        name="mlstm",
    )(*args)


def _top_values(x, count, with_rank=False):
    vals = []
    cur = x
    rank = jnp.full(x.shape, float(count), F32) if with_rank else None
    for r in range(count):
        m = jnp.max(cur, axis=0, keepdims=True)
        vals.append(m)
        if with_rank or r + 1 < count:
            top = cur == m
            cur = jnp.where(top, -jnp.inf, cur)
            if with_rank:
                rank = jnp.where(top, float(r), rank)
    return (vals, rank) if with_rank else vals


def _peer_topk_kernel(q_ref, k_ref, nb_ref, rk_ref, e1_ref, e2_ref):
    dk = PEER_NKEYS
    s1 = _dot3(k_ref[0], q_ref[:, 0:dk], dot=_dot_nt)
    s2 = _dot3(k_ref[1], q_ref[:, dk:2 * dk], dot=_dot_nt)
    v1 = _top_values(s1, PEER_TOPK)
    v2, rank2 = _top_values(s2, PEER_TOPK, with_rank=True)
    rows = [v1[a] + v2[b] for a in range(PEER_TOPK) for b in range(PEER_TOPK // (a + 1))]
    rows += [jnp.full_like(rows[0], -jnp.inf)] * (-len(rows) % 8)
    cand = jnp.concatenate(rows, axis=0)
    best = _top_values(cand, PEER_TOPK)
    zsum = jnp.exp(best[0] - best[0])
    for r in range(1, PEER_TOPK):
        zsum = zsum + jnp.exp(best[r] - best[0])
    theta = best[PEER_TOPK - 1]
    nb = jnp.zeros_like(s1)
    for b in range(PEER_TOPK):
        nb = nb + jnp.where(s1 + v2[b] >= theta, 1.0, 0.0)
    nb_ref[...] = nb
    rk_ref[...] = rank2.astype(BF16)
    e1_ref[...] = jnp.exp(s1 - v1[0]) / zsum
    e2_ref[...] = jnp.exp(s2 - v2[0]).astype(BF16)


def peer_topk(q, keys, tt=256):
    n = q.shape[0]
    nk = PEER_NKEYS
    sds = lambda dt: jax.ShapeDtypeStruct((PEER_HEADS, nk, n), dt)
    spec = pl.BlockSpec((None, nk, tt), lambda i, h: (h, 0, i))
    return pl.pallas_call(
        _peer_topk_kernel,
        grid=(n // tt, PEER_HEADS),
        in_specs=[pl.BlockSpec((tt, 2 * nk), lambda i, h: (i, h)),
                  pl.BlockSpec((None, 2, nk, nk), lambda i, h: (h, 0, 0, 0))],
        out_specs=[spec, spec, spec, spec],
        out_shape=[sds(F32), sds(BF16), sds(F32), sds(BF16)],
        compiler_params=_cp(("parallel", "parallel")),
        name="peer_topk",
    )(q, keys)


def _gelu_tanh(x):
    return 0.5 * x * (1.0 + jnp.tanh(math.sqrt(2.0 / math.pi) * (x + 0.044715 * (x * x * x))))


def _peer_expert_kernel(ht_ref, u_ref, vt_ref, nb_ref, rk_ref, e1_ref, e2_ref, x_ref, gt_ref,
                        o_ref, acc_ref, w_ref, *, n_i1, tt):
    j = pl.program_id(1)
    nk = PEER_NKEYS

    @pl.when(j == 0)
    def _():
        acc_ref[...] = jnp.zeros_like(acc_ref)

    act = _dot(u_ref[...], ht_ref[...])
    zero = jnp.zeros((), BF16)
    for ts in range(tt // 128):
        lanes = slice(ts * 128, (ts + 1) * 128)
        for r in range(n_i1):
            w = None
            for h in range(PEER_HEADS):
                nbr = jnp.broadcast_to(nb_ref[h, r:r + 1, lanes].astype(BF16), (nk, 128))
                e1r = jnp.broadcast_to(e1_ref[h, r:r + 1, lanes].astype(BF16), (nk, 128))
                term = jnp.where(rk_ref[h, :, lanes] < nbr, e2_ref[h, :, lanes] * e1r, zero)
                w = term if w is None else w + term
            w_ref[r * nk:(r + 1) * nk, lanes] = w
    p = w_ref[...] * _gelu_tanh(act).astype(BF16)
    acc_ref[...] += _dot(vt_ref[...], p)

    @pl.when(j == pl.num_programs(1) - 1)
    def _():
        o_ref[...] = x_ref[...] + gt_ref[...] * acc_ref[...].T


def peer_experts(h2t, u_tab, vt_tab, layer, nb, rk, e1, e2, x, mod3, gt_chunk, tt=512, ec=1024):
    n = x.shape[0]
    nk = PEER_NKEYS
    n_i1 = ec // nk
    once = pl.Buffered(1)
    sspec = pl.BlockSpec((PEER_HEADS, nk, tt), lambda i, j: (0, 0, i), pipeline_mode=once)
    rspec = pl.BlockSpec((PEER_HEADS, n_i1, tt), lambda i, j: (0, j, i))
    return pl.pallas_call(
        functools.partial(_peer_expert_kernel, n_i1=n_i1, tt=tt),
        grid=(n // tt, PEER_EXPERTS // ec),
        in_specs=[pl.BlockSpec((D_MODEL, tt), lambda i, j: (0, i)),
                  pl.BlockSpec((None, ec, D_MODEL), lambda i, j: (layer, j, 0)),
                  pl.BlockSpec((None, D_MODEL, ec), lambda i, j: (layer, 0, j)),
                  rspec, sspec, rspec, sspec,
                  pl.BlockSpec((tt, D_MODEL), lambda i, j: (i, 0), pipeline_mode=once),
                  pl.BlockSpec((None, 1, D_MODEL), lambda i, j: (_mod_row(i, tt), 0, gt_chunk))],
        out_specs=pl.BlockSpec((tt, D_MODEL), lambda i, j: (i, 0)),
        out_shape=jax.ShapeDtypeStruct((n, D_MODEL), F32),
        scratch_shapes=[pltpu.VMEM((D_MODEL, tt), F32), pltpu.VMEM((ec, tt), BF16)],
        compiler_params=_cp(("parallel", "arbitrary")),
        name="peer_experts",
    )(h2t, u_tab, vt_tab, nb, rk, e1, e2, x, mod3)


def _reorder_in_proj(w_in, b_in):
    splits = (1024, 1024, 1024, 512, 512, 1024, 1024, 16, 16, 1024, 1024, 1024, 1024, 8, 8, 2048, 2048, 2048)
    offs = np.concatenate([[0], np.cumsum(splits)])
    seg = lambda a, i: a[..., offs[i]:offs[i + 1]]
    main_ids = (0, 1, 2, 3, 4, 5, 6, 9, 10, 11, 12, 15, 16, 17)
    small_ids = (7, 8, 13, 14)
    w_main = jnp.concatenate([seg(w_in, i).astype(BF16) for i in main_ids], axis=-1)
    b_main = jnp.concatenate([seg(b_in, i) for i in main_ids], axis=-1)
    w_small = jnp.concatenate([seg(w_in, i) for i in small_ids], axis=-1)
    b_small = jnp.concatenate([seg(b_in, i) for i in small_ids], axis=-1)
    pad = N_SMALL - w_small.shape[-1]
    w_small = jnp.pad(w_small, ((0, 0), (0, 0), (0, pad)))
    b_small = jnp.pad(b_small, ((0, 0), (0, pad)))
    return w_main, b_main, w_small, b_small


def kernel(x_prompt, x_sample, c, state_gla, state_mlstm_C, state_mlstm_n, state_mlstm_m, c_ctx,
           mod_w, mod_b, norm1_g, norm2_g, final_g, w_in, b_in, hy_conv, hy_w1, hy_b1, hy_w2, hy_b2,
           hy_w3, hy_freq, hy_bias, gla_a2_w, gla_a2_b, gla_norm_g, ml_conv, ml_gate_b, ml_norm_g,
           w_branch, w_out, peer_wq, peer_keys, peer_u, peer_v):
    w_main, b_main, w_small, b_small = _reorder_in_proj(w_in, b_in)
    w_small_hi = w_small.astype(BF16)
    w_small_lo = (w_small - w_small_hi.astype(F32)).astype(BF16)
    w_branch_b = w_branch.astype(BF16)
    w_out_b = w_out.astype(BF16)
    wq_hi = peer_wq.astype(BF16)
    wq_lo = (peer_wq - wq_hi.astype(F32)).astype(BF16)
    u_b = peer_u.astype(BF16)
    vt_b = jnp.swapaxes(peer_v.astype(BF16), 1, 2)
    zero_b = jnp.zeros((1, D_MODEL), F32)
    a2f = jnp.pad(gla_a2_w[:, 0], ((0, 0), (0, N_SMALL - GLA_RANK), (0, 0)))
    a2b = jnp.pad(gla_a2_w[:, 1], ((0, 0), (GLA_RANK, N_SMALL - 2 * GLA_RANK), (0, 0)))
    a2bias = gla_a2_b.reshape(DEPTH, 2, 1, GLA_HEADS * GLA_DK)
    n0_all = state_mlstm_n.reshape(DEC_BATCH, DEPTH, 2, ML_HEADS, 1, ML_DH)

    tabs = {}
    for length in (SEQ, DEC_SEQ):
        fre, fim, g_mat, sgn = _dft_tables(length)
        fre_h, fre_l = _split2(fre)
        fim_h, fim_l = _split2(fim)
        tabs[length] = dict(spec=(fre_h, fre_l, fim_h, fim_l, sgn),
                            f=jnp.concatenate([fre_h, fim_h], axis=0), g=g_mat.astype(BF16))

    cvec = jnp.zeros((MOD_ROWS, D_MODEL), F32).at[:DEC_BATCH].set(c).at[CTX_ROW].set(c_ctx)
    mod_all = mod_table(cvec, mod_w, mod_b)

    x = jnp.concatenate([x_prompt.reshape(N_PROMPT, D_MODEL), x_sample.reshape(N_SAMPLE, D_MODEL)], axis=0)
    groups = ((SEQ, BATCH, 0, SEQ), (DEC_SEQ, DEC_BATCH, N_PROMPT // DEC_SEQ, GRID_W))
    new_gla, new_c, new_n, new_m = [], [], [], []
    for l in range(DEPTH):
        mod3 = mod_all[l].reshape(MOD_ROWS, 1, 6 * D_MODEL)
        h_hi, h_lo = normmod(x, norm1_g[l], mod3, 0, 1)
        zmain = mm_bias(h_hi, w_main, l, b_main[l].reshape(1, N_MAIN))
        zsmall = mm3_bias(h_hi, h_lo, w_small_hi, w_small_lo, l, b_small[l].reshape(1, N_SMALL))
        mi = zsmall[:, 32:40].reshape(N_ROWS, 2, ML_HEADS)
        mf = zsmall[:, 40:48].reshape(N_ROWS, 2, ML_HEADS)
        gcol = jnp.concatenate([mi, mf, jnp.zeros((N_ROWS, 4, ML_HEADS), F32)], axis=1)
        gates_col = jnp.transpose(gcol, (2, 0, 1))
        gates_row = jnp.transpose(gcol.reshape(N_ROWS // CHUNK, CHUNK, 8, ML_HEADS), (3, 0, 2, 1))
        gate_b = ml_gate_b[l].reshape(-1)
        y_hy, y_gla, y_ml = [], [], []
        for gi, (length, n_seq, blk0, seg) in enumerate(groups):
            t = tabs[length]
            sre, sim = hyena_spectrum(length, hy_w1[l], hy_b1[l], hy_w2[l], hy_b2[l], hy_w3[l],
                                      hy_freq[l], t["spec"])
            y_hy.append(hyena(zmain, hy_conv[l], sre, sim, hy_bias[l], t["f"], t["g"],
                              length, n_seq, blk0, seg))
            gla_args = (zmain, zsmall, a2f[l], a2b[l], a2bias[l], gla_norm_g[l].reshape(1, GLA_DV),
                        length, n_seq, blk0)
            ml_args = (zmain, gates_col, gates_row, gate_b, ml_conv[l], ml_norm_g[l].reshape(1, ML_DH),
                       length, n_seq, blk0, seg)
            if gi == 0:
                yg, s_fin = gla(*gla_args, None, l)
                ym, c_fin, n_fin, m_fin = mlstm(*ml_args, None, l)
                new_gla.append(s_fin)
                new_c.append(c_fin)
                new_n.append(n_fin[:, :, :, 0, :])
                new_m.append(m_fin[:, :, :, 0, 0])
            else:
                yg = gla(*gla_args, state_gla, l)
                ym = mlstm(*ml_args, (state_mlstm_C, n0_all, state_mlstm_m[:, l].reshape(-1)), l)
            y_gla.append(yg)
            y_ml.append(ym)
        merged = merge_branches(jnp.concatenate(y_hy, 0), jnp.concatenate(y_gla, 0),
                                jnp.concatenate(y_ml, 0), w_branch_b, l, zmain)
        x = mm_residual(merged, w_out_b, l, x, mod3, 2)
        h2, h2_lo, h2t = normmod(x, norm2_g[l], mod3, 3, 4, transposed=True)
        q = mm3_bias(h2, h2_lo, wq_hi, wq_lo, l, zero_b)
        nb, rk, e1, e2 = peer_topk(q, peer_keys[l])
        x = peer_experts(h2t, u_b, vt_b, l, nb, rk, e1, e2, x, mod3, 5)

    y = final_norm(x, final_g)
    y_prompt = y[:N_PROMPT].reshape(BATCH, SEQ, D_MODEL)
    y_sample = y[N_PROMPT:].reshape(DEC_BATCH, DEC_SEQ, D_MODEL)
    return (y_prompt, y_sample, jnp.stack(new_gla, axis=1), jnp.stack(new_c, axis=1),
            jnp.stack(new_n, axis=1), jnp.stack(new_m, axis=1))
```

```python
import functools
import math

import jax
import jax.numpy as jnp
import numpy as np
from jax import lax
from jax.experimental import pallas as pl
from jax.experimental.pallas import tpu as pltpu

F32 = jnp.float32
BF16 = jnp.bfloat16

D_MODEL = 2048
BATCH = 16
SEQ = 256
DEPTH = 4
DEC_BATCH = 4
DEC_SEQ = 1024
GRID_W = 64
EPS = 1e-6
CHUNK = 64
HY_WIDTH = 1024
HY_EMB = 33
HY_BANDS = (HY_EMB - 1) // 2
HY_FFN = 64
HY_DECAY_TARGET = 1e-2
HY_FAST_PCT = 0.3
HY_SLOW_PCT = 1.5
GLA_HEADS = 4
GLA_DK = 128
GLA_DV = 256
GLA_RANK = 16
GLA_NORMALIZER = 16.0
ML_HEADS = 4
ML_DH = 256
PEER_HEADS = 8
PEER_NKEYS = 128
PEER_EXPERTS = PEER_NKEYS * PEER_NKEYS
PEER_TOPK = 16

N_PROMPT = BATCH * SEQ
N_SAMPLE = DEC_BATCH * DEC_SEQ
N_ROWS = N_PROMPT + N_SAMPLE
CTX_ROW = DEC_BATCH
MOD_ROWS = 8

C_HX1, C_HX2, C_HV = 0, 1024, 2048
C_GQ, C_GK, C_GV, C_GR = 3072, 3584, 4096, 5120
C_MQ, C_MK, C_MV, C_MO = 6144, 7168, 8192, 9216
C_GA, C_GB, C_GC = 10240, 12288, 14336
N_MAIN = 16384
N_SMALL = 128

VMEM_LIMIT = 56 * 1024 * 1024


def _cp(sem):
    return pltpu.CompilerParams(dimension_semantics=sem, vmem_limit_bytes=VMEM_LIMIT)


def _dot(a, b):
    return jnp.dot(a, b, preferred_element_type=F32)


def _dot_nt(a, b):
    return lax.dot_general(a, b, (((1,), (1,)), ((), ())), preferred_element_type=F32)


def _dot_tn(a, b):
    return lax.dot_general(a, b, (((0,), (0,)), ((), ())), preferred_element_type=F32)


def _split2(a):
    hi = a.astype(BF16)
    lo = (a - hi.astype(F32)).astype(BF16)
    return hi, lo


def _split3(a):
    a1 = a.astype(BF16)
    r1 = a - a1.astype(F32)
    a2 = r1.astype(BF16)
    a3 = (r1 - a2.astype(F32)).astype(BF16)
    return a1, a2, a3


def _dot3(a, b, dot=_dot):
    ah, al = _split2(a)
    bh, bl = _split2(b)
    return dot(ah, bh) + (dot(ah, bl) + dot(al, bh))


def _log_sigmoid(x):
    return jnp.minimum(x, 0.0) - jnp.log(1.0 + jnp.exp(-jnp.abs(x)))


def _mod_row(i, tm):
    n_p = N_PROMPT // tm
    return jnp.where(i < n_p, CTX_ROW, (i - n_p) // (DEC_SEQ // tm))


def _mod_kernel(c_ref, w_ref, b_ref, o_ref):
    a = c_ref[...]
    a = a * jax.nn.sigmoid(a)
    o_ref[0] = _dot3(a, w_ref[0]) + b_ref[0]


def mod_table(cvec, mod_w, mod_b):
    tn = 1024
    n = mod_w.shape[-1]
    return pl.pallas_call(
        _mod_kernel,
        grid=(DEPTH, n // tn),
        in_specs=[pl.BlockSpec((MOD_ROWS, D_MODEL), lambda l, j: (0, 0)),
                  pl.BlockSpec((1, D_MODEL, tn), lambda l, j: (l, 0, j)),
                  pl.BlockSpec((1, 1, tn), lambda l, j: (l, 0, j))],
        out_specs=pl.BlockSpec((1, MOD_ROWS, tn), lambda l, j: (l, 0, j)),
        out_shape=jax.ShapeDtypeStruct((DEPTH, MOD_ROWS, n), F32),
        compiler_params=_cp(("parallel", "parallel")),
        name="mod_table",
    )(cvec, mod_w, mod_b.reshape(DEPTH, 1, n))


def _normmod_kernel(x_ref, g_ref, sc_ref, sh_ref, hi_ref, lo_ref, *t_ref):
    x = x_ref[...]
    y = x * lax.rsqrt(jnp.mean(x * x, axis=-1, keepdims=True) + EPS) * g_ref[...]
    h = y * (1.0 + sc_ref[...]) + sh_ref[...]
    hi = h.astype(BF16)
    hi_ref[...] = hi
    lo_ref[...] = (h - hi.astype(F32)).astype(BF16)
    if t_ref:
        t_ref[0][...] = h.T.astype(BF16)


def normmod(x, g, mod3, sh_chunk, sc_chunk, transposed=False):
    tm = 256
    out_shape = [jax.ShapeDtypeStruct((N_ROWS, D_MODEL), BF16)] * 2
    out_specs = [pl.BlockSpec((tm, D_MODEL), lambda i: (i, 0))] * 2
    if transposed:
        out_shape = out_shape + [jax.ShapeDtypeStruct((D_MODEL, N_ROWS), BF16)]
        out_specs = out_specs + [pl.BlockSpec((D_MODEL, tm), lambda i: (0, i))]
    return pl.pallas_call(
        _normmod_kernel,
        grid=(N_ROWS // tm,),
        in_specs=[pl.BlockSpec((tm, D_MODEL), lambda i: (i, 0)),
                  pl.BlockSpec((1, D_MODEL), lambda i: (0, 0)),
                  pl.BlockSpec((None, 1, D_MODEL), lambda i: (_mod_row(i, tm), 0, sc_chunk)),
                  pl.BlockSpec((None, 1, D_MODEL), lambda i: (_mod_row(i, tm), 0, sh_chunk))],
        out_specs=out_specs,
        out_shape=out_shape,
        compiler_params=_cp(("parallel",)),
        name="normmod",
    )(x, g.reshape(1, D_MODEL), mod3, mod3)


def _final_norm_kernel(x_ref, g_ref, o_ref):
    x = x_ref[...]
    o_ref[...] = x * lax.rsqrt(jnp.mean(x * x, axis=-1, keepdims=True) + EPS) * g_ref[...]


def final_norm(x, g):
    tm = 256
    return pl.pallas_call(
        _final_norm_kernel,
        grid=(N_ROWS // tm,),
        in_specs=[pl.BlockSpec((tm, D_MODEL), lambda i: (i, 0)),
                  pl.BlockSpec((1, D_MODEL), lambda i: (0, 0))],
        out_specs=pl.BlockSpec((tm, D_MODEL), lambda i: (i, 0)),
        out_shape=jax.ShapeDtypeStruct((N_ROWS, D_MODEL), F32),
        compiler_params=_cp(("parallel",)),
        name="final_norm",
    )(x, g.reshape(1, D_MODEL))


def _mm_bias_kernel(x_ref, w_ref, b_ref, o_ref):
    o_ref[...] = (_dot(x_ref[...], w_ref[...]) + b_ref[...]).astype(o_ref.dtype)


def mm_bias(x, w, layer, b, out_dtype=F32, tm=512, tn=1024):
    m, k = x.shape
    n = w.shape[-1]
    return pl.pallas_call(
        _mm_bias_kernel,
        grid=(n // tn, m // tm),
        in_specs=[pl.BlockSpec((tm, k), lambda j, i: (i, 0)),
                  pl.BlockSpec((None, k, tn), lambda j, i: (layer, 0, j)),
                  pl.BlockSpec((1, tn), lambda j, i: (0, j))],
        out_specs=pl.BlockSpec((tm, tn), lambda j, i: (i, j)),
        out_shape=jax.ShapeDtypeStruct((m, n), out_dtype),
        compiler_params=_cp(("parallel", "parallel")),
        name="mm_bias",
    )(x, w, b)


def _mm3_bias_kernel(xh_ref, xl_ref, wh_ref, wl_ref, b_ref, o_ref):
    xh = xh_ref[...]
    acc = _dot(xh, wh_ref[...]) + (_dot(xh, wl_ref[...]) + _dot(xl_ref[...], wh_ref[...]))
    o_ref[...] = acc + b_ref[...]


def mm3_bias(xh, xl, wh, wl, layer, b, tm=512, tn=1024):
    m, k = xh.shape
    n = wh.shape[-1]
    tn = min(tn, n)
    return pl.pallas_call(
        _mm3_bias_kernel,
        grid=(n // tn, m // tm),
        in_specs=[pl.BlockSpec((tm, k), lambda j, i: (i, 0)),
                  pl.BlockSpec((tm, k), lambda j, i: (i, 0)),
                  pl.BlockSpec((None, k, tn), lambda j, i: (layer, 0, j)),
                  pl.BlockSpec((None, k, tn), lambda j, i: (layer, 0, j)),
                  pl.BlockSpec((1, tn), lambda j, i: (0, j))],
        out_specs=pl.BlockSpec((tm, tn), lambda j, i: (i, j)),
        out_shape=jax.ShapeDtypeStruct((m, n), F32),
        compiler_params=_cp(("parallel", "parallel")),
        name="mm3_bias",
    )(xh, xl, wh, wl, b)


def _merge_kernel(yh_ref, yg_ref, ym_ref, w_ref, ga_ref, gb_ref, gc_ref, o_ref):
    acc = jax.nn.sigmoid(ga_ref[...]) * _dot(yh_ref[...], w_ref[0])
    acc += jax.nn.sigmoid(gb_ref[...]) * _dot(yg_ref[...], w_ref[1])
    acc += jax.nn.sigmoid(gc_ref[...]) * _dot(ym_ref[...], w_ref[2])
    o_ref[...] = acc.astype(o_ref.dtype)


def merge_branches(y_hy, y_gla, y_ml, w_branch, layer, zmain, tm=512, tn=1024):
    kb = HY_WIDTH
    y_spec = pl.BlockSpec((tm, kb), lambda j, i: (i, 0))

    def gate_spec(col):
        return pl.BlockSpec((tm, tn), lambda j, i, c=col // tn: (i, c + j))

    return pl.pallas_call(
        _merge_kernel,
        grid=(D_MODEL // tn, N_ROWS // tm),
        in_specs=[y_spec, y_spec, y_spec,
                  pl.BlockSpec((None, 3, kb, tn), lambda j, i: (layer, 0, 0, j)),
                  gate_spec(C_GA), gate_spec(C_GB), gate_spec(C_GC)],
        out_specs=pl.BlockSpec((tm, tn), lambda j, i: (i, j)),
        out_shape=jax.ShapeDtypeStruct((N_ROWS, D_MODEL), BF16),
        compiler_params=_cp(("parallel", "parallel")),
        name="merge_branches",
    )(y_hy, y_gla, y_ml, w_branch, zmain, zmain, zmain)


def _mm_resid_kernel(m_ref, w_ref, x_ref, gt_ref, o_ref):
    o_ref[...] = x_ref[...] + gt_ref[...] * _dot(m_ref[...], w_ref[...])


def mm_residual(merged, w, layer, x, mod3, gt_chunk, tm=512, tn=1024):
    k = merged.shape[1]
    return pl.pallas_call(
        _mm_resid_kernel,
        grid=(D_MODEL // tn, N_ROWS // tm),
        in_specs=[pl.BlockSpec((tm, k), lambda j, i: (i, 0)),
                  pl.BlockSpec((None, k, tn), lambda j, i: (layer, 0, j)),
                  pl.BlockSpec((tm, tn), lambda j, i: (i, j)),
                  pl.BlockSpec((None, 1, tn),
                               lambda j, i: (_mod_row(i, tm), 0, gt_chunk * (D_MODEL // tn) + j))],
        out_specs=pl.BlockSpec((tm, tn), lambda j, i: (i, j)),
        out_shape=jax.ShapeDtypeStruct((N_ROWS, D_MODEL), F32),
        compiler_params=_cp(("parallel", "parallel")),
        name="mm_residual",
    )(merged, w, x, mod3)


def _dft_tables(length):
    k = jnp.arange(length, dtype=jnp.int32)
    m = (k[:, None] * k[None, :]) % (2 * length)
    ang = m.astype(F32) * (math.pi / length)
    cos = jnp.cos(ang)
    sin = jnp.sin(ang)
    sgn = jnp.where(k % 2 == 0, 1.0, -1.0).astype(F32)
    fre = cos
    fim = jnp.where(k[:, None] == 0, sgn[None, :], -sin)
    wk = jnp.where(k == 0, 1.0, 2.0).astype(F32) / (2.0 * length)
    g_re = cos.T * wk[None, :]
    g_im = jnp.where(k[None, :] == 0, sgn[:, None] / (2.0 * length), -sin.T / length)
    return fre, fim, jnp.concatenate([g_re, g_im], axis=1), sgn


def _hyfilt_kernel(z_ref, w1_ref, b1_ref, w2_ref, b2_ref, fq_ref, w3f_ref, w3b_ref, tn_ref, dl_ref,
                   sgn_ref, freh_ref, frel_ref, fimh_ref, fiml_ref, sre_ref, sim_ref):
    hid = jnp.sin(fq_ref[0:1] * (_dot3(z_ref[...], w1_ref[...]) + b1_ref[...]))
    hid = jnp.sin(fq_ref[1:2] * (_dot3(hid, w2_ref[...]) + b2_ref[...]))
    decay = jnp.exp(-tn_ref[...] * dl_ref[...])
    fwd = _dot3(hid, w3f_ref[...]) * decay
    bwd = _dot3(hid, w3b_ref[...]) * decay
    row = lax.broadcasted_iota(jnp.int32, fwd.shape, 0)
    bwd = jnp.where(row == 0, 0.0, bwd)
    a = fwd + bwd
    d = fwd - bwd
    ah, al = _split2(a)
    dh, dl2 = _split2(d)
    re = _dot(freh_ref[...], ah) + (_dot(freh_ref[...], al) + _dot(frel_ref[...], ah))
    im = _dot(fimh_ref[...], dh) + (_dot(fimh_ref[...], dl2) + _dot(fiml_ref[...], dh))
    nyq = jnp.sum(sgn_ref[...] * a, axis=0, keepdims=True)
    sre_ref[0] = re
    sim_ref[0] = jnp.where(row == 0, nyq, im)


def hyena_spectrum(length, w1, b1, w2, b2, w3, freq, tabs):
    fre_h, fre_l, fim_h, fim_l, sgn = tabs
    t = jnp.arange(length, dtype=F32)
    t_norm = t / (length - 1)
    bands = jnp.linspace(1e-4, HY_BANDS - 1, HY_BANDS, dtype=F32)
    ang = (2.0 * math.pi / length) * t[:, None] * bands[None, :]
    z = jnp.concatenate([t_norm[:, None], jnp.cos(ang), -jnp.sin(ang),
                         jnp.zeros((length, HY_FFN - HY_EMB), F32)], axis=-1)
    w1p = jnp.pad(w1, ((0, HY_FFN - HY_EMB), (0, 0)))
    max_decay = math.log(HY_DECAY_TARGET) / HY_FAST_PCT
    min_decay = math.log(HY_DECAY_TARGET) / HY_SLOW_PCT
    deltas = jnp.abs(jnp.linspace(min_decay, max_decay, HY_WIDTH, dtype=F32)).reshape(1, HY_WIDTH)
    ct = 256
    nct = HY_WIDTH // ct
    full = lambda shape: pl.BlockSpec(shape, lambda o, j: (0,) * len(shape))
    out_spec = pl.BlockSpec((1, length, ct), lambda o, j: (o, 0, j))
    return pl.pallas_call(
        _hyfilt_kernel,
        grid=(2, nct),
        in_specs=[full((length, HY_FFN)), full((HY_FFN, HY_FFN)), full((1, HY_FFN)),
                  full((HY_FFN, HY_FFN)), full((1, HY_FFN)), full((2, HY_FFN)),
                  pl.BlockSpec((HY_FFN, ct), lambda o, j: (0, o * 2 * nct + j)),
                  pl.BlockSpec((HY_FFN, ct), lambda o, j: (0, o * 2 * nct + nct + j)),
                  full((length, 1)),
                  pl.BlockSpec((1, ct), lambda o, j: (0, j)),
                  full((length, 1)),
                  full((length, length)), full((length, length)),
                  full((length, length)), full((length, length))],
        out_specs=[out_spec, out_spec],
        out_shape=[jax.ShapeDtypeStruct((2, length, HY_WIDTH), F32)] * 2,
        compiler_params=_cp(("parallel", "parallel")),
        name="hyena_spectrum",
    )(z, w1p, b1.reshape(1, HY_FFN), w2, b2.reshape(1, HY_FFN), freq, w3, w3,
      t_norm.reshape(length, 1), deltas, sgn.reshape(length, 1), fre_h, fre_l, fim_h, fim_l)


def _short_conv(x, w, pos, seg, length):
    prev = jnp.where(pos == 0, 0.0, pltpu.roll(x, 1, 0))
    nxt = jnp.where(pos == seg - 1, 0.0, pltpu.roll(x, length - 1, 0))
    return prev * w[0:1] + x * w[1:2] + nxt * w[2:3]


def _hyena_kernel(x1_ref, x2_ref, v_ref, cw_ref, sre_ref, sim_ref, hb_ref, f_ref, g_ref, o_ref,
                  *, length, seg):
    row = lax.broadcasted_iota(jnp.int32, (length, 1), 0)
    pos = row % seg
    row0 = row == 0

    def long_conv(u, o):
        spec = _dot(f_ref[...], u.astype(BF16))
        ur, ui = spec[:length], spec[length:]
        hre, him = sre_ref[o], sim_ref[o]
        uihi = ui * him
        yr = ur * hre - jnp.where(row0, 0.0, uihi)
        yi = jnp.where(row0, uihi, ur * him + ui * hre)
        y = _dot(g_ref[...], jnp.concatenate([yr, yi], axis=0).astype(BF16))
        return y + u * hb_ref[o:o + 1]

    x1 = _short_conv(x1_ref[...], cw_ref[0], pos, seg, length)
    x2 = _short_conv(x2_ref[...], cw_ref[1], pos, seg, length)
    v = _short_conv(v_ref[...], cw_ref[2], pos, seg, length)
    z = x1 * long_conv(v, 0)
    o_ref[...] = (x2 * long_conv(z, 1)).astype(o_ref.dtype)


def hyena(zmain, conv_w, spec_re, spec_im, bias, f_mat, g_mat, length, n_seq, row_blk0, seg):
    ct = 256
    nct = HY_WIDTH // ct

    def zspec(col):
        return pl.BlockSpec((length, ct), lambda b, j, c=col // ct: (row_blk0 + b, c + j))

    return pl.pallas_call(
        functools.partial(_hyena_kernel, length=length, seg=seg),
        grid=(n_seq, nct),
        in_specs=[zspec(C_HX1), zspec(C_HX2), zspec(C_HV),
                  pl.BlockSpec((3, 3, ct), lambda b, j: (0, 0, j)),
                  pl.BlockSpec((2, length, ct), lambda b, j: (0, 0, j)),
                  pl.BlockSpec((2, length, ct), lambda b, j: (0, 0, j)),
                  pl.BlockSpec((2, ct), lambda b, j: (0, j)),
                  pl.BlockSpec((2 * length, length), lambda b, j: (0, 0)),
                  pl.BlockSpec((length, 2 * length), lambda b, j: (0, 0))],
        out_specs=pl.BlockSpec((length, ct), lambda b, j: (b, j)),
        out_shape=jax.ShapeDtypeStruct((n_seq * length, HY_WIDTH), BF16),
        compiler_params=_cp(("parallel", "parallel")),
        name="hyena",
    )(zmain, zmain, zmain, conv_w, spec_re, spec_im, bias, f_mat, g_mat)


def _tri_masks():
    t = lax.broadcasted_iota(jnp.int32, (CHUNK, CHUNK), 0)
    s = lax.broadcasted_iota(jnp.int32, (CHUNK, CHUNK), 1)
    return s <= t, s >= t


def _gla_chunk(q, k, v, g, state, mask, rev):
    tm = jnp.where(mask, 1.0, 0.0).astype(BF16)
    g1, g2, g3 = _split3(g)
    bc = _dot(tm, g1) + (_dot(tm, g2) + _dot(tm, g3))
    b_end = bc[0:1] if rev else bc[CHUNK - 1:CHUNK]
    ref = bc[CHUNK // 2:CHUNK // 2 + 1]
    inter = _dot((q * jnp.exp(bc)).astype(BF16), state.astype(BF16))
    qh = (q * jnp.exp(bc - ref)).astype(BF16)
    kh = (k * jnp.exp(ref - bc)).astype(BF16)
    att = jnp.where(mask, _dot_nt(qh, kh), 0.0)
    vb = v.astype(BF16)
    o = inter + _dot(att.astype(BF16), vb)
    ones = jnp.ones((CHUNK, GLA_DK), BF16)
    tot = _dot_tn(g1, ones) + (_dot_tn(g2, ones) + _dot_tn(g3, ones))
    e = jnp.exp(tot)
    kd = (k * jnp.exp(b_end - bc)).astype(BF16)
    new_state = jnp.concatenate([e, e], axis=1) * state + _dot_tn(kd, vb)
    return o, new_state


def _gla_kernel(*refs, length, has_init):
    if has_init:
        (q_ref, k_ref, v_ref, gr_ref, zs_ref, wf_ref, wb_ref, ab_ref, ng_ref, s0_ref,
         y_ref, o_ref, lg_ref, st_ref) = refs
    else:
        (q_ref, k_ref, v_ref, gr_ref, zs_ref, wf_ref, wb_ref, ab_ref, ng_ref,
         y_ref, sout_ref, o_ref, lg_ref, st_ref) = refs
    zs = zs_ref[...]
    lg_ref[0] = _log_sigmoid(_dot3(zs, wf_ref[...]) + ab_ref[0]) * (1.0 / GLA_NORMALIZER)
    lg_ref[1] = _log_sigmoid(_dot3(zs, wb_ref[...]) + ab_ref[1]) * (1.0 / GLA_NORMALIZER)
    if has_init:
        st_ref[...] = s0_ref[...]
    else:
        st_ref[...] = jnp.zeros_like(st_ref)
    o_ref[...] = jnp.zeros_like(o_ref)
    n = length // CHUNK
    mask_f, mask_b = _tri_masks()
    scale = GLA_DK ** -0.5

    def body(i, carry):
        for d, mask in enumerate((mask_f, mask_b)):
            c = i if d == 0 else n - 1 - i
            rows = pl.ds(pl.multiple_of(c * CHUNK, CHUNK), CHUNK)
            for h in range(GLA_HEADS):
                kc = slice(h * GLA_DK, (h + 1) * GLA_DK)
                vc = slice(h * GLA_DV, (h + 1) * GLA_DV)
                o, s_new = _gla_chunk(q_ref[rows, kc] * scale, k_ref[rows, kc], v_ref[rows, vc],
                                      lg_ref[d, rows, kc], st_ref[d, h], mask, d == 1)
                o_ref[rows, vc] += o
                st_ref[d, h] = s_new
        return carry

    lax.fori_loop(0, n, body, 0)
    for h in range(GLA_HEADS):
        vc = slice(h * GLA_DV, (h + 1) * GLA_DV)
        o = o_ref[:, vc]
        o = o * lax.rsqrt(jnp.mean(o * o, axis=-1, keepdims=True) + EPS) * ng_ref[...]
        gr = gr_ref[:, vc]
        y_ref[:, vc] = (o * (gr * jax.nn.sigmoid(gr))).astype(y_ref.dtype)
    if not has_init:
        sout_ref[...] = st_ref[...]


def gla(zmain, zsmall, wf, wb, ab, norm_g, length, n_seq, row_blk0, state0, layer):
    has_init = state0 is not None
    qk_w = GLA_HEADS * GLA_DK
    v_w = GLA_HEADS * GLA_DV
    once = pl.Buffered(1)

    def zspec(col, width):
        return pl.BlockSpec((length, width), lambda b, c=col // width: (row_blk0 + b, c), pipeline_mode=once)

    in_specs = [zspec(C_GQ, qk_w), zspec(C_GK, qk_w), zspec(C_GV, v_w), zspec(C_GR, v_w),
                pl.BlockSpec((length, N_SMALL), lambda b: (row_blk0 + b, 0)),
                pl.BlockSpec((N_SMALL, qk_w), lambda b: (0, 0)),
                pl.BlockSpec((N_SMALL, qk_w), lambda b: (0, 0)),
                pl.BlockSpec((2, 1, qk_w), lambda b: (0, 0, 0)),
                pl.BlockSpec((1, GLA_DV), lambda b: (0, 0))]
    args = [zmain, zmain, zmain, zmain, zsmall, wf, wb, ab, norm_g]
    y_shape = jax.ShapeDtypeStruct((n_seq * length, v_w), BF16)
    y_spec = pl.BlockSpec((length, v_w), lambda b: (b, 0))
    if has_init:
        in_specs.append(pl.BlockSpec((None, None, 2, GLA_HEADS, GLA_DK, GLA_DV),
                                     lambda b: (b, layer, 0, 0, 0, 0)))
        args.append(state0)
        out_shape, out_specs = y_shape, y_spec
    else:
        out_shape = [y_shape, jax.ShapeDtypeStruct((n_seq, 2, GLA_HEADS, GLA_DK, GLA_DV), F32)]
        out_specs = [y_spec, pl.BlockSpec((None, 2, GLA_HEADS, GLA_DK, GLA_DV), lambda b: (b, 0, 0, 0, 0))]
    return pl.pallas_call(
        functools.partial(_gla_kernel, length=length, has_init=has_init),
        grid=(n_seq,),
        in_specs=in_specs,
        out_specs=out_specs,
        out_shape=out_shape,
        scratch_shapes=[pltpu.VMEM((length, v_w), F32), pltpu.VMEM((2, length, qk_w), F32),
                        pltpu.VMEM((2, GLA_HEADS, GLA_DK, GLA_DV), F32)],
        compiler_params=_cp(("parallel",)),
        name="gla",
    )(*args)


def _mlstm_chunk(q, ks, v, lf_c, li_c, lf_r, li_r, cm, nv, m_prev, mask, mask_t, rev):
    t_n = CHUNK
    tm = jnp.where(mask, 1.0, 0.0).astype(BF16)
    tmt = jnp.where(mask_t, 1.0, 0.0).astype(BF16)
    c1, c2, c3 = _split3(jnp.broadcast_to(lf_c, (t_n, t_n)))
    b_colb = _dot(tm, c1) + (_dot(tm, c2) + _dot(tm, c3))
    r1, r2, r3 = _split3(jnp.broadcast_to(lf_r, (t_n, t_n)))
    b_rowb = _dot(r1, tmt) + (_dot(r2, tmt) + _dot(r3, tmt))
    b_col = b_colb[:, 0:1]
    b_row = b_rowb[0:1, :]
    b_end = b_colb[0:1, 0:1] if rev else b_colb[t_n - 1:t_n, 0:1]
    dmat = jnp.where(mask, b_colb - b_rowb + li_r, -jnp.inf)
    m_t = jnp.maximum(b_col + m_prev, jnp.max(dmat, axis=-1, keepdims=True))
    w_inter = jnp.exp(b_col + m_prev - m_t)
    qb = q.astype(BF16)
    vb = v.astype(BF16)
    sc = _dot_nt(qb, ks.astype(BF16)) * jnp.exp(dmat - m_t)
    num = w_inter * _dot(qb, cm.astype(BF16)) + _dot(sc.astype(BF16), vb)
    den = w_inter * jnp.sum(q * nv, axis=-1, keepdims=True) + jnp.sum(sc, axis=-1, keepdims=True)
    h = num / jnp.maximum(jnp.abs(den), jnp.exp(-m_t))
    g_r = b_end - b_row + li_r
    g_c = b_end - b_col + li_c
    m_new = jnp.maximum(b_end + m_prev, jnp.max(g_r, axis=-1, keepdims=True))
    w_c = jnp.exp(b_end + m_prev - m_new)
    kw = ks * jnp.exp(g_c - m_new)
    cm_new = w_c * cm + _dot_tn(kw.astype(BF16), vb)
    nv_new = w_c * nv + jnp.sum(kw, axis=0, keepdims=True)
    return h, cm_new, nv_new, m_new


def _mlstm_kernel(*refs, length, seg, has_init):
    if has_init:
        (gb_ref, m0_ref, q_ref, k_ref, v_ref, mo_ref, gc_ref, gr_ref, cw_ref, ng_ref, c0_ref, n0_ref,
         y_ref, qc_ref, kc_ref, h_ref, c_ref, n_ref, m_ref) = refs
    else:
        (gb_ref, q_ref, k_ref, v_ref, mo_ref, gc_ref, gr_ref, cw_ref, ng_ref,
         y_ref, cout_ref, nout_ref, mout_ref, qc_ref, kc_ref, h_ref, c_ref, n_ref, m_ref) = refs
    b_idx = pl.program_id(0)
    dh = ML_DH
    row = lax.broadcasted_iota(jnp.int32, (length, 1), 0)
    pos = row % seg
    qc_ref[...] = _short_conv(q_ref[...], cw_ref[0], pos, seg, length)
    kc_ref[...] = _short_conv(k_ref[...], cw_ref[1], pos, seg, length) * (dh ** -0.5)
    h_ref[...] = jnp.zeros_like(h_ref)
    if has_init:
        c_ref[...] = c0_ref[...]
        n_ref[...] = n0_ref[...]
        for d in range(2):
            for h in range(ML_HEADS):
                m_ref[d, h] = jnp.full((1, 128), m0_ref[b_idx * 2 * ML_HEADS + d * ML_HEADS + h], F32)
    else:
        c_ref[...] = jnp.zeros_like(c_ref)
        n_ref[...] = jnp.zeros_like(n_ref)
        m_ref[...] = jnp.zeros_like(m_ref)
    n = length // CHUNK
    mask_f, mask_b = _tri_masks()

    def body(i, carry):
        for d, (mask, mask_t) in enumerate(((mask_f, mask_b), (mask_b, mask_f))):
            c = i if d == 0 else n - 1 - i
            rows = pl.ds(pl.multiple_of(c * CHUNK, CHUNK), CHUNK)
            for h in range(ML_HEADS):
                cols = slice(h * dh, (h + 1) * dh)
                bi = gb_ref[d * 2 * ML_HEADS + h]
                bf = gb_ref[d * 2 * ML_HEADS + ML_HEADS + h]
                gcol = gc_ref[h, rows, :]
                grow = gr_ref[h, c]
                li_c = gcol[:, d:d + 1] + bi
                lf_c = _log_sigmoid(gcol[:, 2 + d:3 + d] + bf)
                li_r = grow[d:d + 1, :] + bi
                lf_r = _log_sigmoid(grow[2 + d:3 + d, :] + bf)
                hc, cm, nv, m_new = _mlstm_chunk(qc_ref[rows, cols], kc_ref[rows, cols], v_ref[rows, cols],
                                                 lf_c, li_c, lf_r, li_r, c_ref[d, h], n_ref[d, h],
                                                 m_ref[d, h][:, 0:1], mask, mask_t, d == 1)
                h_ref[rows, cols] += hc
                c_ref[d, h] = cm
                n_ref[d, h] = nv
                m_ref[d, h] = jnp.broadcast_to(m_new, (1, 128))
        return carry

    lax.fori_loop(0, n, body, 0)
    for h in range(ML_HEADS):
        cols = slice(h * dh, (h + 1) * dh)
        o = h_ref[:, cols]
        o = o * lax.rsqrt(jnp.mean(o * o, axis=-1, keepdims=True) + EPS) * ng_ref[...]
        y_ref[:, cols] = (o * jax.nn.sigmoid(mo_ref[:, cols])).astype(y_ref.dtype)
    if not has_init:
        cout_ref[...] = c_ref[...]
        nout_ref[...] = n_ref[...]
        mout_ref[...] = m_ref[...]


def mlstm(zmain, gates_col, gates_row, gate_b, conv_w, norm_g, length, n_seq, row_blk0, seg, init, layer):
    has_init = init is not None
    dh = ML_DH
    nh = ML_HEADS
    width = nh * dh
    once = pl.Buffered(1)

    def zspec(col):
        return pl.BlockSpec((length, width), lambda b, c=col // width: (row_blk0 + b, c), pipeline_mode=once)

    smem = pl.BlockSpec(memory_space=pltpu.SMEM)
    in_specs = [smem]
    args = [gate_b]
    if has_init:
        c0, n0, m0 = init
        in_specs.append(smem)
        args.append(m0)
    in_specs += [zspec(C_MQ), zspec(C_MK), zspec(C_MV), zspec(C_MO),
                 pl.BlockSpec((nh, length, 8), lambda b: (0, row_blk0 + b, 0)),
                 pl.BlockSpec((nh, length // CHUNK, 8, CHUNK), lambda b: (0, row_blk0 + b, 0, 0)),
                 pl.BlockSpec((2, 3, width), lambda b: (0, 0, 0)),
                 pl.BlockSpec((1, dh), lambda b: (0, 0))]
    args += [zmain, zmain, zmain, zmain, gates_col, gates_row, conv_w, norm_g]
    y_shape = jax.ShapeDtypeStruct((n_seq * length, width), BF16)
    y_spec = pl.BlockSpec((length, width), lambda b: (b, 0))
    if has_init:
        in_specs += [pl.BlockSpec((None, None, 2, nh, dh, dh), lambda b: (b, layer, 0, 0, 0, 0)),
                     pl.BlockSpec((None, None, 2, nh, 1, dh), lambda b: (b, layer, 0, 0, 0, 0))]
        args += [c0, n0]
        out_shape, out_specs = y_shape, y_spec
    else:
        out_shape = [y_shape,
                     jax.ShapeDtypeStruct((n_seq, 2, nh, dh, dh), F32),
                     jax.ShapeDtypeStruct((n_seq, 2, nh, 1, dh), F32),
                     jax.ShapeDtypeStruct((n_seq, 2, nh, 1, 128), F32)]
        out_specs = [y_spec,
                     pl.BlockSpec((None, 2, nh, dh, dh), lambda b: (b, 0, 0, 0, 0)),
                     pl.BlockSpec((None, 2, nh, 1, dh), lambda b: (b, 0, 0, 0, 0)),
                     pl.BlockSpec((None, 2, nh, 1, 128), lambda b: (b, 0, 0, 0, 0))]
    return pl.pallas_call(
        functools.partial(_mlstm_kernel, length=length, seg=seg, has_init=has_init),
        grid=(n_seq,),
        in_specs=in_specs,
        out_specs=out_specs,
        out_shape=out_shape,
        scratch_shapes=[pltpu.VMEM((length, width), F32), pltpu.VMEM((length, width), F32),
                        pltpu.VMEM((length, width), F32),
                        pltpu.VMEM((2, nh, dh, dh), F32), pltpu.VMEM((2, nh, 1, dh), F32),
                        pltpu.VMEM((2, nh, 1, 128), F32)],
        compiler_params=_cp(("parallel",)),
        name="mlstm",
    )(*args)


def _top_values(x, count, with_rank=False):
    vals = []
    cur = x
    rank = jnp.full(x.shape, float(count), F32) if with_rank else None
    for r in range(count):
        m = jnp.max(cur, axis=0, keepdims=True)
        vals.append(m)
        if with_rank or r + 1 < count:
            top = cur == m
            cur = jnp.where(top, -jnp.inf, cur)
            if with_rank:
                rank = jnp.where(top, float(r), rank)
    return (vals, rank) if with_rank else vals


def _peer_topk_kernel(q_ref, k_ref, nb_ref, rk_ref, e1_ref, e2_ref):
    dk = PEER_NKEYS
    s1 = _dot3(k_ref[0], q_ref[:, 0:dk], dot=_dot_nt)
    s2 = _dot3(k_ref[1], q_ref[:, dk:2 * dk], dot=_dot_nt)
    v1 = _top_values(s1, PEER_TOPK)
    v2, rank2 = _top_values(s2, PEER_TOPK, with_rank=True)
    rows = [v1[a] + v2[b] for a in range(PEER_TOPK) for b in range(PEER_TOPK // (a + 1))]
    rows += [jnp.full_like(rows[0], -jnp.inf)] * (-len(rows) % 8)
    cand = jnp.concatenate(rows, axis=0)
    best = _top_values(cand, PEER_TOPK)
    zsum = jnp.exp(best[0] - best[0])
    for r in range(1, PEER_TOPK):
        zsum = zsum + jnp.exp(best[r] - best[0])
    theta = best[PEER_TOPK - 1]
    nb = jnp.zeros_like(s1)
    for b in range(PEER_TOPK):
        nb = nb + jnp.where(s1 + v2[b] >= theta, 1.0, 0.0)
    nb_ref[...] = nb
    rk_ref[...] = pltpu.bitcast(rank2.astype(BF16), jnp.uint32)
    e1_ref[...] = jnp.exp(s1 - v1[0]) / zsum
    e2_ref[...] = pltpu.bitcast(jnp.exp(s2 - v2[0]).astype(BF16), jnp.uint32)


def peer_topk(q, keys, tt=256):
    n = q.shape[0]
    nk = PEER_NKEYS
    spec = pl.BlockSpec((None, nk, tt), lambda i, h: (h, 0, i))
    pspec = pl.BlockSpec((None, nk // 2, tt), lambda i, h: (h, 0, i))
    full = jax.ShapeDtypeStruct((PEER_HEADS, nk, n), F32)
    packed = jax.ShapeDtypeStruct((PEER_HEADS, nk // 2, n), jnp.uint32)
    return pl.pallas_call(
        _peer_topk_kernel,
        grid=(n // tt, PEER_HEADS),
        in_specs=[pl.BlockSpec((tt, 2 * nk), lambda i, h: (i, h)),
                  pl.BlockSpec((None, 2, nk, nk), lambda i, h: (h, 0, 0, 0))],
        out_specs=[spec, pspec, spec, pspec],
        out_shape=[full, packed, full, packed],
        compiler_params=_cp(("parallel", "parallel")),
        name="peer_topk",
    )(q, keys)


def _row_bcast(row, n):
    t = jnp.broadcast_to(row, (16, 128)).astype(BF16)
    return jnp.broadcast_to(t[None], (n // 16, 16, 128)).reshape(n, 128)


def _gelu_tanh(x):
    return 0.5 * x * (1.0 + jnp.tanh(math.sqrt(2.0 / math.pi) * (x + 0.044715 * (x * x * x))))


def _peer_expert_kernel(ht_ref, u_ref, vt_ref, nb_ref, rk_ref, e1_ref, e2_ref, x_ref, gt_ref,
                        o_ref, acc_ref, w_ref, *, n_i1, tt):
    j = pl.program_id(1)
    nk = PEER_NKEYS

    @pl.when(j == 0)
    def _():
        acc_ref[...] = jnp.zeros_like(acc_ref)

    act = _dot(u_ref[...], ht_ref[...])
    zero = jnp.zeros((), BF16)
    for ts in range(tt // 128):
        lanes = slice(ts * 128, (ts + 1) * 128)
        for r in range(n_i1):
            w = None
            for h in range(PEER_HEADS):
                nbr = _row_bcast(nb_ref[h, r:r + 1, lanes], nk)
                e1r = _row_bcast(e1_ref[h, r:r + 1, lanes], nk)
                rk = pltpu.bitcast(rk_ref[h, :, lanes], BF16)
                e2 = pltpu.bitcast(e2_ref[h, :, lanes], BF16)
                term = jnp.where(rk < nbr, e2 * e1r, zero)
                w = term if w is None else w + term
            w_ref[r * nk:(r + 1) * nk, lanes] = w
    p = w_ref[...] * _gelu_tanh(act).astype(BF16)
    acc_ref[...] += _dot(vt_ref[...], p)

    @pl.when(j == pl.num_programs(1) - 1)
    def _():
        o_ref[...] = x_ref[...] + gt_ref[...] * acc_ref[...].T


def peer_experts(h2t, u_tab, vt_tab, layer, nb, rk, e1, e2, x, mod3, gt_chunk, tt=512, ec=1024):
    n = x.shape[0]
    nk = PEER_NKEYS
    n_i1 = ec // nk
    once = pl.Buffered(1)
    sspec = pl.BlockSpec((PEER_HEADS, nk // 2, tt), lambda i, j: (0, 0, i), pipeline_mode=once)
    rspec = pl.BlockSpec((PEER_HEADS, n_i1, tt), lambda i, j: (0, j, i))
    return pl.pallas_call(
        functools.partial(_peer_expert_kernel, n_i1=n_i1, tt=tt),
        grid=(n // tt, PEER_EXPERTS // ec),
        in_specs=[pl.BlockSpec((D_MODEL, tt), lambda i, j: (0, i)),
                  pl.BlockSpec((None, ec, D_MODEL), lambda i, j: (layer, j, 0)),
                  pl.BlockSpec((None, D_MODEL, ec), lambda i, j: (layer, 0, j)),
                  rspec, sspec, rspec, sspec,
                  pl.BlockSpec((tt, D_MODEL), lambda i, j: (i, 0), pipeline_mode=once),
                  pl.BlockSpec((None, 1, D_MODEL), lambda i, j: (_mod_row(i, tt), 0, gt_chunk))],
        out_specs=pl.BlockSpec((tt, D_MODEL), lambda i, j: (i, 0)),
        out_shape=jax.ShapeDtypeStruct((n, D_MODEL), F32),
        scratch_shapes=[pltpu.VMEM((D_MODEL, tt), F32), pltpu.VMEM((ec, tt), BF16)],
        compiler_params=_cp(("parallel", "arbitrary")),
        name="peer_experts",
    )(h2t, u_tab, vt_tab, nb, rk, e1, e2, x, mod3)


def _reorder_in_proj(w_in, b_in):
    splits = (1024, 1024, 1024, 512, 512, 1024, 1024, 16, 16, 1024, 1024, 1024, 1024, 8, 8, 2048, 2048, 2048)
    offs = np.concatenate([[0], np.cumsum(splits)])
    seg = lambda a, i: a[..., offs[i]:offs[i + 1]]
    main_ids = (0, 1, 2, 3, 4, 5, 6, 9, 10, 11, 12, 15, 16, 17)
    small_ids = (7, 8, 13, 14)
    w_main = jnp.concatenate([seg(w_in, i).astype(BF16) for i in main_ids], axis=-1)
    b_main = jnp.concatenate([seg(b_in, i) for i in main_ids], axis=-1)
    w_small = jnp.concatenate([seg(w_in, i) for i in small_ids], axis=-1)
    b_small = jnp.concatenate([seg(b_in, i) for i in small_ids], axis=-1)
    pad = N_SMALL - w_small.shape[-1]
    w_small = jnp.pad(w_small, ((0, 0), (0, 0), (0, pad)))
    b_small = jnp.pad(b_small, ((0, 0), (0, pad)))
    return w_main, b_main, w_small, b_small


def kernel(x_prompt, x_sample, c, state_gla, state_mlstm_C, state_mlstm_n, state_mlstm_m, c_ctx,
           mod_w, mod_b, norm1_g, norm2_g, final_g, w_in, b_in, hy_conv, hy_w1, hy_b1, hy_w2, hy_b2,
           hy_w3, hy_freq, hy_bias, gla_a2_w, gla_a2_b, gla_norm_g, ml_conv, ml_gate_b, ml_norm_g,
           w_branch, w_out, peer_wq, peer_keys, peer_u, peer_v):
    w_main, b_main, w_small, b_small = _reorder_in_proj(w_in, b_in)
    w_small_hi = w_small.astype(BF16)
    w_small_lo = (w_small - w_small_hi.astype(F32)).astype(BF16)
    w_branch_b = w_branch.astype(BF16)
    w_out_b = w_out.astype(BF16)
    wq_hi = peer_wq.astype(BF16)
    wq_lo = (peer_wq - wq_hi.astype(F32)).astype(BF16)
    u_b = peer_u.astype(BF16)
    vt_b = jnp.swapaxes(peer_v.astype(BF16), 1, 2)
    zero_b = jnp.zeros((1, D_MODEL), F32)
    a2f = jnp.pad(gla_a2_w[:, 0], ((0, 0), (0, N_SMALL - GLA_RANK), (0, 0)))
    a2b = jnp.pad(gla_a2_w[:, 1], ((0, 0), (GLA_RANK, N_SMALL - 2 * GLA_RANK), (0, 0)))
    a2bias = gla_a2_b.reshape(DEPTH, 2, 1, GLA_HEADS * GLA_DK)
    n0_all = state_mlstm_n.reshape(DEC_BATCH, DEPTH, 2, ML_HEADS, 1, ML_DH)

    tabs = {}
    for length in (SEQ, DEC_SEQ):
        fre, fim, g_mat, sgn = _dft_tables(length)
        fre_h, fre_l = _split2(fre)
        fim_h, fim_l = _split2(fim)
        tabs[length] = dict(spec=(fre_h, fre_l, fim_h, fim_l, sgn),
                            f=jnp.concatenate([fre_h, fim_h], axis=0), g=g_mat.astype(BF16))

    cvec = jnp.zeros((MOD_ROWS, D_MODEL), F32).at[:DEC_BATCH].set(c).at[CTX_ROW].set(c_ctx)
    mod_all = mod_table(cvec, mod_w, mod_b)

    x = jnp.concatenate([x_prompt.reshape(N_PROMPT, D_MODEL), x_sample.reshape(N_SAMPLE, D_MODEL)], axis=0)
    groups = ((SEQ, BATCH, 0, SEQ), (DEC_SEQ, DEC_BATCH, N_PROMPT // DEC_SEQ, GRID_W))
    new_gla, new_c, new_n, new_m = [], [], [], []
    for l in range(DEPTH):
        mod3 = mod_all[l].reshape(MOD_ROWS, 1, 6 * D_MODEL)
        h_hi, h_lo = normmod(x, norm1_g[l], mod3, 0, 1)
        zmain = mm_bias(h_hi, w_main, l, b_main[l].reshape(1, N_MAIN))
        zsmall = mm3_bias(h_hi, h_lo, w_small_hi, w_small_lo, l, b_small[l].reshape(1, N_SMALL))
        mi = zsmall[:, 32:40].reshape(N_ROWS, 2, ML_HEADS)
        mf = zsmall[:, 40:48].reshape(N_ROWS, 2, ML_HEADS)
        gcol = jnp.concatenate([mi, mf, jnp.zeros((N_ROWS, 4, ML_HEADS), F32)], axis=1)
        gates_col = jnp.transpose(gcol, (2, 0, 1))
        gates_row = jnp.transpose(gcol.reshape(N_ROWS // CHUNK, CHUNK, 8, ML_HEADS), (3, 0, 2, 1))
        gate_b = ml_gate_b[l].reshape(-1)
        y_hy, y_gla, y_ml = [], [], []
        for gi, (length, n_seq, blk0, seg) in enumerate(groups):
            t = tabs[length]
            sre, sim = hyena_spectrum(length, hy_w1[l], hy_b1[l], hy_w2[l], hy_b2[l], hy_w3[l],
                                      hy_freq[l], t["spec"])
            y_hy.append(hyena(zmain, hy_conv[l], sre, sim, hy_bias[l], t["f"], t["g"],
                              length, n_seq, blk0, seg))
            gla_args = (zmain, zsmall, a2f[l], a2b[l], a2bias[l], gla_norm_g[l].reshape(1, GLA_DV),
                        length, n_seq, blk0)
            ml_args = (zmain, gates_col, gates_row, gate_b, ml_conv[l], ml_norm_g[l].reshape(1, ML_DH),
                       length, n_seq, blk0, seg)
            if gi == 0:
                yg, s_fin = gla(*gla_args, None, l)
                ym, c_fin, n_fin, m_fin = mlstm(*ml_args, None, l)
                new_gla.append(s_fin)
                new_c.append(c_fin)
                new_n.append(n_fin[:, :, :, 0, :])
                new_m.append(m_fin[:, :, :, 0, 0])
            else:
                yg = gla(*gla_args, state_gla, l)
                ym = mlstm(*ml_args, (state_mlstm_C, n0_all, state_mlstm_m[:, l].reshape(-1)), l)
            y_gla.append(yg)
            y_ml.append(ym)
        merged = merge_branches(jnp.concatenate(y_hy, 0), jnp.concatenate(y_gla, 0),
                                jnp.concatenate(y_ml, 0), w_branch_b, l, zmain)
        x = mm_residual(merged, w_out_b, l, x, mod3, 2)
        h2, h2_lo, h2t = normmod(x, norm2_g[l], mod3, 3, 4, transposed=True)
        q = mm3_bias(h2, h2_lo, wq_hi, wq_lo, l, zero_b)
        nb, rk, e1, e2 = peer_topk(q, peer_keys[l])
        x = peer_experts(h2t, u_b, vt_b, l, nb, rk, e1, e2, x, mod3, 5)

    y = final_norm(x, final_g)
    y_prompt = y[:N_PROMPT].reshape(BATCH, SEQ, D_MODEL)
    y_sample = y[N_PROMPT:].reshape(DEC_BATCH, DEC_SEQ, D_MODEL)
    return (y_prompt, y_sample, jnp.stack(new_gla, axis=1), jnp.stack(new_c, axis=1),
            jnp.stack(new_n, axis=1), jnp.stack(new_m, axis=1))
```

```python
import functools
import math

import jax
import jax.numpy as jnp
import numpy as np
from jax import lax
from jax.experimental import pallas as pl
from jax.experimental.pallas import tpu as pltpu

F32 = jnp.float32
BF16 = jnp.bfloat16

D_MODEL = 2048
BATCH = 16
SEQ = 256
DEPTH = 4
DEC_BATCH = 4
DEC_SEQ = 1024
GRID_W = 64
EPS = 1e-6
CHUNK = 64
HY_WIDTH = 1024
HY_EMB = 33
HY_BANDS = (HY_EMB - 1) // 2
HY_FFN = 64
HY_DECAY_TARGET = 1e-2
HY_FAST_PCT = 0.3
HY_SLOW_PCT = 1.5
GLA_HEADS = 4
GLA_DK = 128
GLA_DV = 256
GLA_RANK = 16
GLA_NORMALIZER = 16.0
ML_HEADS = 4
ML_DH = 256
PEER_HEADS = 8
PEER_NKEYS = 128
PEER_EXPERTS = PEER_NKEYS * PEER_NKEYS
PEER_TOPK = 16
PEER_CHUNK = 1024

N_PROMPT = BATCH * SEQ
N_SAMPLE = DEC_BATCH * DEC_SEQ
N_ROWS = N_PROMPT + N_SAMPLE
CTX_ROW = DEC_BATCH
MOD_ROWS = 8

C_HX1, C_HX2, C_HV = 0, 1024, 2048
C_GQ, C_GK, C_GV, C_GR = 3072, 3584, 4096, 5120
C_MQ, C_MK, C_MV, C_MO = 6144, 7168, 8192, 9216
C_GA, C_GB, C_GC = 10240, 12288, 14336
N_MAIN = 16384
N_SMALL = 128

VMEM_LIMIT = 56 * 1024 * 1024


def _cp(sem):
    return pltpu.CompilerParams(dimension_semantics=sem, vmem_limit_bytes=VMEM_LIMIT)


def _dot(a, b):
    return jnp.dot(a, b, preferred_element_type=F32)


def _dot_nt(a, b):
    return lax.dot_general(a, b, (((1,), (1,)), ((), ())), preferred_element_type=F32)


def _dot_tn(a, b):
    return lax.dot_general(a, b, (((0,), (0,)), ((), ())), preferred_element_type=F32)


def _split2(a):
    hi = a.astype(BF16)
    lo = (a - hi.astype(F32)).astype(BF16)
    return hi, lo


def _split3(a):
    a1 = a.astype(BF16)
    r1 = a - a1.astype(F32)
    a2 = r1.astype(BF16)
    a3 = (r1 - a2.astype(F32)).astype(BF16)
    return a1, a2, a3


def _dot3(a, b, dot=_dot):
    ah, al = _split2(a)
    bh, bl = _split2(b)
    return dot(ah, bh) + (dot(ah, bl) + dot(al, bh))


def _log_sigmoid(x):
    return jnp.minimum(x, 0.0) - jnp.log(1.0 + jnp.exp(-jnp.abs(x)))


def _mod_row(i, tm):
    n_p = N_PROMPT // tm
    return jnp.where(i < n_p, CTX_ROW, (i - n_p) // (DEC_SEQ // tm))


def _mod_kernel(c_ref, w_ref, b_ref, o_ref):
    a = c_ref[...]
    a = a * jax.nn.sigmoid(a)
    o_ref[0] = _dot3(a, w_ref[0]) + b_ref[0]


def mod_table(cvec, mod_w, mod_b):
    tn = 1024
    n = mod_w.shape[-1]
    return pl.pallas_call(
        _mod_kernel,
        grid=(DEPTH, n // tn),
        in_specs=[pl.BlockSpec((MOD_ROWS, D_MODEL), lambda l, j: (0, 0)),
                  pl.BlockSpec((1, D_MODEL, tn), lambda l, j: (l, 0, j)),
                  pl.BlockSpec((1, 1, tn), lambda l, j: (l, 0, j))],
        out_specs=pl.BlockSpec((1, MOD_ROWS, tn), lambda l, j: (l, 0, j)),
        out_shape=jax.ShapeDtypeStruct((DEPTH, MOD_ROWS, n), F32),
        compiler_params=_cp(("parallel", "parallel")),
        name="mod_table",
    )(cvec, mod_w, mod_b.reshape(DEPTH, 1, n))


def _normmod_kernel(x_ref, g_ref, sc_ref, sh_ref, hi_ref, lo_ref, *t_ref):
    x = x_ref[...]
    y = x * lax.rsqrt(jnp.mean(x * x, axis=-1, keepdims=True) + EPS) * g_ref[...]
    h = y * (1.0 + sc_ref[...]) + sh_ref[...]
    hi = h.astype(BF16)
    hi_ref[...] = hi
    lo_ref[...] = (h - hi.astype(F32)).astype(BF16)
    if t_ref:
        t_ref[0][...] = h.T.astype(BF16)


def normmod(x, g, mod3, sh_chunk, sc_chunk, transposed=False):
    tm = 256
    out_shape = [jax.ShapeDtypeStruct((N_ROWS, D_MODEL), BF16)] * 2
    out_specs = [pl.BlockSpec((tm, D_MODEL), lambda i: (i, 0))] * 2
    if transposed:
        out_shape = out_shape + [jax.ShapeDtypeStruct((D_MODEL, N_ROWS), BF16)]
        out_specs = out_specs + [pl.BlockSpec((D_MODEL, tm), lambda i: (0, i))]
    return pl.pallas_call(
        _normmod_kernel,
        grid=(N_ROWS // tm,),
        in_specs=[pl.BlockSpec((tm, D_MODEL), lambda i: (i, 0)),
                  pl.BlockSpec((1, D_MODEL), lambda i: (0, 0)),
                  pl.BlockSpec((None, 1, D_MODEL), lambda i: (_mod_row(i, tm), 0, sc_chunk)),
                  pl.BlockSpec((None, 1, D_MODEL), lambda i: (_mod_row(i, tm), 0, sh_chunk))],
        out_specs=out_specs,
        out_shape=out_shape,
        compiler_params=_cp(("parallel",)),
        name="normmod",
    )(x, g.reshape(1, D_MODEL), mod3, mod3)


def _final_norm_kernel(x_ref, g_ref, o_ref):
    x = x_ref[...]
    o_ref[...] = x * lax.rsqrt(jnp.mean(x * x, axis=-1, keepdims=True) + EPS) * g_ref[...]


def final_norm(x, g):
    tm = 256
    return pl.pallas_call(
        _final_norm_kernel,
        grid=(N_ROWS // tm,),
        in_specs=[pl.BlockSpec((tm, D_MODEL), lambda i: (i, 0)),
                  pl.BlockSpec((1, D_MODEL), lambda i: (0, 0))],
        out_specs=pl.BlockSpec((tm, D_MODEL), lambda i: (i, 0)),
        out_shape=jax.ShapeDtypeStruct((N_ROWS, D_MODEL), F32),
        compiler_params=_cp(("parallel",)),
        name="final_norm",
    )(x, g.reshape(1, D_MODEL))


def _mm_bias_kernel(x_ref, w_ref, b_ref, o_ref):
    o_ref[...] = (_dot(x_ref[...], w_ref[...]) + b_ref[...]).astype(o_ref.dtype)


def mm_bias(x, w, layer, b, out_dtype=F32, tm=512, tn=1024):
    m, k = x.shape
    n = w.shape[-1]
    return pl.pallas_call(
        _mm_bias_kernel,
        grid=(n // tn, m // tm),
        in_specs=[pl.BlockSpec((tm, k), lambda j, i: (i, 0)),
                  pl.BlockSpec((None, k, tn), lambda j, i: (layer, 0, j)),
                  pl.BlockSpec((1, tn), lambda j, i: (0, j))],
        out_specs=pl.BlockSpec((tm, tn), lambda j, i: (i, j)),
        out_shape=jax.ShapeDtypeStruct((m, n), out_dtype),
        compiler_params=_cp(("parallel", "parallel")),
        name="mm_bias",
    )(x, w, b)


def _mm3_bias_kernel(xh_ref, xl_ref, wh_ref, wl_ref, b_ref, o_ref):
    xh = xh_ref[...]
    acc = _dot(xh, wh_ref[...]) + (_dot(xh, wl_ref[...]) + _dot(xl_ref[...], wh_ref[...]))
    o_ref[...] = acc + b_ref[...]


def mm3_bias(xh, xl, wh, wl, layer, b, tm=512, tn=1024):
    m, k = xh.shape
    n = wh.shape[-1]
    tn = min(tn, n)
    return pl.pallas_call(
        _mm3_bias_kernel,
        grid=(n // tn, m // tm),
        in_specs=[pl.BlockSpec((tm, k), lambda j, i: (i, 0)),
                  pl.BlockSpec((tm, k), lambda j, i: (i, 0)),
                  pl.BlockSpec((None, k, tn), lambda j, i: (layer, 0, j)),
                  pl.BlockSpec((None, k, tn), lambda j, i: (layer, 0, j)),
                  pl.BlockSpec((1, tn), lambda j, i: (0, j))],
        out_specs=pl.BlockSpec((tm, tn), lambda j, i: (i, j)),
        out_shape=jax.ShapeDtypeStruct((m, n), F32),
        compiler_params=_cp(("parallel", "parallel")),
        name="mm3_bias",
    )(xh, xl, wh, wl, b)


def _merge_kernel(yh_ref, yg_ref, ym_ref, w_ref, ga_ref, gb_ref, gc_ref, o_ref):
    acc = jax.nn.sigmoid(ga_ref[...]) * _dot(yh_ref[...], w_ref[0])
    acc += jax.nn.sigmoid(gb_ref[...]) * _dot(yg_ref[...], w_ref[1])
    acc += jax.nn.sigmoid(gc_ref[...]) * _dot(ym_ref[...], w_ref[2])
    o_ref[...] = acc.astype(o_ref.dtype)


def merge_branches(y_hy, y_gla, y_ml, w_branch, layer, zmain, tm=512, tn=1024):
    kb = HY_WIDTH
    y_spec = pl.BlockSpec((tm, kb), lambda j, i: (i, 0))

    def gate_spec(col):
        return pl.BlockSpec((tm, tn), lambda j, i, c=col // tn: (i, c + j))

    return pl.pallas_call(
        _merge_kernel,
        grid=(D_MODEL // tn, N_ROWS // tm),
        in_specs=[y_spec, y_spec, y_spec,
                  pl.BlockSpec((None, 3, kb, tn), lambda j, i: (layer, 0, 0, j)),
                  gate_spec(C_GA), gate_spec(C_GB), gate_spec(C_GC)],
        out_specs=pl.BlockSpec((tm, tn), lambda j, i: (i, j)),
        out_shape=jax.ShapeDtypeStruct((N_ROWS, D_MODEL), BF16),
        compiler_params=_cp(("parallel", "parallel")),
        name="merge_branches",
    )(y_hy, y_gla, y_ml, w_branch, zmain, zmain, zmain)


def _mm_resid_kernel(m_ref, w_ref, x_ref, gt_ref, o_ref):
    o_ref[...] = x_ref[...] + gt_ref[...] * _dot(m_ref[...], w_ref[...])


def mm_residual(merged, w, layer, x, mod3, gt_chunk, tm=512, tn=1024):
    k = merged.shape[1]
    return pl.pallas_call(
        _mm_resid_kernel,
        grid=(D_MODEL // tn, N_ROWS // tm),
        in_specs=[pl.BlockSpec((tm, k), lambda j, i: (i, 0)),
                  pl.BlockSpec((None, k, tn), lambda j, i: (layer, 0, j)),
                  pl.BlockSpec((tm, tn), lambda j, i: (i, j)),
                  pl.BlockSpec((None, 1, tn),
                               lambda j, i: (_mod_row(i, tm), 0, gt_chunk * (D_MODEL // tn) + j))],
        out_specs=pl.BlockSpec((tm, tn), lambda j, i: (i, j)),
        out_shape=jax.ShapeDtypeStruct((N_ROWS, D_MODEL), F32),
        compiler_params=_cp(("parallel", "parallel")),
        name="mm_residual",
    )(merged, w, x, mod3)


def _dft_tables(length):
    k = jnp.arange(length, dtype=jnp.int32)
    m = (k[:, None] * k[None, :]) % (2 * length)
    ang = m.astype(F32) * (math.pi / length)
    cos = jnp.cos(ang)
    sin = jnp.sin(ang)
    sgn = jnp.where(k % 2 == 0, 1.0, -1.0).astype(F32)
    fre = cos
    fim = jnp.where(k[:, None] == 0, sgn[None, :], -sin)
    wk = jnp.where(k == 0, 1.0, 2.0).astype(F32) / (2.0 * length)
    g_re = cos.T * wk[None, :]
    g_im = jnp.where(k[None, :] == 0, sgn[:, None] / (2.0 * length), -sin.T / length)
    return fre, fim, jnp.concatenate([g_re, g_im], axis=1), sgn


def _hyfilt_kernel(z_ref, w1_ref, b1_ref, w2_ref, b2_ref, fq_ref, w3f_ref, w3b_ref, tn_ref, dl_ref,
                   sgn_ref, freh_ref, frel_ref, fimh_ref, fiml_ref, sre_ref, sim_ref):
    hid = jnp.sin(fq_ref[0:1] * (_dot3(z_ref[...], w1_ref[...]) + b1_ref[...]))
    hid = jnp.sin(fq_ref[1:2] * (_dot3(hid, w2_ref[...]) + b2_ref[...]))
    decay = jnp.exp(-tn_ref[...] * dl_ref[...])
    fwd = _dot3(hid, w3f_ref[...]) * decay
    bwd = _dot3(hid, w3b_ref[...]) * decay
    row = lax.broadcasted_iota(jnp.int32, fwd.shape, 0)
    bwd = jnp.where(row == 0, 0.0, bwd)
    a = fwd + bwd
    d = fwd - bwd
    ah, al = _split2(a)
    dh, dl2 = _split2(d)
    re = _dot(freh_ref[...], ah) + (_dot(freh_ref[...], al) + _dot(frel_ref[...], ah))
    im = _dot(fimh_ref[...], dh) + (_dot(fimh_ref[...], dl2) + _dot(fiml_ref[...], dh))
    nyq = jnp.sum(sgn_ref[...] * a, axis=0, keepdims=True)
    sre_ref[0] = re
    sim_ref[0] = jnp.where(row == 0, nyq, im)


def hyena_spectrum(length, w1, b1, w2, b2, w3, freq, tabs):
    fre_h, fre_l, fim_h, fim_l, sgn = tabs
    t = jnp.arange(length, dtype=F32)
    t_norm = t / (length - 1)
    bands = jnp.linspace(1e-4, HY_BANDS - 1, HY_BANDS, dtype=F32)
    ang = (2.0 * math.pi / length) * t[:, None] * bands[None, :]
    z = jnp.concatenate([t_norm[:, None], jnp.cos(ang), -jnp.sin(ang),
                         jnp.zeros((length, HY_FFN - HY_EMB), F32)], axis=-1)
    w1p = jnp.pad(w1, ((0, HY_FFN - HY_EMB), (0, 0)))
    max_decay = math.log(HY_DECAY_TARGET) / HY_FAST_PCT
    min_decay = math.log(HY_DECAY_TARGET) / HY_SLOW_PCT
    deltas = jnp.abs(jnp.linspace(min_decay, max_decay, HY_WIDTH, dtype=F32)).reshape(1, HY_WIDTH)
    ct = 256
    nct = HY_WIDTH // ct
    full = lambda shape: pl.BlockSpec(shape, lambda o, j: (0,) * len(shape))
    out_spec = pl.BlockSpec((1, length, ct), lambda o, j: (o, 0, j))
    return pl.pallas_call(
        _hyfilt_kernel,
        grid=(2, nct),
        in_specs=[full((length, HY_FFN)), full((HY_FFN, HY_FFN)), full((1, HY_FFN)),
                  full((HY_FFN, HY_FFN)), full((1, HY_FFN)), full((2, HY_FFN)),
                  pl.BlockSpec((HY_FFN, ct), lambda o, j: (0, o * 2 * nct + j)),
                  pl.BlockSpec((HY_FFN, ct), lambda o, j: (0, o * 2 * nct + nct + j)),
                  full((length, 1)),
                  pl.BlockSpec((1, ct), lambda o, j: (0, j)),
                  full((length, 1)),
                  full((length, length)), full((length, length)),
                  full((length, length)), full((length, length))],
        out_specs=[out_spec, out_spec],
        out_shape=[jax.ShapeDtypeStruct((2, length, HY_WIDTH), F32)] * 2,
        compiler_params=_cp(("parallel", "parallel")),
        name="hyena_spectrum",
    )(z, w1p, b1.reshape(1, HY_FFN), w2, b2.reshape(1, HY_FFN), freq, w3, w3,
      t_norm.reshape(length, 1), deltas, sgn.reshape(length, 1), fre_h, fre_l, fim_h, fim_l)


def _short_conv(x, w, pos, seg, length):
    prev = jnp.where(pos == 0, 0.0, pltpu.roll(x, 1, 0))
    nxt = jnp.where(pos == seg - 1, 0.0, pltpu.roll(x, length - 1, 0))
    return prev * w[0:1] + x * w[1:2] + nxt * w[2:3]


def _hyena_kernel(x1_ref, x2_ref, v_ref, cw_ref, sre_ref, sim_ref, hb_ref, f_ref, g_ref, o_ref,
                  *, length, seg):
    row = lax.broadcasted_iota(jnp.int32, (length, 1), 0)
    pos = row % seg
    row0 = row == 0

    def long_conv(u, o):
        spec = _dot(f_ref[...], u.astype(BF16))
        ur, ui = spec[:length], spec[length:]
        hre, him = sre_ref[o], sim_ref[o]
        uihi = ui * him
        yr = ur * hre - jnp.where(row0, 0.0, uihi)
        yi = jnp.where(row0, uihi, ur * him + ui * hre)
        y = _dot(g_ref[...], jnp.concatenate([yr, yi], axis=0).astype(BF16))
        return y + u * hb_ref[o:o + 1]

    x1 = _short_conv(x1_ref[...], cw_ref[0], pos, seg, length)
    x2 = _short_conv(x2_ref[...], cw_ref[1], pos, seg, length)
    v = _short_conv(v_ref[...], cw_ref[2], pos, seg, length)
    z = x1 * long_conv(v, 0)
    o_ref[...] = (x2 * long_conv(z, 1)).astype(o_ref.dtype)


def hyena(zmain, conv_w, spec_re, spec_im, bias, f_mat, g_mat, length, n_seq, row_blk0, seg):
    ct = 256
    nct = HY_WIDTH // ct

    def zspec(col):
        return pl.BlockSpec((length, ct), lambda b, j, c=col // ct: (row_blk0 + b, c + j))

    return pl.pallas_call(
        functools.partial(_hyena_kernel, length=length, seg=seg),
        grid=(n_seq, nct),
        in_specs=[zspec(C_HX1), zspec(C_HX2), zspec(C_HV),
                  pl.BlockSpec((3, 3, ct), lambda b, j: (0, 0, j)),
                  pl.BlockSpec((2, length, ct), lambda b, j: (0, 0, j)),
                  pl.BlockSpec((2, length, ct), lambda b, j: (0, 0, j)),
                  pl.BlockSpec((2, ct), lambda b, j: (0, j)),
                  pl.BlockSpec((2 * length, length), lambda b, j: (0, 0)),
                  pl.BlockSpec((length, 2 * length), lambda b, j: (0, 0))],
        out_specs=pl.BlockSpec((length, ct), lambda b, j: (b, j)),
        out_shape=jax.ShapeDtypeStruct((n_seq * length, HY_WIDTH), BF16),
        compiler_params=_cp(("parallel", "parallel")),
        name="hyena",
    )(zmain, zmain, zmain, conv_w, spec_re, spec_im, bias, f_mat, g_mat)


def _tri_masks():
    t = lax.broadcasted_iota(jnp.int32, (CHUNK, CHUNK), 0)
    s = lax.broadcasted_iota(jnp.int32, (CHUNK, CHUNK), 1)
    return s <= t, s >= t


def _gla_chunk(q, k, v, g, state, mask, rev):
    tm = jnp.where(mask, 1.0, 0.0).astype(BF16)
    g1, g2, g3 = _split3(g)
    bc = _dot(tm, g1) + (_dot(tm, g2) + _dot(tm, g3))
    b_end = bc[0:1] if rev else bc[CHUNK - 1:CHUNK]
    ref = bc[CHUNK // 2:CHUNK // 2 + 1]
    inter = _dot((q * jnp.exp(bc)).astype(BF16), state.astype(BF16))
    qh = (q * jnp.exp(bc - ref)).astype(BF16)
    kh = (k * jnp.exp(ref - bc)).astype(BF16)
    att = jnp.where(mask, _dot_nt(qh, kh), 0.0)
    vb = v.astype(BF16)
    o = inter + _dot(att.astype(BF16), vb)
    ones = jnp.ones((CHUNK, GLA_DK), BF16)
    tot = _dot_tn(g1, ones) + (_dot_tn(g2, ones) + _dot_tn(g3, ones))
    e = jnp.exp(tot)
    kd = (k * jnp.exp(b_end - bc)).astype(BF16)
    new_state = jnp.concatenate([e, e], axis=1) * state + _dot_tn(kd, vb)
    return o, new_state


def _gla_kernel(*refs, length, has_init):
    if has_init:
        (q_ref, k_ref, v_ref, gr_ref, zs_ref, wf_ref, wb_ref, ab_ref, ng_ref, s0_ref,
         y_ref, o_ref, lg_ref, st_ref) = refs
    else:
        (q_ref, k_ref, v_ref, gr_ref, zs_ref, wf_ref, wb_ref, ab_ref, ng_ref,
         y_ref, sout_ref, o_ref, lg_ref, st_ref) = refs
    zs = zs_ref[...]
    lg_ref[0] = _log_sigmoid(_dot3(zs, wf_ref[...]) + ab_ref[0]) * (1.0 / GLA_NORMALIZER)
    lg_ref[1] = _log_sigmoid(_dot3(zs, wb_ref[...]) + ab_ref[1]) * (1.0 / GLA_NORMALIZER)
    if has_init:
        st_ref[...] = s0_ref[...]
    else:
        st_ref[...] = jnp.zeros_like(st_ref)
    o_ref[...] = jnp.zeros_like(o_ref)
    n = length // CHUNK
    mask_f, mask_b = _tri_masks()
    scale = GLA_DK ** -0.5

    def body(i, carry):
        for d, mask in enumerate((mask_f, mask_b)):
            c = i if d == 0 else n - 1 - i
            rows = pl.ds(pl.multiple_of(c * CHUNK, CHUNK), CHUNK)
            for h in range(GLA_HEADS):
                kc = slice(h * GLA_DK, (h + 1) * GLA_DK)
                vc = slice(h * GLA_DV, (h + 1) * GLA_DV)
                o, s_new = _gla_chunk(q_ref[rows, kc] * scale, k_ref[rows, kc], v_ref[rows, vc],
                                      lg_ref[d, rows, kc], st_ref[d, h], mask, d == 1)
                o_ref[rows, vc] += o
                st_ref[d, h] = s_new
        return carry

    lax.fori_loop(0, n, body, 0)
    for h in range(GLA_HEADS):
        vc = slice(h * GLA_DV, (h + 1) * GLA_DV)
        o = o_ref[:, vc]
        o = o * lax.rsqrt(jnp.mean(o * o, axis=-1, keepdims=True) + EPS) * ng_ref[...]
        gr = gr_ref[:, vc]
        y_ref[:, vc] = (o * (gr * jax.nn.sigmoid(gr))).astype(y_ref.dtype)
    if not has_init:
        sout_ref[...] = st_ref[...]


def gla(zmain, zsmall, wf, wb, ab, norm_g, length, n_seq, row_blk0, state0, layer):
    has_init = state0 is not None
    qk_w = GLA_HEADS * GLA_DK
    v_w = GLA_HEADS * GLA_DV
    once = pl.Buffered(1)

    def zspec(col, width):
        return pl.BlockSpec((length, width), lambda b, c=col // width: (row_blk0 + b, c), pipeline_mode=once)

    in_specs = [zspec(C_GQ, qk_w), zspec(C_GK, qk_w), zspec(C_GV, v_w), zspec(C_GR, v_w),
                pl.BlockSpec((length, N_SMALL), lambda b: (row_blk0 + b, 0)),
                pl.BlockSpec((N_SMALL, qk_w), lambda b: (0, 0)),
                pl.BlockSpec((N_SMALL, qk_w), lambda b: (0, 0)),
                pl.BlockSpec((2, 1, qk_w), lambda b: (0, 0, 0)),
                pl.BlockSpec((1, GLA_DV), lambda b: (0, 0))]
    args = [zmain, zmain, zmain, zmain, zsmall, wf, wb, ab, norm_g]
    y_shape = jax.ShapeDtypeStruct((n_seq * length, v_w), BF16)
    y_spec = pl.BlockSpec((length, v_w), lambda b: (b, 0))
    if has_init:
        in_specs.append(pl.BlockSpec((None, None, 2, GLA_HEADS, GLA_DK, GLA_DV),
                                     lambda b: (b, layer, 0, 0, 0, 0)))
        args.append(state0)
        out_shape, out_specs = y_shape, y_spec
    else:
        out_shape = [y_shape, jax.ShapeDtypeStruct((n_seq, 2, GLA_HEADS, GLA_DK, GLA_DV), F32)]
        out_specs = [y_spec, pl.BlockSpec((None, 2, GLA_HEADS, GLA_DK, GLA_DV), lambda b: (b, 0, 0, 0, 0))]
    return pl.pallas_call(
        functools.partial(_gla_kernel, length=length, has_init=has_init),
        grid=(n_seq,),
        in_specs=in_specs,
        out_specs=out_specs,
        out_shape=out_shape,
        scratch_shapes=[pltpu.VMEM((length, v_w), F32), pltpu.VMEM((2, length, qk_w), F32),
                        pltpu.VMEM((2, GLA_HEADS, GLA_DK, GLA_DV), F32)],
        compiler_params=_cp(("parallel",)),
        name="gla",
    )(*args)


def _mlstm_chunk(q, ks, v, lf_c, li_c, lf_r, li_r, cm, nv, m_prev, mask, mask_t, rev):
    t_n = CHUNK
    tm = jnp.where(mask, 1.0, 0.0).astype(BF16)
    tmt = jnp.where(mask_t, 1.0, 0.0).astype(BF16)
    c1, c2, c3 = _split3(jnp.broadcast_to(lf_c, (t_n, t_n)))
    b_colb = _dot(tm, c1) + (_dot(tm, c2) + _dot(tm, c3))
    r1, r2, r3 = _split3(jnp.broadcast_to(lf_r, (t_n, t_n)))
    b_rowb = _dot(r1, tmt) + (_dot(r2, tmt) + _dot(r3, tmt))
    b_col = b_colb[:, 0:1]
    b_row = b_rowb[0:1, :]
    b_end = b_colb[0:1, 0:1] if rev else b_colb[t_n - 1:t_n, 0:1]
    dmat = jnp.where(mask, b_colb - b_rowb + li_r, -jnp.inf)
    m_t = jnp.maximum(b_col + m_prev, jnp.max(dmat, axis=-1, keepdims=True))
    w_inter = jnp.exp(b_col + m_prev - m_t)
    qb = q.astype(BF16)
    vb = v.astype(BF16)
    sc = _dot_nt(qb, ks.astype(BF16)) * jnp.exp(dmat - m_t)
    num = w_inter * _dot(qb, cm.astype(BF16)) + _dot(sc.astype(BF16), vb)
    den = w_inter * jnp.sum(q * nv, axis=-1, keepdims=True) + jnp.sum(sc, axis=-1, keepdims=True)
    h = num / jnp.maximum(jnp.abs(den), jnp.exp(-m_t))
    g_r = b_end - b_row + li_r
    g_c = b_end - b_col + li_c
    m_new = jnp.maximum(b_end + m_prev, jnp.max(g_r, axis=-1, keepdims=True))
    w_c = jnp.exp(b_end + m_prev - m_new)
    kw = ks * jnp.exp(g_c - m_new)
    cm_new = w_c * cm + _dot_tn(kw.astype(BF16), vb)
    nv_new = w_c * nv + jnp.sum(kw, axis=0, keepdims=True)
    return h, cm_new, nv_new, m_new


def _mlstm_kernel(*refs, length, seg, has_init):
    if has_init:
        (gb_ref, m0_ref, q_ref, k_ref, v_ref, mo_ref, gc_ref, gr_ref, cw_ref, ng_ref, c0_ref, n0_ref,
         y_ref, qc_ref, kc_ref, h_ref, c_ref, n_ref, m_ref) = refs
    else:
        (gb_ref, q_ref, k_ref, v_ref, mo_ref, gc_ref, gr_ref, cw_ref, ng_ref,
         y_ref, cout_ref, nout_ref, mout_ref, qc_ref, kc_ref, h_ref, c_ref, n_ref, m_ref) = refs
    b_idx = pl.program_id(0)
    dh = ML_DH
    row = lax.broadcasted_iota(jnp.int32, (length, 1), 0)
    pos = row % seg
    qc_ref[...] = _short_conv(q_ref[...], cw_ref[0], pos, seg, length)
    kc_ref[...] = _short_conv(k_ref[...], cw_ref[1], pos, seg, length) * (dh ** -0.5)
    h_ref[...] = jnp.zeros_like(h_ref)
    if has_init:
        c_ref[...] = c0_ref[...]
        n_ref[...] = n0_ref[...]
        for d in range(2):
            for h in range(ML_HEADS):
                m_ref[d, h] = jnp.full((1, 128), m0_ref[b_idx * 2 * ML_HEADS + d * ML_HEADS + h], F32)
    else:
        c_ref[...] = jnp.zeros_like(c_ref)
        n_ref[...] = jnp.zeros_like(n_ref)
        m_ref[...] = jnp.zeros_like(m_ref)
    n = length // CHUNK
    mask_f, mask_b = _tri_masks()

    def body(i, carry):
        for d, (mask, mask_t) in enumerate(((mask_f, mask_b), (mask_b, mask_f))):
            c = i if d == 0 else n - 1 - i
            rows = pl.ds(pl.multiple_of(c * CHUNK, CHUNK), CHUNK)
            for h in range(ML_HEADS):
                cols = slice(h * dh, (h + 1) * dh)
                bi = gb_ref[d * 2 * ML_HEADS + h]
                bf = gb_ref[d * 2 * ML_HEADS + ML_HEADS + h]
                gcol = gc_ref[h, rows, :]
                grow = gr_ref[h, c]
                li_c = gcol[:, d:d + 1] + bi
                lf_c = _log_sigmoid(gcol[:, 2 + d:3 + d] + bf)
                li_r = grow[d:d + 1, :] + bi
                lf_r = _log_sigmoid(grow[2 + d:3 + d, :] + bf)
                hc, cm, nv, m_new = _mlstm_chunk(qc_ref[rows, cols], kc_ref[rows, cols], v_ref[rows, cols],
                                                 lf_c, li_c, lf_r, li_r, c_ref[d, h], n_ref[d, h],
                                                 m_ref[d, h][:, 0:1], mask, mask_t, d == 1)
                h_ref[rows, cols] += hc
                c_ref[d, h] = cm
                n_ref[d, h] = nv
                m_ref[d, h] = jnp.broadcast_to(m_new, (1, 128))
        return carry

    lax.fori_loop(0, n, body, 0)
    for h in range(ML_HEADS):
        cols = slice(h * dh, (h + 1) * dh)
        o = h_ref[:, cols]
        o = o * lax.rsqrt(jnp.mean(o * o, axis=-1, keepdims=True) + EPS) * ng_ref[...]
        y_ref[:, cols] = (o * jax.nn.sigmoid(mo_ref[:, cols])).astype(y_ref.dtype)
    if not has_init:
        cout_ref[...] = c_ref[...]
        nout_ref[...] = n_ref[...]
        mout_ref[...] = m_ref[...]


def mlstm(zmain, gates_col, gates_row, gate_b, conv_w, norm_g, length, n_seq, row_blk0, seg, init, layer):
    has_init = init is not None
    dh = ML_DH
    nh = ML_HEADS
    width = nh * dh
    once = pl.Buffered(1)

    def zspec(col):
        return pl.BlockSpec((length, width), lambda b, c=col // width: (row_blk0 + b, c), pipeline_mode=once)

    smem = pl.BlockSpec(memory_space=pltpu.SMEM)
    in_specs = [smem]
    args = [gate_b]
    if has_init:
        c0, n0, m0 = init
        in_specs.append(smem)
        args.append(m0)
    in_specs += [zspec(C_MQ), zspec(C_MK), zspec(C_MV), zspec(C_MO),
                 pl.BlockSpec((nh, length, 8), lambda b: (0, row_blk0 + b, 0)),
                 pl.BlockSpec((nh, length // CHUNK, 8, CHUNK), lambda b: (0, row_blk0 + b, 0, 0)),
                 pl.BlockSpec((2, 3, width), lambda b: (0, 0, 0)),
                 pl.BlockSpec((1, dh), lambda b: (0, 0))]
    args += [zmain, zmain, zmain, zmain, gates_col, gates_row, conv_w, norm_g]
    y_shape = jax.ShapeDtypeStruct((n_seq * length, width), BF16)
    y_spec = pl.BlockSpec((length, width), lambda b: (b, 0))
    if has_init:
        in_specs += [pl.BlockSpec((None, None, 2, nh, dh, dh), lambda b: (b, layer, 0, 0, 0, 0)),
                     pl.BlockSpec((None, None, 2, nh, 1, dh), lambda b: (b, layer, 0, 0, 0, 0))]
        args += [c0, n0]
        out_shape, out_specs = y_shape, y_spec
    else:
        out_shape = [y_shape,
                     jax.ShapeDtypeStruct((n_seq, 2, nh, dh, dh), F32),
                     jax.ShapeDtypeStruct((n_seq, 2, nh, 1, dh), F32),
                     jax.ShapeDtypeStruct((n_seq, 2, nh, 1, 128), F32)]
        out_specs = [y_spec,
                     pl.BlockSpec((None, 2, nh, dh, dh), lambda b: (b, 0, 0, 0, 0)),
                     pl.BlockSpec((None, 2, nh, 1, dh), lambda b: (b, 0, 0, 0, 0)),
                     pl.BlockSpec((None, 2, nh, 1, 128), lambda b: (b, 0, 0, 0, 0))]
    return pl.pallas_call(
        functools.partial(_mlstm_kernel, length=length, seg=seg, has_init=has_init),
        grid=(n_seq,),
        in_specs=in_specs,
        out_specs=out_specs,
        out_shape=out_shape,
        scratch_shapes=[pltpu.VMEM((length, width), F32), pltpu.VMEM((length, width), F32),
                        pltpu.VMEM((length, width), F32),
                        pltpu.VMEM((2, nh, dh, dh), F32), pltpu.VMEM((2, nh, 1, dh), F32),
                        pltpu.VMEM((2, nh, 1, 128), F32)],
        compiler_params=_cp(("parallel",)),
        name="mlstm",
    )(*args)


def _top_values(x, count, with_rank=False):
    vals = []
    cur = x
    rank = jnp.full(x.shape, float(count), F32) if with_rank else None
    for r in range(count):
        m = jnp.max(cur, axis=0, keepdims=True)
        vals.append(m)
        if with_rank or r + 1 < count:
            top = cur == m
            cur = jnp.where(top, -jnp.inf, cur)
            if with_rank:
                rank = jnp.where(top, float(r), rank)
    return (vals, rank) if with_rank else vals


def _peer_topk_kernel(q_ref, k_ref, nb_ref, rk_ref, e1_ref, e2_ref):
    dk = PEER_NKEYS
    s1 = _dot3(k_ref[0], q_ref[:, 0:dk], dot=_dot_nt)
    s2 = _dot3(k_ref[1], q_ref[:, dk:2 * dk], dot=_dot_nt)
    v1 = _top_values(s1, PEER_TOPK)
    v2, rank2 = _top_values(s2, PEER_TOPK, with_rank=True)
    rows = [v1[a] + v2[b] for a in range(PEER_TOPK) for b in range(PEER_TOPK // (a + 1))]
    rows += [jnp.full_like(rows[0], -jnp.inf)] * (-len(rows) % 8)
    cand = jnp.concatenate(rows, axis=0)
    best = _top_values(cand, PEER_TOPK)
    zsum = jnp.exp(best[0] - best[0])
    for r in range(1, PEER_TOPK):
        zsum = zsum + jnp.exp(best[r] - best[0])
    theta = best[PEER_TOPK - 1]
    nb = jnp.zeros_like(s1)
    for b in range(PEER_TOPK):
        nb = nb + jnp.where(s1 + v2[b] >= theta, 1.0, 0.0)
    nb_ref[...] = nb
    rk_ref[...] = pltpu.bitcast(rank2.astype(BF16), jnp.uint32)
    e1_ref[...] = jnp.exp(s1 - v1[0]) / zsum
    e2_ref[...] = pltpu.bitcast(jnp.exp(s2 - v2[0]).astype(BF16), jnp.uint32)


def peer_topk(q, keys, tt=256):
    n = q.shape[0]
    nk = PEER_NKEYS
    spec = pl.BlockSpec((None, nk, tt), lambda i, h: (h, 0, i))
    pspec = pl.BlockSpec((None, nk // 2, tt), lambda i, h: (h, 0, i))
    full = jax.ShapeDtypeStruct((PEER_HEADS, nk, n), F32)
    packed = jax.ShapeDtypeStruct((PEER_HEADS, nk // 2, n), jnp.uint32)
    return pl.pallas_call(
        _peer_topk_kernel,
        grid=(n // tt, PEER_HEADS),
        in_specs=[pl.BlockSpec((tt, 2 * nk), lambda i, h: (i, h)),
                  pl.BlockSpec((None, 2, nk, nk), lambda i, h: (h, 0, 0, 0))],
        out_specs=[spec, pspec, spec, pspec],
        out_shape=[full, packed, full, packed],
        compiler_params=_cp(("parallel", "parallel")),
        name="peer_topk",
    )(q, keys)


def _row_bcast(row, n):
    t = jnp.broadcast_to(row, (16, 128)).astype(BF16)
    return jnp.broadcast_to(t[None], (n // 16, 16, 128)).reshape(n, 128)


def _gelu_tanh(x):
    return 0.5 * x * (1.0 + jnp.tanh(math.sqrt(2.0 / math.pi) * (x + 0.044715 * (x * x * x))))


def _peer_expert_kernel(ht_ref, u_ref, vt_ref, nb_ref, rk_ref, e1_ref, e2_ref, x_ref, gt_ref,
                        o_ref, acc_ref, w_ref, *, n_i1, tt):
    j = pl.program_id(1)
    nk = PEER_NKEYS

    @pl.when(j == 0)
    def _():
        acc_ref[...] = jnp.zeros_like(acc_ref)

    act = _dot(u_ref[...], ht_ref[...])
    zero = jnp.zeros((), BF16)
    for ts in range(tt // 128):
        lanes = slice(ts * 128, (ts + 1) * 128)
        for r in range(n_i1):
            w = None
            for h in range(PEER_HEADS):
                nbr = _row_bcast(nb_ref[h, r:r + 1, lanes], nk)
                e1r = _row_bcast(e1_ref[h, r:r + 1, lanes], nk)
                rk = pltpu.bitcast(rk_ref[h, :, lanes], BF16)
                e2 = pltpu.bitcast(e2_ref[h, :, lanes], BF16)
                term = jnp.where(rk < nbr, e2 * e1r, zero)
                w = term if w is None else w + term
            w_ref[r * nk:(r + 1) * nk, lanes] = w
    p = w_ref[...] * _gelu_tanh(act).astype(BF16)
    acc_ref[...] += _dot(vt_ref[...], p)

    @pl.when(j == pl.num_programs(1) - 1)
    def _():
        o_ref[...] = x_ref[...] + gt_ref[...] * acc_ref[...].T


def _xpose_cast_kernel(x_ref, o_ref):
    o_ref[...] = x_ref[...].T.astype(BF16)


def transpose_cast_chunks(tab, ec):
    depth, e, d = tab.shape
    return pl.pallas_call(
        _xpose_cast_kernel,
        grid=(depth, e // ec),
        in_specs=[pl.BlockSpec((None, ec, d), lambda l, c: (l, c, 0))],
        out_specs=pl.BlockSpec((None, None, d, ec), lambda l, c: (l, c, 0, 0)),
        out_shape=jax.ShapeDtypeStruct((depth, e // ec, d, ec), BF16),
        compiler_params=_cp(("parallel", "parallel")),
        name="transpose_cast_chunks",
    )(tab)


def peer_experts(h2t, u_tab, vt_tab, layer, nb, rk, e1, e2, x, mod3, gt_chunk, tt=512):
    ec = PEER_CHUNK
    n = x.shape[0]
    nk = PEER_NKEYS
    n_i1 = ec // nk
    once = pl.Buffered(1)
    sspec = pl.BlockSpec((PEER_HEADS, nk // 2, tt), lambda i, j: (0, 0, i), pipeline_mode=once)
    rspec = pl.BlockSpec((PEER_HEADS, n_i1, tt), lambda i, j: (0, j, i))
    return pl.pallas_call(
        functools.partial(_peer_expert_kernel, n_i1=n_i1, tt=tt),
        grid=(n // tt, PEER_EXPERTS // ec),
        in_specs=[pl.BlockSpec((D_MODEL, tt), lambda i, j: (0, i)),
                  pl.BlockSpec((None, ec, D_MODEL), lambda i, j: (layer, j, 0)),
                  pl.BlockSpec((None, None, D_MODEL, ec), lambda i, j: (layer, j, 0, 0)),
                  rspec, sspec, rspec, sspec,
                  pl.BlockSpec((tt, D_MODEL), lambda i, j: (i, 0), pipeline_mode=once),
                  pl.BlockSpec((None, 1, D_MODEL), lambda i, j: (_mod_row(i, tt), 0, gt_chunk))],
        out_specs=pl.BlockSpec((tt, D_MODEL), lambda i, j: (i, 0)),
        out_shape=jax.ShapeDtypeStruct((n, D_MODEL), F32),
        scratch_shapes=[pltpu.VMEM((D_MODEL, tt), F32), pltpu.VMEM((ec, tt), BF16)],
        compiler_params=_cp(("parallel", "arbitrary")),
        name="peer_experts",
    )(h2t, u_tab, vt_tab, nb, rk, e1, e2, x, mod3)


def _reorder_in_proj(w_in, b_in):
    splits = (1024, 1024, 1024, 512, 512, 1024, 1024, 16, 16, 1024, 1024, 1024, 1024, 8, 8, 2048, 2048, 2048)
    offs = np.concatenate([[0], np.cumsum(splits)])
    seg = lambda a, i: a[..., offs[i]:offs[i + 1]]
    main_ids = (0, 1, 2, 3, 4, 5, 6, 9, 10, 11, 12, 15, 16, 17)
    small_ids = (7, 8, 13, 14)
    w_main = jnp.concatenate([seg(w_in, i).astype(BF16) for i in main_ids], axis=-1)
    b_main = jnp.concatenate([seg(b_in, i) for i in main_ids], axis=-1)
    w_small = jnp.concatenate([seg(w_in, i) for i in small_ids], axis=-1)
    b_small = jnp.concatenate([seg(b_in, i) for i in small_ids], axis=-1)
    pad = N_SMALL - w_small.shape[-1]
    w_small = jnp.pad(w_small, ((0, 0), (0, 0), (0, pad)))
    b_small = jnp.pad(b_small, ((0, 0), (0, pad)))
    return w_main, b_main, w_small, b_small


def kernel(x_prompt, x_sample, c, state_gla, state_mlstm_C, state_mlstm_n, state_mlstm_m, c_ctx,
           mod_w, mod_b, norm1_g, norm2_g, final_g, w_in, b_in, hy_conv, hy_w1, hy_b1, hy_w2, hy_b2,
           hy_w3, hy_freq, hy_bias, gla_a2_w, gla_a2_b, gla_norm_g, ml_conv, ml_gate_b, ml_norm_g,
           w_branch, w_out, peer_wq, peer_keys, peer_u, peer_v):
    w_main, b_main, w_small, b_small = _reorder_in_proj(w_in, b_in)
    w_small_hi = w_small.astype(BF16)
    w_small_lo = (w_small - w_small_hi.astype(F32)).astype(BF16)
    w_branch_b = w_branch.astype(BF16)
    w_out_b = w_out.astype(BF16)
    wq_hi = peer_wq.astype(BF16)
    wq_lo = (peer_wq - wq_hi.astype(F32)).astype(BF16)
    u_b = peer_u.astype(BF16)
    vt_b = transpose_cast_chunks(peer_v, PEER_CHUNK)
    zero_b = jnp.zeros((1, D_MODEL), F32)
    a2f = jnp.pad(gla_a2_w[:, 0], ((0, 0), (0, N_SMALL - GLA_RANK), (0, 0)))
    a2b = jnp.pad(gla_a2_w[:, 1], ((0, 0), (GLA_RANK, N_SMALL - 2 * GLA_RANK), (0, 0)))
    a2bias = gla_a2_b.reshape(DEPTH, 2, 1, GLA_HEADS * GLA_DK)
    n0_all = state_mlstm_n.reshape(DEC_BATCH, DEPTH, 2, ML_HEADS, 1, ML_DH)

    tabs = {}
    for length in (SEQ, DEC_SEQ):
        fre, fim, g_mat, sgn = _dft_tables(length)
        fre_h, fre_l = _split2(fre)
        fim_h, fim_l = _split2(fim)
        tabs[length] = dict(spec=(fre_h, fre_l, fim_h, fim_l, sgn),
                            f=jnp.concatenate([fre_h, fim_h], axis=0), g=g_mat.astype(BF16))

    cvec = jnp.zeros((MOD_ROWS, D_MODEL), F32).at[:DEC_BATCH].set(c).at[CTX_ROW].set(c_ctx)
    mod_all = mod_table(cvec, mod_w, mod_b)

    x = jnp.concatenate([x_prompt.reshape(N_PROMPT, D_MODEL), x_sample.reshape(N_SAMPLE, D_MODEL)], axis=0)
    groups = ((SEQ, BATCH, 0, SEQ), (DEC_SEQ, DEC_BATCH, N_PROMPT // DEC_SEQ, GRID_W))
    new_gla, new_c, new_n, new_m = [], [], [], []
    for l in range(DEPTH):
        mod3 = mod_all[l].reshape(MOD_ROWS, 1, 6 * D_MODEL)
        h_hi, h_lo = normmod(x, norm1_g[l], mod3, 0, 1)
        zmain = mm_bias(h_hi, w_main, l, b_main[l].reshape(1, N_MAIN), tm=1024, tn=2048)
        zsmall = mm3_bias(h_hi, h_lo, w_small_hi, w_small_lo, l, b_small[l].reshape(1, N_SMALL))
        mi = zsmall[:, 32:40].reshape(N_ROWS, 2, ML_HEADS)
        mf = zsmall[:, 40:48].reshape(N_ROWS, 2, ML_HEADS)
        gcol = jnp.concatenate([mi, mf, jnp.zeros((N_ROWS, 4, ML_HEADS), F32)], axis=1)
        gates_col = jnp.transpose(gcol, (2, 0, 1))
        gates_row = jnp.transpose(gcol.reshape(N_ROWS // CHUNK, CHUNK, 8, ML_HEADS), (3, 0, 2, 1))
        gate_b = ml_gate_b[l].reshape(-1)
        y_hy, y_gla, y_ml = [], [], []
        for gi, (length, n_seq, blk0, seg) in enumerate(groups):
            t = tabs[length]
            sre, sim = hyena_spectrum(length, hy_w1[l], hy_b1[l], hy_w2[l], hy_b2[l], hy_w3[l],
                                      hy_freq[l], t["spec"])
            y_hy.append(hyena(zmain, hy_conv[l], sre, sim, hy_bias[l], t["f"], t["g"],
                              length, n_seq, blk0, seg))
            gla_args = (zmain, zsmall, a2f[l], a2b[l], a2bias[l], gla_norm_g[l].reshape(1, GLA_DV),
                        length, n_seq, blk0)
            ml_args = (zmain, gates_col, gates_row, gate_b, ml_conv[l], ml_norm_g[l].reshape(1, ML_DH),
                       length, n_seq, blk0, seg)
            if gi == 0:
                yg, s_fin = gla(*gla_args, None, l)
                ym, c_fin, n_fin, m_fin = mlstm(*ml_args, None, l)
                new_gla.append(s_fin)
                new_c.append(c_fin)
                new_n.append(n_fin[:, :, :, 0, :])
                new_m.append(m_fin[:, :, :, 0, 0])
            else:
                yg = gla(*gla_args, state_gla, l)
                ym = mlstm(*ml_args, (state_mlstm_C, n0_all, state_mlstm_m[:, l].reshape(-1)), l)
            y_gla.append(yg)
            y_ml.append(ym)
        merged = merge_branches(jnp.concatenate(y_hy, 0), jnp.concatenate(y_gla, 0),
                                jnp.concatenate(y_ml, 0), w_branch_b, l, zmain)
        x = mm_residual(merged, w_out_b, l, x, mod3, 2)
        h2, h2_lo, h2t = normmod(x, norm2_g[l], mod3, 3, 4, transposed=True)
        q = mm3_bias(h2, h2_lo, wq_hi, wq_lo, l, zero_b)
        nb, rk, e1, e2 = peer_topk(q, peer_keys[l])
        x = peer_experts(h2t, u_b, vt_b, l, nb, rk, e1, e2, x, mod3, 5)

    y = final_norm(x, final_g)
    y_prompt = y[:N_PROMPT].reshape(BATCH, SEQ, D_MODEL)
    y_sample = y[N_PROMPT:].reshape(DEC_BATCH, DEC_SEQ, D_MODEL)
    return (y_prompt, y_sample, jnp.stack(new_gla, axis=1), jnp.stack(new_c, axis=1),
            jnp.stack(new_n, axis=1), jnp.stack(new_m, axis=1))
```

```python
import functools
import math

import jax
import jax.numpy as jnp
import numpy as np
from jax import lax
from jax.experimental import pallas as pl
from jax.experimental.pallas import tpu as pltpu

F32 = jnp.float32
BF16 = jnp.bfloat16

D_MODEL = 2048
BATCH = 16
SEQ = 256
DEPTH = 4
DEC_BATCH = 4
DEC_SEQ = 1024
GRID_W = 64
EPS = 1e-6
CHUNK = 64
HY_WIDTH = 1024
HY_EMB = 33
HY_BANDS = (HY_EMB - 1) // 2
HY_FFN = 64
HY_DECAY_TARGET = 1e-2
HY_FAST_PCT = 0.3
HY_SLOW_PCT = 1.5
GLA_HEADS = 4
GLA_DK = 128
GLA_DV = 256
GLA_RANK = 16
GLA_NORMALIZER = 16.0
ML_HEADS = 4
ML_DH = 256
PEER_HEADS = 8
PEER_NKEYS = 128
PEER_EXPERTS = PEER_NKEYS * PEER_NKEYS
PEER_TOPK = 16
PEER_CHUNK = 1024
PROMPT_SEQS_PER_STEP = 2

N_PROMPT = BATCH * SEQ
N_SAMPLE = DEC_BATCH * DEC_SEQ
N_ROWS = N_PROMPT + N_SAMPLE
CTX_ROW = DEC_BATCH
MOD_ROWS = 8

C_HX1, C_HX2, C_HV = 0, 1024, 2048
C_GQ, C_GK, C_GV, C_GR = 3072, 3584, 4096, 5120
C_MQ, C_MK, C_MV, C_MO = 6144, 7168, 8192, 9216
C_GA, C_GB, C_GC = 10240, 12288, 14336
N_MAIN = 16384
N_SMALL = 128

VMEM_LIMIT = 56 * 1024 * 1024


def _cp(sem):
    return pltpu.CompilerParams(dimension_semantics=sem, vmem_limit_bytes=VMEM_LIMIT)


def _dot(a, b):
    return jnp.dot(a, b, preferred_element_type=F32)


def _dot_nt(a, b):
    return lax.dot_general(a, b, (((1,), (1,)), ((), ())), preferred_element_type=F32)


def _dot_tn(a, b):
    return lax.dot_general(a, b, (((0,), (0,)), ((), ())), preferred_element_type=F32)


def _split2(a):
    hi = a.astype(BF16)
    lo = (a - hi.astype(F32)).astype(BF16)
    return hi, lo


def _split3(a):
    a1 = a.astype(BF16)
    r1 = a - a1.astype(F32)
    a2 = r1.astype(BF16)
    a3 = (r1 - a2.astype(F32)).astype(BF16)
    return a1, a2, a3


def _dot3(a, b, dot=_dot):
    ah, al = _split2(a)
    bh, bl = _split2(b)
    return dot(ah, bh) + (dot(ah, bl) + dot(al, bh))


def _log_sigmoid(x):
    return jnp.minimum(x, 0.0) - jnp.log(1.0 + jnp.exp(-jnp.abs(x)))


def _mod_row(i, tm):
    n_p = N_PROMPT // tm
    return jnp.where(i < n_p, CTX_ROW, (i - n_p) // (DEC_SEQ // tm))


def _mod_kernel(c_ref, w_ref, b_ref, o_ref):
    a = c_ref[...]
    a = a * jax.nn.sigmoid(a)
    o_ref[0] = _dot3(a, w_ref[0]) + b_ref[0]


def mod_table(cvec, mod_w, mod_b):
    tn = 1024
    n = mod_w.shape[-1]
    return pl.pallas_call(
        _mod_kernel,
        grid=(DEPTH, n // tn),
        in_specs=[pl.BlockSpec((MOD_ROWS, D_MODEL), lambda l, j: (0, 0)),
                  pl.BlockSpec((1, D_MODEL, tn), lambda l, j: (l, 0, j)),
                  pl.BlockSpec((1, 1, tn), lambda l, j: (l, 0, j))],
        out_specs=pl.BlockSpec((1, MOD_ROWS, tn), lambda l, j: (l, 0, j)),
        out_shape=jax.ShapeDtypeStruct((DEPTH, MOD_ROWS, n), F32),
        compiler_params=_cp(("parallel", "parallel")),
        name="mod_table",
    )(cvec, mod_w, mod_b.reshape(DEPTH, 1, n))


def _normmod_kernel(x_ref, g_ref, sc_ref, sh_ref, hi_ref, lo_ref, *t_ref):
    x = x_ref[...]
    y = x * lax.rsqrt(jnp.mean(x * x, axis=-1, keepdims=True) + EPS) * g_ref[...]
    h = y * (1.0 + sc_ref[...]) + sh_ref[...]
    hi = h.astype(BF16)
    hi_ref[...] = hi
    lo_ref[...] = (h - hi.astype(F32)).astype(BF16)
    if t_ref:
        t_ref[0][...] = h.T.astype(BF16)


def normmod(x, g, mod3, sh_chunk, sc_chunk, transposed=False):
    tm = 256
    out_shape = [jax.ShapeDtypeStruct((N_ROWS, D_MODEL), BF16)] * 2
    out_specs = [pl.BlockSpec((tm, D_MODEL), lambda i: (i, 0))] * 2
    if transposed:
        out_shape = out_shape + [jax.ShapeDtypeStruct((D_MODEL, N_ROWS), BF16)]
        out_specs = out_specs + [pl.BlockSpec((D_MODEL, tm), lambda i: (0, i))]
    return pl.pallas_call(
        _normmod_kernel,
        grid=(N_ROWS // tm,),
        in_specs=[pl.BlockSpec((tm, D_MODEL), lambda i: (i, 0)),
                  pl.BlockSpec((1, D_MODEL), lambda i: (0, 0)),
                  pl.BlockSpec((None, 1, D_MODEL), lambda i: (_mod_row(i, tm), 0, sc_chunk)),
                  pl.BlockSpec((None, 1, D_MODEL), lambda i: (_mod_row(i, tm), 0, sh_chunk))],
        out_specs=out_specs,
        out_shape=out_shape,
        compiler_params=_cp(("parallel",)),
        name="normmod",
    )(x, g.reshape(1, D_MODEL), mod3, mod3)


def _final_norm_kernel(x_ref, g_ref, o_ref):
    x = x_ref[...]
    o_ref[...] = x * lax.rsqrt(jnp.mean(x * x, axis=-1, keepdims=True) + EPS) * g_ref[...]


def final_norm(x, g, row0, n_rows):
    tm = 256
    return pl.pallas_call(
        _final_norm_kernel,
        grid=(n_rows // tm,),
        in_specs=[pl.BlockSpec((tm, D_MODEL), lambda i: (row0 // tm + i, 0)),
                  pl.BlockSpec((1, D_MODEL), lambda i: (0, 0))],
        out_specs=pl.BlockSpec((tm, D_MODEL), lambda i: (i, 0)),
        out_shape=jax.ShapeDtypeStruct((n_rows, D_MODEL), F32),
        compiler_params=_cp(("parallel",)),
        name="final_norm",
    )(x, g.reshape(1, D_MODEL))


def _mm_bias_kernel(x_ref, w_ref, b_ref, o_ref):
    o_ref[...] = (_dot(x_ref[...], w_ref[...]) + b_ref[...]).astype(o_ref.dtype)


def mm_bias(x, w, layer, b, out_dtype=F32, tm=512, tn=1024):
    m, k = x.shape
    n = w.shape[-1]
    return pl.pallas_call(
        _mm_bias_kernel,
        grid=(n // tn, m // tm),
        in_specs=[pl.BlockSpec((tm, k), lambda j, i: (i, 0)),
                  pl.BlockSpec((None, k, tn), lambda j, i: (layer, 0, j)),
                  pl.BlockSpec((1, tn), lambda j, i: (0, j))],
        out_specs=pl.BlockSpec((tm, tn), lambda j, i: (i, j)),
        out_shape=jax.ShapeDtypeStruct((m, n), out_dtype),
        compiler_params=_cp(("parallel", "parallel")),
        name="mm_bias",
    )(x, w, b)


def _mm3_bias_kernel(xh_ref, xl_ref, wh_ref, wl_ref, b_ref, o_ref):
    xh = xh_ref[...]
    acc = _dot(xh, wh_ref[...]) + (_dot(xh, wl_ref[...]) + _dot(xl_ref[...], wh_ref[...]))
    o_ref[...] = acc + b_ref[...]


def mm3_bias(xh, xl, wh, wl, layer, b, tm=512, tn=1024):
    m, k = xh.shape
    n = wh.shape[-1]
    tn = min(tn, n)
    return pl.pallas_call(
        _mm3_bias_kernel,
        grid=(n // tn, m // tm),
        in_specs=[pl.BlockSpec((tm, k), lambda j, i: (i, 0)),
                  pl.BlockSpec((tm, k), lambda j, i: (i, 0)),
                  pl.BlockSpec((None, k, tn), lambda j, i: (layer, 0, j)),
                  pl.BlockSpec((None, k, tn), lambda j, i: (layer, 0, j)),
                  pl.BlockSpec((1, tn), lambda j, i: (0, j))],
        out_specs=pl.BlockSpec((tm, tn), lambda j, i: (i, j)),
        out_shape=jax.ShapeDtypeStruct((m, n), F32),
        compiler_params=_cp(("parallel", "parallel")),
        name="mm3_bias",
    )(xh, xl, wh, wl, b)


def _merge_kernel(yhp_ref, yhs_ref, ygp_ref, ygs_ref, ymp_ref, yms_ref, w_ref, ga_ref, gb_ref, gc_ref, o_ref,
                  *, n_prompt_tiles):
    is_prompt = pl.program_id(1) < n_prompt_tiles
    pick = lambda p_ref, s_ref: jnp.where(is_prompt, p_ref[...], s_ref[...])
    acc = jax.nn.sigmoid(ga_ref[...]) * _dot(pick(yhp_ref, yhs_ref), w_ref[0])
    acc += jax.nn.sigmoid(gb_ref[...]) * _dot(pick(ygp_ref, ygs_ref), w_ref[1])
    acc += jax.nn.sigmoid(gc_ref[...]) * _dot(pick(ymp_ref, yms_ref), w_ref[2])
    o_ref[...] = acc.astype(o_ref.dtype)


def merge_branches(y_hy, y_gla, y_ml, w_branch, layer, zmain, tm=512, tn=1024):
    kb = HY_WIDTH
    n_p = N_PROMPT // tm
    p_spec = pl.BlockSpec((tm, kb), lambda j, i: (jnp.minimum(i, n_p - 1), 0))
    s_spec = pl.BlockSpec((tm, kb), lambda j, i: (jnp.maximum(i - n_p, 0), 0))

    def gate_spec(col):
        return pl.BlockSpec((tm, tn), lambda j, i, c=col // tn: (i, c + j))

    return pl.pallas_call(
        functools.partial(_merge_kernel, n_prompt_tiles=n_p),
        grid=(D_MODEL // tn, N_ROWS // tm),
        in_specs=[p_spec, s_spec, p_spec, s_spec, p_spec, s_spec,
                  pl.BlockSpec((None, 3, kb, tn), lambda j, i: (layer, 0, 0, j)),
                  gate_spec(C_GA), gate_spec(C_GB), gate_spec(C_GC)],
        out_specs=pl.BlockSpec((tm, tn), lambda j, i: (i, j)),
        out_shape=jax.ShapeDtypeStruct((N_ROWS, D_MODEL), BF16),
        compiler_params=_cp(("parallel", "parallel")),
        name="merge_branches",
    )(*y_hy, *y_gla, *y_ml, w_branch, zmain, zmain, zmain)


def _mm_resid_kernel(m_ref, w_ref, x_ref, gt_ref, o_ref):
    o_ref[...] = x_ref[...] + gt_ref[...] * _dot(m_ref[...], w_ref[...])


def mm_residual(merged, w, layer, x, mod3, gt_chunk, tm=512, tn=1024):
    k = merged.shape[1]
    return pl.pallas_call(
        _mm_resid_kernel,
        grid=(D_MODEL // tn, N_ROWS // tm),
        in_specs=[pl.BlockSpec((tm, k), lambda j, i: (i, 0)),
                  pl.BlockSpec((None, k, tn), lambda j, i: (layer, 0, j)),
                  pl.BlockSpec((tm, tn), lambda j, i: (i, j)),
                  pl.BlockSpec((None, 1, tn),
                               lambda j, i: (_mod_row(i, tm), 0, gt_chunk * (D_MODEL // tn) + j))],
        out_specs=pl.BlockSpec((tm, tn), lambda j, i: (i, j)),
        out_shape=jax.ShapeDtypeStruct((N_ROWS, D_MODEL), F32),
        compiler_params=_cp(("parallel", "parallel")),
        name="mm_residual",
    )(merged, w, x, mod3)


def _dft_tables(length):
    k = jnp.arange(length, dtype=jnp.int32)
    m = (k[:, None] * k[None, :]) % (2 * length)
    ang = m.astype(F32) * (math.pi / length)
    cos = jnp.cos(ang)
    sin = jnp.sin(ang)
    sgn = jnp.where(k % 2 == 0, 1.0, -1.0).astype(F32)
    fre = cos
    fim = jnp.where(k[:, None] == 0, sgn[None, :], -sin)
    wk = jnp.where(k == 0, 1.0, 2.0).astype(F32) / (2.0 * length)
    g_re = cos.T * wk[None, :]
    g_im = jnp.where(k[None, :] == 0, sgn[:, None] / (2.0 * length), -sin.T / length)
    return fre, fim, jnp.concatenate([g_re, g_im], axis=1), sgn


def _hyfilt_kernel(z_ref, w1_ref, b1_ref, w2_ref, b2_ref, fq_ref, w3f_ref, w3b_ref, tn_ref, dl_ref,
                   sgn_ref, freh_ref, frel_ref, fimh_ref, fiml_ref, sre_ref, sim_ref):
    hid = jnp.sin(fq_ref[0:1] * (_dot3(z_ref[...], w1_ref[...]) + b1_ref[...]))
    hid = jnp.sin(fq_ref[1:2] * (_dot3(hid, w2_ref[...]) + b2_ref[...]))
    decay = jnp.exp(-tn_ref[...] * dl_ref[...])
    fwd = _dot3(hid, w3f_ref[...]) * decay
    bwd = _dot3(hid, w3b_ref[...]) * decay
    row = lax.broadcasted_iota(jnp.int32, fwd.shape, 0)
    bwd = jnp.where(row == 0, 0.0, bwd)
    a = fwd + bwd
    d = fwd - bwd
    ah, al = _split2(a)
    dh, dl2 = _split2(d)
    re = _dot(freh_ref[...], ah) + (_dot(freh_ref[...], al) + _dot(frel_ref[...], ah))
    im = _dot(fimh_ref[...], dh) + (_dot(fimh_ref[...], dl2) + _dot(fiml_ref[...], dh))
    nyq = jnp.sum(sgn_ref[...] * a, axis=0, keepdims=True)
    sre_ref[0] = re
    sim_ref[0] = jnp.where(row == 0, nyq, im)


def hyena_spectrum(length, w1, b1, w2, b2, w3, freq, tabs):
    fre_h, fre_l, fim_h, fim_l, sgn = tabs
    t = jnp.arange(length, dtype=F32)
    t_norm = t / (length - 1)
    bands = jnp.linspace(1e-4, HY_BANDS - 1, HY_BANDS, dtype=F32)
    ang = (2.0 * math.pi / length) * t[:, None] * bands[None, :]
    z = jnp.concatenate([t_norm[:, None], jnp.cos(ang), -jnp.sin(ang),
                         jnp.zeros((length, HY_FFN - HY_EMB), F32)], axis=-1)
    w1p = jnp.pad(w1, ((0, HY_FFN - HY_EMB), (0, 0)))
    max_decay = math.log(HY_DECAY_TARGET) / HY_FAST_PCT
    min_decay = math.log(HY_DECAY_TARGET) / HY_SLOW_PCT
    deltas = jnp.abs(jnp.linspace(min_decay, max_decay, HY_WIDTH, dtype=F32)).reshape(1, HY_WIDTH)
    ct = 256
    nct = HY_WIDTH // ct
    full = lambda shape: pl.BlockSpec(shape, lambda o, j: (0,) * len(shape))
    out_spec = pl.BlockSpec((1, length, ct), lambda o, j: (o, 0, j))
    return pl.pallas_call(
        _hyfilt_kernel,
        grid=(2, nct),
        in_specs=[full((length, HY_FFN)), full((HY_FFN, HY_FFN)), full((1, HY_FFN)),
                  full((HY_FFN, HY_FFN)), full((1, HY_FFN)), full((2, HY_FFN)),
                  pl.BlockSpec((HY_FFN, ct), lambda o, j: (0, o * 2 * nct + j)),
                  pl.BlockSpec((HY_FFN, ct), lambda o, j: (0, o * 2 * nct + nct + j)),
                  full((length, 1)),
                  pl.BlockSpec((1, ct), lambda o, j: (0, j)),
                  full((length, 1)),
                  full((length, length)), full((length, length)),
                  full((length, length)), full((length, length))],
        out_specs=[out_spec, out_spec],
        out_shape=[jax.ShapeDtypeStruct((2, length, HY_WIDTH), F32)] * 2,
        compiler_params=_cp(("parallel", "parallel")),
        name="hyena_spectrum",
    )(z, w1p, b1.reshape(1, HY_FFN), w2, b2.reshape(1, HY_FFN), freq, w3, w3,
      t_norm.reshape(length, 1), deltas, sgn.reshape(length, 1), fre_h, fre_l, fim_h, fim_l)


def _short_conv(x, w, pos, seg, length):
    prev = jnp.where(pos == 0, 0.0, pltpu.roll(x, 1, 0))
    nxt = jnp.where(pos == seg - 1, 0.0, pltpu.roll(x, length - 1, 0))
    return prev * w[0:1] + x * w[1:2] + nxt * w[2:3]


def _hyena_kernel(x1_ref, x2_ref, v_ref, cw_ref, sre_ref, sim_ref, hb_ref, f_ref, g_ref, o_ref,
                  *, length, seg):
    row = lax.broadcasted_iota(jnp.int32, (length, 1), 0)
    pos = row % seg
    row0 = row == 0

    def long_conv(u, o):
        spec = _dot(f_ref[...], u.astype(BF16))
        ur, ui = spec[:length], spec[length:]
        hre, him = sre_ref[o], sim_ref[o]
        uihi = ui * him
        yr = ur * hre - jnp.where(row0, 0.0, uihi)
        yi = jnp.where(row0, uihi, ur * him + ui * hre)
        y = _dot(g_ref[...], jnp.concatenate([yr, yi], axis=0).astype(BF16))
        return y + u * hb_ref[o:o + 1]

    x1 = _short_conv(x1_ref[...], cw_ref[0], pos, seg, length)
    x2 = _short_conv(x2_ref[...], cw_ref[1], pos, seg, length)
    v = _short_conv(v_ref[...], cw_ref[2], pos, seg, length)
    z = x1 * long_conv(v, 0)
    o_ref[...] = (x2 * long_conv(z, 1)).astype(o_ref.dtype)


def hyena(zmain, conv_w, spec_re, spec_im, bias, f_mat, g_mat, length, n_seq, row_blk0, seg):
    ct = 256
    nct = HY_WIDTH // ct

    def zspec(col):
        return pl.BlockSpec((length, ct), lambda b, j, c=col // ct: (row_blk0 + b, c + j))

    return pl.pallas_call(
        functools.partial(_hyena_kernel, length=length, seg=seg),
        grid=(n_seq, nct),
        in_specs=[zspec(C_HX1), zspec(C_HX2), zspec(C_HV),
                  pl.BlockSpec((3, 3, ct), lambda b, j: (0, 0, j)),
                  pl.BlockSpec((2, length, ct), lambda b, j: (0, 0, j)),
                  pl.BlockSpec((2, length, ct), lambda b, j: (0, 0, j)),
                  pl.BlockSpec((2, ct), lambda b, j: (0, j)),
                  pl.BlockSpec((2 * length, length), lambda b, j: (0, 0)),
                  pl.BlockSpec((length, 2 * length), lambda b, j: (0, 0))],
        out_specs=pl.BlockSpec((length, ct), lambda b, j: (b, j)),
        out_shape=jax.ShapeDtypeStruct((n_seq * length, HY_WIDTH), BF16),
        compiler_params=_cp(("parallel", "parallel")),
        name="hyena",
    )(zmain, zmain, zmain, conv_w, spec_re, spec_im, bias, f_mat, g_mat)


def _tri_masks():
    t = lax.broadcasted_iota(jnp.int32, (CHUNK, CHUNK), 0)
    s = lax.broadcasted_iota(jnp.int32, (CHUNK, CHUNK), 1)
    return s <= t, s >= t


def _gla_chunk(q, k, v, g, state, mask, rev):
    tm = jnp.where(mask, 1.0, 0.0).astype(BF16)
    g1, g2, g3 = _split3(g)
    bc = _dot(tm, g1) + (_dot(tm, g2) + _dot(tm, g3))
    b_end = bc[0:1] if rev else bc[CHUNK - 1:CHUNK]
    ref = bc[CHUNK // 2:CHUNK // 2 + 1]
    inter = _dot((q * jnp.exp(bc)).astype(BF16), state.astype(BF16))
    qh = (q * jnp.exp(bc - ref)).astype(BF16)
    kh = (k * jnp.exp(ref - bc)).astype(BF16)
    att = jnp.where(mask, _dot_nt(qh, kh), 0.0)
    vb = v.astype(BF16)
    o = inter + _dot(att.astype(BF16), vb)
    ones = jnp.ones((CHUNK, GLA_DK), BF16)
    tot = _dot_tn(g1, ones) + (_dot_tn(g2, ones) + _dot_tn(g3, ones))
    e = jnp.exp(tot)
    kd = (k * jnp.exp(b_end - bc)).astype(BF16)
    new_state = jnp.concatenate([e, e], axis=1) * state + _dot_tn(kd, vb)
    return o, new_state


def _gla_kernel(*refs, length, n_sub, has_init):
    if has_init:
        (q_ref, k_ref, v_ref, gr_ref, zs_ref, wf_ref, wb_ref, ab_ref, ng_ref, s0_ref,
         y_ref, o_ref, lg_ref, st_ref) = refs
    else:
        (q_ref, k_ref, v_ref, gr_ref, zs_ref, wf_ref, wb_ref, ab_ref, ng_ref,
         y_ref, sout_ref, o_ref, lg_ref, st_ref) = refs
    zs = zs_ref[...]
    lg_ref[0] = _log_sigmoid(_dot3(zs, wf_ref[...]) + ab_ref[0]) * (1.0 / GLA_NORMALIZER)
    lg_ref[1] = _log_sigmoid(_dot3(zs, wb_ref[...]) + ab_ref[1]) * (1.0 / GLA_NORMALIZER)
    if has_init:
        st_ref[...] = s0_ref[...]
    else:
        st_ref[...] = jnp.zeros_like(st_ref)
    o_ref[...] = jnp.zeros_like(o_ref)
    n = length // CHUNK
    mask_f, mask_b = _tri_masks()
    scale = GLA_DK ** -0.5

    def body(i, carry):
        for s in range(n_sub):
            for d, mask in enumerate((mask_f, mask_b)):
                c = i if d == 0 else n - 1 - i
                rows = pl.ds(pl.multiple_of(s * length + c * CHUNK, CHUNK), CHUNK)
                for h in range(GLA_HEADS):
                    kc = slice(h * GLA_DK, (h + 1) * GLA_DK)
                    vc = slice(h * GLA_DV, (h + 1) * GLA_DV)
                    o, s_new = _gla_chunk(q_ref[rows, kc] * scale, k_ref[rows, kc], v_ref[rows, vc],
                                          lg_ref[d, rows, kc], st_ref[s, d, h], mask, d == 1)
                    o_ref[rows, vc] += o
                    st_ref[s, d, h] = s_new
        return carry

    lax.fori_loop(0, n, body, 0)
    for h in range(GLA_HEADS):
        vc = slice(h * GLA_DV, (h + 1) * GLA_DV)
        o = o_ref[:, vc]
        o = o * lax.rsqrt(jnp.mean(o * o, axis=-1, keepdims=True) + EPS) * ng_ref[...]
        gr = gr_ref[:, vc]
        y_ref[:, vc] = (o * (gr * jax.nn.sigmoid(gr))).astype(y_ref.dtype)
    if not has_init:
        sout_ref[...] = st_ref[...]


def gla(zmain, zsmall, wf, wb, ab, norm_g, length, n_seq, row_blk0, state0, layer, n_sub=1):
    has_init = state0 is not None
    qk_w = GLA_HEADS * GLA_DK
    v_w = GLA_HEADS * GLA_DV
    once = pl.Buffered(1)
    rows = n_sub * length
    blk0 = row_blk0 // n_sub
    st_blk = (n_sub, 2, GLA_HEADS, GLA_DK, GLA_DV)

    def zspec(col, width):
        return pl.BlockSpec((rows, width), lambda b, c=col // width: (blk0 + b, c), pipeline_mode=once)

    in_specs = [zspec(C_GQ, qk_w), zspec(C_GK, qk_w), zspec(C_GV, v_w), zspec(C_GR, v_w),
                pl.BlockSpec((rows, N_SMALL), lambda b: (blk0 + b, 0)),
                pl.BlockSpec((N_SMALL, qk_w), lambda b: (0, 0)),
                pl.BlockSpec((N_SMALL, qk_w), lambda b: (0, 0)),
                pl.BlockSpec((2, 1, qk_w), lambda b: (0, 0, 0)),
                pl.BlockSpec((1, GLA_DV), lambda b: (0, 0))]
    args = [zmain, zmain, zmain, zmain, zsmall, wf, wb, ab, norm_g]
    y_shape = jax.ShapeDtypeStruct((n_seq * length, v_w), BF16)
    y_spec = pl.BlockSpec((rows, v_w), lambda b: (b, 0))
    if has_init:
        in_specs.append(pl.BlockSpec((n_sub, None, 2, GLA_HEADS, GLA_DK, GLA_DV),
                                     lambda b: (b, layer, 0, 0, 0, 0)))
        args.append(state0)
        out_shape, out_specs = y_shape, y_spec
    else:
        out_shape = [y_shape, jax.ShapeDtypeStruct((n_seq, 2, GLA_HEADS, GLA_DK, GLA_DV), F32)]
        out_specs = [y_spec, pl.BlockSpec(st_blk, lambda b: (b, 0, 0, 0, 0))]
    return pl.pallas_call(
        functools.partial(_gla_kernel, length=length, n_sub=n_sub, has_init=has_init),
        grid=(n_seq // n_sub,),
        in_specs=in_specs,
        out_specs=out_specs,
        out_shape=out_shape,
        scratch_shapes=[pltpu.VMEM((rows, v_w), F32), pltpu.VMEM((2, rows, qk_w), F32),
                        pltpu.VMEM(st_blk, F32)],
        compiler_params=_cp(("parallel",)),
        name="gla",
    )(*args)


def _mlstm_chunk(q, ks, v, lf_c, li_c, lf_r, li_r, cm, nv, m_prev, mask, mask_t, rev):
    t_n = CHUNK
    tm = jnp.where(mask, 1.0, 0.0).astype(BF16)
    tmt = jnp.where(mask_t, 1.0, 0.0).astype(BF16)
    c1, c2, c3 = _split3(jnp.broadcast_to(lf_c, (t_n, t_n)))
    b_colb = _dot(tm, c1) + (_dot(tm, c2) + _dot(tm, c3))
    r1, r2, r3 = _split3(jnp.broadcast_to(lf_r, (t_n, t_n)))
    b_rowb = _dot(r1, tmt) + (_dot(r2, tmt) + _dot(r3, tmt))
    b_col = b_colb[:, 0:1]
    b_row = b_rowb[0:1, :]
    b_end = b_colb[0:1, 0:1] if rev else b_colb[t_n - 1:t_n, 0:1]
    dmat = jnp.where(mask, b_colb - b_rowb + li_r, -jnp.inf)
    m_t = jnp.maximum(b_col + m_prev, jnp.max(dmat, axis=-1, keepdims=True))
    w_inter = jnp.exp(b_col + m_prev - m_t)
    qb = q.astype(BF16)
    vb = v.astype(BF16)
    sc = _dot_nt(qb, ks.astype(BF16)) * jnp.exp(dmat - m_t)
    num = w_inter * _dot(qb, cm.astype(BF16)) + _dot(sc.astype(BF16), vb)
    den = w_inter * jnp.sum(q * nv, axis=-1, keepdims=True) + jnp.sum(sc, axis=-1, keepdims=True)
    h = num / jnp.maximum(jnp.abs(den), jnp.exp(-m_t))
    g_r = b_end - b_row + li_r
    g_c = b_end - b_col + li_c
    m_new = jnp.maximum(b_end + m_prev, jnp.max(g_r, axis=-1, keepdims=True))
    w_c = jnp.exp(b_end + m_prev - m_new)
    kw = ks * jnp.exp(g_c - m_new)
    cm_new = w_c * cm + _dot_tn(kw.astype(BF16), vb)
    nv_new = w_c * nv + jnp.sum(kw, axis=0, keepdims=True)
    return h, cm_new, nv_new, m_new


def _mlstm_kernel(*refs, length, n_sub, seg, has_init):
    if has_init:
        (gb_ref, m0_ref, q_ref, k_ref, v_ref, mo_ref, gc_ref, gr_ref, cw_ref, ng_ref, c0_ref, n0_ref,
         y_ref, qc_ref, kc_ref, h_ref, c_ref, n_ref, m_ref) = refs
    else:
        (gb_ref, q_ref, k_ref, v_ref, mo_ref, gc_ref, gr_ref, cw_ref, ng_ref,
         y_ref, cout_ref, nout_ref, mout_ref, qc_ref, kc_ref, h_ref, c_ref, n_ref, m_ref) = refs
    b_idx = pl.program_id(0)
    dh = ML_DH
    rows_all = n_sub * length
    row = lax.broadcasted_iota(jnp.int32, (rows_all, 1), 0)
    pos = row % seg
    qc_ref[...] = _short_conv(q_ref[...], cw_ref[0], pos, seg, rows_all)
    kc_ref[...] = _short_conv(k_ref[...], cw_ref[1], pos, seg, rows_all) * (dh ** -0.5)
    h_ref[...] = jnp.zeros_like(h_ref)
    if has_init:
        c_ref[...] = c0_ref[...]
        n_ref[...] = n0_ref[...]
        for s in range(n_sub):
            for d in range(2):
                for h in range(ML_HEADS):
                    m0 = m0_ref[(b_idx * n_sub + s) * 2 * ML_HEADS + d * ML_HEADS + h]
                    m_ref[s, d, h] = jnp.full((1, 128), m0, F32)
    else:
        c_ref[...] = jnp.zeros_like(c_ref)
        n_ref[...] = jnp.zeros_like(n_ref)
        m_ref[...] = jnp.zeros_like(m_ref)
    n = length // CHUNK
    mask_f, mask_b = _tri_masks()

    def body(i, carry):
        for s in range(n_sub):
            for d, (mask, mask_t) in enumerate(((mask_f, mask_b), (mask_b, mask_f))):
                c = i if d == 0 else n - 1 - i
                rows = pl.ds(pl.multiple_of(s * length + c * CHUNK, CHUNK), CHUNK)
                for h in range(ML_HEADS):
                    cols = slice(h * dh, (h + 1) * dh)
                    bi = gb_ref[d * 2 * ML_HEADS + h]
                    bf = gb_ref[d * 2 * ML_HEADS + ML_HEADS + h]
                    gcol = gc_ref[h, rows, :]
                    grow = gr_ref[h, s * n + c]
                    li_c = gcol[:, d:d + 1] + bi
                    lf_c = _log_sigmoid(gcol[:, 2 + d:3 + d] + bf)
                    li_r = grow[d:d + 1, :] + bi
                    lf_r = _log_sigmoid(grow[2 + d:3 + d, :] + bf)
                    hc, cm, nv, m_new = _mlstm_chunk(qc_ref[rows, cols], kc_ref[rows, cols], v_ref[rows, cols],
                                                     lf_c, li_c, lf_r, li_r, c_ref[s, d, h], n_ref[s, d, h],
                                                     m_ref[s, d, h][:, 0:1], mask, mask_t, d == 1)
                    h_ref[rows, cols] += hc
                    c_ref[s, d, h] = cm
                    n_ref[s, d, h] = nv
                    m_ref[s, d, h] = jnp.broadcast_to(m_new, (1, 128))
        return carry

    lax.fori_loop(0, n, body, 0)
    for h in range(ML_HEADS):
        cols = slice(h * dh, (h + 1) * dh)
        o = h_ref[:, cols]
        o = o * lax.rsqrt(jnp.mean(o * o, axis=-1, keepdims=True) + EPS) * ng_ref[...]
        y_ref[:, cols] = (o * jax.nn.sigmoid(mo_ref[:, cols])).astype(y_ref.dtype)
    if not has_init:
        cout_ref[...] = c_ref[...]
        nout_ref[...] = n_ref[...]
        mout_ref[...] = m_ref[...]


def mlstm(zmain, gates_col, gates_row, gate_b, conv_w, norm_g, length, n_seq, row_blk0, seg, init, layer,
          n_sub=1):
    has_init = init is not None
    dh = ML_DH
    nh = ML_HEADS
    width = nh * dh
    once = pl.Buffered(1)
    rows = n_sub * length
    blk0 = row_blk0 // n_sub

    def zspec(col):
        return pl.BlockSpec((rows, width), lambda b, c=col // width: (blk0 + b, c), pipeline_mode=once)

    smem = pl.BlockSpec(memory_space=pltpu.SMEM)
    in_specs = [smem]
    args = [gate_b]
    if has_init:
        c0, n0, m0 = init
        in_specs.append(smem)
        args.append(m0)
    in_specs += [zspec(C_MQ), zspec(C_MK), zspec(C_MV), zspec(C_MO),
                 pl.BlockSpec((nh, rows, 8), lambda b: (0, blk0 + b, 0)),
                 pl.BlockSpec((nh, rows // CHUNK, 8, CHUNK), lambda b: (0, blk0 + b, 0, 0)),
                 pl.BlockSpec((2, 3, width), lambda b: (0, 0, 0)),
                 pl.BlockSpec((1, dh), lambda b: (0, 0))]
    args += [zmain, zmain, zmain, zmain, gates_col, gates_row, conv_w, norm_g]
    y_shape = jax.ShapeDtypeStruct((n_seq * length, width), BF16)
    y_spec = pl.BlockSpec((rows, width), lambda b: (b, 0))
    if has_init:
        in_specs += [pl.BlockSpec((n_sub, None, 2, nh, dh, dh), lambda b: (b, layer, 0, 0, 0, 0)),
                     pl.BlockSpec((n_sub, None, 2, nh, 1, dh), lambda b: (b, layer, 0, 0, 0, 0))]
        args += [c0, n0]
        out_shape, out_specs = y_shape, y_spec
    else:
        out_shape = [y_shape,
                     jax.ShapeDtypeStruct((n_seq, 2, nh, dh, dh), F32),
                     jax.ShapeDtypeStruct((n_seq, 2, nh, 1, dh), F32),
                     jax.ShapeDtypeStruct((n_seq, 2, nh, 1, 128), F32)]
        out_specs = [y_spec,
                     pl.BlockSpec((n_sub, 2, nh, dh, dh), lambda b: (b, 0, 0, 0, 0)),
                     pl.BlockSpec((n_sub, 2, nh, 1, dh), lambda b: (b, 0, 0, 0, 0)),
                     pl.BlockSpec((n_sub, 2, nh, 1, 128), lambda b: (b, 0, 0, 0, 0))]
    return pl.pallas_call(
        functools.partial(_mlstm_kernel, length=length, n_sub=n_sub, seg=seg, has_init=has_init),
        grid=(n_seq // n_sub,),
        in_specs=in_specs,
        out_specs=out_specs,
        out_shape=out_shape,
        scratch_shapes=[pltpu.VMEM((rows, width), F32), pltpu.VMEM((rows, width), F32),
                        pltpu.VMEM((rows, width), F32),
                        pltpu.VMEM((n_sub, 2, nh, dh, dh), F32), pltpu.VMEM((n_sub, 2, nh, 1, dh), F32),
                        pltpu.VMEM((n_sub, 2, nh, 1, 128), F32)],
        compiler_params=_cp(("parallel",)),
        name="mlstm",
    )(*args)


def _top_values(x, count, with_rank=False):
    vals = []
    cur = x
    rank = jnp.full(x.shape, float(count), F32) if with_rank else None
    for r in range(count):
        m = jnp.max(cur, axis=0, keepdims=True)
        vals.append(m)
        if with_rank or r + 1 < count:
            top = cur == m
            cur = jnp.where(top, -jnp.inf, cur)
            if with_rank:
                rank = jnp.where(top, float(r), rank)
    return (vals, rank) if with_rank else vals


def _peer_topk_kernel(q_ref, k_ref, nb_ref, rk_ref, e1_ref, e2_ref):
    dk = PEER_NKEYS
    s1 = _dot3(k_ref[0], q_ref[:, 0:dk], dot=_dot_nt)
    s2 = _dot3(k_ref[1], q_ref[:, dk:2 * dk], dot=_dot_nt)
    v1 = _top_values(s1, PEER_TOPK)
    v2, rank2 = _top_values(s2, PEER_TOPK, with_rank=True)
    rows = [v1[a] + v2[b] for a in range(PEER_TOPK) for b in range(PEER_TOPK // (a + 1))]
    rows += [jnp.full_like(rows[0], -jnp.inf)] * (-len(rows) % 8)
    cand = jnp.concatenate(rows, axis=0)
    best = _top_values(cand, PEER_TOPK)
    zsum = jnp.exp(best[0] - best[0])
    for r in range(1, PEER_TOPK):
        zsum = zsum + jnp.exp(best[r] - best[0])
    theta = best[PEER_TOPK - 1]
    dense_b = 4
    nb = jnp.zeros_like(s1)
    for b in range(dense_b):
        nb = nb + jnp.where(s1 + v2[b] >= theta, 1.0, 0.0)
    for a in range(PEER_TOPK // (dense_b + 1)):
        extra = jnp.zeros_like(theta)
        for b in range(dense_b, PEER_TOPK // (a + 1)):
            extra = extra + jnp.where(v1[a] + v2[b] >= theta, 1.0, 0.0)
        nb = nb + jnp.where(s1 == v1[a], extra, 0.0)
    nb_ref[...] = nb
    rk_ref[...] = pltpu.bitcast(rank2.astype(BF16), jnp.uint32)
    e1_ref[...] = jnp.exp(s1 - v1[0]) / zsum
    e2_ref[...] = pltpu.bitcast(jnp.exp(s2 - v2[0]).astype(BF16), jnp.uint32)


def peer_topk(q, keys, tt=256):
    n = q.shape[0]
    nk = PEER_NKEYS
    spec = pl.BlockSpec((None, nk, tt), lambda i, h: (h, 0, i))
    pspec = pl.BlockSpec((None, nk // 2, tt), lambda i, h: (h, 0, i))
    full = jax.ShapeDtypeStruct((PEER_HEADS, nk, n), F32)
    packed = jax.ShapeDtypeStruct((PEER_HEADS, nk // 2, n), jnp.uint32)
    return pl.pallas_call(
        _peer_topk_kernel,
        grid=(n // tt, PEER_HEADS),
        in_specs=[pl.BlockSpec((tt, 2 * nk), lambda i, h: (i, h)),
                  pl.BlockSpec((None, 2, nk, nk), lambda i, h: (h, 0, 0, 0))],
        out_specs=[spec, pspec, spec, pspec],
        out_shape=[full, packed, full, packed],
        compiler_params=_cp(("parallel", "parallel")),
        name="peer_topk",
    )(q, keys)


def _row_bcast(row, n):
    t = jnp.broadcast_to(row, (16, 128)).astype(BF16)
    return jnp.broadcast_to(t[None], (n // 16, 16, 128)).reshape(n, 128)


def _gelu_tanh(x):
    return 0.5 * x * (1.0 + jnp.tanh(math.sqrt(2.0 / math.pi) * (x + 0.044715 * (x * x * x))))


def _peer_expert_kernel(ht_ref, u_ref, vt_ref, nb_ref, rk_ref, e1_ref, e2_ref, x_ref, gt_ref,
                        o_ref, acc_ref, w_ref, *, n_i1, tt):
    j = pl.program_id(1)
    nk = PEER_NKEYS

    @pl.when(j == 0)
    def _():
        acc_ref[...] = jnp.zeros_like(acc_ref)

    act = _dot(u_ref[...], ht_ref[...])
    zero = jnp.zeros((), BF16)
    for ts in range(tt // 128):
        lanes = slice(ts * 128, (ts + 1) * 128)
        for r in range(n_i1):
            w = None
            for h in range(PEER_HEADS):
                nbr = _row_bcast(nb_ref[h, r:r + 1, lanes], nk)
                e1r = _row_bcast(e1_ref[h, r:r + 1, lanes], nk)
                rk = pltpu.bitcast(rk_ref[h, :, lanes], BF16)
                e2 = pltpu.bitcast(e2_ref[h, :, lanes], BF16)
                term = jnp.where(rk < nbr, e2 * e1r, zero)
                w = term if w is None else w + term
            w_ref[r * nk:(r + 1) * nk, lanes] = w
    p = w_ref[...] * _gelu_tanh(act).astype(BF16)
    acc_ref[...] += _dot(vt_ref[...], p)

    @pl.when(j == pl.num_programs(1) - 1)
    def _():
        o_ref[...] = x_ref[...] + gt_ref[...] * acc_ref[...].T


def _xpose_cast_kernel(x_ref, o_ref):
    o_ref[...] = x_ref[...].T.astype(BF16)


def transpose_cast_chunks(tab, ec):
    depth, e, d = tab.shape
    return pl.pallas_call(
        _xpose_cast_kernel,
        grid=(depth, e // ec),
        in_specs=[pl.BlockSpec((None, ec, d), lambda l, c: (l, c, 0))],
        out_specs=pl.BlockSpec((None, None, d, ec), lambda l, c: (l, c, 0, 0)),
        out_shape=jax.ShapeDtypeStruct((depth, e // ec, d, ec), BF16),
        compiler_params=_cp(("parallel", "parallel")),
        name="transpose_cast_chunks",
    )(tab)


def peer_experts(h2t, u_tab, vt_tab, layer, nb, rk, e1, e2, x, mod3, gt_chunk, tt=512):
    ec = PEER_CHUNK
    n = x.shape[0]
    nk = PEER_NKEYS
    n_i1 = ec // nk
    once = pl.Buffered(1)
    sspec = pl.BlockSpec((PEER_HEADS, nk // 2, tt), lambda i, j: (0, 0, i), pipeline_mode=once)
    rspec = pl.BlockSpec((PEER_HEADS, n_i1, tt), lambda i, j: (0, j, i))
    return pl.pallas_call(
        functools.partial(_peer_expert_kernel, n_i1=n_i1, tt=tt),
        grid=(n // tt, PEER_EXPERTS // ec),
        in_specs=[pl.BlockSpec((D_MODEL, tt), lambda i, j: (0, i)),
                  pl.BlockSpec((None, ec, D_MODEL), lambda i, j: (layer, j, 0)),
                  pl.BlockSpec((None, None, D_MODEL, ec), lambda i, j: (layer, j, 0, 0)),
                  rspec, sspec, rspec, sspec,
                  pl.BlockSpec((tt, D_MODEL), lambda i, j: (i, 0), pipeline_mode=once),
                  pl.BlockSpec((None, 1, D_MODEL), lambda i, j: (_mod_row(i, tt), 0, gt_chunk))],
        out_specs=pl.BlockSpec((tt, D_MODEL), lambda i, j: (i, 0)),
        out_shape=jax.ShapeDtypeStruct((n, D_MODEL), F32),
        scratch_shapes=[pltpu.VMEM((D_MODEL, tt), F32), pltpu.VMEM((ec, tt), BF16)],
        compiler_params=_cp(("parallel", "arbitrary")),
        name="peer_experts",
    )(h2t, u_tab, vt_tab, nb, rk, e1, e2, x, mod3)


def _reorder_in_proj(w_in, b_in):
    splits = (1024, 1024, 1024, 512, 512, 1024, 1024, 16, 16, 1024, 1024, 1024, 1024, 8, 8, 2048, 2048, 2048)
    offs = np.concatenate([[0], np.cumsum(splits)])
    seg = lambda a, i: a[..., offs[i]:offs[i + 1]]
    main_ids = (0, 1, 2, 3, 4, 5, 6, 9, 10, 11, 12, 15, 16, 17)
    small_ids = (7, 8, 13, 14)
    w_main = jnp.concatenate([seg(w_in, i).astype(BF16) for i in main_ids], axis=-1)
    b_main = jnp.concatenate([seg(b_in, i) for i in main_ids], axis=-1)
    w_small = jnp.concatenate([seg(w_in, i) for i in small_ids], axis=-1)
    b_small = jnp.concatenate([seg(b_in, i) for i in small_ids], axis=-1)
    pad = N_SMALL - w_small.shape[-1]
    w_small = jnp.pad(w_small, ((0, 0), (0, 0), (0, pad)))
    b_small = jnp.pad(b_small, ((0, 0), (0, pad)))
    return w_main, b_main, w_small, b_small


def kernel(x_prompt, x_sample, c, state_gla, state_mlstm_C, state_mlstm_n, state_mlstm_m, c_ctx,
           mod_w, mod_b, norm1_g, norm2_g, final_g, w_in, b_in, hy_conv, hy_w1, hy_b1, hy_w2, hy_b2,
           hy_w3, hy_freq, hy_bias, gla_a2_w, gla_a2_b, gla_norm_g, ml_conv, ml_gate_b, ml_norm_g,
           w_branch, w_out, peer_wq, peer_keys, peer_u, peer_v):
    w_main, b_main, w_small, b_small = _reorder_in_proj(w_in, b_in)
    w_small_hi = w_small.astype(BF16)
    w_small_lo = (w_small - w_small_hi.astype(F32)).astype(BF16)
    w_branch_b = w_branch.astype(BF16)
    w_out_b = w_out.astype(BF16)
    wq_hi = peer_wq.astype(BF16)
    wq_lo = (peer_wq - wq_hi.astype(F32)).astype(BF16)
    u_b = peer_u.astype(BF16)
    vt_b = transpose_cast_chunks(peer_v, PEER_CHUNK)
    zero_b = jnp.zeros((1, D_MODEL), F32)
    a2f = jnp.pad(gla_a2_w[:, 0], ((0, 0), (0, N_SMALL - GLA_RANK), (0, 0)))
    a2b = jnp.pad(gla_a2_w[:, 1], ((0, 0), (GLA_RANK, N_SMALL - 2 * GLA_RANK), (0, 0)))
    a2bias = gla_a2_b.reshape(DEPTH, 2, 1, GLA_HEADS * GLA_DK)
    n0_all = state_mlstm_n.reshape(DEC_BATCH, DEPTH, 2, ML_HEADS, 1, ML_DH)

    tabs = {}
    for length in (SEQ, DEC_SEQ):
        fre, fim, g_mat, sgn = _dft_tables(length)
        fre_h, fre_l = _split2(fre)
        fim_h, fim_l = _split2(fim)
        tabs[length] = dict(spec=(fre_h, fre_l, fim_h, fim_l, sgn),
                            f=jnp.concatenate([fre_h, fim_h], axis=0), g=g_mat.astype(BF16))

    cvec = jnp.zeros((MOD_ROWS, D_MODEL), F32).at[:DEC_BATCH].set(c).at[CTX_ROW].set(c_ctx)
    mod_all = mod_table(cvec, mod_w, mod_b)

    x = jnp.concatenate([x_prompt.reshape(N_PROMPT, D_MODEL), x_sample.reshape(N_SAMPLE, D_MODEL)], axis=0)
    groups = ((SEQ, BATCH, 0, SEQ), (DEC_SEQ, DEC_BATCH, N_PROMPT // DEC_SEQ, GRID_W))
    new_gla, new_c, new_n, new_m = [], [], [], []
    for l in range(DEPTH):
        mod3 = mod_all[l].reshape(MOD_ROWS, 1, 6 * D_MODEL)
        h_hi, h_lo = normmod(x, norm1_g[l], mod3, 0, 1)
        zmain = mm_bias(h_hi, w_main, l, b_main[l].reshape(1, N_MAIN), tm=1024, tn=2048)
        zsmall = mm3_bias(h_hi, h_lo, w_small_hi, w_small_lo, l, b_small[l].reshape(1, N_SMALL))
        mi = zsmall[:, 32:40].reshape(N_ROWS, 2, ML_HEADS)
        mf = zsmall[:, 40:48].reshape(N_ROWS, 2, ML_HEADS)
        gcol = jnp.concatenate([mi, mf, jnp.zeros((N_ROWS, 4, ML_HEADS), F32)], axis=1)
        gates_col = jnp.transpose(gcol, (2, 0, 1))
        gates_row = jnp.transpose(gcol.reshape(N_ROWS // CHUNK, CHUNK, 8, ML_HEADS), (3, 0, 2, 1))
        gate_b = ml_gate_b[l].reshape(-1)
        y_hy, y_gla, y_ml = [], [], []
        for gi, (length, n_seq, blk0, seg) in enumerate(groups):
            t = tabs[length]
            sre, sim = hyena_spectrum(length, hy_w1[l], hy_b1[l], hy_w2[l], hy_b2[l], hy_w3[l],
                                      hy_freq[l], t["spec"])
            y_hy.append(hyena(zmain, hy_conv[l], sre, sim, hy_bias[l], t["f"], t["g"],
                              length, n_seq, blk0, seg))
            gla_args = (zmain, zsmall, a2f[l], a2b[l], a2bias[l], gla_norm_g[l].reshape(1, GLA_DV),
                        length, n_seq, blk0)
            ml_args = (zmain, gates_col, gates_row, gate_b, ml_conv[l], ml_norm_g[l].reshape(1, ML_DH),
                       length, n_seq, blk0, seg)
            if gi == 0:
                yg, s_fin = gla(*gla_args, None, l, n_sub=PROMPT_SEQS_PER_STEP)
                ym, c_fin, n_fin, m_fin = mlstm(*ml_args, None, l, n_sub=PROMPT_SEQS_PER_STEP)
                new_gla.append(s_fin)
                new_c.append(c_fin)
                new_n.append(n_fin[:, :, :, 0, :])
                new_m.append(m_fin[:, :, :, 0, 0])
            else:
                yg = gla(*gla_args, state_gla, l)
                ym = mlstm(*ml_args, (state_mlstm_C, n0_all, state_mlstm_m[:, l].reshape(-1)), l)
            y_gla.append(yg)
            y_ml.append(ym)
        merged = merge_branches(y_hy, y_gla, y_ml, w_branch_b, l, zmain)
        x = mm_residual(merged, w_out_b, l, x, mod3, 2)
        h2, h2_lo, h2t = normmod(x, norm2_g[l], mod3, 3, 4, transposed=True)
        q = mm3_bias(h2, h2_lo, wq_hi, wq_lo, l, zero_b)
        nb, rk, e1, e2 = peer_topk(q, peer_keys[l])
        x = peer_experts(h2t, u_b, vt_b, l, nb, rk, e1, e2, x, mod3, 5)

    y_prompt = final_norm(x, final_g, 0, N_PROMPT).reshape(BATCH, SEQ, D_MODEL)
    y_sample = final_norm(x, final_g, N_PROMPT, N_SAMPLE).reshape(DEC_BATCH, DEC_SEQ, D_MODEL)
    return (y_prompt, y_sample, jnp.stack(new_gla, axis=1), jnp.stack(new_c, axis=1),
            jnp.stack(new_n, axis=1), jnp.stack(new_m, axis=1))
```

```python
import functools
import math

import jax
import jax.numpy as jnp
import numpy as np
from jax import lax
from jax.experimental import pallas as pl
from jax.experimental.pallas import tpu as pltpu

F32 = jnp.float32
BF16 = jnp.bfloat16

D_MODEL = 2048
BATCH = 16
SEQ = 256
DEPTH = 4
DEC_BATCH = 4
DEC_SEQ = 1024
GRID_W = 64
EPS = 1e-6
CHUNK = 64
HY_WIDTH = 1024
HY_EMB = 33
HY_BANDS = (HY_EMB - 1) // 2
HY_FFN = 64
HY_DECAY_TARGET = 1e-2
HY_FAST_PCT = 0.3
HY_SLOW_PCT = 1.5
GLA_HEADS = 4
GLA_DK = 128
GLA_DV = 256
GLA_RANK = 16
GLA_NORMALIZER = 16.0
ML_HEADS = 4
ML_DH = 256
PEER_HEADS = 8
PEER_NKEYS = 128
PEER_EXPERTS = PEER_NKEYS * PEER_NKEYS
PEER_TOPK = 16
PEER_CHUNK = 1024
PROMPT_SEQS_PER_STEP = 2

N_PROMPT = BATCH * SEQ
N_SAMPLE = DEC_BATCH * DEC_SEQ
N_ROWS = N_PROMPT + N_SAMPLE
CTX_ROW = DEC_BATCH
MOD_ROWS = 8

C_HX1, C_HX2, C_HV = 0, 1024, 2048
C_GQ, C_GK, C_GV, C_GR = 3072, 3584, 4096, 5120
C_MQ, C_MK, C_MV, C_MO = 6144, 7168, 8192, 9216
C_GA, C_GB, C_GC = 10240, 12288, 14336
N_MAIN = 16384
N_SMALL = 128

VMEM_LIMIT = 56 * 1024 * 1024


def _cp(sem):
    return pltpu.CompilerParams(dimension_semantics=sem, vmem_limit_bytes=VMEM_LIMIT)


def _dot(a, b):
    return jnp.dot(a, b, preferred_element_type=F32)


def _dot_nt(a, b):
    return lax.dot_general(a, b, (((1,), (1,)), ((), ())), preferred_element_type=F32)


def _dot_tn(a, b):
    return lax.dot_general(a, b, (((0,), (0,)), ((), ())), preferred_element_type=F32)


def _split2(a):
    hi = a.astype(BF16)
    lo = (a - hi.astype(F32)).astype(BF16)
    return hi, lo


def _split3(a):
    a1 = a.astype(BF16)
    r1 = a - a1.astype(F32)
    a2 = r1.astype(BF16)
    a3 = (r1 - a2.astype(F32)).astype(BF16)
    return a1, a2, a3


def _dot3(a, b, dot=_dot):
    ah, al = _split2(a)
    bh, bl = _split2(b)
    return dot(ah, bh) + (dot(ah, bl) + dot(al, bh))


def _log_sigmoid(x):
    return jnp.minimum(x, 0.0) - jnp.log(1.0 + jnp.exp(-jnp.abs(x)))


def _mod_row(i, tm):
    n_p = N_PROMPT // tm
    return jnp.where(i < n_p, CTX_ROW, (i - n_p) // (DEC_SEQ // tm))


def _mod_kernel(c_ref, w_ref, b_ref, o_ref):
    a = c_ref[...]
    a = a * jax.nn.sigmoid(a)
    o_ref[0] = _dot3(a, w_ref[0]) + b_ref[0]


def mod_table(cvec, mod_w, mod_b):
    tn = 1024
    n = mod_w.shape[-1]
    return pl.pallas_call(
        _mod_kernel,
        grid=(DEPTH, n // tn),
        in_specs=[pl.BlockSpec((MOD_ROWS, D_MODEL), lambda l, j: (0, 0)),
                  pl.BlockSpec((1, D_MODEL, tn), lambda l, j: (l, 0, j)),
                  pl.BlockSpec((1, 1, tn), lambda l, j: (l, 0, j))],
        out_specs=pl.BlockSpec((1, MOD_ROWS, tn), lambda l, j: (l, 0, j)),
        out_shape=jax.ShapeDtypeStruct((DEPTH, MOD_ROWS, n), F32),
        compiler_params=_cp(("parallel", "parallel")),
        name="mod_table",
    )(cvec, mod_w, mod_b.reshape(DEPTH, 1, n))


def _norm_modulate(x_ref, g_ref, sc_ref, sh_ref):
    x = x_ref[...]
    y = x * lax.rsqrt(jnp.mean(x * x, axis=-1, keepdims=True) + EPS) * g_ref[...]
    return y * (1.0 + sc_ref[...]) + sh_ref[...]


def _normmod_gates_kernel(x_ref, g_ref, sc_ref, sh_ref, wh_ref, wl_ref, b_ref, hi_ref, zs_ref):
    h = _norm_modulate(x_ref, g_ref, sc_ref, sh_ref)
    hi = h.astype(BF16)
    lo = (h - hi.astype(F32)).astype(BF16)
    hi_ref[...] = hi
    zs_ref[...] = _dot(hi, wh_ref[...]) + (_dot(hi, wl_ref[...]) + _dot(lo, wh_ref[...])) + b_ref[...]


def _normmod_peer_kernel(x_ref, g_ref, sc_ref, sh_ref, hi_ref, lo_ref, t_ref):
    h = _norm_modulate(x_ref, g_ref, sc_ref, sh_ref)
    hi = h.astype(BF16)
    hi_ref[...] = hi
    lo_ref[...] = (h - hi.astype(F32)).astype(BF16)
    t_ref[...] = h.T.astype(BF16)


def _normmod_specs(tm, sh_chunk, sc_chunk):
    return [pl.BlockSpec((tm, D_MODEL), lambda i: (i, 0)),
            pl.BlockSpec((1, D_MODEL), lambda i: (0, 0)),
            pl.BlockSpec((None, 1, D_MODEL), lambda i: (_mod_row(i, tm), 0, sc_chunk)),
            pl.BlockSpec((None, 1, D_MODEL), lambda i: (_mod_row(i, tm), 0, sh_chunk))]


def normmod_gates(x, g, mod3, sh_chunk, sc_chunk, w_hi, w_lo, layer, b):
    tm = 256
    n = w_hi.shape[-1]
    row_spec = pl.BlockSpec((tm, D_MODEL), lambda i: (i, 0))
    w_spec = pl.BlockSpec((None, D_MODEL, n), lambda i: (layer, 0, 0))
    return pl.pallas_call(
        _normmod_gates_kernel,
        grid=(N_ROWS // tm,),
        in_specs=_normmod_specs(tm, sh_chunk, sc_chunk) + [w_spec, w_spec, pl.BlockSpec((1, n), lambda i: (0, 0))],
        out_specs=[row_spec, pl.BlockSpec((tm, n), lambda i: (i, 0))],
        out_shape=[jax.ShapeDtypeStruct((N_ROWS, D_MODEL), BF16), jax.ShapeDtypeStruct((N_ROWS, n), F32)],
        compiler_params=_cp(("parallel",)),
        name="normmod_gates",
    )(x, g.reshape(1, D_MODEL), mod3, mod3, w_hi, w_lo, b)


def normmod_peer(x, g, mod3, sh_chunk, sc_chunk):
    tm = 256
    row_spec = pl.BlockSpec((tm, D_MODEL), lambda i: (i, 0))
    row_shape = jax.ShapeDtypeStruct((N_ROWS, D_MODEL), BF16)
    return pl.pallas_call(
        _normmod_peer_kernel,
        grid=(N_ROWS // tm,),
        in_specs=_normmod_specs(tm, sh_chunk, sc_chunk),
        out_specs=[row_spec, row_spec, pl.BlockSpec((D_MODEL, tm), lambda i: (0, i))],
        out_shape=[row_shape, row_shape, jax.ShapeDtypeStruct((D_MODEL, N_ROWS), BF16)],
        compiler_params=_cp(("parallel",)),
        name="normmod_peer",
    )(x, g.reshape(1, D_MODEL), mod3, mod3)


def _final_norm_kernel(x_ref, g_ref, o_ref):
    x = x_ref[...]
    o_ref[...] = x * lax.rsqrt(jnp.mean(x * x, axis=-1, keepdims=True) + EPS) * g_ref[...]


def final_norm(x, g, row0, n_rows):
    tm = 256
    return pl.pallas_call(
        _final_norm_kernel,
        grid=(n_rows // tm,),
        in_specs=[pl.BlockSpec((tm, D_MODEL), lambda i: (row0 // tm + i, 0)),
                  pl.BlockSpec((1, D_MODEL), lambda i: (0, 0))],
        out_specs=pl.BlockSpec((tm, D_MODEL), lambda i: (i, 0)),
        out_shape=jax.ShapeDtypeStruct((n_rows, D_MODEL), F32),
        compiler_params=_cp(("parallel",)),
        name="final_norm",
    )(x, g.reshape(1, D_MODEL))


def _mm_bias_kernel(x_ref, w_ref, b_ref, o_ref):
    o_ref[...] = (_dot(x_ref[...], w_ref[...]) + b_ref[...]).astype(o_ref.dtype)


def mm_bias(x, w, layer, b, out_dtype=F32, tm=512, tn=1024):
    m, k = x.shape
    n = w.shape[-1]
    return pl.pallas_call(
        _mm_bias_kernel,
        grid=(n // tn, m // tm),
        in_specs=[pl.BlockSpec((tm, k), lambda j, i: (i, 0)),
                  pl.BlockSpec((None, k, tn), lambda j, i: (layer, 0, j)),
                  pl.BlockSpec((1, tn), lambda j, i: (0, j))],
        out_specs=pl.BlockSpec((tm, tn), lambda j, i: (i, j)),
        out_shape=jax.ShapeDtypeStruct((m, n), out_dtype),
        compiler_params=_cp(("parallel", "parallel")),
        name="mm_bias",
    )(x, w, b)


def _mm3_bias_kernel(xh_ref, xl_ref, wh_ref, wl_ref, b_ref, o_ref):
    xh = xh_ref[...]
    acc = _dot(xh, wh_ref[...]) + (_dot(xh, wl_ref[...]) + _dot(xl_ref[...], wh_ref[...]))
    o_ref[...] = acc + b_ref[...]


def mm3_bias(xh, xl, wh, wl, layer, b, tm=512, tn=1024):
    m, k = xh.shape
    n = wh.shape[-1]
    tn = min(tn, n)
    return pl.pallas_call(
        _mm3_bias_kernel,
        grid=(n // tn, m // tm),
        in_specs=[pl.BlockSpec((tm, k), lambda j, i: (i, 0)),
                  pl.BlockSpec((tm, k), lambda j, i: (i, 0)),
                  pl.BlockSpec((None, k, tn), lambda j, i: (layer, 0, j)),
                  pl.BlockSpec((None, k, tn), lambda j, i: (layer, 0, j)),
                  pl.BlockSpec((1, tn), lambda j, i: (0, j))],
        out_specs=pl.BlockSpec((tm, tn), lambda j, i: (i, j)),
        out_shape=jax.ShapeDtypeStruct((m, n), F32),
        compiler_params=_cp(("parallel", "parallel")),
        name="mm3_bias",
    )(xh, xl, wh, wl, b)


def _merge_kernel(yhp_ref, yhs_ref, ygp_ref, ygs_ref, ymp_ref, yms_ref, w_ref, ga_ref, gb_ref, gc_ref, o_ref,
                  *, n_prompt_tiles):
    is_prompt = pl.program_id(1) < n_prompt_tiles
    pick = lambda p_ref, s_ref: jnp.where(is_prompt, p_ref[...], s_ref[...])
    acc = jax.nn.sigmoid(ga_ref[...]) * _dot(pick(yhp_ref, yhs_ref), w_ref[0])
    acc += jax.nn.sigmoid(gb_ref[...]) * _dot(pick(ygp_ref, ygs_ref), w_ref[1])
    acc += jax.nn.sigmoid(gc_ref[...]) * _dot(pick(ymp_ref, yms_ref), w_ref[2])
    o_ref[...] = acc.astype(o_ref.dtype)


def merge_branches(y_hy, y_gla, y_ml, w_branch, layer, zmain, tm=512, tn=1024):
    kb = HY_WIDTH
    n_p = N_PROMPT // tm
    p_spec = pl.BlockSpec((tm, kb), lambda j, i: (jnp.minimum(i, n_p - 1), 0))
    s_spec = pl.BlockSpec((tm, kb), lambda j, i: (jnp.maximum(i - n_p, 0), 0))

    def gate_spec(col):
        return pl.BlockSpec((tm, tn), lambda j, i, c=col // tn: (i, c + j))

    return pl.pallas_call(
        functools.partial(_merge_kernel, n_prompt_tiles=n_p),
        grid=(D_MODEL // tn, N_ROWS // tm),
        in_specs=[p_spec, s_spec, p_spec, s_spec, p_spec, s_spec,
                  pl.BlockSpec((None, 3, kb, tn), lambda j, i: (layer, 0, 0, j)),
                  gate_spec(C_GA), gate_spec(C_GB), gate_spec(C_GC)],
        out_specs=pl.BlockSpec((tm, tn), lambda j, i: (i, j)),
        out_shape=jax.ShapeDtypeStruct((N_ROWS, D_MODEL), BF16),
        compiler_params=_cp(("parallel", "parallel")),
        name="merge_branches",
    )(*y_hy, *y_gla, *y_ml, w_branch, zmain, zmain, zmain)


def _mm_resid_kernel(m_ref, w_ref, x_ref, gt_ref, o_ref):
    o_ref[...] = x_ref[...] + gt_ref[...] * _dot(m_ref[...], w_ref[...])


def mm_residual(merged, w, layer, x, mod3, gt_chunk, tm=512, tn=1024):
    k = merged.shape[1]
    return pl.pallas_call(
        _mm_resid_kernel,
        grid=(D_MODEL // tn, N_ROWS // tm),
        in_specs=[pl.BlockSpec((tm, k), lambda j, i: (i, 0)),
                  pl.BlockSpec((None, k, tn), lambda j, i: (layer, 0, j)),
                  pl.BlockSpec((tm, tn), lambda j, i: (i, j)),
                  pl.BlockSpec((None, 1, tn),
                               lambda j, i: (_mod_row(i, tm), 0, gt_chunk * (D_MODEL // tn) + j))],
        out_specs=pl.BlockSpec((tm, tn), lambda j, i: (i, j)),
        out_shape=jax.ShapeDtypeStruct((N_ROWS, D_MODEL), F32),
        compiler_params=_cp(("parallel", "parallel")),
        name="mm_residual",
    )(merged, w, x, mod3)


def _dft_tables(length):
    k = jnp.arange(length, dtype=jnp.int32)
    m = (k[:, None] * k[None, :]) % (2 * length)
    ang = m.astype(F32) * (math.pi / length)
    cos = jnp.cos(ang)
    sin = jnp.sin(ang)
    sgn = jnp.where(k % 2 == 0, 1.0, -1.0).astype(F32)
    fre = cos
    fim = jnp.where(k[:, None] == 0, sgn[None, :], -sin)
    wk = jnp.where(k == 0, 1.0, 2.0).astype(F32) / (2.0 * length)
    g_re = cos.T * wk[None, :]
    g_im = jnp.where(k[None, :] == 0, sgn[:, None] / (2.0 * length), -sin.T / length)
    return fre, fim, jnp.concatenate([g_re, g_im], axis=1), sgn


def _hyfilt_kernel(z_ref, w1_ref, b1_ref, w2_ref, b2_ref, fq_ref, w3f_ref, w3b_ref, tn_ref, dl_ref,
                   sgn_ref, freh_ref, frel_ref, fimh_ref, fiml_ref, sre_ref, sim_ref):
    hid = jnp.sin(fq_ref[0:1] * (_dot3(z_ref[...], w1_ref[...]) + b1_ref[...]))
    hid = jnp.sin(fq_ref[1:2] * (_dot3(hid, w2_ref[...]) + b2_ref[...]))
    decay = jnp.exp(-tn_ref[...] * dl_ref[...])
    fwd = _dot3(hid, w3f_ref[...]) * decay
    bwd = _dot3(hid, w3b_ref[...]) * decay
    row = lax.broadcasted_iota(jnp.int32, fwd.shape, 0)
    bwd = jnp.where(row == 0, 0.0, bwd)
    a = fwd + bwd
    d = fwd - bwd
    ah, al = _split2(a)
    dh, dl2 = _split2(d)
    re = _dot(freh_ref[...], ah) + (_dot(freh_ref[...], al) + _dot(frel_ref[...], ah))
    im = _dot(fimh_ref[...], dh) + (_dot(fimh_ref[...], dl2) + _dot(fiml_ref[...], dh))
    nyq = jnp.sum(sgn_ref[...] * a, axis=0, keepdims=True)
    sre_ref[0] = re
    sim_ref[0] = jnp.where(row == 0, nyq, im)


def hyena_spectrum(length, w1, b1, w2, b2, w3, freq, tabs):
    fre_h, fre_l, fim_h, fim_l, sgn = tabs
    t = jnp.arange(length, dtype=F32)
    t_norm = t / (length - 1)
    bands = jnp.linspace(1e-4, HY_BANDS - 1, HY_BANDS, dtype=F32)
    ang = (2.0 * math.pi / length) * t[:, None] * bands[None, :]
    z = jnp.concatenate([t_norm[:, None], jnp.cos(ang), -jnp.sin(ang),
                         jnp.zeros((length, HY_FFN - HY_EMB), F32)], axis=-1)
    w1p = jnp.pad(w1, ((0, HY_FFN - HY_EMB), (0, 0)))
    max_decay = math.log(HY_DECAY_TARGET) / HY_FAST_PCT
    min_decay = math.log(HY_DECAY_TARGET) / HY_SLOW_PCT
    deltas = jnp.abs(jnp.linspace(min_decay, max_decay, HY_WIDTH, dtype=F32)).reshape(1, HY_WIDTH)
    ct = 256
    nct = HY_WIDTH // ct
    full = lambda shape: pl.BlockSpec(shape, lambda o, j: (0,) * len(shape))
    out_spec = pl.BlockSpec((1, length, ct), lambda o, j: (o, 0, j))
    return pl.pallas_call(
        _hyfilt_kernel,
        grid=(2, nct),
        in_specs=[full((length, HY_FFN)), full((HY_FFN, HY_FFN)), full((1, HY_FFN)),
                  full((HY_FFN, HY_FFN)), full((1, HY_FFN)), full((2, HY_FFN)),
                  pl.BlockSpec((HY_FFN, ct), lambda o, j: (0, o * 2 * nct + j)),
                  pl.BlockSpec((HY_FFN, ct), lambda o, j: (0, o * 2 * nct + nct + j)),
                  full((length, 1)),
                  pl.BlockSpec((1, ct), lambda o, j: (0, j)),
                  full((length, 1)),
                  full((length, length)), full((length, length)),
                  full((length, length)), full((length, length))],
        out_specs=[out_spec, out_spec],
        out_shape=[jax.ShapeDtypeStruct((2, length, HY_WIDTH), F32)] * 2,
        compiler_params=_cp(("parallel", "parallel")),
        name="hyena_spectrum",
    )(z, w1p, b1.reshape(1, HY_FFN), w2, b2.reshape(1, HY_FFN), freq, w3, w3,
      t_norm.reshape(length, 1), deltas, sgn.reshape(length, 1), fre_h, fre_l, fim_h, fim_l)


def _short_conv(x, w, pos, seg, length):
    prev = jnp.where(pos == 0, 0.0, pltpu.roll(x, 1, 0))
    nxt = jnp.where(pos == seg - 1, 0.0, pltpu.roll(x, length - 1, 0))
    return prev * w[0:1] + x * w[1:2] + nxt * w[2:3]


def _hyena_kernel(x1_ref, x2_ref, v_ref, cw_ref, sre_ref, sim_ref, hb_ref, f_ref, g_ref, o_ref,
                  *, length, seg):
    row = lax.broadcasted_iota(jnp.int32, (length, 1), 0)
    pos = row % seg
    row0 = row == 0

    def long_conv(u, o):
        spec = _dot(f_ref[...], u.astype(BF16))
        ur, ui = spec[:length], spec[length:]
        hre, him = sre_ref[o], sim_ref[o]
        uihi = ui * him
        yr = ur * hre - jnp.where(row0, 0.0, uihi)
        yi = jnp.where(row0, uihi, ur * him + ui * hre)
        y = _dot(g_ref[...], jnp.concatenate([yr, yi], axis=0).astype(BF16))
        return y + u * hb_ref[o:o + 1]

    x1 = _short_conv(x1_ref[...], cw_ref[0], pos, seg, length)
    x2 = _short_conv(x2_ref[...], cw_ref[1], pos, seg, length)
    v = _short_conv(v_ref[...], cw_ref[2], pos, seg, length)
    z = x1 * long_conv(v, 0)
    o_ref[...] = (x2 * long_conv(z, 1)).astype(o_ref.dtype)


def hyena(zmain, conv_w, spec_re, spec_im, bias, f_mat, g_mat, length, n_seq, row_blk0, seg):
    ct = 256
    nct = HY_WIDTH // ct

    def zspec(col):
        return pl.BlockSpec((length, ct), lambda b, j, c=col // ct: (row_blk0 + b, c + j))

    return pl.pallas_call(
        functools.partial(_hyena_kernel, length=length, seg=seg),
        grid=(n_seq, nct),
        in_specs=[zspec(C_HX1), zspec(C_HX2), zspec(C_HV),
                  pl.BlockSpec((3, 3, ct), lambda b, j: (0, 0, j)),
                  pl.BlockSpec((2, length, ct), lambda b, j: (0, 0, j)),
                  pl.BlockSpec((2, length, ct), lambda b, j: (0, 0, j)),
                  pl.BlockSpec((2, ct), lambda b, j: (0, j)),
                  pl.BlockSpec((2 * length, length), lambda b, j: (0, 0)),
                  pl.BlockSpec((length, 2 * length), lambda b, j: (0, 0))],
        out_specs=pl.BlockSpec((length, ct), lambda b, j: (b, j)),
        out_shape=jax.ShapeDtypeStruct((n_seq * length, HY_WIDTH), BF16),
        compiler_params=_cp(("parallel", "parallel")),
        name="hyena",
    )(zmain, zmain, zmain, conv_w, spec_re, spec_im, bias, f_mat, g_mat)


def _tri_masks():
    t = lax.broadcasted_iota(jnp.int32, (CHUNK, CHUNK), 0)
    s = lax.broadcasted_iota(jnp.int32, (CHUNK, CHUNK), 1)
    return s <= t, s >= t


def _gla_chunk(q, k, v, g, state, mask, rev):
    tm = jnp.where(mask, 1.0, 0.0).astype(BF16)
    g1, g2, g3 = _split3(g)
    bc = _dot(tm, g1) + (_dot(tm, g2) + _dot(tm, g3))
    b_end = bc[0:1] if rev else bc[CHUNK - 1:CHUNK]
    ref = bc[CHUNK // 2:CHUNK // 2 + 1]
    inter = _dot((q * jnp.exp(bc)).astype(BF16), state.astype(BF16))
    qh = (q * jnp.exp(bc - ref)).astype(BF16)
    kh = (k * jnp.exp(ref - bc)).astype(BF16)
    att = jnp.where(mask, _dot_nt(qh, kh), 0.0)
    vb = v.astype(BF16)
    o = inter + _dot(att.astype(BF16), vb)
    ones = jnp.ones((CHUNK, GLA_DK), BF16)
    tot = _dot_tn(g1, ones) + (_dot_tn(g2, ones) + _dot_tn(g3, ones))
    e = jnp.exp(tot)
    kd = (k * jnp.exp(b_end - bc)).astype(BF16)
    new_state = jnp.concatenate([e, e], axis=1) * state + _dot_tn(kd, vb)
    return o, new_state


def _gla_kernel(*refs, length, n_sub, has_init):
    if has_init:
        (q_ref, k_ref, v_ref, gr_ref, zs_ref, wf_ref, wb_ref, ab_ref, ng_ref, s0_ref,
         y_ref, o_ref, lg_ref, st_ref) = refs
    else:
        (q_ref, k_ref, v_ref, gr_ref, zs_ref, wf_ref, wb_ref, ab_ref, ng_ref,
         y_ref, sout_ref, o_ref, lg_ref, st_ref) = refs
    zs = zs_ref[...]
    lg_ref[0] = _log_sigmoid(_dot3(zs, wf_ref[...]) + ab_ref[0]) * (1.0 / GLA_NORMALIZER)
    lg_ref[1] = _log_sigmoid(_dot3(zs, wb_ref[...]) + ab_ref[1]) * (1.0 / GLA_NORMALIZER)
    if has_init:
        st_ref[...] = s0_ref[...]
    else:
        st_ref[...] = jnp.zeros_like(st_ref)
    o_ref[...] = jnp.zeros_like(o_ref)
    n = length // CHUNK
    mask_f, mask_b = _tri_masks()
    scale = GLA_DK ** -0.5

    def body(i, carry):
        for s in range(n_sub):
            for d, mask in enumerate((mask_f, mask_b)):
                c = i if d == 0 else n - 1 - i
                rows = pl.ds(pl.multiple_of(s * length + c * CHUNK, CHUNK), CHUNK)
                for h in range(GLA_HEADS):
                    kc = slice(h * GLA_DK, (h + 1) * GLA_DK)
                    vc = slice(h * GLA_DV, (h + 1) * GLA_DV)
                    o, s_new = _gla_chunk(q_ref[rows, kc] * scale, k_ref[rows, kc], v_ref[rows, vc],
                                          lg_ref[d, rows, kc], st_ref[s, d, h], mask, d == 1)
                    o_ref[rows, vc] += o
                    st_ref[s, d, h] = s_new
        return carry

    lax.fori_loop(0, n, body, 0)
    for h in range(GLA_HEADS):
        vc = slice(h * GLA_DV, (h + 1) * GLA_DV)
        o = o_ref[:, vc]
        o = o * lax.rsqrt(jnp.mean(o * o, axis=-1, keepdims=True) + EPS) * ng_ref[...]
        gr = gr_ref[:, vc]
        y_ref[:, vc] = (o * (gr * jax.nn.sigmoid(gr))).astype(y_ref.dtype)
    if not has_init:
        sout_ref[...] = st_ref[...]


def gla(zmain, zsmall, wf, wb, ab, norm_g, length, n_seq, row_blk0, state0, layer, n_sub=1):
    has_init = state0 is not None
    qk_w = GLA_HEADS * GLA_DK
    v_w = GLA_HEADS * GLA_DV
    once = pl.Buffered(1)
    rows = n_sub * length
    blk0 = row_blk0 // n_sub
    st_blk = (n_sub, 2, GLA_HEADS, GLA_DK, GLA_DV)

    def zspec(col, width):
        return pl.BlockSpec((rows, width), lambda b, c=col // width: (blk0 + b, c), pipeline_mode=once)

    in_specs = [zspec(C_GQ, qk_w), zspec(C_GK, qk_w), zspec(C_GV, v_w), zspec(C_GR, v_w),
                pl.BlockSpec((rows, N_SMALL), lambda b: (blk0 + b, 0)),
                pl.BlockSpec((N_SMALL, qk_w), lambda b: (0, 0)),
                pl.BlockSpec((N_SMALL, qk_w), lambda b: (0, 0)),
                pl.BlockSpec((2, 1, qk_w), lambda b: (0, 0, 0)),
                pl.BlockSpec((1, GLA_DV), lambda b: (0, 0))]
    args = [zmain, zmain, zmain, zmain, zsmall, wf, wb, ab, norm_g]
    y_shape = jax.ShapeDtypeStruct((n_seq * length, v_w), BF16)
    y_spec = pl.BlockSpec((rows, v_w), lambda b: (b, 0))
    if has_init:
        in_specs.append(pl.BlockSpec((n_sub, None, 2, GLA_HEADS, GLA_DK, GLA_DV),
                                     lambda b: (b, layer, 0, 0, 0, 0)))
        args.append(state0)
        out_shape, out_specs = y_shape, y_spec
    else:
        out_shape = [y_shape, jax.ShapeDtypeStruct((n_seq, 2, GLA_HEADS, GLA_DK, GLA_DV), F32)]
        out_specs = [y_spec, pl.BlockSpec(st_blk, lambda b: (b, 0, 0, 0, 0))]
    return pl.pallas_call(
        functools.partial(_gla_kernel, length=length, n_sub=n_sub, has_init=has_init),
        grid=(n_seq // n_sub,),
        in_specs=in_specs,
        out_specs=out_specs,
        out_shape=out_shape,
        scratch_shapes=[pltpu.VMEM((rows, v_w), F32), pltpu.VMEM((2, rows, qk_w), F32),
                        pltpu.VMEM(st_blk, F32)],
        compiler_params=_cp(("parallel",)),
        name="gla",
    )(*args)


def _mlstm_chunk(q, ks, v, lf_c, li_c, lf_r, li_r, cm, nv, m_prev, mask, mask_t, rev):
    t_n = CHUNK
    tm = jnp.where(mask, 1.0, 0.0).astype(BF16)
    tmt = jnp.where(mask_t, 1.0, 0.0).astype(BF16)
    c1, c2, c3 = _split3(jnp.broadcast_to(lf_c, (t_n, t_n)))
    b_colb = _dot(tm, c1) + (_dot(tm, c2) + _dot(tm, c3))
    r1, r2, r3 = _split3(jnp.broadcast_to(lf_r, (t_n, t_n)))
    b_rowb = _dot(r1, tmt) + (_dot(r2, tmt) + _dot(r3, tmt))
    b_col = b_colb[:, 0:1]
    b_row = b_rowb[0:1, :]
    b_end = b_colb[0:1, 0:1] if rev else b_colb[t_n - 1:t_n, 0:1]
    dmat = jnp.where(mask, b_colb - b_rowb + li_r, -jnp.inf)
    m_t = jnp.maximum(b_col + m_prev, jnp.max(dmat, axis=-1, keepdims=True))
    w_inter = jnp.exp(b_col + m_prev - m_t)
    qb = q.astype(BF16)
    vb = v.astype(BF16)
    sc = _dot_nt(qb, ks.astype(BF16)) * jnp.exp(dmat - m_t)
    num = w_inter * _dot(qb, cm.astype(BF16)) + _dot(sc.astype(BF16), vb)
    den = w_inter * jnp.sum(q * nv, axis=-1, keepdims=True) + jnp.sum(sc, axis=-1, keepdims=True)
    h = num / jnp.maximum(jnp.abs(den), jnp.exp(-m_t))
    g_r = b_end - b_row + li_r
    g_c = b_end - b_col + li_c
    m_new = jnp.maximum(b_end + m_prev, jnp.max(g_r, axis=-1, keepdims=True))
    w_c = jnp.exp(b_end + m_prev - m_new)
    kw = ks * jnp.exp(g_c - m_new)
    cm_new = w_c * cm + _dot_tn(kw.astype(BF16), vb)
    nv_new = w_c * nv + jnp.sum(kw, axis=0, keepdims=True)
    return h, cm_new, nv_new, m_new


def _mlstm_kernel(*refs, length, n_sub, seg, has_init):
    if has_init:
        (gb_ref, m0_ref, q_ref, k_ref, v_ref, mo_ref, gc_ref, gr_ref, cw_ref, ng_ref, c0_ref, n0_ref,
         y_ref, qc_ref, kc_ref, h_ref, c_ref, n_ref, m_ref) = refs
    else:
        (gb_ref, q_ref, k_ref, v_ref, mo_ref, gc_ref, gr_ref, cw_ref, ng_ref,
         y_ref, cout_ref, nout_ref, mout_ref, qc_ref, kc_ref, h_ref, c_ref, n_ref, m_ref) = refs
    b_idx = pl.program_id(0)
    dh = ML_DH
    rows_all = n_sub * length
    row = lax.broadcasted_iota(jnp.int32, (rows_all, 1), 0)
    pos = row % seg
    qc_ref[...] = _short_conv(q_ref[...], cw_ref[0], pos, seg, rows_all)
    kc_ref[...] = _short_conv(k_ref[...], cw_ref[1], pos, seg, rows_all) * (dh ** -0.5)
    h_ref[...] = jnp.zeros_like(h_ref)
    if has_init:
        c_ref[...] = c0_ref[...]
        n_ref[...] = n0_ref[...]
        for s in range(n_sub):
            for d in range(2):
                for h in range(ML_HEADS):
                    m0 = m0_ref[(b_idx * n_sub + s) * 2 * ML_HEADS + d * ML_HEADS + h]
                    m_ref[s, d, h] = jnp.full((1, 128), m0, F32)
    else:
        c_ref[...] = jnp.zeros_like(c_ref)
        n_ref[...] = jnp.zeros_like(n_ref)
        m_ref[...] = jnp.zeros_like(m_ref)
    n = length // CHUNK
    mask_f, mask_b = _tri_masks()

    def body(i, carry):
        for s in range(n_sub):
            for d, (mask, mask_t) in enumerate(((mask_f, mask_b), (mask_b, mask_f))):
                c = i if d == 0 else n - 1 - i
                rows = pl.ds(pl.multiple_of(s * length + c * CHUNK, CHUNK), CHUNK)
                for h in range(ML_HEADS):
                    cols = slice(h * dh, (h + 1) * dh)
                    bi = gb_ref[d * 2 * ML_HEADS + h]
                    bf = gb_ref[d * 2 * ML_HEADS + ML_HEADS + h]
                    gcol = gc_ref[h, rows, :]
                    grow = gr_ref[h, s * n + c]
                    li_c = gcol[:, d:d + 1] + bi
                    lf_c = _log_sigmoid(gcol[:, 2 + d:3 + d] + bf)
                    li_r = grow[d:d + 1, :] + bi
                    lf_r = _log_sigmoid(grow[2 + d:3 + d, :] + bf)
                    hc, cm, nv, m_new = _mlstm_chunk(qc_ref[rows, cols], kc_ref[rows, cols], v_ref[rows, cols],
                                                     lf_c, li_c, lf_r, li_r, c_ref[s, d, h], n_ref[s, d, h],
                                                     m_ref[s, d, h][:, 0:1], mask, mask_t, d == 1)
                    h_ref[rows, cols] += hc
                    c_ref[s, d, h] = cm
                    n_ref[s, d, h] = nv
                    m_ref[s, d, h] = jnp.broadcast_to(m_new, (1, 128))
        return carry

    lax.fori_loop(0, n, body, 0)
    for h in range(ML_HEADS):
        cols = slice(h * dh, (h + 1) * dh)
        o = h_ref[:, cols]
        o = o * lax.rsqrt(jnp.mean(o * o, axis=-1, keepdims=True) + EPS) * ng_ref[...]
        y_ref[:, cols] = (o * jax.nn.sigmoid(mo_ref[:, cols])).astype(y_ref.dtype)
    if not has_init:
        cout_ref[...] = c_ref[...]
        nout_ref[...] = n_ref[...]
        mout_ref[...] = m_ref[...]


def mlstm(zmain, gates_col, gates_row, gate_b, conv_w, norm_g, length, n_seq, row_blk0, seg, init, layer,
          n_sub=1):
    has_init = init is not None
    dh = ML_DH
    nh = ML_HEADS
    width = nh * dh
    once = pl.Buffered(1)
    rows = n_sub * length
    blk0 = row_blk0 // n_sub

    def zspec(col):
        return pl.BlockSpec((rows, width), lambda b, c=col // width: (blk0 + b, c), pipeline_mode=once)

    smem = pl.BlockSpec(memory_space=pltpu.SMEM)
    in_specs = [smem]
    args = [gate_b]
    if has_init:
        c0, n0, m0 = init
        in_specs.append(smem)
        args.append(m0)
    in_specs += [zspec(C_MQ), zspec(C_MK), zspec(C_MV), zspec(C_MO),
                 pl.BlockSpec((nh, rows, 8), lambda b: (0, blk0 + b, 0)),
                 pl.BlockSpec((nh, rows // CHUNK, 8, CHUNK), lambda b: (0, blk0 + b, 0, 0)),
                 pl.BlockSpec((2, 3, width), lambda b: (0, 0, 0)),
                 pl.BlockSpec((1, dh), lambda b: (0, 0))]
    args += [zmain, zmain, zmain, zmain, gates_col, gates_row, conv_w, norm_g]
    y_shape = jax.ShapeDtypeStruct((n_seq * length, width), BF16)
    y_spec = pl.BlockSpec((rows, width), lambda b: (b, 0))
    if has_init:
        in_specs += [pl.BlockSpec((n_sub, None, 2, nh, dh, dh), lambda b: (b, layer, 0, 0, 0, 0)),
                     pl.BlockSpec((n_sub, None, 2, nh, 1, dh), lambda b: (b, layer, 0, 0, 0, 0))]
        args += [c0, n0]
        out_shape, out_specs = y_shape, y_spec
    else:
        out_shape = [y_shape,
                     jax.ShapeDtypeStruct((n_seq, 2, nh, dh, dh), F32),
                     jax.ShapeDtypeStruct((n_seq, 2, nh, 1, dh), F32),
                     jax.ShapeDtypeStruct((n_seq, 2, nh, 1, 128), F32)]
        out_specs = [y_spec,
                     pl.BlockSpec((n_sub, 2, nh, dh, dh), lambda b: (b, 0, 0, 0, 0)),
                     pl.BlockSpec((n_sub, 2, nh, 1, dh), lambda b: (b, 0, 0, 0, 0)),
                     pl.BlockSpec((n_sub, 2, nh, 1, 128), lambda b: (b, 0, 0, 0, 0))]
    return pl.pallas_call(
        functools.partial(_mlstm_kernel, length=length, n_sub=n_sub, seg=seg, has_init=has_init),
        grid=(n_seq // n_sub,),
        in_specs=in_specs,
        out_specs=out_specs,
        out_shape=out_shape,
        scratch_shapes=[pltpu.VMEM((rows, width), F32), pltpu.VMEM((rows, width), F32),
                        pltpu.VMEM((rows, width), F32),
                        pltpu.VMEM((n_sub, 2, nh, dh, dh), F32), pltpu.VMEM((n_sub, 2, nh, 1, dh), F32),
                        pltpu.VMEM((n_sub, 2, nh, 1, 128), F32)],
        compiler_params=_cp(("parallel",)),
        name="mlstm",
    )(*args)


def _top_values(x, count, with_rank=False):
    vals = []
    cur = x
    rank = jnp.full(x.shape, float(count), F32) if with_rank else None
    for r in range(count):
        m = jnp.max(cur, axis=0, keepdims=True)
        vals.append(m)
        if with_rank or r + 1 < count:
            top = cur == m
            cur = jnp.where(top, -jnp.inf, cur)
            if with_rank:
                rank = jnp.where(top, float(r), rank)
    return (vals, rank) if with_rank else vals


def _peer_topk_kernel(q_ref, k_ref, nb_ref, rk_ref, e1_ref, e2_ref):
    dk = PEER_NKEYS
    s1 = _dot3(k_ref[0], q_ref[:, 0:dk], dot=_dot_nt)
    s2 = _dot3(k_ref[1], q_ref[:, dk:2 * dk], dot=_dot_nt)
    v1 = _top_values(s1, PEER_TOPK)
    v2, rank2 = _top_values(s2, PEER_TOPK, with_rank=True)
    rows = [v1[a] + v2[b] for a in range(PEER_TOPK) for b in range(PEER_TOPK // (a + 1))]
    rows += [jnp.full_like(rows[0], -jnp.inf)] * (-len(rows) % 8)
    cand = jnp.concatenate(rows, axis=0)
    best = _top_values(cand, PEER_TOPK)
    zsum = jnp.exp(best[0] - best[0])
    for r in range(1, PEER_TOPK):
        zsum = zsum + jnp.exp(best[r] - best[0])
    theta = best[PEER_TOPK - 1]
    dense_b = 4
    nb = jnp.zeros_like(s1)
    for b in range(dense_b):
        nb = nb + jnp.where(s1 + v2[b] >= theta, 1.0, 0.0)
    for a in range(PEER_TOPK // (dense_b + 1)):
        extra = jnp.zeros_like(theta)
        for b in range(dense_b, PEER_TOPK // (a + 1)):
            extra = extra + jnp.where(v1[a] + v2[b] >= theta, 1.0, 0.0)
        nb = nb + jnp.where(s1 == v1[a], extra, 0.0)
    nb_ref[...] = nb
    rk_ref[...] = pltpu.bitcast(rank2.astype(BF16), jnp.uint32)
    e1_ref[...] = jnp.exp(s1 - v1[0]) / zsum
    e2_ref[...] = pltpu.bitcast(jnp.exp(s2 - v2[0]).astype(BF16), jnp.uint32)


def peer_topk(q, keys, tt=256):
    n = q.shape[0]
    nk = PEER_NKEYS
    spec = pl.BlockSpec((None, nk, tt), lambda i, h: (h, 0, i))
    pspec = pl.BlockSpec((None, nk // 2, tt), lambda i, h: (h, 0, i))
    full = jax.ShapeDtypeStruct((PEER_HEADS, nk, n), F32)
    packed = jax.ShapeDtypeStruct((PEER_HEADS, nk // 2, n), jnp.uint32)
    return pl.pallas_call(
        _peer_topk_kernel,
        grid=(n // tt, PEER_HEADS),
        in_specs=[pl.BlockSpec((tt, 2 * nk), lambda i, h: (i, h)),
                  pl.BlockSpec((None, 2, nk, nk), lambda i, h: (h, 0, 0, 0))],
        out_specs=[spec, pspec, spec, pspec],
        out_shape=[full, packed, full, packed],
        compiler_params=_cp(("parallel", "parallel")),
        name="peer_topk",
    )(q, keys)


def _row_bcast(row, n):
    t = jnp.broadcast_to(row, (16, 128)).astype(BF16)
    return jnp.broadcast_to(t[None], (n // 16, 16, 128)).reshape(n, 128)


def _gelu_tanh(x):
    return 0.5 * x * (1.0 + jnp.tanh(math.sqrt(2.0 / math.pi) * (x + 0.044715 * (x * x * x))))


def _peer_expert_kernel(ht_ref, u_ref, vt_ref, nb_ref, rk_ref, e1_ref, e2_ref, x_ref, gt_ref,
                        o_ref, acc_ref, w_ref, *, n_i1, tt):
    j = pl.program_id(1)
    nk = PEER_NKEYS

    @pl.when(j == 0)
    def _():
        acc_ref[...] = jnp.zeros_like(acc_ref)

    act = _dot(u_ref[...], ht_ref[...])
    zero = jnp.zeros((), BF16)
    for ts in range(tt // 128):
        lanes = slice(ts * 128, (ts + 1) * 128)
        for r in range(n_i1):
            w = None
            for h in range(PEER_HEADS):
                nbr = _row_bcast(nb_ref[h, r:r + 1, lanes], nk)
                e1r = _row_bcast(e1_ref[h, r:r + 1, lanes], nk)
                rk = pltpu.bitcast(rk_ref[h, :, lanes], BF16)
                e2 = pltpu.bitcast(e2_ref[h, :, lanes], BF16)
                term = jnp.where(rk < nbr, e2 * e1r, zero)
                w = term if w is None else w + term
            w_ref[r * nk:(r + 1) * nk, lanes] = w
    p = w_ref[...] * _gelu_tanh(act).astype(BF16)
    acc_ref[...] += _dot(vt_ref[...], p)

    @pl.when(j == pl.num_programs(1) - 1)
    def _():
        o_ref[...] = x_ref[...] + gt_ref[...] * acc_ref[...].T


def _xpose_cast_kernel(x_ref, o_ref):
    o_ref[...] = x_ref[...].T.astype(BF16)


def transpose_cast_chunks(tab, ec):
    depth, e, d = tab.shape
    return pl.pallas_call(
        _xpose_cast_kernel,
        grid=(depth, e // ec),
        in_specs=[pl.BlockSpec((None, ec, d), lambda l, c: (l, c, 0))],
        out_specs=pl.BlockSpec((None, None, d, ec), lambda l, c: (l, c, 0, 0)),
        out_shape=jax.ShapeDtypeStruct((depth, e // ec, d, ec), BF16),
        compiler_params=_cp(("parallel", "parallel")),
        name="transpose_cast_chunks",
    )(tab)


def peer_experts(h2t, u_tab, vt_tab, layer, nb, rk, e1, e2, x, mod3, gt_chunk, tt=512):
    ec = PEER_CHUNK
    n = x.shape[0]
    nk = PEER_NKEYS
    n_i1 = ec // nk
    once = pl.Buffered(1)
    sspec = pl.BlockSpec((PEER_HEADS, nk // 2, tt), lambda i, j: (0, 0, i), pipeline_mode=once)
    rspec = pl.BlockSpec((PEER_HEADS, n_i1, tt), lambda i, j: (0, j, i))
    return pl.pallas_call(
        functools.partial(_peer_expert_kernel, n_i1=n_i1, tt=tt),
        grid=(n // tt, PEER_EXPERTS // ec),
        in_specs=[pl.BlockSpec((D_MODEL, tt), lambda i, j: (0, i)),
                  pl.BlockSpec((None, ec, D_MODEL), lambda i, j: (layer, j, 0)),
                  pl.BlockSpec((None, None, D_MODEL, ec), lambda i, j: (layer, j, 0, 0)),
                  rspec, sspec, rspec, sspec,
                  pl.BlockSpec((tt, D_MODEL), lambda i, j: (i, 0), pipeline_mode=once),
                  pl.BlockSpec((None, 1, D_MODEL), lambda i, j: (_mod_row(i, tt), 0, gt_chunk))],
        out_specs=pl.BlockSpec((tt, D_MODEL), lambda i, j: (i, 0)),
        out_shape=jax.ShapeDtypeStruct((n, D_MODEL), F32),
        scratch_shapes=[pltpu.VMEM((D_MODEL, tt), F32), pltpu.VMEM((ec, tt), BF16)],
        compiler_params=_cp(("parallel", "arbitrary")),
        name="peer_experts",
    )(h2t, u_tab, vt_tab, nb, rk, e1, e2, x, mod3)


def _reorder_cast_kernel(main_ref, spill_ref, o_ref, *, tile_groups):
    j = pl.program_id(1)

    def emit(shift):
        if shift == 0:
            o_ref[...] = main_ref[...].astype(BF16)
        else:
            shifted = jnp.concatenate([main_ref[:, shift:], spill_ref[:, :shift]], axis=1)
            o_ref[...] = shifted.astype(BF16)

    first = 0
    for n_tiles, shift in tile_groups:
        pl.when((j >= first) & (j < first + n_tiles))(functools.partial(emit, shift))
        first += n_tiles


def reorder_cast_in_proj(w_in):
    tn, tr = 1024, 1024
    depth, k, _ = w_in.shape
    tile_groups = ((C_MQ // tn, 0), ((C_GA - C_MQ) // tn, 2 * GLA_RANK), ((N_MAIN - C_GA) // tn, 2 * GLA_RANK + 4 * ML_HEADS))
    return pl.pallas_call(
        functools.partial(_reorder_cast_kernel, tile_groups=tile_groups),
        grid=(depth, N_MAIN // tn, k // tr),
        in_specs=[pl.BlockSpec((None, tr, tn), lambda l, j, r: (l, r, j)),
                  pl.BlockSpec((None, tr, 128), lambda l, j, r: (l, r, (j + 1) * (tn // 128)))],
        out_specs=pl.BlockSpec((None, tr, tn), lambda l, j, r: (l, r, j)),
        out_shape=jax.ShapeDtypeStruct((depth, k, N_MAIN), BF16),
        compiler_params=_cp(("parallel", "parallel", "parallel")),
        name="reorder_cast_in_proj",
    )(w_in, w_in)


def _reorder_in_proj(w_in, b_in):
    splits = (1024, 1024, 1024, 512, 512, 1024, 1024, 16, 16, 1024, 1024, 1024, 1024, 8, 8, 2048, 2048, 2048)
    offs = np.concatenate([[0], np.cumsum(splits)])
    seg = lambda a, i: a[..., offs[i]:offs[i + 1]]
    main_ids = (0, 1, 2, 3, 4, 5, 6, 9, 10, 11, 12, 15, 16, 17)
    small_ids = (7, 8, 13, 14)
    assert offs[9] - C_MQ == 2 * GLA_RANK and offs[15] - C_GA == 2 * GLA_RANK + 4 * ML_HEADS
    w_main = reorder_cast_in_proj(w_in)
    b_main = jnp.concatenate([seg(b_in, i) for i in main_ids], axis=-1)
    w_small = jnp.concatenate([seg(w_in, i) for i in small_ids], axis=-1)
    b_small = jnp.concatenate([seg(b_in, i) for i in small_ids], axis=-1)
    pad = N_SMALL - w_small.shape[-1]
    w_small = jnp.pad(w_small, ((0, 0), (0, 0), (0, pad)))
    b_small = jnp.pad(b_small, ((0, 0), (0, pad)))
    return w_main, b_main, w_small, b_small


def kernel(x_prompt, x_sample, c, state_gla, state_mlstm_C, state_mlstm_n, state_mlstm_m, c_ctx,
           mod_w, mod_b, norm1_g, norm2_g, final_g, w_in, b_in, hy_conv, hy_w1, hy_b1, hy_w2, hy_b2,
           hy_w3, hy_freq, hy_bias, gla_a2_w, gla_a2_b, gla_norm_g, ml_conv, ml_gate_b, ml_norm_g,
           w_branch, w_out, peer_wq, peer_keys, peer_u, peer_v):
    w_main, b_main, w_small, b_small = _reorder_in_proj(w_in, b_in)
    w_small_hi = w_small.astype(BF16)
    w_small_lo = (w_small - w_small_hi.astype(F32)).astype(BF16)
    w_branch_b = w_branch.astype(BF16)
    w_out_b = w_out.astype(BF16)
    wq_hi = peer_wq.astype(BF16)
    wq_lo = (peer_wq - wq_hi.astype(F32)).astype(BF16)
    u_b = peer_u.astype(BF16)
    vt_b = transpose_cast_chunks(peer_v, PEER_CHUNK)
    zero_b = jnp.zeros((1, D_MODEL), F32)
    a2f = jnp.pad(gla_a2_w[:, 0], ((0, 0), (0, N_SMALL - GLA_RANK), (0, 0)))
    a2b = jnp.pad(gla_a2_w[:, 1], ((0, 0), (GLA_RANK, N_SMALL - 2 * GLA_RANK), (0, 0)))
    a2bias = gla_a2_b.reshape(DEPTH, 2, 1, GLA_HEADS * GLA_DK)
    n0_all = state_mlstm_n.reshape(DEC_BATCH, DEPTH, 2, ML_HEADS, 1, ML_DH)

    tabs = {}
    for length in (SEQ, DEC_SEQ):
        fre, fim, g_mat, sgn = _dft_tables(length)
        fre_h, fre_l = _split2(fre)
        fim_h, fim_l = _split2(fim)
        tabs[length] = dict(spec=(fre_h, fre_l, fim_h, fim_l, sgn),
                            f=jnp.concatenate([fre_h, fim_h], axis=0), g=g_mat.astype(BF16))

    cvec = jnp.zeros((MOD_ROWS, D_MODEL), F32).at[:DEC_BATCH].set(c).at[CTX_ROW].set(c_ctx)
    mod_all = mod_table(cvec, mod_w, mod_b)

    x = jnp.concatenate([x_prompt.reshape(N_PROMPT, D_MODEL), x_sample.reshape(N_SAMPLE, D_MODEL)], axis=0)
    groups = ((SEQ, BATCH, 0, SEQ), (DEC_SEQ, DEC_BATCH, N_PROMPT // DEC_SEQ, GRID_W))
    new_gla, new_c, new_n, new_m = [], [], [], []
    for l in range(DEPTH):
        mod3 = mod_all[l].reshape(MOD_ROWS, 1, 6 * D_MODEL)
        h_hi, zsmall = normmod_gates(x, norm1_g[l], mod3, 0, 1, w_small_hi, w_small_lo, l,
                                     b_small[l].reshape(1, N_SMALL))
        zmain = mm_bias(h_hi, w_main, l, b_main[l].reshape(1, N_MAIN), tm=1024, tn=2048)
        mi = zsmall[:, 32:40].reshape(N_ROWS, 2, ML_HEADS)
        mf = zsmall[:, 40:48].reshape(N_ROWS, 2, ML_HEADS)
        gcol = jnp.concatenate([mi, mf, jnp.zeros((N_ROWS, 4, ML_HEADS), F32)], axis=1)
        gates_col = jnp.transpose(gcol, (2, 0, 1))
        gates_row = jnp.transpose(gcol.reshape(N_ROWS // CHUNK, CHUNK, 8, ML_HEADS), (3, 0, 2, 1))
        gate_b = ml_gate_b[l].reshape(-1)
        y_hy, y_gla, y_ml = [], [], []
        for gi, (length, n_seq, blk0, seg) in enumerate(groups):
            t = tabs[length]
            sre, sim = hyena_spectrum(length, hy_w1[l], hy_b1[l], hy_w2[l], hy_b2[l], hy_w3[l],
                                      hy_freq[l], t["spec"])
            y_hy.append(hyena(zmain, hy_conv[l], sre, sim, hy_bias[l], t["f"], t["g"],
                              length, n_seq, blk0, seg))
            gla_args = (zmain, zsmall, a2f[l], a2b[l], a2bias[l], gla_norm_g[l].reshape(1, GLA_DV),
                        length, n_seq, blk0)
            ml_args = (zmain, gates_col, gates_row, gate_b, ml_conv[l], ml_norm_g[l].reshape(1, ML_DH),
                       length, n_seq, blk0, seg)
            if gi == 0:
                yg, s_fin = gla(*gla_args, None, l, n_sub=PROMPT_SEQS_PER_STEP)
                ym, c_fin, n_fin, m_fin = mlstm(*ml_args, None, l, n_sub=PROMPT_SEQS_PER_STEP)
                new_gla.append(s_fin)
                new_c.append(c_fin)
                new_n.append(n_fin[:, :, :, 0, :])
                new_m.append(m_fin[:, :, :, 0, 0])
            else:
                yg = gla(*gla_args, state_gla, l)
                ym = mlstm(*ml_args, (state_mlstm_C, n0_all, state_mlstm_m[:, l].reshape(-1)), l)
            y_gla.append(yg)
            y_ml.append(ym)
        merged = merge_branches(y_hy, y_gla, y_ml, w_branch_b, l, zmain)
        x = mm_residual(merged, w_out_b, l, x, mod3, 2)
        h2, h2_lo, h2t = normmod_peer(x, norm2_g[l], mod3, 3, 4)
        q = mm3_bias(h2, h2_lo, wq_hi, wq_lo, l, zero_b)
        nb, rk, e1, e2 = peer_topk(q, peer_keys[l])
        x = peer_experts(h2t, u_b, vt_b, l, nb, rk, e1, e2, x, mod3, 5)

    y_prompt = final_norm(x, final_g, 0, N_PROMPT).reshape(BATCH, SEQ, D_MODEL)
    y_sample = final_norm(x, final_g, N_PROMPT, N_SAMPLE).reshape(DEC_BATCH, DEC_SEQ, D_MODEL)
    return (y_prompt, y_sample, jnp.stack(new_gla, axis=1), jnp.stack(new_c, axis=1),
            jnp.stack(new_n, axis=1), jnp.stack(new_m, axis=1))
```

```python
import functools
import math

import jax
import jax.numpy as jnp
import numpy as np
from jax import lax
from jax.experimental import pallas as pl
from jax.experimental.pallas import tpu as pltpu

F32 = jnp.float32
BF16 = jnp.bfloat16

D_MODEL = 2048
BATCH = 16
SEQ = 256
DEPTH = 4
DEC_BATCH = 4
DEC_SEQ = 1024
GRID_W = 64
EPS = 1e-6
CHUNK = 64
HY_WIDTH = 1024
HY_EMB = 33
HY_BANDS = (HY_EMB - 1) // 2
HY_FFN = 64
HY_DECAY_TARGET = 1e-2
HY_FAST_PCT = 0.3
HY_SLOW_PCT = 1.5
GLA_HEADS = 4
GLA_DK = 128
GLA_DV = 256
GLA_RANK = 16
GLA_NORMALIZER = 16.0
ML_HEADS = 4
ML_DH = 256
PEER_HEADS = 8
PEER_NKEYS = 128
PEER_EXPERTS = PEER_NKEYS * PEER_NKEYS
PEER_TOPK = 16
PEER_CHUNK = 1024
PROMPT_SEQS_PER_STEP = 2

N_PROMPT = BATCH * SEQ
N_SAMPLE = DEC_BATCH * DEC_SEQ
N_ROWS = N_PROMPT + N_SAMPLE
CTX_ROW = DEC_BATCH
MOD_ROWS = 8

C_HX1, C_HX2, C_HV = 0, 1024, 2048
C_GQ, C_GK, C_GV, C_GR = 3072, 3584, 4096, 5120
C_MQ, C_MK, C_MV, C_MO = 6144, 7168, 8192, 9216
C_GA, C_GB, C_GC = 10240, 12288, 14336
N_MAIN = 16384
N_SMALL = 128

VMEM_LIMIT = 56 * 1024 * 1024


def _cp(sem):
    return pltpu.CompilerParams(dimension_semantics=sem, vmem_limit_bytes=VMEM_LIMIT)


def _dot(a, b):
    return jnp.dot(a, b, preferred_element_type=F32)


def _dot_nt(a, b):
    return lax.dot_general(a, b, (((1,), (1,)), ((), ())), preferred_element_type=F32)


def _dot_tn(a, b):
    return lax.dot_general(a, b, (((0,), (0,)), ((), ())), preferred_element_type=F32)


def _split2(a):
    hi = a.astype(BF16)
    lo = (a - hi.astype(F32)).astype(BF16)
    return hi, lo


def _split3(a):
    a1 = a.astype(BF16)
    r1 = a - a1.astype(F32)
    a2 = r1.astype(BF16)
    a3 = (r1 - a2.astype(F32)).astype(BF16)
    return a1, a2, a3


def _dot3(a, b, dot=_dot):
    ah, al = _split2(a)
    bh, bl = _split2(b)
    return dot(ah, bh) + (dot(ah, bl) + dot(al, bh))


def _log_sigmoid(x):
    return jnp.minimum(x, 0.0) - jnp.log(1.0 + jnp.exp(-jnp.abs(x)))


def _mod_row(i, tm):
    n_p = N_PROMPT // tm
    return jnp.where(i < n_p, CTX_ROW, (i - n_p) // (DEC_SEQ // tm))


def _mod_kernel(c_ref, w_ref, b_ref, o_ref):
    a = c_ref[...]
    a = a * jax.nn.sigmoid(a)
    o_ref[0] = _dot3(a, w_ref[0]) + b_ref[0]


def mod_table(cvec, mod_w, mod_b):
    tn = 1024
    n = mod_w.shape[-1]
    return pl.pallas_call(
        _mod_kernel,
        grid=(DEPTH, n // tn),
        in_specs=[pl.BlockSpec((MOD_ROWS, D_MODEL), lambda l, j: (0, 0)),
                  pl.BlockSpec((1, D_MODEL, tn), lambda l, j: (l, 0, j)),
                  pl.BlockSpec((1, 1, tn), lambda l, j: (l, 0, j))],
        out_specs=pl.BlockSpec((1, MOD_ROWS, tn), lambda l, j: (l, 0, j)),
        out_shape=jax.ShapeDtypeStruct((DEPTH, MOD_ROWS, n), F32),
        compiler_params=_cp(("parallel", "parallel")),
        name="mod_table",
    )(cvec, mod_w, mod_b.reshape(DEPTH, 1, n))


def _norm_modulate(x_ref, g_ref, sc_ref, sh_ref):
    x = x_ref[...]
    y = x * lax.rsqrt(jnp.mean(x * x, axis=-1, keepdims=True) + EPS) * g_ref[...]
    return y * (1.0 + sc_ref[...]) + sh_ref[...]


def _normmod_gates_kernel(x_ref, g_ref, sc_ref, sh_ref, wh_ref, wl_ref, b_ref, hi_ref, zs_ref):
    h = _norm_modulate(x_ref, g_ref, sc_ref, sh_ref)
    hi = h.astype(BF16)
    lo = (h - hi.astype(F32)).astype(BF16)
    hi_ref[...] = hi
    zs_ref[...] = _dot(hi, wh_ref[...]) + (_dot(hi, wl_ref[...]) + _dot(lo, wh_ref[...])) + b_ref[...]


def _normmod_peer_kernel(x_ref, g_ref, sc_ref, sh_ref, hi_ref, lo_ref, t_ref):
    h = _norm_modulate(x_ref, g_ref, sc_ref, sh_ref)
    hi = h.astype(BF16)
    hi_ref[...] = hi
    lo_ref[...] = (h - hi.astype(F32)).astype(BF16)
    t_ref[...] = h.T.astype(BF16)


def _normmod_specs(tm, sh_chunk, sc_chunk):
    return [pl.BlockSpec((tm, D_MODEL), lambda i: (i, 0)),
            pl.BlockSpec((1, D_MODEL), lambda i: (0, 0)),
            pl.BlockSpec((None, 1, D_MODEL), lambda i: (_mod_row(i, tm), 0, sc_chunk)),
            pl.BlockSpec((None, 1, D_MODEL), lambda i: (_mod_row(i, tm), 0, sh_chunk))]


def normmod_gates(x, g, mod3, sh_chunk, sc_chunk, w_hi, w_lo, layer, b):
    tm = 256
    n = w_hi.shape[-1]
    row_spec = pl.BlockSpec((tm, D_MODEL), lambda i: (i, 0))
    w_spec = pl.BlockSpec((None, D_MODEL, n), lambda i: (layer, 0, 0))
    return pl.pallas_call(
        _normmod_gates_kernel,
        grid=(N_ROWS // tm,),
        in_specs=_normmod_specs(tm, sh_chunk, sc_chunk) + [w_spec, w_spec, pl.BlockSpec((1, n), lambda i: (0, 0))],
        out_specs=[row_spec, pl.BlockSpec((tm, n), lambda i: (i, 0))],
        out_shape=[jax.ShapeDtypeStruct((N_ROWS, D_MODEL), BF16), jax.ShapeDtypeStruct((N_ROWS, n), F32)],
        compiler_params=_cp(("parallel",)),
        name="normmod_gates",
    )(x, g.reshape(1, D_MODEL), mod3, mod3, w_hi, w_lo, b)


def normmod_peer(x, g, mod3, sh_chunk, sc_chunk):
    tm = 256
    row_spec = pl.BlockSpec((tm, D_MODEL), lambda i: (i, 0))
    row_shape = jax.ShapeDtypeStruct((N_ROWS, D_MODEL), BF16)
    return pl.pallas_call(
        _normmod_peer_kernel,
        grid=(N_ROWS // tm,),
        in_specs=_normmod_specs(tm, sh_chunk, sc_chunk),
        out_specs=[row_spec, row_spec, pl.BlockSpec((D_MODEL, tm), lambda i: (0, i))],
        out_shape=[row_shape, row_shape, jax.ShapeDtypeStruct((D_MODEL, N_ROWS), BF16)],
        compiler_params=_cp(("parallel",)),
        name="normmod_peer",
    )(x, g.reshape(1, D_MODEL), mod3, mod3)


def _final_norm_kernel(x_ref, g_ref, o_ref):
    x = x_ref[...]
    o_ref[...] = x * lax.rsqrt(jnp.mean(x * x, axis=-1, keepdims=True) + EPS) * g_ref[...]


def final_norm(x, g, row0, n_rows):
    tm = 256
    return pl.pallas_call(
        _final_norm_kernel,
        grid=(n_rows // tm,),
        in_specs=[pl.BlockSpec((tm, D_MODEL), lambda i: (row0 // tm + i, 0)),
                  pl.BlockSpec((1, D_MODEL), lambda i: (0, 0))],
        out_specs=pl.BlockSpec((tm, D_MODEL), lambda i: (i, 0)),
        out_shape=jax.ShapeDtypeStruct((n_rows, D_MODEL), F32),
        compiler_params=_cp(("parallel",)),
        name="final_norm",
    )(x, g.reshape(1, D_MODEL))


def _mm_bias_kernel(x_ref, w_ref, b_ref, o_ref):
    o_ref[...] = (_dot(x_ref[...], w_ref[...]) + b_ref[...]).astype(o_ref.dtype)


def mm_bias(x, w, layer, b, out_dtype=F32, tm=512, tn=1024):
    m, k = x.shape
    n = w.shape[-1]
    return pl.pallas_call(
        _mm_bias_kernel,
        grid=(n // tn, m // tm),
        in_specs=[pl.BlockSpec((tm, k), lambda j, i: (i, 0)),
                  pl.BlockSpec((None, k, tn), lambda j, i: (layer, 0, j)),
                  pl.BlockSpec((1, tn), lambda j, i: (0, j))],
        out_specs=pl.BlockSpec((tm, tn), lambda j, i: (i, j)),
        out_shape=jax.ShapeDtypeStruct((m, n), out_dtype),
        compiler_params=_cp(("parallel", "parallel")),
        name="mm_bias",
    )(x, w, b)


def _mm3_bias_kernel(xh_ref, xl_ref, wh_ref, wl_ref, b_ref, o_ref):
    xh = xh_ref[...]
    acc = _dot(xh, wh_ref[...]) + (_dot(xh, wl_ref[...]) + _dot(xl_ref[...], wh_ref[...]))
    o_ref[...] = acc + b_ref[...]


def mm3_bias(xh, xl, wh, wl, layer, b, tm=512, tn=1024):
    m, k = xh.shape
    n = wh.shape[-1]
    tn = min(tn, n)
    return pl.pallas_call(
        _mm3_bias_kernel,
        grid=(n // tn, m // tm),
        in_specs=[pl.BlockSpec((tm, k), lambda j, i: (i, 0)),
                  pl.BlockSpec((tm, k), lambda j, i: (i, 0)),
                  pl.BlockSpec((None, k, tn), lambda j, i: (layer, 0, j)),
                  pl.BlockSpec((None, k, tn), lambda j, i: (layer, 0, j)),
                  pl.BlockSpec((1, tn), lambda j, i: (0, j))],
        out_specs=pl.BlockSpec((tm, tn), lambda j, i: (i, j)),
        out_shape=jax.ShapeDtypeStruct((m, n), F32),
        compiler_params=_cp(("parallel", "parallel")),
        name="mm3_bias",
    )(xh, xl, wh, wl, b)


def _merge_kernel(yhp_ref, yhs_ref, ygp_ref, ygs_ref, ymp_ref, yms_ref, w_ref, ga_ref, gb_ref, gc_ref, o_ref,
                  *, n_prompt_tiles):
    is_prompt = pl.program_id(1) < n_prompt_tiles
    pick = lambda p_ref, s_ref: jnp.where(is_prompt, p_ref[...], s_ref[...])
    acc = jax.nn.sigmoid(ga_ref[...]) * _dot(pick(yhp_ref, yhs_ref), w_ref[0])
    acc += jax.nn.sigmoid(gb_ref[...]) * _dot(pick(ygp_ref, ygs_ref), w_ref[1])
    acc += jax.nn.sigmoid(gc_ref[...]) * _dot(pick(ymp_ref, yms_ref), w_ref[2])
    o_ref[...] = acc.astype(o_ref.dtype)


def merge_branches(y_hy, y_gla, y_ml, w_branch, layer, zmain, tm=512, tn=1024):
    kb = HY_WIDTH
    n_p = N_PROMPT // tm
    p_spec = pl.BlockSpec((tm, kb), lambda j, i: (jnp.minimum(i, n_p - 1), 0))
    s_spec = pl.BlockSpec((tm, kb), lambda j, i: (jnp.maximum(i - n_p, 0), 0))

    def gate_spec(col):
        return pl.BlockSpec((tm, tn), lambda j, i, c=col // tn: (i, c + j))

    return pl.pallas_call(
        functools.partial(_merge_kernel, n_prompt_tiles=n_p),
        grid=(D_MODEL // tn, N_ROWS // tm),
        in_specs=[p_spec, s_spec, p_spec, s_spec, p_spec, s_spec,
                  pl.BlockSpec((None, 3, kb, tn), lambda j, i: (layer, 0, 0, j)),
                  gate_spec(C_GA), gate_spec(C_GB), gate_spec(C_GC)],
        out_specs=pl.BlockSpec((tm, tn), lambda j, i: (i, j)),
        out_shape=jax.ShapeDtypeStruct((N_ROWS, D_MODEL), BF16),
        compiler_params=_cp(("parallel", "parallel")),
        name="merge_branches",
    )(*y_hy, *y_gla, *y_ml, w_branch, zmain, zmain, zmain)


def _mm_resid_kernel(m_ref, w_ref, x_ref, gt_ref, o_ref):
    o_ref[...] = x_ref[...] + gt_ref[...] * _dot(m_ref[...], w_ref[...])


def mm_residual(merged, w, layer, x, mod3, gt_chunk, tm=512, tn=1024):
    k = merged.shape[1]
    return pl.pallas_call(
        _mm_resid_kernel,
        grid=(D_MODEL // tn, N_ROWS // tm),
        in_specs=[pl.BlockSpec((tm, k), lambda j, i: (i, 0)),
                  pl.BlockSpec((None, k, tn), lambda j, i: (layer, 0, j)),
                  pl.BlockSpec((tm, tn), lambda j, i: (i, j)),
                  pl.BlockSpec((None, 1, tn),
                               lambda j, i: (_mod_row(i, tm), 0, gt_chunk * (D_MODEL // tn) + j))],
        out_specs=pl.BlockSpec((tm, tn), lambda j, i: (i, j)),
        out_shape=jax.ShapeDtypeStruct((N_ROWS, D_MODEL), F32),
        compiler_params=_cp(("parallel", "parallel")),
        name="mm_residual",
    )(merged, w, x, mod3)


def _dft_tables(length):
    k = jnp.arange(length, dtype=jnp.int32)
    m = (k[:, None] * k[None, :]) % (2 * length)
    ang = m.astype(F32) * (math.pi / length)
    cos = jnp.cos(ang)
    sin = jnp.sin(ang)
    sgn = jnp.where(k % 2 == 0, 1.0, -1.0).astype(F32)
    fre = cos
    fim = jnp.where(k[:, None] == 0, sgn[None, :], -sin)
    wk = jnp.where(k == 0, 1.0, 2.0).astype(F32) / (2.0 * length)
    g_re = cos.T * wk[None, :]
    g_im = jnp.where(k[None, :] == 0, sgn[:, None] / (2.0 * length), -sin.T / length)
    return fre, fim, jnp.concatenate([g_re, g_im], axis=1), sgn


def _hyfilt_kernel(z_ref, w1_ref, b1_ref, w2_ref, b2_ref, fq_ref, w3f_ref, w3b_ref, tn_ref, dl_ref,
                   sgn_ref, freh_ref, frel_ref, fimh_ref, fiml_ref, sre_ref, sim_ref):
    hid = jnp.sin(fq_ref[0:1] * (_dot3(z_ref[...], w1_ref[...]) + b1_ref[...]))
    hid = jnp.sin(fq_ref[1:2] * (_dot3(hid, w2_ref[...]) + b2_ref[...]))
    decay = jnp.exp(-tn_ref[...] * dl_ref[...])
    fwd = _dot3(hid, w3f_ref[...]) * decay
    bwd = _dot3(hid, w3b_ref[...]) * decay
    row = lax.broadcasted_iota(jnp.int32, fwd.shape, 0)
    bwd = jnp.where(row == 0, 0.0, bwd)
    a = fwd + bwd
    d = fwd - bwd
    ah, al = _split2(a)
    dh, dl2 = _split2(d)
    re = _dot(freh_ref[...], ah) + (_dot(freh_ref[...], al) + _dot(frel_ref[...], ah))
    im = _dot(fimh_ref[...], dh) + (_dot(fimh_ref[...], dl2) + _dot(fiml_ref[...], dh))
    nyq = jnp.sum(sgn_ref[...] * a, axis=0, keepdims=True)
    sre_ref[0] = re
    sim_ref[0] = jnp.where(row == 0, nyq, im)


def hyena_spectrum(length, w1, b1, w2, b2, w3, freq, tabs):
    fre_h, fre_l, fim_h, fim_l, sgn = tabs
    t = jnp.arange(length, dtype=F32)
    t_norm = t / (length - 1)
    bands = jnp.linspace(1e-4, HY_BANDS - 1, HY_BANDS, dtype=F32)
    ang = (2.0 * math.pi / length) * t[:, None] * bands[None, :]
    z = jnp.concatenate([t_norm[:, None], jnp.cos(ang), -jnp.sin(ang),
                         jnp.zeros((length, HY_FFN - HY_EMB), F32)], axis=-1)
    w1p = jnp.pad(w1, ((0, HY_FFN - HY_EMB), (0, 0)))
    max_decay = math.log(HY_DECAY_TARGET) / HY_FAST_PCT
    min_decay = math.log(HY_DECAY_TARGET) / HY_SLOW_PCT
    deltas = jnp.abs(jnp.linspace(min_decay, max_decay, HY_WIDTH, dtype=F32)).reshape(1, HY_WIDTH)
    ct = 256
    nct = HY_WIDTH // ct
    full = lambda shape: pl.BlockSpec(shape, lambda o, j: (0,) * len(shape))
    out_spec = pl.BlockSpec((1, length, ct), lambda o, j: (o, 0, j))
    return pl.pallas_call(
        _hyfilt_kernel,
        grid=(2, nct),
        in_specs=[full((length, HY_FFN)), full((HY_FFN, HY_FFN)), full((1, HY_FFN)),
                  full((HY_FFN, HY_FFN)), full((1, HY_FFN)), full((2, HY_FFN)),
                  pl.BlockSpec((HY_FFN, ct), lambda o, j: (0, o * 2 * nct + j)),
                  pl.BlockSpec((HY_FFN, ct), lambda o, j: (0, o * 2 * nct + nct + j)),
                  full((length, 1)),
                  pl.BlockSpec((1, ct), lambda o, j: (0, j)),
                  full((length, 1)),
                  full((length, length)), full((length, length)),
                  full((length, length)), full((length, length))],
        out_specs=[out_spec, out_spec],
        out_shape=[jax.ShapeDtypeStruct((2, length, HY_WIDTH), F32)] * 2,
        compiler_params=_cp(("parallel", "parallel")),
        name="hyena_spectrum",
    )(z, w1p, b1.reshape(1, HY_FFN), w2, b2.reshape(1, HY_FFN), freq, w3, w3,
      t_norm.reshape(length, 1), deltas, sgn.reshape(length, 1), fre_h, fre_l, fim_h, fim_l)


def _short_conv(x, w, pos, seg, length):
    prev = jnp.where(pos == 0, 0.0, pltpu.roll(x, 1, 0))
    nxt = jnp.where(pos == seg - 1, 0.0, pltpu.roll(x, length - 1, 0))
    return prev * w[0:1] + x * w[1:2] + nxt * w[2:3]


def _hyena_kernel(x1_ref, x2_ref, v_ref, cw_ref, sre_ref, sim_ref, hb_ref, f_ref, g_ref, o_ref,
                  *, length, seg):
    row = lax.broadcasted_iota(jnp.int32, (length, 1), 0)
    pos = row % seg
    row0 = row == 0

    def long_conv(u, o):
        spec = _dot(f_ref[...], u.astype(BF16))
        ur, ui = spec[:length], spec[length:]
        hre, him = sre_ref[o], sim_ref[o]
        uihi = ui * him
        yr = ur * hre - jnp.where(row0, 0.0, uihi)
        yi = jnp.where(row0, uihi, ur * him + ui * hre)
        y = _dot(g_ref[...], jnp.concatenate([yr, yi], axis=0).astype(BF16))
        return y + u * hb_ref[o:o + 1]

    x1 = _short_conv(x1_ref[...], cw_ref[0], pos, seg, length)
    x2 = _short_conv(x2_ref[...], cw_ref[1], pos, seg, length)
    v = _short_conv(v_ref[...], cw_ref[2], pos, seg, length)
    z = x1 * long_conv(v, 0)
    o_ref[...] = (x2 * long_conv(z, 1)).astype(o_ref.dtype)


def hyena(zmain, conv_w, spec_re, spec_im, bias, f_mat, g_mat, length, n_seq, row_blk0, seg):
    ct = 256
    nct = HY_WIDTH // ct

    def zspec(col):
        return pl.BlockSpec((length, ct), lambda b, j, c=col // ct: (row_blk0 + b, c + j))

    return pl.pallas_call(
        functools.partial(_hyena_kernel, length=length, seg=seg),
        grid=(n_seq, nct),
        in_specs=[zspec(C_HX1), zspec(C_HX2), zspec(C_HV),
                  pl.BlockSpec((3, 3, ct), lambda b, j: (0, 0, j)),
                  pl.BlockSpec((2, length, ct), lambda b, j: (0, 0, j)),
                  pl.BlockSpec((2, length, ct), lambda b, j: (0, 0, j)),
                  pl.BlockSpec((2, ct), lambda b, j: (0, j)),
                  pl.BlockSpec((2 * length, length), lambda b, j: (0, 0)),
                  pl.BlockSpec((length, 2 * length), lambda b, j: (0, 0))],
        out_specs=pl.BlockSpec((length, ct), lambda b, j: (b, j)),
        out_shape=jax.ShapeDtypeStruct((n_seq * length, HY_WIDTH), BF16),
        compiler_params=_cp(("parallel", "parallel")),
        name="hyena",
    )(zmain, zmain, zmain, conv_w, spec_re, spec_im, bias, f_mat, g_mat)


def _tri_masks():
    t = lax.broadcasted_iota(jnp.int32, (CHUNK, CHUNK), 0)
    s = lax.broadcasted_iota(jnp.int32, (CHUNK, CHUNK), 1)
    return s <= t, s >= t


def _gla_chunks(qs, ks, vs, gs, states, masks, revs):
    idx = range(len(qs))
    ones = jnp.ones((CHUNK, GLA_DK), BF16)
    tms = [jnp.where(masks[i], 1.0, 0.0).astype(BF16) for i in idx]
    gsp = [_split3(gs[i]) for i in idx]
    bcs = [_dot(tms[i], gsp[i][0]) + (_dot(tms[i], gsp[i][1]) + _dot(tms[i], gsp[i][2])) for i in idx]
    tots = [_dot_tn(gsp[i][0], ones) + (_dot_tn(gsp[i][1], ones) + _dot_tn(gsp[i][2], ones)) for i in idx]
    b_ends = [bcs[i][0:1] if revs[i] else bcs[i][CHUNK - 1:CHUNK] for i in idx]
    refs = [bcs[i][CHUNK // 2:CHUNK // 2 + 1] for i in idx]
    inters = [_dot((qs[i] * jnp.exp(bcs[i])).astype(BF16), states[i].astype(BF16)) for i in idx]
    qhs = [(qs[i] * jnp.exp(bcs[i] - refs[i])).astype(BF16) for i in idx]
    khs = [(ks[i] * jnp.exp(refs[i] - bcs[i])).astype(BF16) for i in idx]
    atts = [jnp.where(masks[i], _dot_nt(qhs[i], khs[i]), 0.0).astype(BF16) for i in idx]
    vbs = [vs[i].astype(BF16) for i in idx]
    outs = [inters[i] + _dot(atts[i], vbs[i]) for i in idx]
    kds = [(ks[i] * jnp.exp(b_ends[i] - bcs[i])).astype(BF16) for i in idx]
    es = [jnp.exp(tots[i]) for i in idx]
    new_states = [jnp.concatenate([es[i], es[i]], axis=1) * states[i] + _dot_tn(kds[i], vbs[i]) for i in idx]
    return outs, new_states


def _gla_kernel(*refs, length, n_sub, has_init):
    if has_init:
        (q_ref, k_ref, v_ref, gr_ref, zs_ref, wf_ref, wb_ref, ab_ref, ng_ref, s0_ref,
         y_ref, o_ref, lg_ref, st_ref) = refs
    else:
        (q_ref, k_ref, v_ref, gr_ref, zs_ref, wf_ref, wb_ref, ab_ref, ng_ref,
         y_ref, sout_ref, o_ref, lg_ref, st_ref) = refs
    zs = zs_ref[...]
    lg_ref[0] = _log_sigmoid(_dot3(zs, wf_ref[...]) + ab_ref[0]) * (1.0 / GLA_NORMALIZER)
    lg_ref[1] = _log_sigmoid(_dot3(zs, wb_ref[...]) + ab_ref[1]) * (1.0 / GLA_NORMALIZER)
    if has_init:
        st_ref[...] = s0_ref[...]
    else:
        st_ref[...] = jnp.zeros_like(st_ref)
    o_ref[...] = jnp.zeros_like(o_ref)
    n = length // CHUNK
    mask_f, mask_b = _tri_masks()
    scale = GLA_DK ** -0.5

    def body(i, carry):
        chains = []
        for s in range(n_sub):
            for d, mask in enumerate((mask_f, mask_b)):
                c = i if d == 0 else n - 1 - i
                rows = pl.ds(pl.multiple_of(s * length + c * CHUNK, CHUNK), CHUNK)
                for h in range(GLA_HEADS):
                    chains.append((s, d, h, rows, mask))
        kcs = [slice(h * GLA_DK, (h + 1) * GLA_DK) for (_, _, h, _, _) in chains]
        vcs = [slice(h * GLA_DV, (h + 1) * GLA_DV) for (_, _, h, _, _) in chains]
        outs, new_states = _gla_chunks(
            [q_ref[ch[3], kc] * scale for ch, kc in zip(chains, kcs)],
            [k_ref[ch[3], kc] for ch, kc in zip(chains, kcs)],
            [v_ref[ch[3], vc] for ch, vc in zip(chains, vcs)],
            [lg_ref[ch[1], ch[3], kc] for ch, kc in zip(chains, kcs)],
            [st_ref[ch[0], ch[1], ch[2]] for ch in chains],
            [ch[4] for ch in chains], [ch[1] == 1 for ch in chains])
        for ch, vc, o, s_new in zip(chains, vcs, outs, new_states):
            o_ref[ch[3], vc] += o
            st_ref[ch[0], ch[1], ch[2]] = s_new
        return carry

    lax.fori_loop(0, n, body, 0)
    for h in range(GLA_HEADS):
        vc = slice(h * GLA_DV, (h + 1) * GLA_DV)
        o = o_ref[:, vc]
        o = o * lax.rsqrt(jnp.mean(o * o, axis=-1, keepdims=True) + EPS) * ng_ref[...]
        gr = gr_ref[:, vc]
        y_ref[:, vc] = (o * (gr * jax.nn.sigmoid(gr))).astype(y_ref.dtype)
    if not has_init:
        sout_ref[...] = st_ref[...]


def gla(zmain, zsmall, wf, wb, ab, norm_g, length, n_seq, row_blk0, state0, layer, n_sub=1):
    has_init = state0 is not None
    qk_w = GLA_HEADS * GLA_DK
    v_w = GLA_HEADS * GLA_DV
    once = pl.Buffered(1)
    rows = n_sub * length
    blk0 = row_blk0 // n_sub
    st_blk = (n_sub, 2, GLA_HEADS, GLA_DK, GLA_DV)

    def zspec(col, width):
        return pl.BlockSpec((rows, width), lambda b, c=col // width: (blk0 + b, c), pipeline_mode=once)

    in_specs = [zspec(C_GQ, qk_w), zspec(C_GK, qk_w), zspec(C_GV, v_w), zspec(C_GR, v_w),
                pl.BlockSpec((rows, N_SMALL), lambda b: (blk0 + b, 0)),
                pl.BlockSpec((N_SMALL, qk_w), lambda b: (0, 0)),
                pl.BlockSpec((N_SMALL, qk_w), lambda b: (0, 0)),
                pl.BlockSpec((2, 1, qk_w), lambda b: (0, 0, 0)),
                pl.BlockSpec((1, GLA_DV), lambda b: (0, 0))]
    args = [zmain, zmain, zmain, zmain, zsmall, wf, wb, ab, norm_g]
    y_shape = jax.ShapeDtypeStruct((n_seq * length, v_w), BF16)
    y_spec = pl.BlockSpec((rows, v_w), lambda b: (b, 0))
    if has_init:
        in_specs.append(pl.BlockSpec((n_sub, None, 2, GLA_HEADS, GLA_DK, GLA_DV),
                                     lambda b: (b, layer, 0, 0, 0, 0)))
        args.append(state0)
        out_shape, out_specs = y_shape, y_spec
    else:
        out_shape = [y_shape, jax.ShapeDtypeStruct((n_seq, 2, GLA_HEADS, GLA_DK, GLA_DV), F32)]
        out_specs = [y_spec, pl.BlockSpec(st_blk, lambda b: (b, 0, 0, 0, 0))]
    return pl.pallas_call(
        functools.partial(_gla_kernel, length=length, n_sub=n_sub, has_init=has_init),
        grid=(n_seq // n_sub,),
        in_specs=in_specs,
        out_specs=out_specs,
        out_shape=out_shape,
        scratch_shapes=[pltpu.VMEM((rows, v_w), F32), pltpu.VMEM((2, rows, qk_w), F32),
                        pltpu.VMEM(st_blk, F32)],
        compiler_params=_cp(("parallel",)),
        name="gla",
    )(*args)


def _mlstm_chunks(qs, kss, vs, lf_cs, li_cs, lf_rs, li_rs, cms, nvs, m_prevs, masks, mask_ts, revs):
    t_n = CHUNK
    idx = range(len(qs))
    tms = [jnp.where(masks[i], 1.0, 0.0).astype(BF16) for i in idx]
    tmts = [jnp.where(mask_ts[i], 1.0, 0.0).astype(BF16) for i in idx]
    csp = [_split3(jnp.broadcast_to(lf_cs[i], (t_n, t_n))) for i in idx]
    rsp = [_split3(jnp.broadcast_to(lf_rs[i], (t_n, t_n))) for i in idx]
    b_colbs = [_dot(tms[i], csp[i][0]) + (_dot(tms[i], csp[i][1]) + _dot(tms[i], csp[i][2])) for i in idx]
    b_rowbs = [_dot(rsp[i][0], tmts[i]) + (_dot(rsp[i][1], tmts[i]) + _dot(rsp[i][2], tmts[i])) for i in idx]
    qbs = [qs[i].astype(BF16) for i in idx]
    vbs = [vs[i].astype(BF16) for i in idx]
    qks = [_dot_nt(qbs[i], kss[i].astype(BF16)) for i in idx]
    qcs = [_dot(qbs[i], cms[i].astype(BF16)) for i in idx]
    qns = [jnp.sum(qs[i] * nvs[i], axis=-1, keepdims=True) for i in idx]
    b_cols = [b_colbs[i][:, 0:1] for i in idx]
    b_rows = [b_rowbs[i][0:1, :] for i in idx]
    b_ends = [b_colbs[i][0:1, 0:1] if revs[i] else b_colbs[i][t_n - 1:t_n, 0:1] for i in idx]
    dmats = [jnp.where(masks[i], b_colbs[i] - b_rowbs[i] + li_rs[i], -jnp.inf) for i in idx]
    m_ts = [jnp.maximum(b_cols[i] + m_prevs[i], jnp.max(dmats[i], axis=-1, keepdims=True)) for i in idx]
    w_inters = [jnp.exp(b_cols[i] + m_prevs[i] - m_ts[i]) for i in idx]
    scs = [qks[i] * jnp.exp(dmats[i] - m_ts[i]) for i in idx]
    svs = [_dot(scs[i].astype(BF16), vbs[i]) for i in idx]
    g_rs = [b_ends[i] - b_rows[i] + li_rs[i] for i in idx]
    g_cs = [b_ends[i] - b_cols[i] + li_cs[i] for i in idx]
    m_news = [jnp.maximum(b_ends[i] + m_prevs[i], jnp.max(g_rs[i], axis=-1, keepdims=True)) for i in idx]
    w_cs = [jnp.exp(b_ends[i] + m_prevs[i] - m_news[i]) for i in idx]
    kws = [kss[i] * jnp.exp(g_cs[i] - m_news[i]) for i in idx]
    kvs = [_dot_tn(kws[i].astype(BF16), vbs[i]) for i in idx]
    nums = [w_inters[i] * qcs[i] + svs[i] for i in idx]
    dens = [w_inters[i] * qns[i] + jnp.sum(scs[i], axis=-1, keepdims=True) for i in idx]
    hs = [nums[i] / jnp.maximum(jnp.abs(dens[i]), jnp.exp(-m_ts[i])) for i in idx]
    cm_news = [w_cs[i] * cms[i] + kvs[i] for i in idx]
    nv_news = [w_cs[i] * nvs[i] + jnp.sum(kws[i], axis=0, keepdims=True) for i in idx]
    return hs, cm_news, nv_news, m_news


def _mlstm_kernel(*refs, length, n_sub, seg, has_init):
    if has_init:
        (gb_ref, m0_ref, q_ref, k_ref, v_ref, mo_ref, gc_ref, gr_ref, cw_ref, ng_ref, c0_ref, n0_ref,
         y_ref, qc_ref, kc_ref, h_ref, c_ref, n_ref, m_ref) = refs
    else:
        (gb_ref, q_ref, k_ref, v_ref, mo_ref, gc_ref, gr_ref, cw_ref, ng_ref,
         y_ref, cout_ref, nout_ref, mout_ref, qc_ref, kc_ref, h_ref, c_ref, n_ref, m_ref) = refs
    b_idx = pl.program_id(0)
    dh = ML_DH
    rows_all = n_sub * length
    row = lax.broadcasted_iota(jnp.int32, (rows_all, 1), 0)
    pos = row % seg
    qc_ref[...] = _short_conv(q_ref[...], cw_ref[0], pos, seg, rows_all)
    kc_ref[...] = _short_conv(k_ref[...], cw_ref[1], pos, seg, rows_all) * (dh ** -0.5)
    h_ref[...] = jnp.zeros_like(h_ref)
    if has_init:
        c_ref[...] = c0_ref[...]
        n_ref[...] = n0_ref[...]
        for s in range(n_sub):
            for d in range(2):
                for h in range(ML_HEADS):
                    m0 = m0_ref[(b_idx * n_sub + s) * 2 * ML_HEADS + d * ML_HEADS + h]
                    m_ref[s, d, h] = jnp.full((1, 128), m0, F32)
    else:
        c_ref[...] = jnp.zeros_like(c_ref)
        n_ref[...] = jnp.zeros_like(n_ref)
        m_ref[...] = jnp.zeros_like(m_ref)
    n = length // CHUNK
    mask_f, mask_b = _tri_masks()

    def body(i, carry):
        chains = []
        for s in range(n_sub):
            for d, (mask, mask_t) in enumerate(((mask_f, mask_b), (mask_b, mask_f))):
                c = i if d == 0 else n - 1 - i
                rows = pl.ds(pl.multiple_of(s * length + c * CHUNK, CHUNK), CHUNK)
                for h in range(ML_HEADS):
                    chains.append((s, d, h, rows, mask, mask_t, c))
        args = [[] for _ in range(10)]
        for (s, d, h, rows, mask, mask_t, c) in chains:
            cols = slice(h * dh, (h + 1) * dh)
            bi = gb_ref[d * 2 * ML_HEADS + h]
            bf = gb_ref[d * 2 * ML_HEADS + ML_HEADS + h]
            gcol = gc_ref[h, rows, :]
            grow = gr_ref[h, s * n + c]
            vals = (qc_ref[rows, cols], kc_ref[rows, cols], v_ref[rows, cols],
                    _log_sigmoid(gcol[:, 2 + d:3 + d] + bf), gcol[:, d:d + 1] + bi,
                    _log_sigmoid(grow[2 + d:3 + d, :] + bf), grow[d:d + 1, :] + bi,
                    c_ref[s, d, h], n_ref[s, d, h], m_ref[s, d, h][:, 0:1])
            for lst, val in zip(args, vals):
                lst.append(val)
        hs, cms, nvs, m_news = _mlstm_chunks(*args, [ch[4] for ch in chains], [ch[5] for ch in chains],
                                             [ch[1] == 1 for ch in chains])
        for (s, d, h, rows, _, _, _), hc, cm, nv, m_new in zip(chains, hs, cms, nvs, m_news):
            h_ref[rows, h * dh:(h + 1) * dh] += hc
            c_ref[s, d, h] = cm
            n_ref[s, d, h] = nv
            m_ref[s, d, h] = jnp.broadcast_to(m_new, (1, 128))
        return carry

    lax.fori_loop(0, n, body, 0)
    for h in range(ML_HEADS):
        cols = slice(h * dh, (h + 1) * dh)
        o = h_ref[:, cols]
        o = o * lax.rsqrt(jnp.mean(o * o, axis=-1, keepdims=True) + EPS) * ng_ref[...]
        y_ref[:, cols] = (o * jax.nn.sigmoid(mo_ref[:, cols])).astype(y_ref.dtype)
    if not has_init:
        cout_ref[...] = c_ref[...]
        nout_ref[...] = n_ref[...]
        mout_ref[...] = m_ref[...]


def mlstm(zmain, gates_col, gates_row, gate_b, conv_w, norm_g, length, n_seq, row_blk0, seg, init, layer,
          n_sub=1):
    has_init = init is not None
    dh = ML_DH
    nh = ML_HEADS
    width = nh * dh
    once = pl.Buffered(1)
    rows = n_sub * length
    blk0 = row_blk0 // n_sub

    def zspec(col):
        return pl.BlockSpec((rows, width), lambda b, c=col // width: (blk0 + b, c), pipeline_mode=once)

    smem = pl.BlockSpec(memory_space=pltpu.SMEM)
    in_specs = [smem]
    args = [gate_b]
    if has_init:
        c0, n0, m0 = init
        in_specs.append(smem)
        args.append(m0)
    in_specs += [zspec(C_MQ), zspec(C_MK), zspec(C_MV), zspec(C_MO),
                 pl.BlockSpec((nh, rows, 8), lambda b: (0, blk0 + b, 0)),
                 pl.BlockSpec((nh, rows // CHUNK, 8, CHUNK), lambda b: (0, blk0 + b, 0, 0)),
                 pl.BlockSpec((2, 3, width), lambda b: (0, 0, 0)),
                 pl.BlockSpec((1, dh), lambda b: (0, 0))]
    args += [zmain, zmain, zmain, zmain, gates_col, gates_row, conv_w, norm_g]
    y_shape = jax.ShapeDtypeStruct((n_seq * length, width), BF16)
    y_spec = pl.BlockSpec((rows, width), lambda b: (b, 0))
    if has_init:
        in_specs += [pl.BlockSpec((n_sub, None, 2, nh, dh, dh), lambda b: (b, layer, 0, 0, 0, 0)),
                     pl.BlockSpec((n_sub, None, 2, nh, 1, dh), lambda b: (b, layer, 0, 0, 0, 0))]
        args += [c0, n0]
        out_shape, out_specs = y_shape, y_spec
    else:
        out_shape = [y_shape,
                     jax.ShapeDtypeStruct((n_seq, 2, nh, dh, dh), F32),
                     jax.ShapeDtypeStruct((n_seq, 2, nh, 1, dh), F32),
                     jax.ShapeDtypeStruct((n_seq, 2, nh, 1, 128), F32)]
        out_specs = [y_spec,
                     pl.BlockSpec((n_sub, 2, nh, dh, dh), lambda b: (b, 0, 0, 0, 0)),
                     pl.BlockSpec((n_sub, 2, nh, 1, dh), lambda b: (b, 0, 0, 0, 0)),
                     pl.BlockSpec((n_sub, 2, nh, 1, 128), lambda b: (b, 0, 0, 0, 0))]
    return pl.pallas_call(
        functools.partial(_mlstm_kernel, length=length, n_sub=n_sub, seg=seg, has_init=has_init),
        grid=(n_seq // n_sub,),
        in_specs=in_specs,
        out_specs=out_specs,
        out_shape=out_shape,
        scratch_shapes=[pltpu.VMEM((rows, width), F32), pltpu.VMEM((rows, width), F32),
                        pltpu.VMEM((rows, width), F32),
                        pltpu.VMEM((n_sub, 2, nh, dh, dh), F32), pltpu.VMEM((n_sub, 2, nh, 1, dh), F32),
                        pltpu.VMEM((n_sub, 2, nh, 1, 128), F32)],
        compiler_params=_cp(("parallel",)),
        name="mlstm",
    )(*args)


def _top_values(x, count, with_rank=False):
    vals = []
    cur = x
    rank = jnp.full(x.shape, float(count), F32) if with_rank else None
    for r in range(count):
        m = jnp.max(cur, axis=0, keepdims=True)
        vals.append(m)
        if with_rank or r + 1 < count:
            top = cur == m
            cur = jnp.where(top, -jnp.inf, cur)
            if with_rank:
                rank = jnp.where(top, float(r), rank)
    return (vals, rank) if with_rank else vals


def _peer_topk_kernel(q_ref, k_ref, nb_ref, rk_ref, e1_ref, e2_ref):
    dk = PEER_NKEYS
    s1 = _dot3(k_ref[0], q_ref[:, 0:dk], dot=_dot_nt)
    s2 = _dot3(k_ref[1], q_ref[:, dk:2 * dk], dot=_dot_nt)
    v1 = _top_values(s1, PEER_TOPK)
    v2, rank2 = _top_values(s2, PEER_TOPK, with_rank=True)
    rows = [v1[a] + v2[b] for a in range(PEER_TOPK) for b in range(PEER_TOPK // (a + 1))]
    rows += [jnp.full_like(rows[0], -jnp.inf)] * (-len(rows) % 8)
    cand = jnp.concatenate(rows, axis=0)
    best = _top_values(cand, PEER_TOPK)
    zsum = jnp.exp(best[0] - best[0])
    for r in range(1, PEER_TOPK):
        zsum = zsum + jnp.exp(best[r] - best[0])
    theta = best[PEER_TOPK - 1]
    dense_b = 4
    nb = jnp.zeros_like(s1)
    for b in range(dense_b):
        nb = nb + jnp.where(s1 + v2[b] >= theta, 1.0, 0.0)
    for a in range(PEER_TOPK // (dense_b + 1)):
        extra = jnp.zeros_like(theta)
        for b in range(dense_b, PEER_TOPK // (a + 1)):
            extra = extra + jnp.where(v1[a] + v2[b] >= theta, 1.0, 0.0)
        nb = nb + jnp.where(s1 == v1[a], extra, 0.0)
    nb_ref[...] = nb
    rk_ref[...] = pltpu.bitcast(rank2.astype(BF16), jnp.uint32)
    e1_ref[...] = jnp.exp(s1 - v1[0]) / zsum
    e2_ref[...] = pltpu.bitcast(jnp.exp(s2 - v2[0]).astype(BF16), jnp.uint32)


def peer_topk(q, keys, tt=256):
    n = q.shape[0]
    nk = PEER_NKEYS
    spec = pl.BlockSpec((None, nk, tt), lambda i, h: (h, 0, i))
    pspec = pl.BlockSpec((None, nk // 2, tt), lambda i, h: (h, 0, i))
    full = jax.ShapeDtypeStruct((PEER_HEADS, nk, n), F32)
    packed = jax.ShapeDtypeStruct((PEER_HEADS, nk // 2, n), jnp.uint32)
    return pl.pallas_call(
        _peer_topk_kernel,
        grid=(n // tt, PEER_HEADS),
        in_specs=[pl.BlockSpec((tt, 2 * nk), lambda i, h: (i, h)),
                  pl.BlockSpec((None, 2, nk, nk), lambda i, h: (h, 0, 0, 0))],
        out_specs=[spec, pspec, spec, pspec],
        out_shape=[full, packed, full, packed],
        compiler_params=_cp(("parallel", "parallel")),
        name="peer_topk",
    )(q, keys)


def _row_bcast(row, n):
    t = jnp.broadcast_to(row, (16, 128)).astype(BF16)
    return jnp.broadcast_to(t[None], (n // 16, 16, 128)).reshape(n, 128)


def _gelu_tanh(x):
    return 0.5 * x * (1.0 + jnp.tanh(math.sqrt(2.0 / math.pi) * (x + 0.044715 * (x * x * x))))


def _peer_expert_kernel(ht_ref, u_ref, vt_ref, nb_ref, rk_ref, e1_ref, e2_ref, x_ref, gt_ref,
                        o_ref, acc_ref, w_ref, *, n_i1, tt):
    j = pl.program_id(1)
    nk = PEER_NKEYS

    @pl.when(j == 0)
    def _():
        acc_ref[...] = jnp.zeros_like(acc_ref)

    act = _dot(u_ref[...], ht_ref[...])
    zero = jnp.zeros((), BF16)
    for ts in range(tt // 128):
        lanes = slice(ts * 128, (ts + 1) * 128)
        for r in range(n_i1):
            w = None
            for h in range(PEER_HEADS):
                nbr = _row_bcast(nb_ref[h, r:r + 1, lanes], nk)
                e1r = _row_bcast(e1_ref[h, r:r + 1, lanes], nk)
                rk = pltpu.bitcast(rk_ref[h, :, lanes], BF16)
                e2 = pltpu.bitcast(e2_ref[h, :, lanes], BF16)
                term = jnp.where(rk < nbr, e2 * e1r, zero)
                w = term if w is None else w + term
            w_ref[r * nk:(r + 1) * nk, lanes] = w
    p = w_ref[...] * _gelu_tanh(act).astype(BF16)
    acc_ref[...] += _dot(vt_ref[...], p)

    @pl.when(j == pl.num_programs(1) - 1)
    def _():
        o_ref[...] = x_ref[...] + gt_ref[...] * acc_ref[...].T


def _xpose_cast_kernel(x_ref, o_ref):
    o_ref[...] = x_ref[...].T.astype(BF16)


def transpose_cast_chunks(tab, ec):
    depth, e, d = tab.shape
    return pl.pallas_call(
        _xpose_cast_kernel,
        grid=(depth, e // ec),
        in_specs=[pl.BlockSpec((None, ec, d), lambda l, c: (l, c, 0))],
        out_specs=pl.BlockSpec((None, None, d, ec), lambda l, c: (l, c, 0, 0)),
        out_shape=jax.ShapeDtypeStruct((depth, e // ec, d, ec), BF16),
        compiler_params=_cp(("parallel", "parallel")),
        name="transpose_cast_chunks",
    )(tab)


def peer_experts(h2t, u_tab, vt_tab, layer, nb, rk, e1, e2, x, mod3, gt_chunk, tt=512):
    ec = PEER_CHUNK
    n = x.shape[0]
    nk = PEER_NKEYS
    n_i1 = ec // nk
    once = pl.Buffered(1)
    sspec = pl.BlockSpec((PEER_HEADS, nk // 2, tt), lambda i, j: (0, 0, i), pipeline_mode=once)
    rspec = pl.BlockSpec((PEER_HEADS, n_i1, tt), lambda i, j: (0, j, i))
    return pl.pallas_call(
        functools.partial(_peer_expert_kernel, n_i1=n_i1, tt=tt),
        grid=(n // tt, PEER_EXPERTS // ec),
        in_specs=[pl.BlockSpec((D_MODEL, tt), lambda i, j: (0, i)),
                  pl.BlockSpec((None, ec, D_MODEL), lambda i, j: (layer, j, 0)),
                  pl.BlockSpec((None, None, D_MODEL, ec), lambda i, j: (layer, j, 0, 0)),
                  rspec, sspec, rspec, sspec,
                  pl.BlockSpec((tt, D_MODEL), lambda i, j: (i, 0), pipeline_mode=once),
                  pl.BlockSpec((None, 1, D_MODEL), lambda i, j: (_mod_row(i, tt), 0, gt_chunk))],
        out_specs=pl.BlockSpec((tt, D_MODEL), lambda i, j: (i, 0)),
        out_shape=jax.ShapeDtypeStruct((n, D_MODEL), F32),
        scratch_shapes=[pltpu.VMEM((D_MODEL, tt), F32), pltpu.VMEM((ec, tt), BF16)],
        compiler_params=_cp(("parallel", "arbitrary")),
        name="peer_experts",
    )(h2t, u_tab, vt_tab, nb, rk, e1, e2, x, mod3)


def _reorder_in_proj(w_in, b_in):
    splits = (1024, 1024, 1024, 512, 512, 1024, 1024, 16, 16, 1024, 1024, 1024, 1024, 8, 8, 2048, 2048, 2048)
    offs = np.concatenate([[0], np.cumsum(splits)])
    seg = lambda a, i: a[..., offs[i]:offs[i + 1]]
    main_ids = (0, 1, 2, 3, 4, 5, 6, 9, 10, 11, 12, 15, 16, 17)
    small_ids = (7, 8, 13, 14)
    w_main = jnp.concatenate([seg(w_in, i).astype(BF16) for i in main_ids], axis=-1)
    b_main = jnp.concatenate([seg(b_in, i) for i in main_ids], axis=-1)
    w_small = jnp.concatenate([seg(w_in, i) for i in small_ids], axis=-1)
    b_small = jnp.concatenate([seg(b_in, i) for i in small_ids], axis=-1)
    pad = N_SMALL - w_small.shape[-1]
    w_small = jnp.pad(w_small, ((0, 0), (0, 0), (0, pad)))
    b_small = jnp.pad(b_small, ((0, 0), (0, pad)))
    return w_main, b_main, w_small, b_small


def kernel(x_prompt, x_sample, c, state_gla, state_mlstm_C, state_mlstm_n, state_mlstm_m, c_ctx,
           mod_w, mod_b, norm1_g, norm2_g, final_g, w_in, b_in, hy_conv, hy_w1, hy_b1, hy_w2, hy_b2,
           hy_w3, hy_freq, hy_bias, gla_a2_w, gla_a2_b, gla_norm_g, ml_conv, ml_gate_b, ml_norm_g,
           w_branch, w_out, peer_wq, peer_keys, peer_u, peer_v):
    w_main, b_main, w_small, b_small = _reorder_in_proj(w_in, b_in)
    w_small_hi = w_small.astype(BF16)
    w_small_lo = (w_small - w_small_hi.astype(F32)).astype(BF16)
    w_branch_b = w_branch.astype(BF16)
    w_out_b = w_out.astype(BF16)
    wq_hi = peer_wq.astype(BF16)
    wq_lo = (peer_wq - wq_hi.astype(F32)).astype(BF16)
    u_b = peer_u.astype(BF16)
    vt_b = transpose_cast_chunks(peer_v, PEER_CHUNK)
    zero_b = jnp.zeros((1, D_MODEL), F32)
    a2f = jnp.pad(gla_a2_w[:, 0], ((0, 0), (0, N_SMALL - GLA_RANK), (0, 0)))
    a2b = jnp.pad(gla_a2_w[:, 1], ((0, 0), (GLA_RANK, N_SMALL - 2 * GLA_RANK), (0, 0)))
    a2bias = gla_a2_b.reshape(DEPTH, 2, 1, GLA_HEADS * GLA_DK)
    n0_all = state_mlstm_n.reshape(DEC_BATCH, DEPTH, 2, ML_HEADS, 1, ML_DH)

    tabs = {}
    for length in (SEQ, DEC_SEQ):
        fre, fim, g_mat, sgn = _dft_tables(length)
        fre_h, fre_l = _split2(fre)
        fim_h, fim_l = _split2(fim)
        tabs[length] = dict(spec=(fre_h, fre_l, fim_h, fim_l, sgn),
                            f=jnp.concatenate([fre_h, fim_h], axis=0), g=g_mat.astype(BF16))

    cvec = jnp.zeros((MOD_ROWS, D_MODEL), F32).at[:DEC_BATCH].set(c).at[CTX_ROW].set(c_ctx)
    mod_all = mod_table(cvec, mod_w, mod_b)

    x = jnp.concatenate([x_prompt.reshape(N_PROMPT, D_MODEL), x_sample.reshape(N_SAMPLE, D_MODEL)], axis=0)
    groups = ((SEQ, BATCH, 0, SEQ), (DEC_SEQ, DEC_BATCH, N_PROMPT // DEC_SEQ, GRID_W))
    new_gla, new_c, new_n, new_m = [], [], [], []
    for l in range(DEPTH):
        mod3 = mod_all[l].reshape(MOD_ROWS, 1, 6 * D_MODEL)
        h_hi, zsmall = normmod_gates(x, norm1_g[l], mod3, 0, 1, w_small_hi, w_small_lo, l,
                                     b_small[l].reshape(1, N_SMALL))
        zmain = mm_bias(h_hi, w_main, l, b_main[l].reshape(1, N_MAIN), tm=1024, tn=2048)
        mi = zsmall[:, 32:40].reshape(N_ROWS, 2, ML_HEADS)
        mf = zsmall[:, 40:48].reshape(N_ROWS, 2, ML_HEADS)
        gcol = jnp.concatenate([mi, mf, jnp.zeros((N_ROWS, 4, ML_HEADS), F32)], axis=1)
        gates_col = jnp.transpose(gcol, (2, 0, 1))
        gates_row = jnp.transpose(gcol.reshape(N_ROWS // CHUNK, CHUNK, 8, ML_HEADS), (3, 0, 2, 1))
        gate_b = ml_gate_b[l].reshape(-1)
        y_hy, y_gla, y_ml = [], [], []
        for gi, (length, n_seq, blk0, seg) in enumerate(groups):
            t = tabs[length]
            sre, sim = hyena_spectrum(length, hy_w1[l], hy_b1[l], hy_w2[l], hy_b2[l], hy_w3[l],
                                      hy_freq[l], t["spec"])
            y_hy.append(hyena(zmain, hy_conv[l], sre, sim, hy_bias[l], t["f"], t["g"],
                              length, n_seq, blk0, seg))
            gla_args = (zmain, zsmall, a2f[l], a2b[l], a2bias[l], gla_norm_g[l].reshape(1, GLA_DV),
                        length, n_seq, blk0)
            ml_args = (zmain, gates_col, gates_row, gate_b, ml_conv[l], ml_norm_g[l].reshape(1, ML_DH),
                       length, n_seq, blk0, seg)
            if gi == 0:
                yg, s_fin = gla(*gla_args, None, l, n_sub=PROMPT_SEQS_PER_STEP)
                ym, c_fin, n_fin, m_fin = mlstm(*ml_args, None, l, n_sub=PROMPT_SEQS_PER_STEP)
                new_gla.append(s_fin)
                new_c.append(c_fin)
                new_n.append(n_fin[:, :, :, 0, :])
                new_m.append(m_fin[:, :, :, 0, 0])
            else:
                yg = gla(*gla_args, state_gla, l)
                ym = mlstm(*ml_args, (state_mlstm_C, n0_all, state_mlstm_m[:, l].reshape(-1)), l)
            y_gla.append(yg)
            y_ml.append(ym)
        merged = merge_branches(y_hy, y_gla, y_ml, w_branch_b, l, zmain)
        x = mm_residual(merged, w_out_b, l, x, mod3, 2)
        h2, h2_lo, h2t = normmod_peer(x, norm2_g[l], mod3, 3, 4)
        q = mm3_bias(h2, h2_lo, wq_hi, wq_lo, l, zero_b)
        nb, rk, e1, e2 = peer_topk(q, peer_keys[l])
        x = peer_experts(h2t, u_b, vt_b, l, nb, rk, e1, e2, x, mod3, 5)

    y_prompt = final_norm(x, final_g, 0, N_PROMPT).reshape(BATCH, SEQ, D_MODEL)
    y_sample = final_norm(x, final_g, N_PROMPT, N_SAMPLE).reshape(DEC_BATCH, DEC_SEQ, D_MODEL)
    return (y_prompt, y_sample, jnp.stack(new_gla, axis=1), jnp.stack(new_c, axis=1),
            jnp.stack(new_n, axis=1), jnp.stack(new_m, axis=1))
```

```python
import functools
import math

import jax
import jax.numpy as jnp
import numpy as np
from jax import lax
from jax.experimental import pallas as pl
from jax.experimental.pallas import tpu as pltpu

F32 = jnp.float32
BF16 = jnp.bfloat16

D_MODEL = 2048
BATCH = 16
SEQ = 256
DEPTH = 4
DEC_BATCH = 4
DEC_SEQ = 1024
GRID_W = 64
EPS = 1e-6
CHUNK = 64
HY_WIDTH = 1024
HY_EMB = 33
HY_BANDS = (HY_EMB - 1) // 2
HY_FFN = 64
HY_DECAY_TARGET = 1e-2
HY_FAST_PCT = 0.3
HY_SLOW_PCT = 1.5
GLA_HEADS = 4
GLA_DK = 128
GLA_DV = 256
GLA_RANK = 16
GLA_NORMALIZER = 16.0
ML_HEADS = 4
ML_DH = 256
PEER_HEADS = 8
PEER_NKEYS = 128
PEER_EXPERTS = PEER_NKEYS * PEER_NKEYS
PEER_TOPK = 16
PEER_CHUNK = 1024
PROMPT_SEQS_PER_STEP = 2
HY_CHAIN_W = 256

N_PROMPT = BATCH * SEQ
N_SAMPLE = DEC_BATCH * DEC_SEQ
N_ROWS = N_PROMPT + N_SAMPLE
CTX_ROW = DEC_BATCH
MOD_ROWS = 8

C_HX1, C_HX2, C_HV = 0, 1024, 2048
C_GQ, C_GK, C_GV, C_GR = 3072, 3584, 4096, 5120
C_MQ, C_MK, C_MV, C_MO = 6144, 7168, 8192, 9216
C_GA, C_GB, C_GC = 10240, 12288, 14336
N_MAIN = 16384
N_SMALL = 128

VMEM_LIMIT = 56 * 1024 * 1024


def _cp(sem):
    return pltpu.CompilerParams(dimension_semantics=sem, vmem_limit_bytes=VMEM_LIMIT)


def _dot(a, b):
    return jnp.dot(a, b, preferred_element_type=F32)


def _dot_nt(a, b):
    return lax.dot_general(a, b, (((1,), (1,)), ((), ())), preferred_element_type=F32)


def _dot_tn(a, b):
    return lax.dot_general(a, b, (((0,), (0,)), ((), ())), preferred_element_type=F32)


def _split2(a):
    hi = a.astype(BF16)
    lo = (a - hi.astype(F32)).astype(BF16)
    return hi, lo


def _split3(a):
    a1 = a.astype(BF16)
    r1 = a - a1.astype(F32)
    a2 = r1.astype(BF16)
    a3 = (r1 - a2.astype(F32)).astype(BF16)
    return a1, a2, a3


def _dot3(a, b, dot=_dot):
    ah, al = _split2(a)
    bh, bl = _split2(b)
    return dot(ah, bh) + (dot(ah, bl) + dot(al, bh))


def _log_sigmoid(x):
    return jnp.minimum(x, 0.0) - jnp.log(1.0 + jnp.exp(-jnp.abs(x)))


def _mod_row(i, tm):
    n_p = N_PROMPT // tm
    return jnp.where(i < n_p, CTX_ROW, (i - n_p) // (DEC_SEQ // tm))


def _mod_kernel(c_ref, w_ref, b_ref, o_ref):
    a = c_ref[...]
    a = a * jax.nn.sigmoid(a)
    o_ref[0] = _dot3(a, w_ref[0]) + b_ref[0]


def mod_table(cvec, mod_w, mod_b):
    tn = 1024
    n = mod_w.shape[-1]
    return pl.pallas_call(
        _mod_kernel,
        grid=(DEPTH, n // tn),
        in_specs=[pl.BlockSpec((MOD_ROWS, D_MODEL), lambda l, j: (0, 0)),
                  pl.BlockSpec((1, D_MODEL, tn), lambda l, j: (l, 0, j)),
                  pl.BlockSpec((1, 1, tn), lambda l, j: (l, 0, j))],
        out_specs=pl.BlockSpec((1, MOD_ROWS, tn), lambda l, j: (l, 0, j)),
        out_shape=jax.ShapeDtypeStruct((DEPTH, MOD_ROWS, n), F32),
        compiler_params=_cp(("parallel", "parallel")),
        name="mod_table",
    )(cvec, mod_w, mod_b.reshape(DEPTH, 1, n))


def _norm_modulate(x_ref, g_ref, sc_ref, sh_ref):
    x = x_ref[...]
    y = x * lax.rsqrt(jnp.mean(x * x, axis=-1, keepdims=True) + EPS) * g_ref[...]
    return y * (1.0 + sc_ref[...]) + sh_ref[...]


def _normmod_gates_kernel(x_ref, g_ref, sc_ref, sh_ref, wh_ref, wl_ref, b_ref, hi_ref, zs_ref):
    h = _norm_modulate(x_ref, g_ref, sc_ref, sh_ref)
    hi = h.astype(BF16)
    lo = (h - hi.astype(F32)).astype(BF16)
    hi_ref[...] = hi
    zs_ref[...] = _dot(hi, wh_ref[...]) + (_dot(hi, wl_ref[...]) + _dot(lo, wh_ref[...])) + b_ref[...]


def _normmod_peer_kernel(x_ref, g_ref, sc_ref, sh_ref, hi_ref, lo_ref, t_ref):
    h = _norm_modulate(x_ref, g_ref, sc_ref, sh_ref)
    hi = h.astype(BF16)
    hi_ref[...] = hi
    lo_ref[...] = (h - hi.astype(F32)).astype(BF16)
    t_ref[...] = h.T.astype(BF16)


def _normmod_specs(tm, sh_chunk, sc_chunk):
    return [pl.BlockSpec((tm, D_MODEL), lambda i: (i, 0)),
            pl.BlockSpec((1, D_MODEL), lambda i: (0, 0)),
            pl.BlockSpec((None, 1, D_MODEL), lambda i: (_mod_row(i, tm), 0, sc_chunk)),
            pl.BlockSpec((None, 1, D_MODEL), lambda i: (_mod_row(i, tm), 0, sh_chunk))]


def normmod_gates(x, g, mod3, sh_chunk, sc_chunk, w_hi, w_lo, layer, b):
    tm = 256
    n = w_hi.shape[-1]
    row_spec = pl.BlockSpec((tm, D_MODEL), lambda i: (i, 0))
    w_spec = pl.BlockSpec((None, D_MODEL, n), lambda i: (layer, 0, 0))
    return pl.pallas_call(
        _normmod_gates_kernel,
        grid=(N_ROWS // tm,),
        in_specs=_normmod_specs(tm, sh_chunk, sc_chunk) + [w_spec, w_spec, pl.BlockSpec((1, n), lambda i: (0, 0))],
        out_specs=[row_spec, pl.BlockSpec((tm, n), lambda i: (i, 0))],
        out_shape=[jax.ShapeDtypeStruct((N_ROWS, D_MODEL), BF16), jax.ShapeDtypeStruct((N_ROWS, n), F32)],
        compiler_params=_cp(("parallel",)),
        name="normmod_gates",
    )(x, g.reshape(1, D_MODEL), mod3, mod3, w_hi, w_lo, b)


def normmod_peer(x, g, mod3, sh_chunk, sc_chunk):
    tm = 256
    row_spec = pl.BlockSpec((tm, D_MODEL), lambda i: (i, 0))
    row_shape = jax.ShapeDtypeStruct((N_ROWS, D_MODEL), BF16)
    return pl.pallas_call(
        _normmod_peer_kernel,
        grid=(N_ROWS // tm,),
        in_specs=_normmod_specs(tm, sh_chunk, sc_chunk),
        out_specs=[row_spec, row_spec, pl.BlockSpec((D_MODEL, tm), lambda i: (0, i))],
        out_shape=[row_shape, row_shape, jax.ShapeDtypeStruct((D_MODEL, N_ROWS), BF16)],
        compiler_params=_cp(("parallel",)),
        name="normmod_peer",
    )(x, g.reshape(1, D_MODEL), mod3, mod3)


def _final_norm_kernel(x_ref, g_ref, o_ref):
    x = x_ref[...]
    o_ref[...] = x * lax.rsqrt(jnp.mean(x * x, axis=-1, keepdims=True) + EPS) * g_ref[...]


def final_norm(x, g, row0, n_rows):
    tm = 256
    return pl.pallas_call(
        _final_norm_kernel,
        grid=(n_rows // tm,),
        in_specs=[pl.BlockSpec((tm, D_MODEL), lambda i: (row0 // tm + i, 0)),
                  pl.BlockSpec((1, D_MODEL), lambda i: (0, 0))],
        out_specs=pl.BlockSpec((tm, D_MODEL), lambda i: (i, 0)),
        out_shape=jax.ShapeDtypeStruct((n_rows, D_MODEL), F32),
        compiler_params=_cp(("parallel",)),
        name="final_norm",
    )(x, g.reshape(1, D_MODEL))


def _mm_bias_kernel(x_ref, w_ref, b_ref, o_ref):
    o_ref[...] = (_dot(x_ref[...], w_ref[...]) + b_ref[...]).astype(o_ref.dtype)


def mm_bias(x, w, layer, b, out_dtype=F32, tm=512, tn=1024):
    m, k = x.shape
    n = w.shape[-1]
    return pl.pallas_call(
        _mm_bias_kernel,
        grid=(n // tn, m // tm),
        in_specs=[pl.BlockSpec((tm, k), lambda j, i: (i, 0)),
                  pl.BlockSpec((None, k, tn), lambda j, i: (layer, 0, j)),
                  pl.BlockSpec((1, tn), lambda j, i: (0, j))],
        out_specs=pl.BlockSpec((tm, tn), lambda j, i: (i, j)),
        out_shape=jax.ShapeDtypeStruct((m, n), out_dtype),
        compiler_params=_cp(("parallel", "parallel")),
        name="mm_bias",
    )(x, w, b)


def _mm3_bias_kernel(xh_ref, xl_ref, wh_ref, wl_ref, b_ref, o_ref):
    xh = xh_ref[...]
    acc = _dot(xh, wh_ref[...]) + (_dot(xh, wl_ref[...]) + _dot(xl_ref[...], wh_ref[...]))
    o_ref[...] = acc + b_ref[...]


def mm3_bias(xh, xl, wh, wl, layer, b, tm=512, tn=1024):
    m, k = xh.shape
    n = wh.shape[-1]
    tn = min(tn, n)
    return pl.pallas_call(
        _mm3_bias_kernel,
        grid=(n // tn, m // tm),
        in_specs=[pl.BlockSpec((tm, k), lambda j, i: (i, 0)),
                  pl.BlockSpec((tm, k), lambda j, i: (i, 0)),
                  pl.BlockSpec((None, k, tn), lambda j, i: (layer, 0, j)),
                  pl.BlockSpec((None, k, tn), lambda j, i: (layer, 0, j)),
                  pl.BlockSpec((1, tn), lambda j, i: (0, j))],
        out_specs=pl.BlockSpec((tm, tn), lambda j, i: (i, j)),
        out_shape=jax.ShapeDtypeStruct((m, n), F32),
        compiler_params=_cp(("parallel", "parallel")),
        name="mm3_bias",
    )(xh, xl, wh, wl, b)


def _merge_kernel(yhp_ref, yhs_ref, ygp_ref, ygs_ref, ymp_ref, yms_ref, w_ref, ga_ref, gb_ref, gc_ref, o_ref,
                  *, n_prompt_tiles):
    is_prompt = pl.program_id(1) < n_prompt_tiles
    pick = lambda p_ref, s_ref: jnp.where(is_prompt, p_ref[...], s_ref[...])
    acc = jax.nn.sigmoid(ga_ref[...]) * _dot(pick(yhp_ref, yhs_ref), w_ref[0])
    acc += jax.nn.sigmoid(gb_ref[...]) * _dot(pick(ygp_ref, ygs_ref), w_ref[1])
    acc += jax.nn.sigmoid(gc_ref[...]) * _dot(pick(ymp_ref, yms_ref), w_ref[2])
    o_ref[...] = acc.astype(o_ref.dtype)


def merge_branches(y_hy, y_gla, y_ml, w_branch, layer, zmain, tm=512, tn=1024):
    kb = HY_WIDTH
    n_p = N_PROMPT // tm
    p_spec = pl.BlockSpec((tm, kb), lambda j, i: (jnp.minimum(i, n_p - 1), 0))
    s_spec = pl.BlockSpec((tm, kb), lambda j, i: (jnp.maximum(i - n_p, 0), 0))

    def gate_spec(col):
        return pl.BlockSpec((tm, tn), lambda j, i, c=col // tn: (i, c + j))

    return pl.pallas_call(
        functools.partial(_merge_kernel, n_prompt_tiles=n_p),
        grid=(D_MODEL // tn, N_ROWS // tm),
        in_specs=[p_spec, s_spec, p_spec, s_spec, p_spec, s_spec,
                  pl.BlockSpec((None, 3, kb, tn), lambda j, i: (layer, 0, 0, j)),
                  gate_spec(C_GA), gate_spec(C_GB), gate_spec(C_GC)],
        out_specs=pl.BlockSpec((tm, tn), lambda j, i: (i, j)),
        out_shape=jax.ShapeDtypeStruct((N_ROWS, D_MODEL), BF16),
        compiler_params=_cp(("parallel", "parallel")),
        name="merge_branches",
    )(*y_hy, *y_gla, *y_ml, w_branch, zmain, zmain, zmain)


def _mm_resid_kernel(m_ref, w_ref, x_ref, gt_ref, o_ref):
    o_ref[...] = x_ref[...] + gt_ref[...] * _dot(m_ref[...], w_ref[...])


def mm_residual(merged, w, layer, x, mod3, gt_chunk, tm=512, tn=1024):
    k = merged.shape[1]
    return pl.pallas_call(
        _mm_resid_kernel,
        grid=(D_MODEL // tn, N_ROWS // tm),
        in_specs=[pl.BlockSpec((tm, k), lambda j, i: (i, 0)),
                  pl.BlockSpec((None, k, tn), lambda j, i: (layer, 0, j)),
                  pl.BlockSpec((tm, tn), lambda j, i: (i, j)),
                  pl.BlockSpec((None, 1, tn),
                               lambda j, i: (_mod_row(i, tm), 0, gt_chunk * (D_MODEL // tn) + j))],
        out_specs=pl.BlockSpec((tm, tn), lambda j, i: (i, j)),
        out_shape=jax.ShapeDtypeStruct((N_ROWS, D_MODEL), F32),
        compiler_params=_cp(("parallel", "parallel")),
        name="mm_residual",
    )(merged, w, x, mod3)


def _dft_tables(length):
    k = jnp.arange(length, dtype=jnp.int32)
    m = (k[:, None] * k[None, :]) % (2 * length)
    ang = m.astype(F32) * (math.pi / length)
    cos = jnp.cos(ang)
    sin = jnp.sin(ang)
    sgn = jnp.where(k % 2 == 0, 1.0, -1.0).astype(F32)
    fre = cos
    fim = jnp.where(k[:, None] == 0, sgn[None, :], -sin)
    wk = jnp.where(k == 0, 1.0, 2.0).astype(F32) / (2.0 * length)
    g_re = cos.T * wk[None, :]
    g_im = jnp.where(k[None, :] == 0, sgn[:, None] / (2.0 * length), -sin.T / length)
    return fre, fim, jnp.concatenate([g_re, g_im], axis=1), sgn


def _hyfilt_kernel(z_ref, w1_ref, b1_ref, w2_ref, b2_ref, fq_ref, w3f_ref, w3b_ref, tn_ref, dl_ref,
                   sgn_ref, freh_ref, frel_ref, fimh_ref, fiml_ref, sre_ref, sim_ref):
    hid = jnp.sin(fq_ref[0:1] * (_dot3(z_ref[...], w1_ref[...]) + b1_ref[...]))
    hid = jnp.sin(fq_ref[1:2] * (_dot3(hid, w2_ref[...]) + b2_ref[...]))
    decay = jnp.exp(-tn_ref[...] * dl_ref[...])
    fwd = _dot3(hid, w3f_ref[...]) * decay
    bwd = _dot3(hid, w3b_ref[...]) * decay
    row = lax.broadcasted_iota(jnp.int32, fwd.shape, 0)
    bwd = jnp.where(row == 0, 0.0, bwd)
    a = fwd + bwd
    d = fwd - bwd
    ah, al = _split2(a)
    dh, dl2 = _split2(d)
    re = _dot(freh_ref[...], ah) + (_dot(freh_ref[...], al) + _dot(frel_ref[...], ah))
    im = _dot(fimh_ref[...], dh) + (_dot(fimh_ref[...], dl2) + _dot(fiml_ref[...], dh))
    nyq = jnp.sum(sgn_ref[...] * a, axis=0, keepdims=True)
    sre_ref[0] = re
    sim_ref[0] = jnp.where(row == 0, nyq, im)


def hyena_spectrum(length, w1, b1, w2, b2, w3, freq, tabs):
    fre_h, fre_l, fim_h, fim_l, sgn = tabs
    t = jnp.arange(length, dtype=F32)
    t_norm = t / (length - 1)
    bands = jnp.linspace(1e-4, HY_BANDS - 1, HY_BANDS, dtype=F32)
    ang = (2.0 * math.pi / length) * t[:, None] * bands[None, :]
    z = jnp.concatenate([t_norm[:, None], jnp.cos(ang), -jnp.sin(ang),
                         jnp.zeros((length, HY_FFN - HY_EMB), F32)], axis=-1)
    w1p = jnp.pad(w1, ((0, HY_FFN - HY_EMB), (0, 0)))
    max_decay = math.log(HY_DECAY_TARGET) / HY_FAST_PCT
    min_decay = math.log(HY_DECAY_TARGET) / HY_SLOW_PCT
    deltas = jnp.abs(jnp.linspace(min_decay, max_decay, HY_WIDTH, dtype=F32)).reshape(1, HY_WIDTH)
    ct = 256
    nct = HY_WIDTH // ct
    full = lambda shape: pl.BlockSpec(shape, lambda o, j: (0,) * len(shape))
    out_spec = pl.BlockSpec((1, length, ct), lambda o, j: (o, 0, j))
    return pl.pallas_call(
        _hyfilt_kernel,
        grid=(2, nct),
        in_specs=[full((length, HY_FFN)), full((HY_FFN, HY_FFN)), full((1, HY_FFN)),
                  full((HY_FFN, HY_FFN)), full((1, HY_FFN)), full((2, HY_FFN)),
                  pl.BlockSpec((HY_FFN, ct), lambda o, j: (0, o * 2 * nct + j)),
                  pl.BlockSpec((HY_FFN, ct), lambda o, j: (0, o * 2 * nct + nct + j)),
                  full((length, 1)),
                  pl.BlockSpec((1, ct), lambda o, j: (0, j)),
                  full((length, 1)),
                  full((length, length)), full((length, length)),
                  full((length, length)), full((length, length))],
        out_specs=[out_spec, out_spec],
        out_shape=[jax.ShapeDtypeStruct((2, length, HY_WIDTH), F32)] * 2,
        compiler_params=_cp(("parallel", "parallel")),
        name="hyena_spectrum",
    )(z, w1p, b1.reshape(1, HY_FFN), w2, b2.reshape(1, HY_FFN), freq, w3, w3,
      t_norm.reshape(length, 1), deltas, sgn.reshape(length, 1), fre_h, fre_l, fim_h, fim_l)


def _short_conv(x, w, pos, seg, length):
    prev = jnp.where(pos == 0, 0.0, pltpu.roll(x, 1, 0))
    nxt = jnp.where(pos == seg - 1, 0.0, pltpu.roll(x, length - 1, 0))
    return prev * w[0:1] + x * w[1:2] + nxt * w[2:3]


def _hyena_kernel(x1_ref, x2_ref, v_ref, cw_ref, sre_ref, sim_ref, hb_ref, f_ref, g_ref, o_ref,
                  *, length, seg):
    row = lax.broadcasted_iota(jnp.int32, (length, 1), 0)
    pos = row % seg
    row0 = row == 0
    groups = [slice(c * HY_CHAIN_W, (c + 1) * HY_CHAIN_W) for c in range(x1_ref.shape[1] // HY_CHAIN_W)]

    def long_convs(us, o):
        specs = [_dot(f_ref[...], u.astype(BF16)) for u in us]
        ys = []
        for spec, cols in zip(specs, groups):
            ur, ui = spec[:length], spec[length:]
            hre, him = sre_ref[o, :, cols], sim_ref[o, :, cols]
            uihi = ui * him
            yr = ur * hre - jnp.where(row0, 0.0, uihi)
            yi = jnp.where(row0, uihi, ur * him + ui * hre)
            ys.append(jnp.concatenate([yr, yi], axis=0).astype(BF16))
        outs = [_dot(g_ref[...], y) for y in ys]
        return [y + u * hb_ref[o:o + 1, cols] for y, u, cols in zip(outs, us, groups)]

    conv = lambda ref, k, cols: _short_conv(ref[:, cols], cw_ref[k, :, cols], pos, seg, length)
    vs = [conv(v_ref, 2, cols) for cols in groups]
    c1 = long_convs(vs, 0)
    zs = [conv(x1_ref, 0, cols) * c for cols, c in zip(groups, c1)]
    c2 = long_convs(zs, 1)
    for cols, c in zip(groups, c2):
        o_ref[:, cols] = (conv(x2_ref, 1, cols) * c).astype(o_ref.dtype)


def hyena(zmain, conv_w, spec_re, spec_im, bias, f_mat, g_mat, length, n_seq, row_blk0, seg):
    ct = 2 * HY_CHAIN_W
    nct = HY_WIDTH // ct
    once = pl.Buffered(1)

    def zspec(col):
        return pl.BlockSpec((length, ct), lambda j, b, c=col // ct: (row_blk0 + b, c + j))

    return pl.pallas_call(
        functools.partial(_hyena_kernel, length=length, seg=seg),
        grid=(nct, n_seq),
        in_specs=[zspec(C_HX1), zspec(C_HX2), zspec(C_HV),
                  pl.BlockSpec((3, 3, ct), lambda j, b: (0, 0, j)),
                  pl.BlockSpec((2, length, ct), lambda j, b: (0, 0, j), pipeline_mode=once),
                  pl.BlockSpec((2, length, ct), lambda j, b: (0, 0, j), pipeline_mode=once),
                  pl.BlockSpec((2, ct), lambda j, b: (0, j)),
                  pl.BlockSpec((2 * length, length), lambda j, b: (0, 0), pipeline_mode=once),
                  pl.BlockSpec((length, 2 * length), lambda j, b: (0, 0), pipeline_mode=once)],
        out_specs=pl.BlockSpec((length, ct), lambda j, b: (b, j)),
        out_shape=jax.ShapeDtypeStruct((n_seq * length, HY_WIDTH), BF16),
        compiler_params=_cp(("parallel", "parallel")),
        name="hyena",
    )(zmain, zmain, zmain, conv_w, spec_re, spec_im, bias, f_mat, g_mat)


def _tri_masks():
    t = lax.broadcasted_iota(jnp.int32, (CHUNK, CHUNK), 0)
    s = lax.broadcasted_iota(jnp.int32, (CHUNK, CHUNK), 1)
    return s <= t, s >= t


def _gla_chunks(qs, ks, vs, gs, states, masks, revs):
    idx = range(len(qs))
    ones = jnp.ones((CHUNK, GLA_DK), BF16)
    tms = [jnp.where(masks[i], 1.0, 0.0).astype(BF16) for i in idx]
    gsp = [_split3(gs[i]) for i in idx]
    bcs = [_dot(tms[i], gsp[i][0]) + (_dot(tms[i], gsp[i][1]) + _dot(tms[i], gsp[i][2])) for i in idx]
    tots = [_dot_tn(gsp[i][0], ones) + (_dot_tn(gsp[i][1], ones) + _dot_tn(gsp[i][2], ones)) for i in idx]
    b_ends = [bcs[i][0:1] if revs[i] else bcs[i][CHUNK - 1:CHUNK] for i in idx]
    refs = [bcs[i][CHUNK // 2:CHUNK // 2 + 1] for i in idx]
    inters = [_dot((qs[i] * jnp.exp(bcs[i])).astype(BF16), states[i].astype(BF16)) for i in idx]
    qhs = [(qs[i] * jnp.exp(bcs[i] - refs[i])).astype(BF16) for i in idx]
    khs = [(ks[i] * jnp.exp(refs[i] - bcs[i])).astype(BF16) for i in idx]
    atts = [jnp.where(masks[i], _dot_nt(qhs[i], khs[i]), 0.0).astype(BF16) for i in idx]
    vbs = [vs[i].astype(BF16) for i in idx]
    outs = [inters[i] + _dot(atts[i], vbs[i]) for i in idx]
    kds = [(ks[i] * jnp.exp(b_ends[i] - bcs[i])).astype(BF16) for i in idx]
    es = [jnp.exp(tots[i]) for i in idx]
    new_states = [jnp.concatenate([es[i], es[i]], axis=1) * states[i] + _dot_tn(kds[i], vbs[i]) for i in idx]
    return outs, new_states


def _gla_kernel(*refs, length, n_sub, has_init):
    if has_init:
        (q_ref, k_ref, v_ref, gr_ref, zs_ref, wf_ref, wb_ref, ab_ref, ng_ref, s0_ref,
         y_ref, o_ref, lg_ref, st_ref) = refs
    else:
        (q_ref, k_ref, v_ref, gr_ref, zs_ref, wf_ref, wb_ref, ab_ref, ng_ref,
         y_ref, sout_ref, o_ref, lg_ref, st_ref) = refs
    zs = zs_ref[...]
    lg_ref[0] = _log_sigmoid(_dot3(zs, wf_ref[...]) + ab_ref[0]) * (1.0 / GLA_NORMALIZER)
    lg_ref[1] = _log_sigmoid(_dot3(zs, wb_ref[...]) + ab_ref[1]) * (1.0 / GLA_NORMALIZER)
    if has_init:
        st_ref[...] = s0_ref[...]
    else:
        st_ref[...] = jnp.zeros_like(st_ref)
    o_ref[...] = jnp.zeros_like(o_ref)
    n = length // CHUNK
    mask_f, mask_b = _tri_masks()
    scale = GLA_DK ** -0.5

    def body(i, carry):
        chains = []
        for s in range(n_sub):
            for d, mask in enumerate((mask_f, mask_b)):
                c = i if d == 0 else n - 1 - i
                rows = pl.ds(pl.multiple_of(s * length + c * CHUNK, CHUNK), CHUNK)
                for h in range(GLA_HEADS):
                    chains.append((s, d, h, rows, mask))
        kcs = [slice(h * GLA_DK, (h + 1) * GLA_DK) for (_, _, h, _, _) in chains]
        vcs = [slice(h * GLA_DV, (h + 1) * GLA_DV) for (_, _, h, _, _) in chains]
        outs, new_states = _gla_chunks(
            [q_ref[ch[3], kc] * scale for ch, kc in zip(chains, kcs)],
            [k_ref[ch[3], kc] for ch, kc in zip(chains, kcs)],
            [v_ref[ch[3], vc] for ch, vc in zip(chains, vcs)],
            [lg_ref[ch[1], ch[3], kc] for ch, kc in zip(chains, kcs)],
            [st_ref[ch[0], ch[1], ch[2]] for ch in chains],
            [ch[4] for ch in chains], [ch[1] == 1 for ch in chains])
        for ch, vc, o, s_new in zip(chains, vcs, outs, new_states):
            o_ref[ch[3], vc] += o
            st_ref[ch[0], ch[1], ch[2]] = s_new
        return carry

    lax.fori_loop(0, n, body, 0)
    for h in range(GLA_HEADS):
        vc = slice(h * GLA_DV, (h + 1) * GLA_DV)
        o = o_ref[:, vc]
        o = o * lax.rsqrt(jnp.mean(o * o, axis=-1, keepdims=True) + EPS) * ng_ref[...]
        gr = gr_ref[:, vc]
        y_ref[:, vc] = (o * (gr * jax.nn.sigmoid(gr))).astype(y_ref.dtype)
    if not has_init:
        sout_ref[...] = st_ref[...]


def gla(zmain, zsmall, wf, wb, ab, norm_g, length, n_seq, row_blk0, state0, layer, n_sub=1):
    has_init = state0 is not None
    qk_w = GLA_HEADS * GLA_DK
    v_w = GLA_HEADS * GLA_DV
    once = pl.Buffered(1)
    rows = n_sub * length
    blk0 = row_blk0 // n_sub
    st_blk = (n_sub, 2, GLA_HEADS, GLA_DK, GLA_DV)

    def zspec(col, width):
        return pl.BlockSpec((rows, width), lambda b, c=col // width: (blk0 + b, c), pipeline_mode=once)

    in_specs = [zspec(C_GQ, qk_w), zspec(C_GK, qk_w), zspec(C_GV, v_w), zspec(C_GR, v_w),
                pl.BlockSpec((rows, N_SMALL), lambda b: (blk0 + b, 0)),
                pl.BlockSpec((N_SMALL, qk_w), lambda b: (0, 0)),
                pl.BlockSpec((N_SMALL, qk_w), lambda b: (0, 0)),
                pl.BlockSpec((2, 1, qk_w), lambda b: (0, 0, 0)),
                pl.BlockSpec((1, GLA_DV), lambda b: (0, 0))]
    args = [zmain, zmain, zmain, zmain, zsmall, wf, wb, ab, norm_g]
    y_shape = jax.ShapeDtypeStruct((n_seq * length, v_w), BF16)
    y_spec = pl.BlockSpec((rows, v_w), lambda b: (b, 0))
    if has_init:
        in_specs.append(pl.BlockSpec((n_sub, None, 2, GLA_HEADS, GLA_DK, GLA_DV),
                                     lambda b: (b, layer, 0, 0, 0, 0)))
        args.append(state0)
        out_shape, out_specs = y_shape, y_spec
    else:
        out_shape = [y_shape, jax.ShapeDtypeStruct((n_seq, 2, GLA_HEADS, GLA_DK, GLA_DV), F32)]
        out_specs = [y_spec, pl.BlockSpec(st_blk, lambda b: (b, 0, 0, 0, 0))]
    return pl.pallas_call(
        functools.partial(_gla_kernel, length=length, n_sub=n_sub, has_init=has_init),
        grid=(n_seq // n_sub,),
        in_specs=in_specs,
        out_specs=out_specs,
        out_shape=out_shape,
        scratch_shapes=[pltpu.VMEM((rows, v_w), F32), pltpu.VMEM((2, rows, qk_w), F32),
                        pltpu.VMEM(st_blk, F32)],
        compiler_params=_cp(("parallel",)),
        name="gla",
    )(*args)


def _mlstm_chunks(qs, kss, vs, lf_cs, li_cs, lf_rs, li_rs, cms, nvs, m_prevs, masks, mask_ts, revs):
    t_n = CHUNK
    idx = range(len(qs))
    tms = [jnp.where(masks[i], 1.0, 0.0).astype(BF16) for i in idx]
    tmts = [jnp.where(mask_ts[i], 1.0, 0.0).astype(BF16) for i in idx]
    csp = [_split3(jnp.broadcast_to(lf_cs[i], (t_n, t_n))) for i in idx]
    rsp = [_split3(jnp.broadcast_to(lf_rs[i], (t_n, t_n))) for i in idx]
    b_colbs = [_dot(tms[i], csp[i][0]) + (_dot(tms[i], csp[i][1]) + _dot(tms[i], csp[i][2])) for i in idx]
    b_rowbs = [_dot(rsp[i][0], tmts[i]) + (_dot(rsp[i][1], tmts[i]) + _dot(rsp[i][2], tmts[i])) for i in idx]
    qbs = [qs[i].astype(BF16) for i in idx]
    vbs = [vs[i].astype(BF16) for i in idx]
    qks = [_dot_nt(qbs[i], kss[i].astype(BF16)) for i in idx]
    qcs = [_dot(qbs[i], cms[i].astype(BF16)) for i in idx]
    qns = [jnp.sum(qs[i] * nvs[i], axis=-1, keepdims=True) for i in idx]
    b_cols = [b_colbs[i][:, 0:1] for i in idx]
    b_rows = [b_rowbs[i][0:1, :] for i in idx]
    b_ends = [b_colbs[i][0:1, 0:1] if revs[i] else b_colbs[i][t_n - 1:t_n, 0:1] for i in idx]
    dmats = [jnp.where(masks[i], b_colbs[i] - b_rowbs[i] + li_rs[i], -jnp.inf) for i in idx]
    m_ts = [jnp.maximum(b_cols[i] + m_prevs[i], jnp.max(dmats[i], axis=-1, keepdims=True)) for i in idx]
    w_inters = [jnp.exp(b_cols[i] + m_prevs[i] - m_ts[i]) for i in idx]
    scs = [qks[i] * jnp.exp(dmats[i] - m_ts[i]) for i in idx]
    svs = [_dot(scs[i].astype(BF16), vbs[i]) for i in idx]
    g_rs = [b_ends[i] - b_rows[i] + li_rs[i] for i in idx]
    g_cs = [b_ends[i] - b_cols[i] + li_cs[i] for i in idx]
    m_news = [jnp.maximum(b_ends[i] + m_prevs[i], jnp.max(g_rs[i], axis=-1, keepdims=True)) for i in idx]
    w_cs = [jnp.exp(b_ends[i] + m_prevs[i] - m_news[i]) for i in idx]
    kws = [kss[i] * jnp.exp(g_cs[i] - m_news[i]) for i in idx]
    kvs = [_dot_tn(kws[i].astype(BF16), vbs[i]) for i in idx]
    nums = [w_inters[i] * qcs[i] + svs[i] for i in idx]
    dens = [w_inters[i] * qns[i] + jnp.sum(scs[i], axis=-1, keepdims=True) for i in idx]
    hs = [nums[i] / jnp.maximum(jnp.abs(dens[i]), jnp.exp(-m_ts[i])) for i in idx]
    cm_news = [w_cs[i] * cms[i] + kvs[i] for i in idx]
    nv_news = [w_cs[i] * nvs[i] + jnp.sum(kws[i], axis=0, keepdims=True) for i in idx]
    return hs, cm_news, nv_news, m_news


def _mlstm_kernel(*refs, length, n_sub, seg, has_init):
    if has_init:
        (gb_ref, m0_ref, q_ref, k_ref, v_ref, mo_ref, gc_ref, gr_ref, cw_ref, ng_ref, c0_ref, n0_ref,
         y_ref, qc_ref, kc_ref, h_ref, c_ref, n_ref, m_ref) = refs
    else:
        (gb_ref, q_ref, k_ref, v_ref, mo_ref, gc_ref, gr_ref, cw_ref, ng_ref,
         y_ref, cout_ref, nout_ref, mout_ref, qc_ref, kc_ref, h_ref, c_ref, n_ref, m_ref) = refs
    b_idx = pl.program_id(0)
    dh = ML_DH
    rows_all = n_sub * length
    row = lax.broadcasted_iota(jnp.int32, (rows_all, 1), 0)
    pos = row % seg
    qc_ref[...] = _short_conv(q_ref[...], cw_ref[0], pos, seg, rows_all)
    kc_ref[...] = _short_conv(k_ref[...], cw_ref[1], pos, seg, rows_all) * (dh ** -0.5)
    h_ref[...] = jnp.zeros_like(h_ref)
    if has_init:
        c_ref[...] = c0_ref[...]
        n_ref[...] = n0_ref[...]
        for s in range(n_sub):
            for d in range(2):
                for h in range(ML_HEADS):
                    m0 = m0_ref[(b_idx * n_sub + s) * 2 * ML_HEADS + d * ML_HEADS + h]
                    m_ref[s, d, h] = jnp.full((1, 128), m0, F32)
    else:
        c_ref[...] = jnp.zeros_like(c_ref)
        n_ref[...] = jnp.zeros_like(n_ref)
        m_ref[...] = jnp.zeros_like(m_ref)
    n = length // CHUNK
    mask_f, mask_b = _tri_masks()

    def body(i, carry):
        chains = []
        for s in range(n_sub):
            for d, (mask, mask_t) in enumerate(((mask_f, mask_b), (mask_b, mask_f))):
                c = i if d == 0 else n - 1 - i
                rows = pl.ds(pl.multiple_of(s * length + c * CHUNK, CHUNK), CHUNK)
                for h in range(ML_HEADS):
                    chains.append((s, d, h, rows, mask, mask_t, c))
        args = [[] for _ in range(10)]
        for (s, d, h, rows, mask, mask_t, c) in chains:
            cols = slice(h * dh, (h + 1) * dh)
            bi = gb_ref[d * 2 * ML_HEADS + h]
            bf = gb_ref[d * 2 * ML_HEADS + ML_HEADS + h]
            gcol = gc_ref[h, rows, :]
            grow = gr_ref[h, s * n + c]
            vals = (qc_ref[rows, cols], kc_ref[rows, cols], v_ref[rows, cols],
                    _log_sigmoid(gcol[:, 2 + d:3 + d] + bf), gcol[:, d:d + 1] + bi,
                    _log_sigmoid(grow[2 + d:3 + d, :] + bf), grow[d:d + 1, :] + bi,
                    c_ref[s, d, h], n_ref[s, d, h], m_ref[s, d, h][:, 0:1])
            for lst, val in zip(args, vals):
                lst.append(val)
        hs, cms, nvs, m_news = _mlstm_chunks(*args, [ch[4] for ch in chains], [ch[5] for ch in chains],
                                             [ch[1] == 1 for ch in chains])
        for (s, d, h, rows, _, _, _), hc, cm, nv, m_new in zip(chains, hs, cms, nvs, m_news):
            h_ref[rows, h * dh:(h + 1) * dh] += hc
            c_ref[s, d, h] = cm
            n_ref[s, d, h] = nv
            m_ref[s, d, h] = jnp.broadcast_to(m_new, (1, 128))
        return carry

    lax.fori_loop(0, n, body, 0)
    for h in range(ML_HEADS):
        cols = slice(h * dh, (h + 1) * dh)
        o = h_ref[:, cols]
        o = o * lax.rsqrt(jnp.mean(o * o, axis=-1, keepdims=True) + EPS) * ng_ref[...]
        y_ref[:, cols] = (o * jax.nn.sigmoid(mo_ref[:, cols])).astype(y_ref.dtype)
    if not has_init:
        cout_ref[...] = c_ref[...]
        nout_ref[...] = n_ref[...]
        mout_ref[...] = m_ref[...]


def mlstm(zmain, gates_col, gates_row, gate_b, conv_w, norm_g, length, n_seq, row_blk0, seg, init, layer,
          n_sub=1):
    has_init = init is not None
    dh = ML_DH
    nh = ML_HEADS
    width = nh * dh
    once = pl.Buffered(1)
    rows = n_sub * length
    blk0 = row_blk0 // n_sub

    def zspec(col):
        return pl.BlockSpec((rows, width), lambda b, c=col // width: (blk0 + b, c), pipeline_mode=once)

    smem = pl.BlockSpec(memory_space=pltpu.SMEM)
    in_specs = [smem]
    args = [gate_b]
    if has_init:
        c0, n0, m0 = init
        in_specs.append(smem)
        args.append(m0)
    in_specs += [zspec(C_MQ), zspec(C_MK), zspec(C_MV), zspec(C_MO),
                 pl.BlockSpec((nh, rows, 8), lambda b: (0, blk0 + b, 0)),
                 pl.BlockSpec((nh, rows // CHUNK, 8, CHUNK), lambda b: (0, blk0 + b, 0, 0)),
                 pl.BlockSpec((2, 3, width), lambda b: (0, 0, 0)),
                 pl.BlockSpec((1, dh), lambda b: (0, 0))]
    args += [zmain, zmain, zmain, zmain, gates_col, gates_row, conv_w, norm_g]
    y_shape = jax.ShapeDtypeStruct((n_seq * length, width), BF16)
    y_spec = pl.BlockSpec((rows, width), lambda b: (b, 0))
    if has_init:
        in_specs += [pl.BlockSpec((n_sub, None, 2, nh, dh, dh), lambda b: (b, layer, 0, 0, 0, 0)),
                     pl.BlockSpec((n_sub, None, 2, nh, 1, dh), lambda b: (b, layer, 0, 0, 0, 0))]
        args += [c0, n0]
        out_shape, out_specs = y_shape, y_spec
    else:
        out_shape = [y_shape,
                     jax.ShapeDtypeStruct((n_seq, 2, nh, dh, dh), F32),
                     jax.ShapeDtypeStruct((n_seq, 2, nh, 1, dh), F32),
                     jax.ShapeDtypeStruct((n_seq, 2, nh, 1, 128), F32)]
        out_specs = [y_spec,
                     pl.BlockSpec((n_sub, 2, nh, dh, dh), lambda b: (b, 0, 0, 0, 0)),
                     pl.BlockSpec((n_sub, 2, nh, 1, dh), lambda b: (b, 0, 0, 0, 0)),
                     pl.BlockSpec((n_sub, 2, nh, 1, 128), lambda b: (b, 0, 0, 0, 0))]
    return pl.pallas_call(
        functools.partial(_mlstm_kernel, length=length, n_sub=n_sub, seg=seg, has_init=has_init),
        grid=(n_seq // n_sub,),
        in_specs=in_specs,
        out_specs=out_specs,
        out_shape=out_shape,
        scratch_shapes=[pltpu.VMEM((rows, width), F32), pltpu.VMEM((rows, width), F32),
                        pltpu.VMEM((rows, width), F32),
                        pltpu.VMEM((n_sub, 2, nh, dh, dh), F32), pltpu.VMEM((n_sub, 2, nh, 1, dh), F32),
                        pltpu.VMEM((n_sub, 2, nh, 1, 128), F32)],
        compiler_params=_cp(("parallel",)),
        name="mlstm",
    )(*args)


def _top_values(xs, count, with_rank=None):
    with_rank = with_rank or [False] * len(xs)
    curs = list(xs)
    vals = [[] for _ in xs]
    ranks = [jnp.full(x.shape, float(count), F32) if wr else None for x, wr in zip(xs, with_rank)]
    for r in range(count):
        ms = [jnp.max(c, axis=0, keepdims=True) for c in curs]
        for v, m in zip(vals, ms):
            v.append(m)
        need = [wr or r + 1 < count for wr in with_rank]
        tops = [c == m if nd else None for c, m, nd in zip(curs, ms, need)]
        curs = [jnp.where(t, -jnp.inf, c) if nd else c for c, t, nd in zip(curs, tops, need)]
        ranks = [jnp.where(t, float(r), rk) if wr else rk for rk, t, wr in zip(ranks, tops, with_rank)]
    return vals, ranks


def _peer_topk_kernel(q_ref, k_ref, nb_ref, rk_ref, e1_ref, e2_ref):
    dk = PEER_NKEYS
    s1 = _dot3(k_ref[0], q_ref[:, 0:dk], dot=_dot_nt)
    s2 = _dot3(k_ref[1], q_ref[:, dk:2 * dk], dot=_dot_nt)
    blocks = [slice(b * 128, (b + 1) * 128) for b in range(s1.shape[1] // 128)]
    nblk = len(blocks)
    s1s = [s1[:, lanes] for lanes in blocks]
    s2s = [s2[:, lanes] for lanes in blocks]
    vals, ranks = _top_values(s1s + s2s, PEER_TOPK, [False] * nblk + [True] * nblk)
    v1s, v2s, rank2s = vals[:nblk], vals[nblk:], ranks[nblk:]
    cands = []
    for v1, v2 in zip(v1s, v2s):
        rows = [v1[a] + v2[b] for a in range(PEER_TOPK) for b in range(PEER_TOPK // (a + 1))]
        rows += [jnp.full_like(rows[0], -jnp.inf)] * (-len(rows) % 8)
        cands.append(jnp.concatenate(rows, axis=0))
    bests, _ = _top_values(cands, PEER_TOPK)
    dense_b = 4
    for lanes, s1b, s2b, v1, v2, rank2, best in zip(blocks, s1s, s2s, v1s, v2s, rank2s, bests):
        zsum = jnp.ones_like(best[0])
        for r in range(1, PEER_TOPK):
            zsum = zsum + jnp.exp(best[r] - best[0])
        theta = best[PEER_TOPK - 1]
        nb = jnp.zeros_like(s1b)
        for b in range(dense_b):
            nb = nb + jnp.where(s1b + v2[b] >= theta, 1.0, 0.0)
        for a in range(PEER_TOPK // (dense_b + 1)):
            extra = jnp.zeros_like(theta)
            for b in range(dense_b, PEER_TOPK // (a + 1)):
                extra = extra + jnp.where(v1[a] + v2[b] >= theta, 1.0, 0.0)
            nb = nb + jnp.where(s1b == v1[a], extra, 0.0)
        nb_ref[:, lanes] = nb
        rk_ref[:, lanes] = pltpu.bitcast(rank2.astype(BF16), jnp.uint32)
        e1_ref[:, lanes] = jnp.exp(s1b - v1[0]) / zsum
        e2_ref[:, lanes] = pltpu.bitcast(jnp.exp(s2b - v2[0]).astype(BF16), jnp.uint32)


def peer_topk(q, keys, tt=256):
    n = q.shape[0]
    nk = PEER_NKEYS
    spec = pl.BlockSpec((None, nk, tt), lambda i, h: (h, 0, i))
    pspec = pl.BlockSpec((None, nk // 2, tt), lambda i, h: (h, 0, i))
    full = jax.ShapeDtypeStruct((PEER_HEADS, nk, n), F32)
    packed = jax.ShapeDtypeStruct((PEER_HEADS, nk // 2, n), jnp.uint32)
    return pl.pallas_call(
        _peer_topk_kernel,
        grid=(n // tt, PEER_HEADS),
        in_specs=[pl.BlockSpec((tt, 2 * nk), lambda i, h: (i, h)),
                  pl.BlockSpec((None, 2, nk, nk), lambda i, h: (h, 0, 0, 0))],
        out_specs=[spec, pspec, spec, pspec],
        out_shape=[full, packed, full, packed],
        compiler_params=_cp(("parallel", "parallel")),
        name="peer_topk",
    )(q, keys)


def _row_bcast(row, n):
    t = jnp.broadcast_to(row, (16, 128)).astype(BF16)
    return jnp.broadcast_to(t[None], (n // 16, 16, 128)).reshape(n, 128)


def _gelu_tanh(x):
    return 0.5 * x * (1.0 + jnp.tanh(math.sqrt(2.0 / math.pi) * (x + 0.044715 * (x * x * x))))


def _peer_expert_kernel(ht_ref, u_ref, vt_ref, nb_ref, rk_ref, e1_ref, e2_ref, x_ref, gt_ref,
                        o_ref, acc_ref, w_ref, *, n_i1, tt):
    j = pl.program_id(1)
    nk = PEER_NKEYS

    @pl.when(j == 0)
    def _():
        acc_ref[...] = jnp.zeros_like(acc_ref)

    zero = jnp.zeros((), BF16)
    halves = 2
    rows_per_half = n_i1 // halves
    for half in range(halves):
        rows_h = slice(half * rows_per_half * nk, (half + 1) * rows_per_half * nk)
        act = _dot(u_ref[rows_h, :], ht_ref[...])
        for r in range(half * rows_per_half, (half + 1) * rows_per_half):
            for ts in range(tt // 128):
                lanes = slice(ts * 128, (ts + 1) * 128)
                w = None
                for h in range(PEER_HEADS):
                    nbr = _row_bcast(nb_ref[h, r:r + 1, lanes], nk)
                    e1r = _row_bcast(e1_ref[h, r:r + 1, lanes], nk)
                    rk = pltpu.bitcast(rk_ref[h, :, lanes], BF16)
                    e2 = pltpu.bitcast(e2_ref[h, :, lanes], BF16)
                    term = jnp.where(rk < nbr, e2 * e1r, zero)
                    w = term if w is None else w + term
                w_ref[r * nk:(r + 1) * nk, lanes] = w
        w_ref[rows_h, :] = w_ref[rows_h, :] * _gelu_tanh(act).astype(BF16)
    acc_ref[...] += _dot(vt_ref[...], w_ref[...])

    @pl.when(j == pl.num_programs(1) - 1)
    def _():
        o_ref[...] = x_ref[...] + gt_ref[...] * acc_ref[...].T


def _xpose_cast_kernel(x_ref, o_ref):
    o_ref[...] = x_ref[...].T.astype(BF16)


def transpose_cast_chunks(tab, ec):
    depth, e, d = tab.shape
    return pl.pallas_call(
        _xpose_cast_kernel,
        grid=(depth, e // ec),
        in_specs=[pl.BlockSpec((None, ec, d), lambda l, c: (l, c, 0))],
        out_specs=pl.BlockSpec((None, None, d, ec), lambda l, c: (l, c, 0, 0)),
        out_shape=jax.ShapeDtypeStruct((depth, e // ec, d, ec), BF16),
        compiler_params=_cp(("parallel", "parallel")),
        name="transpose_cast_chunks",
    )(tab)


def peer_experts(h2t, u_tab, vt_tab, layer, nb, rk, e1, e2, x, mod3, gt_chunk, tt=512):
    ec = PEER_CHUNK
    n = x.shape[0]
    nk = PEER_NKEYS
    n_i1 = ec // nk
    once = pl.Buffered(1)
    sspec = pl.BlockSpec((PEER_HEADS, nk // 2, tt), lambda i, j: (0, 0, i), pipeline_mode=once)
    rspec = pl.BlockSpec((PEER_HEADS, n_i1, tt), lambda i, j: (0, j, i))
    return pl.pallas_call(
        functools.partial(_peer_expert_kernel, n_i1=n_i1, tt=tt),
        grid=(n // tt, PEER_EXPERTS // ec),
        in_specs=[pl.BlockSpec((D_MODEL, tt), lambda i, j: (0, i)),
                  pl.BlockSpec((None, ec, D_MODEL), lambda i, j: (layer, j, 0)),
                  pl.BlockSpec((None, None, D_MODEL, ec), lambda i, j: (layer, j, 0, 0)),
                  rspec, sspec, rspec, sspec,
                  pl.BlockSpec((tt, D_MODEL), lambda i, j: (i, 0), pipeline_mode=once),
                  pl.BlockSpec((None, 1, D_MODEL), lambda i, j: (_mod_row(i, tt), 0, gt_chunk))],
        out_specs=pl.BlockSpec((tt, D_MODEL), lambda i, j: (i, 0)),
        out_shape=jax.ShapeDtypeStruct((n, D_MODEL), F32),
        scratch_shapes=[pltpu.VMEM((D_MODEL, tt), F32), pltpu.VMEM((ec, tt), BF16)],
        compiler_params=_cp(("parallel", "arbitrary")),
        name="peer_experts",
    )(h2t, u_tab, vt_tab, nb, rk, e1, e2, x, mod3)


def _reorder_in_proj(w_in, b_in):
    splits = (1024, 1024, 1024, 512, 512, 1024, 1024, 16, 16, 1024, 1024, 1024, 1024, 8, 8, 2048, 2048, 2048)
    offs = np.concatenate([[0], np.cumsum(splits)])
    seg = lambda a, i: a[..., offs[i]:offs[i + 1]]
    main_ids = (0, 1, 2, 3, 4, 5, 6, 9, 10, 11, 12, 15, 16, 17)
    small_ids = (7, 8, 13, 14)
    w_main = jnp.concatenate([seg(w_in, i).astype(BF16) for i in main_ids], axis=-1)
    b_main = jnp.concatenate([seg(b_in, i) for i in main_ids], axis=-1)
    w_small = jnp.concatenate([seg(w_in, i) for i in small_ids], axis=-1)
    b_small = jnp.concatenate([seg(b_in, i) for i in small_ids], axis=-1)
    pad = N_SMALL - w_small.shape[-1]
    w_small = jnp.pad(w_small, ((0, 0), (0, 0), (0, pad)))
    b_small = jnp.pad(b_small, ((0, 0), (0, pad)))
    return w_main, b_main, w_small, b_small


def kernel(x_prompt, x_sample, c, state_gla, state_mlstm_C, state_mlstm_n, state_mlstm_m, c_ctx,
           mod_w, mod_b, norm1_g, norm2_g, final_g, w_in, b_in, hy_conv, hy_w1, hy_b1, hy_w2, hy_b2,
           hy_w3, hy_freq, hy_bias, gla_a2_w, gla_a2_b, gla_norm_g, ml_conv, ml_gate_b, ml_norm_g,
           w_branch, w_out, peer_wq, peer_keys, peer_u, peer_v):
    w_main, b_main, w_small, b_small = _reorder_in_proj(w_in, b_in)
    w_small_hi = w_small.astype(BF16)
    w_small_lo = (w_small - w_small_hi.astype(F32)).astype(BF16)
    w_branch_b = w_branch.astype(BF16)
    w_out_b = w_out.astype(BF16)
    wq_hi = peer_wq.astype(BF16)
    wq_lo = (peer_wq - wq_hi.astype(F32)).astype(BF16)
    u_b = peer_u.astype(BF16)
    vt_b = transpose_cast_chunks(peer_v, PEER_CHUNK)
    zero_b = jnp.zeros((1, D_MODEL), F32)
    a2f = jnp.pad(gla_a2_w[:, 0], ((0, 0), (0, N_SMALL - GLA_RANK), (0, 0)))
    a2b = jnp.pad(gla_a2_w[:, 1], ((0, 0), (GLA_RANK, N_SMALL - 2 * GLA_RANK), (0, 0)))
    a2bias = gla_a2_b.reshape(DEPTH, 2, 1, GLA_HEADS * GLA_DK)
    n0_all = state_mlstm_n.reshape(DEC_BATCH, DEPTH, 2, ML_HEADS, 1, ML_DH)

    tabs = {}
    for length in (SEQ, DEC_SEQ):
        fre, fim, g_mat, sgn = _dft_tables(length)
        fre_h, fre_l = _split2(fre)
        fim_h, fim_l = _split2(fim)
        tabs[length] = dict(spec=(fre_h, fre_l, fim_h, fim_l, sgn),
                            f=jnp.concatenate([fre_h, fim_h], axis=0), g=g_mat.astype(BF16))

    cvec = jnp.zeros((MOD_ROWS, D_MODEL), F32).at[:DEC_BATCH].set(c).at[CTX_ROW].set(c_ctx)
    mod_all = mod_table(cvec, mod_w, mod_b)

    x = jnp.concatenate([x_prompt.reshape(N_PROMPT, D_MODEL), x_sample.reshape(N_SAMPLE, D_MODEL)], axis=0)
    groups = ((SEQ, BATCH, 0, SEQ), (DEC_SEQ, DEC_BATCH, N_PROMPT // DEC_SEQ, GRID_W))
    new_gla, new_c, new_n, new_m = [], [], [], []
    for l in range(DEPTH):
        mod3 = mod_all[l].reshape(MOD_ROWS, 1, 6 * D_MODEL)
        h_hi, zsmall = normmod_gates(x, norm1_g[l], mod3, 0, 1, w_small_hi, w_small_lo, l,
                                     b_small[l].reshape(1, N_SMALL))
        zmain = mm_bias(h_hi, w_main, l, b_main[l].reshape(1, N_MAIN), tm=1024, tn=2048)
        mi = zsmall[:, 32:40].reshape(N_ROWS, 2, ML_HEADS)
        mf = zsmall[:, 40:48].reshape(N_ROWS, 2, ML_HEADS)
        gcol = jnp.concatenate([mi, mf, jnp.zeros((N_ROWS, 4, ML_HEADS), F32)], axis=1)
        gates_col = jnp.transpose(gcol, (2, 0, 1))
        gates_row = jnp.transpose(gcol.reshape(N_ROWS // CHUNK, CHUNK, 8, ML_HEADS), (3, 0, 2, 1))
        gate_b = ml_gate_b[l].reshape(-1)
        y_hy, y_gla, y_ml = [], [], []
        for gi, (length, n_seq, blk0, seg) in enumerate(groups):
            t = tabs[length]
            sre, sim = hyena_spectrum(length, hy_w1[l], hy_b1[l], hy_w2[l], hy_b2[l], hy_w3[l],
                                      hy_freq[l], t["spec"])
            y_hy.append(hyena(zmain, hy_conv[l], sre, sim, hy_bias[l], t["f"], t["g"],
                              length, n_seq, blk0, seg))
            gla_args = (zmain, zsmall, a2f[l], a2b[l], a2bias[l], gla_norm_g[l].reshape(1, GLA_DV),
                        length, n_seq, blk0)
            ml_args = (zmain, gates_col, gates_row, gate_b, ml_conv[l], ml_norm_g[l].reshape(1, ML_DH),
                       length, n_seq, blk0, seg)
            if gi == 0:
                yg, s_fin = gla(*gla_args, None, l, n_sub=PROMPT_SEQS_PER_STEP)
                ym, c_fin, n_fin, m_fin = mlstm(*ml_args, None, l, n_sub=PROMPT_SEQS_PER_STEP)
                new_gla.append(s_fin)
                new_c.append(c_fin)
                new_n.append(n_fin[:, :, :, 0, :])
                new_m.append(m_fin[:, :, :, 0, 0])
            else:
                yg = gla(*gla_args, state_gla, l)
                ym = mlstm(*ml_args, (state_mlstm_C, n0_all, state_mlstm_m[:, l].reshape(-1)), l)
            y_gla.append(yg)
            y_ml.append(ym)
        merged = merge_branches(y_hy, y_gla, y_ml, w_branch_b, l, zmain)
        x = mm_residual(merged, w_out_b, l, x, mod3, 2)
        h2, h2_lo, h2t = normmod_peer(x, norm2_g[l], mod3, 3, 4)
        q = mm3_bias(h2, h2_lo, wq_hi, wq_lo, l, zero_b)
        nb, rk, e1, e2 = peer_topk(q, peer_keys[l])
        x = peer_experts(h2t, u_b, vt_b, l, nb, rk, e1, e2, x, mod3, 5)

    y_prompt = final_norm(x, final_g, 0, N_PROMPT).reshape(BATCH, SEQ, D_MODEL)
    y_sample = final_norm(x, final_g, N_PROMPT, N_SAMPLE).reshape(DEC_BATCH, DEC_SEQ, D_MODEL)
    return (y_prompt, y_sample, jnp.stack(new_gla, axis=1), jnp.stack(new_c, axis=1),
            jnp.stack(new_n, axis=1), jnp.stack(new_m, axis=1))
```

```python
import functools
import math

import jax
import jax.numpy as jnp
import numpy as np
from jax import lax
from jax.experimental import pallas as pl
from jax.experimental.pallas import tpu as pltpu

F32 = jnp.float32
BF16 = jnp.bfloat16

D_MODEL = 2048
BATCH = 16
SEQ = 256
DEPTH = 4
DEC_BATCH = 4
DEC_SEQ = 1024
GRID_W = 64
EPS = 1e-6
CHUNK = 64
HY_WIDTH = 1024
HY_EMB = 33
HY_BANDS = (HY_EMB - 1) // 2
HY_FFN = 64
HY_DECAY_TARGET = 1e-2
HY_FAST_PCT = 0.3
HY_SLOW_PCT = 1.5
GLA_HEADS = 4
GLA_DK = 128
GLA_DV = 256
GLA_RANK = 16
GLA_NORMALIZER = 16.0
ML_HEADS = 4
ML_DH = 256
PEER_HEADS = 8
PEER_NKEYS = 128
PEER_EXPERTS = PEER_NKEYS * PEER_NKEYS
PEER_TOPK = 16
PEER_CHUNK = 1024
PROMPT_SEQS_PER_STEP = 2
HY_CHAIN_W = 256

N_PROMPT = BATCH * SEQ
N_SAMPLE = DEC_BATCH * DEC_SEQ
N_ROWS = N_PROMPT + N_SAMPLE
CTX_ROW = DEC_BATCH
MOD_ROWS = 8

C_HX1, C_HX2, C_HV = 0, 1024, 2048
C_GQ, C_GK, C_GV, C_GR = 3072, 3584, 4096, 5120
C_MQ, C_MK, C_MV, C_MO = 6144, 7168, 8192, 9216
C_GA, C_GB, C_GC = 10240, 12288, 14336
N_MAIN = 16384
N_SMALL = 128

VMEM_LIMIT = 56 * 1024 * 1024


def _cp(sem):
    return pltpu.CompilerParams(dimension_semantics=sem, vmem_limit_bytes=VMEM_LIMIT)


def _dot(a, b):
    return jnp.dot(a, b, preferred_element_type=F32)


def _dot_nt(a, b):
    return lax.dot_general(a, b, (((1,), (1,)), ((), ())), preferred_element_type=F32)


def _dot_tn(a, b):
    return lax.dot_general(a, b, (((0,), (0,)), ((), ())), preferred_element_type=F32)


def _split2(a):
    hi = a.astype(BF16)
    lo = (a - hi.astype(F32)).astype(BF16)
    return hi, lo


def _split3(a):
    a1 = a.astype(BF16)
    r1 = a - a1.astype(F32)
    a2 = r1.astype(BF16)
    a3 = (r1 - a2.astype(F32)).astype(BF16)
    return a1, a2, a3


def _dot3(a, b, dot=_dot):
    ah, al = _split2(a)
    bh, bl = _split2(b)
    return dot(ah, bh) + (dot(ah, bl) + dot(al, bh))


def _log_sigmoid(x):
    return jnp.minimum(x, 0.0) - jnp.log(1.0 + jnp.exp(-jnp.abs(x)))


def _mod_row(i, tm):
    n_p = N_PROMPT // tm
    return jnp.where(i < n_p, CTX_ROW, (i - n_p) // (DEC_SEQ // tm))


def _mod_kernel(c_ref, w_ref, b_ref, o_ref):
    a = c_ref[...]
    a = a * jax.nn.sigmoid(a)
    o_ref[0] = _dot3(a, w_ref[0]) + b_ref[0]


def mod_table(cvec, mod_w, mod_b):
    tn = 1024
    n = mod_w.shape[-1]
    return pl.pallas_call(
        _mod_kernel,
        grid=(DEPTH, n // tn),
        in_specs=[pl.BlockSpec((MOD_ROWS, D_MODEL), lambda l, j: (0, 0)),
                  pl.BlockSpec((1, D_MODEL, tn), lambda l, j: (l, 0, j)),
                  pl.BlockSpec((1, 1, tn), lambda l, j: (l, 0, j))],
        out_specs=pl.BlockSpec((1, MOD_ROWS, tn), lambda l, j: (l, 0, j)),
        out_shape=jax.ShapeDtypeStruct((DEPTH, MOD_ROWS, n), F32),
        compiler_params=_cp(("parallel", "parallel")),
        name="mod_table",
    )(cvec, mod_w, mod_b.reshape(DEPTH, 1, n))


def _norm_modulate(x_ref, g_ref, sc_ref, sh_ref):
    x = x_ref[...]
    y = x * lax.rsqrt(jnp.mean(x * x, axis=-1, keepdims=True) + EPS) * g_ref[...]
    return y * (1.0 + sc_ref[...]) + sh_ref[...]


def _normmod_gates_kernel(x_ref, g_ref, sc_ref, sh_ref, wh_ref, wl_ref, b_ref, hi_ref, zs_ref):
    h = _norm_modulate(x_ref, g_ref, sc_ref, sh_ref)
    hi = h.astype(BF16)
    lo = (h - hi.astype(F32)).astype(BF16)
    hi_ref[...] = hi
    zs_ref[...] = _dot(hi, wh_ref[...]) + (_dot(hi, wl_ref[...]) + _dot(lo, wh_ref[...])) + b_ref[...]


def _normmod_peer_kernel(x_ref, g_ref, sc_ref, sh_ref, hi_ref, lo_ref, t_ref):
    h = _norm_modulate(x_ref, g_ref, sc_ref, sh_ref)
    hi = h.astype(BF16)
    hi_ref[...] = hi
    lo_ref[...] = (h - hi.astype(F32)).astype(BF16)
    t_ref[...] = h.T.astype(BF16)


def _normmod_specs(tm, sh_chunk, sc_chunk):
    return [pl.BlockSpec((tm, D_MODEL), lambda i: (i, 0)),
            pl.BlockSpec((1, D_MODEL), lambda i: (0, 0)),
            pl.BlockSpec((None, 1, D_MODEL), lambda i: (_mod_row(i, tm), 0, sc_chunk)),
            pl.BlockSpec((None, 1, D_MODEL), lambda i: (_mod_row(i, tm), 0, sh_chunk))]


def normmod_gates(x, g, mod3, sh_chunk, sc_chunk, w_hi, w_lo, layer, b):
    tm = 256
    n = w_hi.shape[-1]
    row_spec = pl.BlockSpec((tm, D_MODEL), lambda i: (i, 0))
    w_spec = pl.BlockSpec((None, D_MODEL, n), lambda i: (layer, 0, 0))
    return pl.pallas_call(
        _normmod_gates_kernel,
        grid=(N_ROWS // tm,),
        in_specs=_normmod_specs(tm, sh_chunk, sc_chunk) + [w_spec, w_spec, pl.BlockSpec((1, n), lambda i: (0, 0))],
        out_specs=[row_spec, pl.BlockSpec((tm, n), lambda i: (i, 0))],
        out_shape=[jax.ShapeDtypeStruct((N_ROWS, D_MODEL), BF16), jax.ShapeDtypeStruct((N_ROWS, n), F32)],
        compiler_params=_cp(("parallel",)),
        name="normmod_gates",
    )(x, g.reshape(1, D_MODEL), mod3, mod3, w_hi, w_lo, b)


def normmod_peer(x, g, mod3, sh_chunk, sc_chunk):
    tm = 256
    row_spec = pl.BlockSpec((tm, D_MODEL), lambda i: (i, 0))
    row_shape = jax.ShapeDtypeStruct((N_ROWS, D_MODEL), BF16)
    return pl.pallas_call(
        _normmod_peer_kernel,
        grid=(N_ROWS // tm,),
        in_specs=_normmod_specs(tm, sh_chunk, sc_chunk),
        out_specs=[row_spec, row_spec, pl.BlockSpec((D_MODEL, tm), lambda i: (0, i))],
        out_shape=[row_shape, row_shape, jax.ShapeDtypeStruct((D_MODEL, N_ROWS), BF16)],
        compiler_params=_cp(("parallel",)),
        name="normmod_peer",
    )(x, g.reshape(1, D_MODEL), mod3, mod3)


def _final_norm_kernel(x_ref, g_ref, o_ref):
    x = x_ref[...]
    o_ref[...] = x * lax.rsqrt(jnp.mean(x * x, axis=-1, keepdims=True) + EPS) * g_ref[...]


def final_norm(x, g, row0, n_rows):
    tm = 256
    return pl.pallas_call(
        _final_norm_kernel,
        grid=(n_rows // tm,),
        in_specs=[pl.BlockSpec((tm, D_MODEL), lambda i: (row0 // tm + i, 0)),
                  pl.BlockSpec((1, D_MODEL), lambda i: (0, 0))],
        out_specs=pl.BlockSpec((tm, D_MODEL), lambda i: (i, 0)),
        out_shape=jax.ShapeDtypeStruct((n_rows, D_MODEL), F32),
        compiler_params=_cp(("parallel",)),
        name="final_norm",
    )(x, g.reshape(1, D_MODEL))


def _mm_bias_kernel(x_ref, w_ref, b_ref, o_ref):
    o_ref[...] = (_dot(x_ref[...], w_ref[...]) + b_ref[...]).astype(o_ref.dtype)


def mm_bias(x, w, layer, b, out_dtype=F32, tm=512, tn=1024):
    m, k = x.shape
    n = w.shape[-1]
    return pl.pallas_call(
        _mm_bias_kernel,
        grid=(n // tn, m // tm),
        in_specs=[pl.BlockSpec((tm, k), lambda j, i: (i, 0)),
                  pl.BlockSpec((None, k, tn), lambda j, i: (layer, 0, j)),
                  pl.BlockSpec((1, tn), lambda j, i: (0, j))],
        out_specs=pl.BlockSpec((tm, tn), lambda j, i: (i, j)),
        out_shape=jax.ShapeDtypeStruct((m, n), out_dtype),
        compiler_params=_cp(("parallel", "parallel")),
        name="mm_bias",
    )(x, w, b)


def _mm3_bias_kernel(xh_ref, xl_ref, wh_ref, wl_ref, b_ref, o_ref):
    xh = xh_ref[...]
    acc = _dot(xh, wh_ref[...]) + (_dot(xh, wl_ref[...]) + _dot(xl_ref[...], wh_ref[...]))
    o_ref[...] = acc + b_ref[...]


def mm3_bias(xh, xl, wh, wl, layer, b, tm=512, tn=1024):
    m, k = xh.shape
    n = wh.shape[-1]
    tn = min(tn, n)
    return pl.pallas_call(
        _mm3_bias_kernel,
        grid=(n // tn, m // tm),
        in_specs=[pl.BlockSpec((tm, k), lambda j, i: (i, 0)),
                  pl.BlockSpec((tm, k), lambda j, i: (i, 0)),
                  pl.BlockSpec((None, k, tn), lambda j, i: (layer, 0, j)),
                  pl.BlockSpec((None, k, tn), lambda j, i: (layer, 0, j)),
                  pl.BlockSpec((1, tn), lambda j, i: (0, j))],
        out_specs=pl.BlockSpec((tm, tn), lambda j, i: (i, j)),
        out_shape=jax.ShapeDtypeStruct((m, n), F32),
        compiler_params=_cp(("parallel", "parallel")),
        name="mm3_bias",
    )(xh, xl, wh, wl, b)


def _merge_kernel(yhp_ref, yhs_ref, ygp_ref, ygs_ref, ymp_ref, yms_ref, w_ref, ga_ref, gb_ref, gc_ref, o_ref,
                  *, n_prompt_tiles):
    is_prompt = pl.program_id(1) < n_prompt_tiles
    pick = lambda p_ref, s_ref: jnp.where(is_prompt, p_ref[...], s_ref[...])
    acc = jax.nn.sigmoid(ga_ref[...]) * _dot(pick(yhp_ref, yhs_ref), w_ref[0])
    acc += jax.nn.sigmoid(gb_ref[...]) * _dot(pick(ygp_ref, ygs_ref), w_ref[1])
    acc += jax.nn.sigmoid(gc_ref[...]) * _dot(pick(ymp_ref, yms_ref), w_ref[2])
    o_ref[...] = acc.astype(o_ref.dtype)


def merge_branches(y_hy, y_gla, y_ml, w_branch, layer, zmain, tm=512, tn=1024):
    kb = HY_WIDTH
    n_p = N_PROMPT // tm
    p_spec = pl.BlockSpec((tm, kb), lambda j, i: (jnp.minimum(i, n_p - 1), 0))
    s_spec = pl.BlockSpec((tm, kb), lambda j, i: (jnp.maximum(i - n_p, 0), 0))

    def gate_spec(col):
        return pl.BlockSpec((tm, tn), lambda j, i, c=col // tn: (i, c + j))

    return pl.pallas_call(
        functools.partial(_merge_kernel, n_prompt_tiles=n_p),
        grid=(D_MODEL // tn, N_ROWS // tm),
        in_specs=[p_spec, s_spec, p_spec, s_spec, p_spec, s_spec,
                  pl.BlockSpec((None, 3, kb, tn), lambda j, i: (layer, 0, 0, j)),
                  gate_spec(C_GA), gate_spec(C_GB), gate_spec(C_GC)],
        out_specs=pl.BlockSpec((tm, tn), lambda j, i: (i, j)),
        out_shape=jax.ShapeDtypeStruct((N_ROWS, D_MODEL), BF16),
        compiler_params=_cp(("parallel", "parallel")),
        name="merge_branches",
    )(*y_hy, *y_gla, *y_ml, w_branch, zmain, zmain, zmain)


def _mm_resid_kernel(m_ref, w_ref, x_ref, gt_ref, o_ref):
    o_ref[...] = x_ref[...] + gt_ref[...] * _dot(m_ref[...], w_ref[...])


def mm_residual(merged, w, layer, x, mod3, gt_chunk, tm=512, tn=1024):
    k = merged.shape[1]
    return pl.pallas_call(
        _mm_resid_kernel,
        grid=(D_MODEL // tn, N_ROWS // tm),
        in_specs=[pl.BlockSpec((tm, k), lambda j, i: (i, 0)),
                  pl.BlockSpec((None, k, tn), lambda j, i: (layer, 0, j)),
                  pl.BlockSpec((tm, tn), lambda j, i: (i, j)),
                  pl.BlockSpec((None, 1, tn),
                               lambda j, i: (_mod_row(i, tm), 0, gt_chunk * (D_MODEL // tn) + j))],
        out_specs=pl.BlockSpec((tm, tn), lambda j, i: (i, j)),
        out_shape=jax.ShapeDtypeStruct((N_ROWS, D_MODEL), F32),
        compiler_params=_cp(("parallel", "parallel")),
        name="mm_residual",
    )(merged, w, x, mod3)


def _dft_tables(length):
    k = jnp.arange(length, dtype=jnp.int32)
    m = (k[:, None] * k[None, :]) % (2 * length)
    ang = m.astype(F32) * (math.pi / length)
    cos = jnp.cos(ang)
    sin = jnp.sin(ang)
    sgn = jnp.where(k % 2 == 0, 1.0, -1.0).astype(F32)
    fre = cos
    fim = jnp.where(k[:, None] == 0, sgn[None, :], -sin)
    wk = jnp.where(k == 0, 1.0, 2.0).astype(F32) / (2.0 * length)
    g_re = cos.T * wk[None, :]
    g_im = jnp.where(k[None, :] == 0, sgn[:, None] / (2.0 * length), -sin.T / length)
    return fre, fim, jnp.concatenate([g_re, g_im], axis=1), sgn


def _hyfilt_kernel(z_ref, w1_ref, b1_ref, w2_ref, b2_ref, fq_ref, w3f_ref, w3b_ref, tn_ref, dl_ref,
                   sgn_ref, freh_ref, frel_ref, fimh_ref, fiml_ref, sre_ref, sim_ref):
    hid = jnp.sin(fq_ref[0:1] * (_dot3(z_ref[...], w1_ref[...]) + b1_ref[...]))
    hid = jnp.sin(fq_ref[1:2] * (_dot3(hid, w2_ref[...]) + b2_ref[...]))
    decay = jnp.exp(-tn_ref[...] * dl_ref[...])
    fwd = _dot3(hid, w3f_ref[...]) * decay
    bwd = _dot3(hid, w3b_ref[...]) * decay
    row = lax.broadcasted_iota(jnp.int32, fwd.shape, 0)
    bwd = jnp.where(row == 0, 0.0, bwd)
    a = fwd + bwd
    d = fwd - bwd
    ah, al = _split2(a)
    dh, dl2 = _split2(d)
    re = _dot(freh_ref[...], ah) + (_dot(freh_ref[...], al) + _dot(frel_ref[...], ah))
    im = _dot(fimh_ref[...], dh) + (_dot(fimh_ref[...], dl2) + _dot(fiml_ref[...], dh))
    nyq = jnp.sum(sgn_ref[...] * a, axis=0, keepdims=True)
    sre_ref[0] = re
    sim_ref[0] = jnp.where(row == 0, nyq, im)


def hyena_spectrum(length, w1, b1, w2, b2, w3, freq, tabs):
    fre_h, fre_l, fim_h, fim_l, sgn = tabs
    t = jnp.arange(length, dtype=F32)
    t_norm = t / (length - 1)
    bands = jnp.linspace(1e-4, HY_BANDS - 1, HY_BANDS, dtype=F32)
    ang = (2.0 * math.pi / length) * t[:, None] * bands[None, :]
    z = jnp.concatenate([t_norm[:, None], jnp.cos(ang), -jnp.sin(ang),
                         jnp.zeros((length, HY_FFN - HY_EMB), F32)], axis=-1)
    w1p = jnp.pad(w1, ((0, HY_FFN - HY_EMB), (0, 0)))
    max_decay = math.log(HY_DECAY_TARGET) / HY_FAST_PCT
    min_decay = math.log(HY_DECAY_TARGET) / HY_SLOW_PCT
    deltas = jnp.abs(jnp.linspace(min_decay, max_decay, HY_WIDTH, dtype=F32)).reshape(1, HY_WIDTH)
    ct = 256
    nct = HY_WIDTH // ct
    full = lambda shape: pl.BlockSpec(shape, lambda o, j: (0,) * len(shape))
    out_spec = pl.BlockSpec((1, length, ct), lambda o, j: (o, 0, j))
    return pl.pallas_call(
        _hyfilt_kernel,
        grid=(2, nct),
        in_specs=[full((length, HY_FFN)), full((HY_FFN, HY_FFN)), full((1, HY_FFN)),
                  full((HY_FFN, HY_FFN)), full((1, HY_FFN)), full((2, HY_FFN)),
                  pl.BlockSpec((HY_FFN, ct), lambda o, j: (0, o * 2 * nct + j)),
                  pl.BlockSpec((HY_FFN, ct), lambda o, j: (0, o * 2 * nct + nct + j)),
                  full((length, 1)),
                  pl.BlockSpec((1, ct), lambda o, j: (0, j)),
                  full((length, 1)),
                  full((length, length)), full((length, length)),
                  full((length, length)), full((length, length))],
        out_specs=[out_spec, out_spec],
        out_shape=[jax.ShapeDtypeStruct((2, length, HY_WIDTH), F32)] * 2,
        compiler_params=_cp(("parallel", "parallel")),
        name="hyena_spectrum",
    )(z, w1p, b1.reshape(1, HY_FFN), w2, b2.reshape(1, HY_FFN), freq, w3, w3,
      t_norm.reshape(length, 1), deltas, sgn.reshape(length, 1), fre_h, fre_l, fim_h, fim_l)


def _short_conv(x, w, pos, seg, length):
    prev = jnp.where(pos == 0, 0.0, pltpu.roll(x, 1, 0))
    nxt = jnp.where(pos == seg - 1, 0.0, pltpu.roll(x, length - 1, 0))
    return prev * w[0:1] + x * w[1:2] + nxt * w[2:3]


def _hyena_kernel(x1_ref, x2_ref, v_ref, cw_ref, sre_ref, sim_ref, hb_ref, f_ref, g_ref, o_ref,
                  *, length, seg):
    row = lax.broadcasted_iota(jnp.int32, (length, 1), 0)
    pos = row % seg
    row0 = row == 0
    groups = [slice(c * HY_CHAIN_W, (c + 1) * HY_CHAIN_W) for c in range(x1_ref.shape[1] // HY_CHAIN_W)]

    def long_convs(us, o):
        specs = [_dot(f_ref[...], u.astype(BF16)) for u in us]
        ys = []
        for spec, cols in zip(specs, groups):
            ur, ui = spec[:length], spec[length:]
            hre, him = sre_ref[o, :, cols], sim_ref[o, :, cols]
            uihi = ui * him
            yr = ur * hre - jnp.where(row0, 0.0, uihi)
            yi = jnp.where(row0, uihi, ur * him + ui * hre)
            ys.append(jnp.concatenate([yr, yi], axis=0).astype(BF16))
        outs = [_dot(g_ref[...], y) for y in ys]
        return [y + u * hb_ref[o:o + 1, cols] for y, u, cols in zip(outs, us, groups)]

    conv = lambda ref, k, cols: _short_conv(ref[:, cols], cw_ref[k, :, cols], pos, seg, length)
    vs = [conv(v_ref, 2, cols) for cols in groups]
    c1 = long_convs(vs, 0)
    zs = [conv(x1_ref, 0, cols) * c for cols, c in zip(groups, c1)]
    c2 = long_convs(zs, 1)
    for cols, c in zip(groups, c2):
        o_ref[:, cols] = (conv(x2_ref, 1, cols) * c).astype(o_ref.dtype)


def hyena(zmain, conv_w, spec_re, spec_im, bias, f_mat, g_mat, length, n_seq, row_blk0, seg):
    ct = 2 * HY_CHAIN_W
    nct = HY_WIDTH // ct
    once = pl.Buffered(1)

    def zspec(col):
        return pl.BlockSpec((length, ct), lambda j, b, c=col // ct: (row_blk0 + b, c + j))

    return pl.pallas_call(
        functools.partial(_hyena_kernel, length=length, seg=seg),
        grid=(nct, n_seq),
        in_specs=[zspec(C_HX1), zspec(C_HX2), zspec(C_HV),
                  pl.BlockSpec((3, 3, ct), lambda j, b: (0, 0, j)),
                  pl.BlockSpec((2, length, ct), lambda j, b: (0, 0, j), pipeline_mode=once),
                  pl.BlockSpec((2, length, ct), lambda j, b: (0, 0, j), pipeline_mode=once),
                  pl.BlockSpec((2, ct), lambda j, b: (0, j)),
                  pl.BlockSpec((2 * length, length), lambda j, b: (0, 0), pipeline_mode=once),
                  pl.BlockSpec((length, 2 * length), lambda j, b: (0, 0), pipeline_mode=once)],
        out_specs=pl.BlockSpec((length, ct), lambda j, b: (b, j)),
        out_shape=jax.ShapeDtypeStruct((n_seq * length, HY_WIDTH), BF16),
        compiler_params=_cp(("parallel", "parallel")),
        name="hyena",
    )(zmain, zmain, zmain, conv_w, spec_re, spec_im, bias, f_mat, g_mat)


def _tri_masks():
    t = lax.broadcasted_iota(jnp.int32, (CHUNK, CHUNK), 0)
    s = lax.broadcasted_iota(jnp.int32, (CHUNK, CHUNK), 1)
    return s <= t, s >= t


def _gla_chunks(qs, ks, vs, gs, states, masks, revs):
    idx = range(len(qs))
    ones = jnp.ones((CHUNK, GLA_DK), BF16)
    tms = [jnp.where(masks[i], 1.0, 0.0).astype(BF16) for i in idx]
    gsp = [_split3(gs[i]) for i in idx]
    bcs = [_dot(tms[i], gsp[i][0]) + (_dot(tms[i], gsp[i][1]) + _dot(tms[i], gsp[i][2])) for i in idx]
    tots = [_dot_tn(gsp[i][0], ones) + (_dot_tn(gsp[i][1], ones) + _dot_tn(gsp[i][2], ones)) for i in idx]
    b_ends = [bcs[i][0:1] if revs[i] else bcs[i][CHUNK - 1:CHUNK] for i in idx]
    refs = [bcs[i][CHUNK // 2:CHUNK // 2 + 1] for i in idx]
    inters = [_dot((qs[i] * jnp.exp(bcs[i])).astype(BF16), states[i].astype(BF16)) for i in idx]
    qhs = [(qs[i] * jnp.exp(bcs[i] - refs[i])).astype(BF16) for i in idx]
    khs = [(ks[i] * jnp.exp(refs[i] - bcs[i])).astype(BF16) for i in idx]
    atts = [jnp.where(masks[i], _dot_nt(qhs[i], khs[i]), 0.0).astype(BF16) for i in idx]
    vbs = [vs[i].astype(BF16) for i in idx]
    outs = [inters[i] + _dot(atts[i], vbs[i]) for i in idx]
    kds = [(ks[i] * jnp.exp(b_ends[i] - bcs[i])).astype(BF16) for i in idx]
    es = [jnp.exp(tots[i]) for i in idx]
    new_states = [jnp.concatenate([es[i], es[i]], axis=1) * states[i] + _dot_tn(kds[i], vbs[i]) for i in idx]
    return outs, new_states


def _gla_kernel(*refs, length, n_sub, has_init):
    if has_init:
        (q_ref, k_ref, v_ref, gr_ref, zs_ref, wf_ref, wb_ref, ab_ref, ng_ref, s0_ref,
         y_ref, o_ref, lg_ref, st_ref) = refs
    else:
        (q_ref, k_ref, v_ref, gr_ref, zs_ref, wf_ref, wb_ref, ab_ref, ng_ref,
         y_ref, sout_ref, o_ref, lg_ref, st_ref) = refs
    zs = zs_ref[...]
    lg_ref[0] = _log_sigmoid(_dot3(zs, wf_ref[...]) + ab_ref[0]) * (1.0 / GLA_NORMALIZER)
    lg_ref[1] = _log_sigmoid(_dot3(zs, wb_ref[...]) + ab_ref[1]) * (1.0 / GLA_NORMALIZER)
    if has_init:
        st_ref[...] = s0_ref[...]
    else:
        st_ref[...] = jnp.zeros_like(st_ref)
    o_ref[...] = jnp.zeros_like(o_ref)
    n = length // CHUNK
    mask_f, mask_b = _tri_masks()
    scale = GLA_DK ** -0.5

    def body(i, carry):
        chains = []
        for s in range(n_sub):
            for d, mask in enumerate((mask_f, mask_b)):
                c = i if d == 0 else n - 1 - i
                rows = pl.ds(pl.multiple_of(s * length + c * CHUNK, CHUNK), CHUNK)
                for h in range(GLA_HEADS):
                    chains.append((s, d, h, rows, mask))
        kcs = [slice(h * GLA_DK, (h + 1) * GLA_DK) for (_, _, h, _, _) in chains]
        vcs = [slice(h * GLA_DV, (h + 1) * GLA_DV) for (_, _, h, _, _) in chains]
        outs, new_states = _gla_chunks(
            [q_ref[ch[3], kc] * scale for ch, kc in zip(chains, kcs)],
            [k_ref[ch[3], kc] for ch, kc in zip(chains, kcs)],
            [v_ref[ch[3], vc] for ch, vc in zip(chains, vcs)],
            [lg_ref[ch[1], ch[3], kc] for ch, kc in zip(chains, kcs)],
            [st_ref[ch[0], ch[1], ch[2]] for ch in chains],
            [ch[4] for ch in chains], [ch[1] == 1 for ch in chains])
        for ch, vc, o, s_new in zip(chains, vcs, outs, new_states):
            o_ref[ch[3], vc] += o
            st_ref[ch[0], ch[1], ch[2]] = s_new
        return carry

    lax.fori_loop(0, n, body, 0)
    for h in range(GLA_HEADS):
        vc = slice(h * GLA_DV, (h + 1) * GLA_DV)
        o = o_ref[:, vc]
        o = o * lax.rsqrt(jnp.mean(o * o, axis=-1, keepdims=True) + EPS) * ng_ref[...]
        gr = gr_ref[:, vc]
        y_ref[:, vc] = (o * (gr * jax.nn.sigmoid(gr))).astype(y_ref.dtype)
    if not has_init:
        sout_ref[...] = st_ref[...]


def gla(zmain, zsmall, wf, wb, ab, norm_g, length, n_seq, row_blk0, state0, layer, n_sub=1):
    has_init = state0 is not None
    qk_w = GLA_HEADS * GLA_DK
    v_w = GLA_HEADS * GLA_DV
    once = pl.Buffered(1)
    rows = n_sub * length
    blk0 = row_blk0 // n_sub
    st_blk = (n_sub, 2, GLA_HEADS, GLA_DK, GLA_DV)

    def zspec(col, width):
        return pl.BlockSpec((rows, width), lambda b, c=col // width: (blk0 + b, c), pipeline_mode=once)

    in_specs = [zspec(C_GQ, qk_w), zspec(C_GK, qk_w), zspec(C_GV, v_w), zspec(C_GR, v_w),
                pl.BlockSpec((rows, N_SMALL), lambda b: (blk0 + b, 0)),
                pl.BlockSpec((N_SMALL, qk_w), lambda b: (0, 0)),
                pl.BlockSpec((N_SMALL, qk_w), lambda b: (0, 0)),
                pl.BlockSpec((2, 1, qk_w), lambda b: (0, 0, 0)),
                pl.BlockSpec((1, GLA_DV), lambda b: (0, 0))]
    args = [zmain, zmain, zmain, zmain, zsmall, wf, wb, ab, norm_g]
    y_shape = jax.ShapeDtypeStruct((n_seq * length, v_w), BF16)
    y_spec = pl.BlockSpec((rows, v_w), lambda b: (b, 0))
    if has_init:
        in_specs.append(pl.BlockSpec((n_sub, None, 2, GLA_HEADS, GLA_DK, GLA_DV),
                                     lambda b: (b, layer, 0, 0, 0, 0)))
        args.append(state0)
        out_shape, out_specs = y_shape, y_spec
    else:
        out_shape = [y_shape, jax.ShapeDtypeStruct((n_seq, 2, GLA_HEADS, GLA_DK, GLA_DV), F32)]
        out_specs = [y_spec, pl.BlockSpec(st_blk, lambda b: (b, 0, 0, 0, 0))]
    return pl.pallas_call(
        functools.partial(_gla_kernel, length=length, n_sub=n_sub, has_init=has_init),
        grid=(n_seq // n_sub,),
        in_specs=in_specs,
        out_specs=out_specs,
        out_shape=out_shape,
        scratch_shapes=[pltpu.VMEM((rows, v_w), F32), pltpu.VMEM((2, rows, qk_w), F32),
                        pltpu.VMEM(st_blk, F32)],
        compiler_params=_cp(("parallel",)),
        name="gla",
    )(*args)


def _mlstm_chunks(qs, kss, vs, lf_cs, li_cs, lf_rs, li_rs, cms, nvs, m_prevs, masks, mask_ts, revs):
    t_n = CHUNK
    idx = range(len(qs))
    tms = [jnp.where(masks[i], 1.0, 0.0).astype(BF16) for i in idx]
    tmts = [jnp.where(mask_ts[i], 1.0, 0.0).astype(BF16) for i in idx]
    csp = [_split3(jnp.broadcast_to(lf_cs[i], (t_n, t_n))) for i in idx]
    rsp = [_split3(jnp.broadcast_to(lf_rs[i], (t_n, t_n))) for i in idx]
    b_colbs = [_dot(tms[i], csp[i][0]) + (_dot(tms[i], csp[i][1]) + _dot(tms[i], csp[i][2])) for i in idx]
    b_rowbs = [_dot(rsp[i][0], tmts[i]) + (_dot(rsp[i][1], tmts[i]) + _dot(rsp[i][2], tmts[i])) for i in idx]
    qbs = [qs[i].astype(BF16) for i in idx]
    vbs = [vs[i].astype(BF16) for i in idx]
    qks = [_dot_nt(qbs[i], kss[i].astype(BF16)) for i in idx]
    qcs = [_dot(qbs[i], cms[i].astype(BF16)) for i in idx]
    qns = [jnp.sum(qs[i] * nvs[i], axis=-1, keepdims=True) for i in idx]
    b_cols = [b_colbs[i][:, 0:1] for i in idx]
    b_rows = [b_rowbs[i][0:1, :] for i in idx]
    b_ends = [b_colbs[i][0:1, 0:1] if revs[i] else b_colbs[i][t_n - 1:t_n, 0:1] for i in idx]
    dmats = [jnp.where(masks[i], b_colbs[i] - b_rowbs[i] + li_rs[i], -jnp.inf) for i in idx]
    m_ts = [jnp.maximum(b_cols[i] + m_prevs[i], jnp.max(dmats[i], axis=-1, keepdims=True)) for i in idx]
    w_inters = [jnp.exp(b_cols[i] + m_prevs[i] - m_ts[i]) for i in idx]
    scs = [qks[i] * jnp.exp(dmats[i] - m_ts[i]) for i in idx]
    svs = [_dot(scs[i].astype(BF16), vbs[i]) for i in idx]
    g_rs = [b_ends[i] - b_rows[i] + li_rs[i] for i in idx]
    g_cs = [b_ends[i] - b_cols[i] + li_cs[i] for i in idx]
    m_news = [jnp.maximum(b_ends[i] + m_prevs[i], jnp.max(g_rs[i], axis=-1, keepdims=True)) for i in idx]
    w_cs = [jnp.exp(b_ends[i] + m_prevs[i] - m_news[i]) for i in idx]
    kws = [kss[i] * jnp.exp(g_cs[i] - m_news[i]) for i in idx]
    kvs = [_dot_tn(kws[i].astype(BF16), vbs[i]) for i in idx]
    nums = [w_inters[i] * qcs[i] + svs[i] for i in idx]
    dens = [w_inters[i] * qns[i] + jnp.sum(scs[i], axis=-1, keepdims=True) for i in idx]
    hs = [nums[i] / jnp.maximum(jnp.abs(dens[i]), jnp.exp(-m_ts[i])) for i in idx]
    cm_news = [w_cs[i] * cms[i] + kvs[i] for i in idx]
    nv_news = [w_cs[i] * nvs[i] + jnp.sum(kws[i], axis=0, keepdims=True) for i in idx]
    return hs, cm_news, nv_news, m_news


def _mlstm_kernel(*refs, length, n_sub, seg, has_init):
    if has_init:
        (gb_ref, m0_ref, q_ref, k_ref, v_ref, mo_ref, gc_ref, gr_ref, cw_ref, ng_ref, c0_ref, n0_ref,
         y_ref, qc_ref, kc_ref, h_ref, c_ref, n_ref, m_ref) = refs
    else:
        (gb_ref, q_ref, k_ref, v_ref, mo_ref, gc_ref, gr_ref, cw_ref, ng_ref,
         y_ref, cout_ref, nout_ref, mout_ref, qc_ref, kc_ref, h_ref, c_ref, n_ref, m_ref) = refs
    b_idx = pl.program_id(0)
    dh = ML_DH
    rows_all = n_sub * length
    row = lax.broadcasted_iota(jnp.int32, (rows_all, 1), 0)
    pos = row % seg
    qc_ref[...] = _short_conv(q_ref[...], cw_ref[0], pos, seg, rows_all)
    kc_ref[...] = _short_conv(k_ref[...], cw_ref[1], pos, seg, rows_all) * (dh ** -0.5)
    h_ref[...] = jnp.zeros_like(h_ref)
    if has_init:
        c_ref[...] = c0_ref[...]
        n_ref[...] = n0_ref[...]
        for s in range(n_sub):
            for d in range(2):
                for h in range(ML_HEADS):
                    m0 = m0_ref[(b_idx * n_sub + s) * 2 * ML_HEADS + d * ML_HEADS + h]
                    m_ref[s, d, h] = jnp.full((1, 128), m0, F32)
    else:
        c_ref[...] = jnp.zeros_like(c_ref)
        n_ref[...] = jnp.zeros_like(n_ref)
        m_ref[...] = jnp.zeros_like(m_ref)
    n = length // CHUNK
    mask_f, mask_b = _tri_masks()

    def body(i, carry):
        chains = []
        for s in range(n_sub):
            for d, (mask, mask_t) in enumerate(((mask_f, mask_b), (mask_b, mask_f))):
                c = i if d == 0 else n - 1 - i
                rows = pl.ds(pl.multiple_of(s * length + c * CHUNK, CHUNK), CHUNK)
                for h in range(ML_HEADS):
                    chains.append((s, d, h, rows, mask, mask_t, c))
        args = [[] for _ in range(10)]
        for (s, d, h, rows, mask, mask_t, c) in chains:
            cols = slice(h * dh, (h + 1) * dh)
            bi = gb_ref[d * 2 * ML_HEADS + h]
            bf = gb_ref[d * 2 * ML_HEADS + ML_HEADS + h]
            gcol = gc_ref[h, rows, :]
            grow = gr_ref[h, s * n + c]
            vals = (qc_ref[rows, cols], kc_ref[rows, cols], v_ref[rows, cols],
                    _log_sigmoid(gcol[:, 2 + d:3 + d] + bf), gcol[:, d:d + 1] + bi,
                    _log_sigmoid(grow[2 + d:3 + d, :] + bf), grow[d:d + 1, :] + bi,
                    c_ref[s, d, h], n_ref[s, d, h], m_ref[s, d, h][:, 0:1])
            for lst, val in zip(args, vals):
                lst.append(val)
        hs, cms, nvs, m_news = _mlstm_chunks(*args, [ch[4] for ch in chains], [ch[5] for ch in chains],
                                             [ch[1] == 1 for ch in chains])
        for (s, d, h, rows, _, _, _), hc, cm, nv, m_new in zip(chains, hs, cms, nvs, m_news):
            h_ref[rows, h * dh:(h + 1) * dh] += hc
            c_ref[s, d, h] = cm
            n_ref[s, d, h] = nv
            m_ref[s, d, h] = jnp.broadcast_to(m_new, (1, 128))
        return carry

    lax.fori_loop(0, n, body, 0)
    for h in range(ML_HEADS):
        cols = slice(h * dh, (h + 1) * dh)
        o = h_ref[:, cols]
        o = o * lax.rsqrt(jnp.mean(o * o, axis=-1, keepdims=True) + EPS) * ng_ref[...]
        y_ref[:, cols] = (o * jax.nn.sigmoid(mo_ref[:, cols])).astype(y_ref.dtype)
    if not has_init:
        cout_ref[...] = c_ref[...]
        nout_ref[...] = n_ref[...]
        mout_ref[...] = m_ref[...]


def mlstm(zmain, gates_col, gates_row, gate_b, conv_w, norm_g, length, n_seq, row_blk0, seg, init, layer,
          n_sub=1):
    has_init = init is not None
    dh = ML_DH
    nh = ML_HEADS
    width = nh * dh
    once = pl.Buffered(1)
    rows = n_sub * length
    blk0 = row_blk0 // n_sub

    def zspec(col):
        return pl.BlockSpec((rows, width), lambda b, c=col // width: (blk0 + b, c), pipeline_mode=once)

    smem = pl.BlockSpec(memory_space=pltpu.SMEM)
    in_specs = [smem]
    args = [gate_b]
    if has_init:
        c0, n0, m0 = init
        in_specs.append(smem)
        args.append(m0)
    in_specs += [zspec(C_MQ), zspec(C_MK), zspec(C_MV), zspec(C_MO),
                 pl.BlockSpec((nh, rows, 8), lambda b: (0, blk0 + b, 0)),
                 pl.BlockSpec((nh, rows // CHUNK, 8, CHUNK), lambda b: (0, blk0 + b, 0, 0)),
                 pl.BlockSpec((2, 3, width), lambda b: (0, 0, 0)),
                 pl.BlockSpec((1, dh), lambda b: (0, 0))]
    args += [zmain, zmain, zmain, zmain, gates_col, gates_row, conv_w, norm_g]
    y_shape = jax.ShapeDtypeStruct((n_seq * length, width), BF16)
    y_spec = pl.BlockSpec((rows, width), lambda b: (b, 0))
    if has_init:
        in_specs += [pl.BlockSpec((n_sub, None, 2, nh, dh, dh), lambda b: (b, layer, 0, 0, 0, 0)),
                     pl.BlockSpec((n_sub, None, 2, nh, 1, dh), lambda b: (b, layer, 0, 0, 0, 0))]
        args += [c0, n0]
        out_shape, out_specs = y_shape, y_spec
    else:
        out_shape = [y_shape,
                     jax.ShapeDtypeStruct((n_seq, 2, nh, dh, dh), F32),
                     jax.ShapeDtypeStruct((n_seq, 2, nh, 1, dh), F32),
                     jax.ShapeDtypeStruct((n_seq, 2, nh, 1, 128), F32)]
        out_specs = [y_spec,
                     pl.BlockSpec((n_sub, 2, nh, dh, dh), lambda b: (b, 0, 0, 0, 0)),
                     pl.BlockSpec((n_sub, 2, nh, 1, dh), lambda b: (b, 0, 0, 0, 0)),
                     pl.BlockSpec((n_sub, 2, nh, 1, 128), lambda b: (b, 0, 0, 0, 0))]
    return pl.pallas_call(
        functools.partial(_mlstm_kernel, length=length, n_sub=n_sub, seg=seg, has_init=has_init),
        grid=(n_seq // n_sub,),
        in_specs=in_specs,
        out_specs=out_specs,
        out_shape=out_shape,
        scratch_shapes=[pltpu.VMEM((rows, width), F32), pltpu.VMEM((rows, width), F32),
                        pltpu.VMEM((rows, width), F32),
                        pltpu.VMEM((n_sub, 2, nh, dh, dh), F32), pltpu.VMEM((n_sub, 2, nh, 1, dh), F32),
                        pltpu.VMEM((n_sub, 2, nh, 1, 128), F32)],
        compiler_params=_cp(("parallel",)),
        name="mlstm",
    )(*args)


def _top_values(xs, count, with_rank=None):
    with_rank = with_rank or [False] * len(xs)
    curs = list(xs)
    vals = [[] for _ in xs]
    ranks = [jnp.full(x.shape, float(count), F32) if wr else None for x, wr in zip(xs, with_rank)]
    for r in range(count):
        ms = [jnp.max(c, axis=0, keepdims=True) for c in curs]
        for v, m in zip(vals, ms):
            v.append(m)
        need = [wr or r + 1 < count for wr in with_rank]
        tops = [c == m if nd else None for c, m, nd in zip(curs, ms, need)]
        curs = [jnp.where(t, -jnp.inf, c) if nd else c for c, t, nd in zip(curs, tops, need)]
        ranks = [jnp.where(t, float(r), rk) if wr else rk for rk, t, wr in zip(ranks, tops, with_rank)]
    return vals, ranks


def _peer_topk_kernel(q_ref, k_ref, nb_ref, rk_ref, e1_ref, e2_ref):
    dk = PEER_NKEYS
    s1 = _dot3(k_ref[0], q_ref[:, 0:dk], dot=_dot_nt)
    s2 = _dot3(k_ref[1], q_ref[:, dk:2 * dk], dot=_dot_nt)
    blocks = [slice(b * 128, (b + 1) * 128) for b in range(s1.shape[1] // 128)]
    nblk = len(blocks)
    s1s = [s1[:, lanes] for lanes in blocks]
    s2s = [s2[:, lanes] for lanes in blocks]
    vals, ranks = _top_values(s1s + s2s, PEER_TOPK, [False] * nblk + [True] * nblk)
    v1s, v2s, rank2s = vals[:nblk], vals[nblk:], ranks[nblk:]
    cands = []
    for v1, v2 in zip(v1s, v2s):
        rows = [v1[a] + v2[b] for a in range(PEER_TOPK) for b in range(PEER_TOPK // (a + 1))]
        rows += [jnp.full_like(rows[0], -jnp.inf)] * (-len(rows) % 8)
        cands.append(jnp.concatenate(rows, axis=0))
    bests, _ = _top_values(cands, PEER_TOPK)
    dense_b = 4
    for lanes, s1b, s2b, v1, v2, rank2, best in zip(blocks, s1s, s2s, v1s, v2s, rank2s, bests):
        zsum = jnp.ones_like(best[0])
        for r in range(1, PEER_TOPK):
            zsum = zsum + jnp.exp(best[r] - best[0])
        theta = best[PEER_TOPK - 1]
        nb = jnp.zeros_like(s1b)
        for b in range(dense_b):
            nb = nb + jnp.where(s1b + v2[b] >= theta, 1.0, 0.0)
        for a in range(PEER_TOPK // (dense_b + 1)):
            extra = jnp.zeros_like(theta)
            for b in range(dense_b, PEER_TOPK // (a + 1)):
                extra = extra + jnp.where(v1[a] + v2[b] >= theta, 1.0, 0.0)
            nb = nb + jnp.where(s1b == v1[a], extra, 0.0)
        nb_ref[:, lanes] = nb
        rk_ref[:, lanes] = pltpu.bitcast(rank2.astype(BF16), jnp.uint32)
        e1_ref[:, lanes] = jnp.exp(s1b - v1[0]) / zsum
        e2_ref[:, lanes] = pltpu.bitcast(jnp.exp(s2b - v2[0]).astype(BF16), jnp.uint32)


def peer_topk(q, keys, tt=512):
    n = q.shape[0]
    nk = PEER_NKEYS
    spec = pl.BlockSpec((None, nk, tt), lambda i, h: (h, 0, i))
    pspec = pl.BlockSpec((None, nk // 2, tt), lambda i, h: (h, 0, i))
    full = jax.ShapeDtypeStruct((PEER_HEADS, nk, n), F32)
    packed = jax.ShapeDtypeStruct((PEER_HEADS, nk // 2, n), jnp.uint32)
    return pl.pallas_call(
        _peer_topk_kernel,
        grid=(n // tt, PEER_HEADS),
        in_specs=[pl.BlockSpec((tt, 2 * nk), lambda i, h: (i, h)),
                  pl.BlockSpec((None, 2, nk, nk), lambda i, h: (h, 0, 0, 0))],
        out_specs=[spec, pspec, spec, pspec],
        out_shape=[full, packed, full, packed],
        compiler_params=_cp(("parallel", "parallel")),
        name="peer_topk",
    )(q, keys)


def _row_bcast(row, n):
    t = jnp.broadcast_to(row, (16, 128)).astype(BF16)
    return jnp.broadcast_to(t[None], (n // 16, 16, 128)).reshape(n, 128)


def _gelu_tanh(x):
    return 0.5 * x * (1.0 + jnp.tanh(math.sqrt(2.0 / math.pi) * (x + 0.044715 * (x * x * x))))


def _peer_expert_kernel(ht_ref, u_ref, vt_ref, nb_ref, rk_ref, e1_ref, e2_ref, x_ref, gt_ref,
                        o_ref, acc_ref, w_ref, *, n_i1, tt):
    j = pl.program_id(1)
    nk = PEER_NKEYS

    @pl.when(j == 0)
    def _():
        acc_ref[...] = jnp.zeros_like(acc_ref)

    act = _dot(u_ref[...], ht_ref[...])
    zero = jnp.zeros((), BF16)
    for ts in range(tt // 128):
        lanes = slice(ts * 128, (ts + 1) * 128)
        for r in range(n_i1):
            w = None
            for h in range(PEER_HEADS):
                nbr = _row_bcast(nb_ref[h, r:r + 1, lanes], nk)
                e1r = _row_bcast(e1_ref[h, r:r + 1, lanes], nk)
                rk = pltpu.bitcast(rk_ref[h, :, lanes], BF16)
                e2 = pltpu.bitcast(e2_ref[h, :, lanes], BF16)
                term = jnp.where(rk < nbr, e2 * e1r, zero)
                w = term if w is None else w + term
            w_ref[r * nk:(r + 1) * nk, lanes] = w
    p = w_ref[...] * _gelu_tanh(act).astype(BF16)
    acc_ref[...] += _dot(vt_ref[...], p)

    @pl.when(j == pl.num_programs(1) - 1)
    def _():
        o_ref[...] = x_ref[...] + gt_ref[...] * acc_ref[...].T


def _xpose_cast_kernel(x_ref, o_ref):
    o_ref[...] = x_ref[...].T.astype(BF16)


def transpose_cast_chunks(tab, ec):
    depth, e, d = tab.shape
    return pl.pallas_call(
        _xpose_cast_kernel,
        grid=(depth, e // ec),
        in_specs=[pl.BlockSpec((None, ec, d), lambda l, c: (l, c, 0))],
        out_specs=pl.BlockSpec((None, None, d, ec), lambda l, c: (l, c, 0, 0)),
        out_shape=jax.ShapeDtypeStruct((depth, e // ec, d, ec), BF16),
        compiler_params=_cp(("parallel", "parallel")),
        name="transpose_cast_chunks",
    )(tab)


def peer_experts(h2t, u_tab, vt_tab, layer, nb, rk, e1, e2, x, mod3, gt_chunk, tt=512):
    ec = PEER_CHUNK
    n = x.shape[0]
    nk = PEER_NKEYS
    n_i1 = ec // nk
    once = pl.Buffered(1)
    sspec = pl.BlockSpec((PEER_HEADS, nk // 2, tt), lambda i, j: (0, 0, i), pipeline_mode=once)
    rspec = pl.BlockSpec((PEER_HEADS, n_i1, tt), lambda i, j: (0, j, i))
    return pl.pallas_call(
        functools.partial(_peer_expert_kernel, n_i1=n_i1, tt=tt),
        grid=(n // tt, PEER_EXPERTS // ec),
        in_specs=[pl.BlockSpec((D_MODEL, tt), lambda i, j: (0, i)),
                  pl.BlockSpec((None, ec, D_MODEL), lambda i, j: (layer, j, 0)),
                  pl.BlockSpec((None, None, D_MODEL, ec), lambda i, j: (layer, j, 0, 0)),
                  rspec, sspec, rspec, sspec,
                  pl.BlockSpec((tt, D_MODEL), lambda i, j: (i, 0), pipeline_mode=once),
                  pl.BlockSpec((None, 1, D_MODEL), lambda i, j: (_mod_row(i, tt), 0, gt_chunk))],
        out_specs=pl.BlockSpec((tt, D_MODEL), lambda i, j: (i, 0)),
        out_shape=jax.ShapeDtypeStruct((n, D_MODEL), F32),
        scratch_shapes=[pltpu.VMEM((D_MODEL, tt), F32), pltpu.VMEM((ec, tt), BF16)],
        compiler_params=_cp(("parallel", "arbitrary")),
        name="peer_experts",
    )(h2t, u_tab, vt_tab, nb, rk, e1, e2, x, mod3)


def _reorder_in_proj(w_in, b_in):
    splits = (1024, 1024, 1024, 512, 512, 1024, 1024, 16, 16, 1024, 1024, 1024, 1024, 8, 8, 2048, 2048, 2048)
    offs = np.concatenate([[0], np.cumsum(splits)])
    seg = lambda a, i: a[..., offs[i]:offs[i + 1]]
    main_ids = (0, 1, 2, 3, 4, 5, 6, 9, 10, 11, 12, 15, 16, 17)
    small_ids = (7, 8, 13, 14)
    w_main = jnp.concatenate([seg(w_in, i).astype(BF16) for i in main_ids], axis=-1)
    b_main = jnp.concatenate([seg(b_in, i) for i in main_ids], axis=-1)
    w_small = jnp.concatenate([seg(w_in, i) for i in small_ids], axis=-1)
    b_small = jnp.concatenate([seg(b_in, i) for i in small_ids], axis=-1)
    pad = N_SMALL - w_small.shape[-1]
    w_small = jnp.pad(w_small, ((0, 0), (0, 0), (0, pad)))
    b_small = jnp.pad(b_small, ((0, 0), (0, pad)))
    return w_main, b_main, w_small, b_small


def kernel(x_prompt, x_sample, c, state_gla, state_mlstm_C, state_mlstm_n, state_mlstm_m, c_ctx,
           mod_w, mod_b, norm1_g, norm2_g, final_g, w_in, b_in, hy_conv, hy_w1, hy_b1, hy_w2, hy_b2,
           hy_w3, hy_freq, hy_bias, gla_a2_w, gla_a2_b, gla_norm_g, ml_conv, ml_gate_b, ml_norm_g,
           w_branch, w_out, peer_wq, peer_keys, peer_u, peer_v):
    w_main, b_main, w_small, b_small = _reorder_in_proj(w_in, b_in)
    w_small_hi = w_small.astype(BF16)
    w_small_lo = (w_small - w_small_hi.astype(F32)).astype(BF16)
    w_branch_b = w_branch.astype(BF16)
    w_out_b = w_out.astype(BF16)
    wq_hi = peer_wq.astype(BF16)
    wq_lo = (peer_wq - wq_hi.astype(F32)).astype(BF16)
    u_b = peer_u.astype(BF16)
    vt_b = transpose_cast_chunks(peer_v, PEER_CHUNK)
    zero_b = jnp.zeros((1, D_MODEL), F32)
    a2f = jnp.pad(gla_a2_w[:, 0], ((0, 0), (0, N_SMALL - GLA_RANK), (0, 0)))
    a2b = jnp.pad(gla_a2_w[:, 1], ((0, 0), (GLA_RANK, N_SMALL - 2 * GLA_RANK), (0, 0)))
    a2bias = gla_a2_b.reshape(DEPTH, 2, 1, GLA_HEADS * GLA_DK)
    n0_all = state_mlstm_n.reshape(DEC_BATCH, DEPTH, 2, ML_HEADS, 1, ML_DH)

    tabs = {}
    for length in (SEQ, DEC_SEQ):
        fre, fim, g_mat, sgn = _dft_tables(length)
        fre_h, fre_l = _split2(fre)
        fim_h, fim_l = _split2(fim)
        tabs[length] = dict(spec=(fre_h, fre_l, fim_h, fim_l, sgn),
                            f=jnp.concatenate([fre_h, fim_h], axis=0), g=g_mat.astype(BF16))

    cvec = jnp.zeros((MOD_ROWS, D_MODEL), F32).at[:DEC_BATCH].set(c).at[CTX_ROW].set(c_ctx)
    mod_all = mod_table(cvec, mod_w, mod_b)

    x = jnp.concatenate([x_prompt.reshape(N_PROMPT, D_MODEL), x_sample.reshape(N_SAMPLE, D_MODEL)], axis=0)
    groups = ((SEQ, BATCH, 0, SEQ), (DEC_SEQ, DEC_BATCH, N_PROMPT // DEC_SEQ, GRID_W))
    new_gla, new_c, new_n, new_m = [], [], [], []
    for l in range(DEPTH):
        mod3 = mod_all[l].reshape(MOD_ROWS, 1, 6 * D_MODEL)
        h_hi, zsmall = normmod_gates(x, norm1_g[l], mod3, 0, 1, w_small_hi, w_small_lo, l,
                                     b_small[l].reshape(1, N_SMALL))
        zmain = mm_bias(h_hi, w_main, l, b_main[l].reshape(1, N_MAIN), tm=1024, tn=2048)
        mi = zsmall[:, 32:40].reshape(N_ROWS, 2, ML_HEADS)
        mf = zsmall[:, 40:48].reshape(N_ROWS, 2, ML_HEADS)
        gcol = jnp.concatenate([mi, mf, jnp.zeros((N_ROWS, 4, ML_HEADS), F32)], axis=1)
        gates_col = jnp.transpose(gcol, (2, 0, 1))
        gates_row = jnp.transpose(gcol.reshape(N_ROWS // CHUNK, CHUNK, 8, ML_HEADS), (3, 0, 2, 1))
        gate_b = ml_gate_b[l].reshape(-1)
        y_hy, y_gla, y_ml = [], [], []
        for gi, (length, n_seq, blk0, seg) in enumerate(groups):
            t = tabs[length]
            sre, sim = hyena_spectrum(length, hy_w1[l], hy_b1[l], hy_w2[l], hy_b2[l], hy_w3[l],
                                      hy_freq[l], t["spec"])
            y_hy.append(hyena(zmain, hy_conv[l], sre, sim, hy_bias[l], t["f"], t["g"],
                              length, n_seq, blk0, seg))
            gla_args = (zmain, zsmall, a2f[l], a2b[l], a2bias[l], gla_norm_g[l].reshape(1, GLA_DV),
                        length, n_seq, blk0)
            ml_args = (zmain, gates_col, gates_row, gate_b, ml_conv[l], ml_norm_g[l].reshape(1, ML_DH),
                       length, n_seq, blk0, seg)
            if gi == 0:
                yg, s_fin = gla(*gla_args, None, l, n_sub=PROMPT_SEQS_PER_STEP)
                ym, c_fin, n_fin, m_fin = mlstm(*ml_args, None, l, n_sub=PROMPT_SEQS_PER_STEP)
                new_gla.append(s_fin)
                new_c.append(c_fin)
                new_n.append(n_fin[:, :, :, 0, :])
                new_m.append(m_fin[:, :, :, 0, 0])
            else:
                yg = gla(*gla_args, state_gla, l)
                ym = mlstm(*ml_args, (state_mlstm_C, n0_all, state_mlstm_m[:, l].reshape(-1)), l)
            y_gla.append(yg)
            y_ml.append(ym)
        merged = merge_branches(y_hy, y_gla, y_ml, w_branch_b, l, zmain)
        x = mm_residual(merged, w_out_b, l, x, mod3, 2)
        h2, h2_lo, h2t = normmod_peer(x, norm2_g[l], mod3, 3, 4)
        q = mm3_bias(h2, h2_lo, wq_hi, wq_lo, l, zero_b)
        nb, rk, e1, e2 = peer_topk(q, peer_keys[l])
        x = peer_experts(h2t, u_b, vt_b, l, nb, rk, e1, e2, x, mod3, 5)

    y_prompt = final_norm(x, final_g, 0, N_PROMPT).reshape(BATCH, SEQ, D_MODEL)
    y_sample = final_norm(x, final_g, N_PROMPT, N_SAMPLE).reshape(DEC_BATCH, DEC_SEQ, D_MODEL)
    return (y_prompt, y_sample, jnp.stack(new_gla, axis=1), jnp.stack(new_c, axis=1),
            jnp.stack(new_n, axis=1), jnp.stack(new_m, axis=1))
```

```python
import functools
import math

import jax
import jax.numpy as jnp
import numpy as np
from jax import lax
from jax.experimental import pallas as pl
from jax.experimental.pallas import tpu as pltpu

F32 = jnp.float32
BF16 = jnp.bfloat16

D_MODEL = 2048
BATCH = 16
SEQ = 256
DEPTH = 4
DEC_BATCH = 4
DEC_SEQ = 1024
GRID_W = 64
EPS = 1e-6
CHUNK = 64
HY_WIDTH = 1024
HY_EMB = 33
HY_BANDS = (HY_EMB - 1) // 2
HY_FFN = 64
HY_DECAY_TARGET = 1e-2
HY_FAST_PCT = 0.3
HY_SLOW_PCT = 1.5
GLA_HEADS = 4
GLA_DK = 128
GLA_DV = 256
GLA_RANK = 16
GLA_NORMALIZER = 16.0
ML_HEADS = 4
ML_DH = 256
PEER_HEADS = 8
PEER_NKEYS = 128
PEER_EXPERTS = PEER_NKEYS * PEER_NKEYS
PEER_TOPK = 16
PEER_CHUNK = 1024
PROMPT_SEQS_PER_STEP = 2
HY_CHAIN_W = 256

N_PROMPT = BATCH * SEQ
N_SAMPLE = DEC_BATCH * DEC_SEQ
N_ROWS = N_PROMPT + N_SAMPLE
CTX_ROW = DEC_BATCH
MOD_ROWS = 8

C_HX1, C_HX2, C_HV = 0, 1024, 2048
C_GQ, C_GK, C_GV, C_GR = 3072, 3584, 4096, 5120
C_MQ, C_MK, C_MV, C_MO = 6144, 7168, 8192, 9216
C_GA, C_GB, C_GC = 10240, 12288, 14336
N_MAIN = 16384
N_SMALL = 128

VMEM_LIMIT = 56 * 1024 * 1024


def _cp(sem):
    return pltpu.CompilerParams(dimension_semantics=sem, vmem_limit_bytes=VMEM_LIMIT)


def _dot(a, b):
    return jnp.dot(a, b, preferred_element_type=F32)


def _dot_nt(a, b):
    return lax.dot_general(a, b, (((1,), (1,)), ((), ())), preferred_element_type=F32)


def _dot_tn(a, b):
    return lax.dot_general(a, b, (((0,), (0,)), ((), ())), preferred_element_type=F32)


def _split2(a):
    hi = a.astype(BF16)
    lo = (a - hi.astype(F32)).astype(BF16)
    return hi, lo


def _split3(a):
    a1 = a.astype(BF16)
    r1 = a - a1.astype(F32)
    a2 = r1.astype(BF16)
    a3 = (r1 - a2.astype(F32)).astype(BF16)
    return a1, a2, a3


def _dot3(a, b, dot=_dot):
    ah, al = _split2(a)
    bh, bl = _split2(b)
    return dot(ah, bh) + (dot(ah, bl) + dot(al, bh))


def _log_sigmoid(x):
    return jnp.minimum(x, 0.0) - jnp.log(1.0 + jnp.exp(-jnp.abs(x)))


def _mod_row(i, tm):
    n_p = N_PROMPT // tm
    return jnp.where(i < n_p, CTX_ROW, (i - n_p) // (DEC_SEQ // tm))


def _mod_kernel(c_ref, w_ref, b_ref, o_ref):
    a = c_ref[...]
    a = a * jax.nn.sigmoid(a)
    o_ref[0] = _dot3(a, w_ref[0]) + b_ref[0]


def mod_table(cvec, mod_w, mod_b):
    tn = 1024
    n = mod_w.shape[-1]
    return pl.pallas_call(
        _mod_kernel,
        grid=(DEPTH, n // tn),
        in_specs=[pl.BlockSpec((MOD_ROWS, D_MODEL), lambda l, j: (0, 0)),
                  pl.BlockSpec((1, D_MODEL, tn), lambda l, j: (l, 0, j)),
                  pl.BlockSpec((1, 1, tn), lambda l, j: (l, 0, j))],
        out_specs=pl.BlockSpec((1, MOD_ROWS, tn), lambda l, j: (l, 0, j)),
        out_shape=jax.ShapeDtypeStruct((DEPTH, MOD_ROWS, n), F32),
        compiler_params=_cp(("parallel", "parallel")),
        name="mod_table",
    )(cvec, mod_w, mod_b.reshape(DEPTH, 1, n))


def _norm_modulate(x_ref, g_ref, sc_ref, sh_ref):
    x = x_ref[...]
    y = x * lax.rsqrt(jnp.mean(x * x, axis=-1, keepdims=True) + EPS) * g_ref[...]
    return y * (1.0 + sc_ref[...]) + sh_ref[...]


def _normmod_gates_kernel(x_ref, g_ref, sc_ref, sh_ref, wh_ref, wl_ref, b_ref, hi_ref, zs_ref):
    h = _norm_modulate(x_ref, g_ref, sc_ref, sh_ref)
    hi = h.astype(BF16)
    lo = (h - hi.astype(F32)).astype(BF16)
    hi_ref[...] = hi
    zs_ref[...] = _dot(hi, wh_ref[...]) + (_dot(hi, wl_ref[...]) + _dot(lo, wh_ref[...])) + b_ref[...]


def _normmod_peer_kernel(x_ref, g_ref, sc_ref, sh_ref, hi_ref, lo_ref, t_ref):
    h = _norm_modulate(x_ref, g_ref, sc_ref, sh_ref)
    hi = h.astype(BF16)
    hi_ref[...] = hi
    lo_ref[...] = (h - hi.astype(F32)).astype(BF16)
    t_ref[...] = h.T.astype(BF16)


def _normmod_specs(tm, sh_chunk, sc_chunk):
    return [pl.BlockSpec((tm, D_MODEL), lambda i: (i, 0)),
            pl.BlockSpec((1, D_MODEL), lambda i: (0, 0)),
            pl.BlockSpec((None, 1, D_MODEL), lambda i: (_mod_row(i, tm), 0, sc_chunk)),
            pl.BlockSpec((None, 1, D_MODEL), lambda i: (_mod_row(i, tm), 0, sh_chunk))]


def normmod_gates(x, g, mod3, sh_chunk, sc_chunk, w_hi, w_lo, layer, b):
    tm = 512
    n = w_hi.shape[-1]
    row_spec = pl.BlockSpec((tm, D_MODEL), lambda i: (i, 0))
    w_spec = pl.BlockSpec((None, D_MODEL, n), lambda i: (layer, 0, 0))
    return pl.pallas_call(
        _normmod_gates_kernel,
        grid=(N_ROWS // tm,),
        in_specs=_normmod_specs(tm, sh_chunk, sc_chunk) + [w_spec, w_spec, pl.BlockSpec((1, n), lambda i: (0, 0))],
        out_specs=[row_spec, pl.BlockSpec((tm, n), lambda i: (i, 0))],
        out_shape=[jax.ShapeDtypeStruct((N_ROWS, D_MODEL), BF16), jax.ShapeDtypeStruct((N_ROWS, n), F32)],
        compiler_params=_cp(("parallel",)),
        name="normmod_gates",
    )(x, g.reshape(1, D_MODEL), mod3, mod3, w_hi, w_lo, b)


def normmod_peer(x, g, mod3, sh_chunk, sc_chunk):
    tm = 512
    row_spec = pl.BlockSpec((tm, D_MODEL), lambda i: (i, 0))
    row_shape = jax.ShapeDtypeStruct((N_ROWS, D_MODEL), BF16)
    return pl.pallas_call(
        _normmod_peer_kernel,
        grid=(N_ROWS // tm,),
        in_specs=_normmod_specs(tm, sh_chunk, sc_chunk),
        out_specs=[row_spec, row_spec, pl.BlockSpec((D_MODEL, tm), lambda i: (0, i))],
        out_shape=[row_shape, row_shape, jax.ShapeDtypeStruct((D_MODEL, N_ROWS), BF16)],
        compiler_params=_cp(("parallel",)),
        name="normmod_peer",
    )(x, g.reshape(1, D_MODEL), mod3, mod3)


def _final_norm_kernel(x_ref, g_ref, o_ref):
    x = x_ref[...]
    o_ref[...] = x * lax.rsqrt(jnp.mean(x * x, axis=-1, keepdims=True) + EPS) * g_ref[...]


def final_norm(x, g, row0, n_rows):
    tm = 256
    return pl.pallas_call(
        _final_norm_kernel,
        grid=(n_rows // tm,),
        in_specs=[pl.BlockSpec((tm, D_MODEL), lambda i: (row0 // tm + i, 0)),
                  pl.BlockSpec((1, D_MODEL), lambda i: (0, 0))],
        out_specs=pl.BlockSpec((tm, D_MODEL), lambda i: (i, 0)),
        out_shape=jax.ShapeDtypeStruct((n_rows, D_MODEL), F32),
        compiler_params=_cp(("parallel",)),
        name="final_norm",
    )(x, g.reshape(1, D_MODEL))


def _mm_bias_kernel(x_ref, w_ref, b_ref, o_ref):
    o_ref[...] = (_dot(x_ref[...], w_ref[...]) + b_ref[...]).astype(o_ref.dtype)


def mm_bias(x, w, layer, b, out_dtype=F32, tm=512, tn=1024):
    m, k = x.shape
    n = w.shape[-1]
    return pl.pallas_call(
        _mm_bias_kernel,
        grid=(n // tn, m // tm),
        in_specs=[pl.BlockSpec((tm, k), lambda j, i: (i, 0)),
                  pl.BlockSpec((None, k, tn), lambda j, i: (layer, 0, j)),
                  pl.BlockSpec((1, tn), lambda j, i: (0, j))],
        out_specs=pl.BlockSpec((tm, tn), lambda j, i: (i, j)),
        out_shape=jax.ShapeDtypeStruct((m, n), out_dtype),
        compiler_params=_cp(("parallel", "parallel")),
        name="mm_bias",
    )(x, w, b)


def _mm3_bias_kernel(xh_ref, xl_ref, wh_ref, wl_ref, b_ref, o_ref):
    xh = xh_ref[...]
    acc = _dot(xh, wh_ref[...]) + (_dot(xh, wl_ref[...]) + _dot(xl_ref[...], wh_ref[...]))
    o_ref[...] = acc + b_ref[...]


def mm3_bias(xh, xl, wh, wl, layer, b, tm=512, tn=1024):
    m, k = xh.shape
    n = wh.shape[-1]
    tn = min(tn, n)
    return pl.pallas_call(
        _mm3_bias_kernel,
        grid=(n // tn, m // tm),
        in_specs=[pl.BlockSpec((tm, k), lambda j, i: (i, 0)),
                  pl.BlockSpec((tm, k), lambda j, i: (i, 0)),
                  pl.BlockSpec((None, k, tn), lambda j, i: (layer, 0, j)),
                  pl.BlockSpec((None, k, tn), lambda j, i: (layer, 0, j)),
                  pl.BlockSpec((1, tn), lambda j, i: (0, j))],
        out_specs=pl.BlockSpec((tm, tn), lambda j, i: (i, j)),
        out_shape=jax.ShapeDtypeStruct((m, n), F32),
        compiler_params=_cp(("parallel", "parallel")),
        name="mm3_bias",
    )(xh, xl, wh, wl, b)


def _merge_kernel(yhp_ref, yhs_ref, ygp_ref, ygs_ref, ymp_ref, yms_ref, w_ref, ga_ref, gb_ref, gc_ref, o_ref,
                  *, n_prompt_tiles):
    is_prompt = pl.program_id(1) < n_prompt_tiles
    pick = lambda p_ref, s_ref: jnp.where(is_prompt, p_ref[...], s_ref[...])
    acc = jax.nn.sigmoid(ga_ref[...]) * _dot(pick(yhp_ref, yhs_ref), w_ref[0])
    acc += jax.nn.sigmoid(gb_ref[...]) * _dot(pick(ygp_ref, ygs_ref), w_ref[1])
    acc += jax.nn.sigmoid(gc_ref[...]) * _dot(pick(ymp_ref, yms_ref), w_ref[2])
    o_ref[...] = acc.astype(o_ref.dtype)


def merge_branches(y_hy, y_gla, y_ml, w_branch, layer, zmain, tm=512, tn=1024):
    kb = HY_WIDTH
    n_p = N_PROMPT // tm
    p_spec = pl.BlockSpec((tm, kb), lambda j, i: (jnp.minimum(i, n_p - 1), 0))
    s_spec = pl.BlockSpec((tm, kb), lambda j, i: (jnp.maximum(i - n_p, 0), 0))

    def gate_spec(col):
        return pl.BlockSpec((tm, tn), lambda j, i, c=col // tn: (i, c + j))

    return pl.pallas_call(
        functools.partial(_merge_kernel, n_prompt_tiles=n_p),
        grid=(D_MODEL // tn, N_ROWS // tm),
        in_specs=[p_spec, s_spec, p_spec, s_spec, p_spec, s_spec,
                  pl.BlockSpec((None, 3, kb, tn), lambda j, i: (layer, 0, 0, j)),
                  gate_spec(C_GA), gate_spec(C_GB), gate_spec(C_GC)],
        out_specs=pl.BlockSpec((tm, tn), lambda j, i: (i, j)),
        out_shape=jax.ShapeDtypeStruct((N_ROWS, D_MODEL), BF16),
        compiler_params=_cp(("parallel", "parallel")),
        name="merge_branches",
    )(*y_hy, *y_gla, *y_ml, w_branch, zmain, zmain, zmain)


def _mm_resid_kernel(m_ref, w_ref, x_ref, gt_ref, o_ref):
    o_ref[...] = x_ref[...] + gt_ref[...] * _dot(m_ref[...], w_ref[...])


def mm_residual(merged, w, layer, x, mod3, gt_chunk, tm=512, tn=1024):
    k = merged.shape[1]
    return pl.pallas_call(
        _mm_resid_kernel,
        grid=(D_MODEL // tn, N_ROWS // tm),
        in_specs=[pl.BlockSpec((tm, k), lambda j, i: (i, 0)),
                  pl.BlockSpec((None, k, tn), lambda j, i: (layer, 0, j)),
                  pl.BlockSpec((tm, tn), lambda j, i: (i, j)),
                  pl.BlockSpec((None, 1, tn),
                               lambda j, i: (_mod_row(i, tm), 0, gt_chunk * (D_MODEL // tn) + j))],
        out_specs=pl.BlockSpec((tm, tn), lambda j, i: (i, j)),
        out_shape=jax.ShapeDtypeStruct((N_ROWS, D_MODEL), F32),
        compiler_params=_cp(("parallel", "parallel")),
        name="mm_residual",
    )(merged, w, x, mod3)


def _dft_tables(length):
    k = jnp.arange(length, dtype=jnp.int32)
    m = (k[:, None] * k[None, :]) % (2 * length)
    ang = m.astype(F32) * (math.pi / length)
    cos = jnp.cos(ang)
    sin = jnp.sin(ang)
    sgn = jnp.where(k % 2 == 0, 1.0, -1.0).astype(F32)
    fre = cos
    fim = jnp.where(k[:, None] == 0, sgn[None, :], -sin)
    wk = jnp.where(k == 0, 1.0, 2.0).astype(F32) / (2.0 * length)
    g_re = cos.T * wk[None, :]
    g_im = jnp.where(k[None, :] == 0, sgn[:, None] / (2.0 * length), -sin.T / length)
    return fre, fim, jnp.concatenate([g_re, g_im], axis=1), sgn


def _hyfilt_kernel(z_ref, w1_ref, b1_ref, w2_ref, b2_ref, fq_ref, w3f_ref, w3b_ref, tn_ref, dl_ref,
                   sgn_ref, freh_ref, frel_ref, fimh_ref, fiml_ref, sre_ref, sim_ref):
    hid = jnp.sin(fq_ref[0:1] * (_dot3(z_ref[...], w1_ref[...]) + b1_ref[...]))
    hid = jnp.sin(fq_ref[1:2] * (_dot3(hid, w2_ref[...]) + b2_ref[...]))
    decay = jnp.exp(-tn_ref[...] * dl_ref[...])
    fwd = _dot3(hid, w3f_ref[...]) * decay
    bwd = _dot3(hid, w3b_ref[...]) * decay
    row = lax.broadcasted_iota(jnp.int32, fwd.shape, 0)
    bwd = jnp.where(row == 0, 0.0, bwd)
    a = fwd + bwd
    d = fwd - bwd
    ah, al = _split2(a)
    dh, dl2 = _split2(d)
    re = _dot(freh_ref[...], ah) + (_dot(freh_ref[...], al) + _dot(frel_ref[...], ah))
    im = _dot(fimh_ref[...], dh) + (_dot(fimh_ref[...], dl2) + _dot(fiml_ref[...], dh))
    nyq = jnp.sum(sgn_ref[...] * a, axis=0, keepdims=True)
    sre_ref[0] = re
    sim_ref[0] = jnp.where(row == 0, nyq, im)


def hyena_spectrum(length, w1, b1, w2, b2, w3, freq, tabs):
    fre_h, fre_l, fim_h, fim_l, sgn = tabs
    t = jnp.arange(length, dtype=F32)
    t_norm = t / (length - 1)
    bands = jnp.linspace(1e-4, HY_BANDS - 1, HY_BANDS, dtype=F32)
    ang = (2.0 * math.pi / length) * t[:, None] * bands[None, :]
    z = jnp.concatenate([t_norm[:, None], jnp.cos(ang), -jnp.sin(ang),
                         jnp.zeros((length, HY_FFN - HY_EMB), F32)], axis=-1)
    w1p = jnp.pad(w1, ((0, HY_FFN - HY_EMB), (0, 0)))
    max_decay = math.log(HY_DECAY_TARGET) / HY_FAST_PCT
    min_decay = math.log(HY_DECAY_TARGET) / HY_SLOW_PCT
    deltas = jnp.abs(jnp.linspace(min_decay, max_decay, HY_WIDTH, dtype=F32)).reshape(1, HY_WIDTH)
    ct = 256
    nct = HY_WIDTH // ct
    full = lambda shape: pl.BlockSpec(shape, lambda o, j: (0,) * len(shape))
    out_spec = pl.BlockSpec((1, length, ct), lambda o, j: (o, 0, j))
    return pl.pallas_call(
        _hyfilt_kernel,
        grid=(2, nct),
        in_specs=[full((length, HY_FFN)), full((HY_FFN, HY_FFN)), full((1, HY_FFN)),
                  full((HY_FFN, HY_FFN)), full((1, HY_FFN)), full((2, HY_FFN)),
                  pl.BlockSpec((HY_FFN, ct), lambda o, j: (0, o * 2 * nct + j)),
                  pl.BlockSpec((HY_FFN, ct), lambda o, j: (0, o * 2 * nct + nct + j)),
                  full((length, 1)),
                  pl.BlockSpec((1, ct), lambda o, j: (0, j)),
                  full((length, 1)),
                  full((length, length)), full((length, length)),
                  full((length, length)), full((length, length))],
        out_specs=[out_spec, out_spec],
        out_shape=[jax.ShapeDtypeStruct((2, length, HY_WIDTH), F32)] * 2,
        compiler_params=_cp(("parallel", "parallel")),
        name="hyena_spectrum",
    )(z, w1p, b1.reshape(1, HY_FFN), w2, b2.reshape(1, HY_FFN), freq, w3, w3,
      t_norm.reshape(length, 1), deltas, sgn.reshape(length, 1), fre_h, fre_l, fim_h, fim_l)


def _short_conv(x, w, pos, seg, length):
    prev = jnp.where(pos == 0, 0.0, pltpu.roll(x, 1, 0))
    nxt = jnp.where(pos == seg - 1, 0.0, pltpu.roll(x, length - 1, 0))
    return prev * w[0:1] + x * w[1:2] + nxt * w[2:3]


def _hyena_kernel(x1_ref, x2_ref, v_ref, cw_ref, sre_ref, sim_ref, hb_ref, f_ref, g_ref, o_ref,
                  *, length, seg):
    row = lax.broadcasted_iota(jnp.int32, (length, 1), 0)
    pos = row % seg
    row0 = row == 0
    groups = [slice(c * HY_CHAIN_W, (c + 1) * HY_CHAIN_W) for c in range(x1_ref.shape[1] // HY_CHAIN_W)]

    def long_convs(us, o):
        specs = [_dot(f_ref[...], u.astype(BF16)) for u in us]
        ys = []
        for spec, cols in zip(specs, groups):
            ur, ui = spec[:length], spec[length:]
            hre, him = sre_ref[o, :, cols], sim_ref[o, :, cols]
            uihi = ui * him
            yr = ur * hre - jnp.where(row0, 0.0, uihi)
            yi = jnp.where(row0, uihi, ur * him + ui * hre)
            ys.append(jnp.concatenate([yr, yi], axis=0).astype(BF16))
        outs = [_dot(g_ref[...], y) for y in ys]
        return [y + u * hb_ref[o:o + 1, cols] for y, u, cols in zip(outs, us, groups)]

    conv = lambda ref, k, cols: _short_conv(ref[:, cols], cw_ref[k, :, cols], pos, seg, length)
    vs = [conv(v_ref, 2, cols) for cols in groups]
    c1 = long_convs(vs, 0)
    zs = [conv(x1_ref, 0, cols) * c for cols, c in zip(groups, c1)]
    c2 = long_convs(zs, 1)
    for cols, c in zip(groups, c2):
        o_ref[:, cols] = (conv(x2_ref, 1, cols) * c).astype(o_ref.dtype)


def hyena(zmain, conv_w, spec_re, spec_im, bias, f_mat, g_mat, length, n_seq, row_blk0, seg):
    ct = 2 * HY_CHAIN_W
    nct = HY_WIDTH // ct
    once = pl.Buffered(1)

    def zspec(col):
        return pl.BlockSpec((length, ct), lambda j, b, c=col // ct: (row_blk0 + b, c + j))

    return pl.pallas_call(
        functools.partial(_hyena_kernel, length=length, seg=seg),
        grid=(nct, n_seq),
        in_specs=[zspec(C_HX1), zspec(C_HX2), zspec(C_HV),
                  pl.BlockSpec((3, 3, ct), lambda j, b: (0, 0, j)),
                  pl.BlockSpec((2, length, ct), lambda j, b: (0, 0, j), pipeline_mode=once),
                  pl.BlockSpec((2, length, ct), lambda j, b: (0, 0, j), pipeline_mode=once),
                  pl.BlockSpec((2, ct), lambda j, b: (0, j)),
                  pl.BlockSpec((2 * length, length), lambda j, b: (0, 0), pipeline_mode=once),
                  pl.BlockSpec((length, 2 * length), lambda j, b: (0, 0), pipeline_mode=once)],
        out_specs=pl.BlockSpec((length, ct), lambda j, b: (b, j)),
        out_shape=jax.ShapeDtypeStruct((n_seq * length, HY_WIDTH), BF16),
        compiler_params=_cp(("parallel", "parallel")),
        name="hyena",
    )(zmain, zmain, zmain, conv_w, spec_re, spec_im, bias, f_mat, g_mat)


def _tri_masks():
    t = lax.broadcasted_iota(jnp.int32, (CHUNK, CHUNK), 0)
    s = lax.broadcasted_iota(jnp.int32, (CHUNK, CHUNK), 1)
    return s <= t, s >= t


def _gla_chunks(qs, ks, vs, gs, states, masks, revs):
    idx = range(len(qs))
    ones = jnp.ones((CHUNK, GLA_DK), BF16)
    tms = [jnp.where(masks[i], 1.0, 0.0).astype(BF16) for i in idx]
    gsp = [_split3(gs[i]) for i in idx]
    bcs = [_dot(tms[i], gsp[i][0]) + (_dot(tms[i], gsp[i][1]) + _dot(tms[i], gsp[i][2])) for i in idx]
    tots = [_dot_tn(gsp[i][0], ones) + (_dot_tn(gsp[i][1], ones) + _dot_tn(gsp[i][2], ones)) for i in idx]
    b_ends = [bcs[i][0:1] if revs[i] else bcs[i][CHUNK - 1:CHUNK] for i in idx]
    refs = [bcs[i][CHUNK // 2:CHUNK // 2 + 1] for i in idx]
    inters = [_dot((qs[i] * jnp.exp(bcs[i])).astype(BF16), states[i].astype(BF16)) for i in idx]
    qhs = [(qs[i] * jnp.exp(bcs[i] - refs[i])).astype(BF16) for i in idx]
    khs = [(ks[i] * jnp.exp(refs[i] - bcs[i])).astype(BF16) for i in idx]
    atts = [jnp.where(masks[i], _dot_nt(qhs[i], khs[i]), 0.0).astype(BF16) for i in idx]
    vbs = [vs[i].astype(BF16) for i in idx]
    outs = [inters[i] + _dot(atts[i], vbs[i]) for i in idx]
    kds = [(ks[i] * jnp.exp(b_ends[i] - bcs[i])).astype(BF16) for i in idx]
    es = [jnp.exp(tots[i]) for i in idx]
    new_states = [jnp.concatenate([es[i], es[i]], axis=1) * states[i] + _dot_tn(kds[i], vbs[i]) for i in idx]
    return outs, new_states


def _gla_kernel(*refs, length, n_sub, has_init):
    if has_init:
        (q_ref, k_ref, v_ref, gr_ref, zs_ref, wf_ref, wb_ref, ab_ref, ng_ref, s0_ref,
         y_ref, o_ref, lg_ref, st_ref) = refs
    else:
        (q_ref, k_ref, v_ref, gr_ref, zs_ref, wf_ref, wb_ref, ab_ref, ng_ref,
         y_ref, sout_ref, o_ref, lg_ref, st_ref) = refs
    zs = zs_ref[...]
    lg_ref[0] = _log_sigmoid(_dot3(zs, wf_ref[...]) + ab_ref[0]) * (1.0 / GLA_NORMALIZER)
    lg_ref[1] = _log_sigmoid(_dot3(zs, wb_ref[...]) + ab_ref[1]) * (1.0 / GLA_NORMALIZER)
    if has_init:
        st_ref[...] = s0_ref[...]
    else:
        st_ref[...] = jnp.zeros_like(st_ref)
    o_ref[...] = jnp.zeros_like(o_ref)
    n = length // CHUNK
    mask_f, mask_b = _tri_masks()
    scale = GLA_DK ** -0.5

    def body(i, carry):
        chains = []
        for s in range(n_sub):
            for d, mask in enumerate((mask_f, mask_b)):
                c = i if d == 0 else n - 1 - i
                rows = pl.ds(pl.multiple_of(s * length + c * CHUNK, CHUNK), CHUNK)
                for h in range(GLA_HEADS):
                    chains.append((s, d, h, rows, mask))
        kcs = [slice(h * GLA_DK, (h + 1) * GLA_DK) for (_, _, h, _, _) in chains]
        vcs = [slice(h * GLA_DV, (h + 1) * GLA_DV) for (_, _, h, _, _) in chains]
        outs, new_states = _gla_chunks(
            [q_ref[ch[3], kc] * scale for ch, kc in zip(chains, kcs)],
            [k_ref[ch[3], kc] for ch, kc in zip(chains, kcs)],
            [v_ref[ch[3], vc] for ch, vc in zip(chains, vcs)],
            [lg_ref[ch[1], ch[3], kc] for ch, kc in zip(chains, kcs)],
            [st_ref[ch[0], ch[1], ch[2]] for ch in chains],
            [ch[4] for ch in chains], [ch[1] == 1 for ch in chains])
        for ch, vc, o, s_new in zip(chains, vcs, outs, new_states):
            o_ref[ch[3], vc] += o
            st_ref[ch[0], ch[1], ch[2]] = s_new
        return carry

    lax.fori_loop(0, n, body, 0)
    for h in range(GLA_HEADS):
        vc = slice(h * GLA_DV, (h + 1) * GLA_DV)
        o = o_ref[:, vc]
        o = o * lax.rsqrt(jnp.mean(o * o, axis=-1, keepdims=True) + EPS) * ng_ref[...]
        gr = gr_ref[:, vc]
        y_ref[:, vc] = (o * (gr * jax.nn.sigmoid(gr))).astype(y_ref.dtype)
    if not has_init:
        sout_ref[...] = st_ref[...]


def gla(zmain, zsmall, wf, wb, ab, norm_g, length, n_seq, row_blk0, state0, layer, n_sub=1):
    has_init = state0 is not None
    qk_w = GLA_HEADS * GLA_DK
    v_w = GLA_HEADS * GLA_DV
    once = pl.Buffered(1)
    rows = n_sub * length
    blk0 = row_blk0 // n_sub
    st_blk = (n_sub, 2, GLA_HEADS, GLA_DK, GLA_DV)

    def zspec(col, width):
        return pl.BlockSpec((rows, width), lambda b, c=col // width: (blk0 + b, c), pipeline_mode=once)

    in_specs = [zspec(C_GQ, qk_w), zspec(C_GK, qk_w), zspec(C_GV, v_w), zspec(C_GR, v_w),
                pl.BlockSpec((rows, N_SMALL), lambda b: (blk0 + b, 0)),
                pl.BlockSpec((N_SMALL, qk_w), lambda b: (0, 0)),
                pl.BlockSpec((N_SMALL, qk_w), lambda b: (0, 0)),
                pl.BlockSpec((2, 1, qk_w), lambda b: (0, 0, 0)),
                pl.BlockSpec((1, GLA_DV), lambda b: (0, 0))]
    args = [zmain, zmain, zmain, zmain, zsmall, wf, wb, ab, norm_g]
    y_shape = jax.ShapeDtypeStruct((n_seq * length, v_w), BF16)
    y_spec = pl.BlockSpec((rows, v_w), lambda b: (b, 0))
    if has_init:
        in_specs.append(pl.BlockSpec((n_sub, None, 2, GLA_HEADS, GLA_DK, GLA_DV),
                                     lambda b: (b, layer, 0, 0, 0, 0)))
        args.append(state0)
        out_shape, out_specs = y_shape, y_spec
    else:
        out_shape = [y_shape, jax.ShapeDtypeStruct((n_seq, 2, GLA_HEADS, GLA_DK, GLA_DV), F32)]
        out_specs = [y_spec, pl.BlockSpec(st_blk, lambda b: (b, 0, 0, 0, 0))]
    return pl.pallas_call(
        functools.partial(_gla_kernel, length=length, n_sub=n_sub, has_init=has_init),
        grid=(n_seq // n_sub,),
        in_specs=in_specs,
        out_specs=out_specs,
        out_shape=out_shape,
        scratch_shapes=[pltpu.VMEM((rows, v_w), F32), pltpu.VMEM((2, rows, qk_w), F32),
                        pltpu.VMEM(st_blk, F32)],
        compiler_params=_cp(("parallel",)),
        name="gla",
    )(*args)


def _mlstm_chunks(qs, kss, vs, lf_cs, li_cs, lf_rs, li_rs, cms, nvs, m_prevs, masks, mask_ts, revs):
    t_n = CHUNK
    idx = range(len(qs))
    tms = [jnp.where(masks[i], 1.0, 0.0).astype(BF16) for i in idx]
    tmts = [jnp.where(mask_ts[i], 1.0, 0.0).astype(BF16) for i in idx]
    csp = [_split3(jnp.broadcast_to(lf_cs[i], (t_n, t_n))) for i in idx]
    rsp = [_split3(jnp.broadcast_to(lf_rs[i], (t_n, t_n))) for i in idx]
    b_colbs = [_dot(tms[i], csp[i][0]) + (_dot(tms[i], csp[i][1]) + _dot(tms[i], csp[i][2])) for i in idx]
    b_rowbs = [_dot(rsp[i][0], tmts[i]) + (_dot(rsp[i][1], tmts[i]) + _dot(rsp[i][2], tmts[i])) for i in idx]
    qbs = [qs[i].astype(BF16) for i in idx]
    vbs = [vs[i].astype(BF16) for i in idx]
    qks = [_dot_nt(qbs[i], kss[i].astype(BF16)) for i in idx]
    qcs = [_dot(qbs[i], cms[i].astype(BF16)) for i in idx]
    qns = [jnp.sum(qs[i] * nvs[i], axis=-1, keepdims=True) for i in idx]
    b_cols = [b_colbs[i][:, 0:1] for i in idx]
    b_rows = [b_rowbs[i][0:1, :] for i in idx]
    b_ends = [b_colbs[i][0:1, 0:1] if revs[i] else b_colbs[i][t_n - 1:t_n, 0:1] for i in idx]
    dmats = [jnp.where(masks[i], b_colbs[i] - b_rowbs[i] + li_rs[i], -jnp.inf) for i in idx]
    m_ts = [jnp.maximum(b_cols[i] + m_prevs[i], jnp.max(dmats[i], axis=-1, keepdims=True)) for i in idx]
    w_inters = [jnp.exp(b_cols[i] + m_prevs[i] - m_ts[i]) for i in idx]
    scs = [qks[i] * jnp.exp(dmats[i] - m_ts[i]) for i in idx]
    svs = [_dot(scs[i].astype(BF16), vbs[i]) for i in idx]
    g_rs = [b_ends[i] - b_rows[i] + li_rs[i] for i in idx]
    g_cs = [b_ends[i] - b_cols[i] + li_cs[i] for i in idx]
    m_news = [jnp.maximum(b_ends[i] + m_prevs[i], jnp.max(g_rs[i], axis=-1, keepdims=True)) for i in idx]
    w_cs = [jnp.exp(b_ends[i] + m_prevs[i] - m_news[i]) for i in idx]
    kws = [kss[i] * jnp.exp(g_cs[i] - m_news[i]) for i in idx]
    kvs = [_dot_tn(kws[i].astype(BF16), vbs[i]) for i in idx]
    nums = [w_inters[i] * qcs[i] + svs[i] for i in idx]
    dens = [w_inters[i] * qns[i] + jnp.sum(scs[i], axis=-1, keepdims=True) for i in idx]
    hs = [nums[i] / jnp.maximum(jnp.abs(dens[i]), jnp.exp(-m_ts[i])) for i in idx]
    cm_news = [w_cs[i] * cms[i] + kvs[i] for i in idx]
    nv_news = [w_cs[i] * nvs[i] + jnp.sum(kws[i], axis=0, keepdims=True) for i in idx]
    return hs, cm_news, nv_news, m_news


def _mlstm_kernel(*refs, length, n_sub, seg, has_init):
    if has_init:
        (gb_ref, m0_ref, q_ref, k_ref, v_ref, mo_ref, gc_ref, gr_ref, cw_ref, ng_ref, c0_ref, n0_ref,
         y_ref, qc_ref, kc_ref, h_ref, c_ref, n_ref, m_ref) = refs
    else:
        (gb_ref, q_ref, k_ref, v_ref, mo_ref, gc_ref, gr_ref, cw_ref, ng_ref,
         y_ref, cout_ref, nout_ref, mout_ref, qc_ref, kc_ref, h_ref, c_ref, n_ref, m_ref) = refs
    b_idx = pl.program_id(0)
    dh = ML_DH
    rows_all = n_sub * length
    row = lax.broadcasted_iota(jnp.int32, (rows_all, 1), 0)
    pos = row % seg
    qc_ref[...] = _short_conv(q_ref[...], cw_ref[0], pos, seg, rows_all)
    kc_ref[...] = _short_conv(k_ref[...], cw_ref[1], pos, seg, rows_all) * (dh ** -0.5)
    h_ref[...] = jnp.zeros_like(h_ref)
    if has_init:
        c_ref[...] = c0_ref[...]
        n_ref[...] = n0_ref[...]
        for s in range(n_sub):
            for d in range(2):
                for h in range(ML_HEADS):
                    m0 = m0_ref[(b_idx * n_sub + s) * 2 * ML_HEADS + d * ML_HEADS + h]
                    m_ref[s, d, h] = jnp.full((1, 128), m0, F32)
    else:
        c_ref[...] = jnp.zeros_like(c_ref)
        n_ref[...] = jnp.zeros_like(n_ref)
        m_ref[...] = jnp.zeros_like(m_ref)
    n = length // CHUNK
    mask_f, mask_b = _tri_masks()

    def body(i, carry):
        chains = []
        for s in range(n_sub):
            for d, (mask, mask_t) in enumerate(((mask_f, mask_b), (mask_b, mask_f))):
                c = i if d == 0 else n - 1 - i
                rows = pl.ds(pl.multiple_of(s * length + c * CHUNK, CHUNK), CHUNK)
                for h in range(ML_HEADS):
                    chains.append((s, d, h, rows, mask, mask_t, c))
        args = [[] for _ in range(10)]
        for (s, d, h, rows, mask, mask_t, c) in chains:
            cols = slice(h * dh, (h + 1) * dh)
            bi = gb_ref[d * 2 * ML_HEADS + h]
            bf = gb_ref[d * 2 * ML_HEADS + ML_HEADS + h]
            gcol = gc_ref[h, rows, :]
            grow = gr_ref[h, s * n + c]
            vals = (qc_ref[rows, cols], kc_ref[rows, cols], v_ref[rows, cols],
                    _log_sigmoid(gcol[:, 2 + d:3 + d] + bf), gcol[:, d:d + 1] + bi,
                    _log_sigmoid(grow[2 + d:3 + d, :] + bf), grow[d:d + 1, :] + bi,
                    c_ref[s, d, h], n_ref[s, d, h], m_ref[s, d, h][:, 0:1])
            for lst, val in zip(args, vals):
                lst.append(val)
        hs, cms, nvs, m_news = _mlstm_chunks(*args, [ch[4] for ch in chains], [ch[5] for ch in chains],
                                             [ch[1] == 1 for ch in chains])
        for (s, d, h, rows, _, _, _), hc, cm, nv, m_new in zip(chains, hs, cms, nvs, m_news):
            h_ref[rows, h * dh:(h + 1) * dh] += hc
            c_ref[s, d, h] = cm
            n_ref[s, d, h] = nv
            m_ref[s, d, h] = jnp.broadcast_to(m_new, (1, 128))
        return carry

    lax.fori_loop(0, n, body, 0)
    for h in range(ML_HEADS):
        cols = slice(h * dh, (h + 1) * dh)
        o = h_ref[:, cols]
        o = o * lax.rsqrt(jnp.mean(o * o, axis=-1, keepdims=True) + EPS) * ng_ref[...]
        y_ref[:, cols] = (o * jax.nn.sigmoid(mo_ref[:, cols])).astype(y_ref.dtype)
    if not has_init:
        cout_ref[...] = c_ref[...]
        nout_ref[...] = n_ref[...]
        mout_ref[...] = m_ref[...]


def mlstm(zmain, gates_col, gates_row, gate_b, conv_w, norm_g, length, n_seq, row_blk0, seg, init, layer,
          n_sub=1):
    has_init = init is not None
    dh = ML_DH
    nh = ML_HEADS
    width = nh * dh
    once = pl.Buffered(1)
    rows = n_sub * length
    blk0 = row_blk0 // n_sub

    def zspec(col):
        return pl.BlockSpec((rows, width), lambda b, c=col // width: (blk0 + b, c), pipeline_mode=once)

    smem = pl.BlockSpec(memory_space=pltpu.SMEM)
    in_specs = [smem]
    args = [gate_b]
    if has_init:
        c0, n0, m0 = init
        in_specs.append(smem)
        args.append(m0)
    in_specs += [zspec(C_MQ), zspec(C_MK), zspec(C_MV), zspec(C_MO),
                 pl.BlockSpec((nh, rows, 8), lambda b: (0, blk0 + b, 0)),
                 pl.BlockSpec((nh, rows // CHUNK, 8, CHUNK), lambda b: (0, blk0 + b, 0, 0)),
                 pl.BlockSpec((2, 3, width), lambda b: (0, 0, 0)),
                 pl.BlockSpec((1, dh), lambda b: (0, 0))]
    args += [zmain, zmain, zmain, zmain, gates_col, gates_row, conv_w, norm_g]
    y_shape = jax.ShapeDtypeStruct((n_seq * length, width), BF16)
    y_spec = pl.BlockSpec((rows, width), lambda b: (b, 0))
    if has_init:
        in_specs += [pl.BlockSpec((n_sub, None, 2, nh, dh, dh), lambda b: (b, layer, 0, 0, 0, 0)),
                     pl.BlockSpec((n_sub, None, 2, nh, 1, dh), lambda b: (b, layer, 0, 0, 0, 0))]
        args += [c0, n0]
        out_shape, out_specs = y_shape, y_spec
    else:
        out_shape = [y_shape,
                     jax.ShapeDtypeStruct((n_seq, 2, nh, dh, dh), F32),
                     jax.ShapeDtypeStruct((n_seq, 2, nh, 1, dh), F32),
                     jax.ShapeDtypeStruct((n_seq, 2, nh, 1, 128), F32)]
        out_specs = [y_spec,
                     pl.BlockSpec((n_sub, 2, nh, dh, dh), lambda b: (b, 0, 0, 0, 0)),
                     pl.BlockSpec((n_sub, 2, nh, 1, dh), lambda b: (b, 0, 0, 0, 0)),
                     pl.BlockSpec((n_sub, 2, nh, 1, 128), lambda b: (b, 0, 0, 0, 0))]
    return pl.pallas_call(
        functools.partial(_mlstm_kernel, length=length, n_sub=n_sub, seg=seg, has_init=has_init),
        grid=(n_seq // n_sub,),
        in_specs=in_specs,
        out_specs=out_specs,
        out_shape=out_shape,
        scratch_shapes=[pltpu.VMEM((rows, width), F32), pltpu.VMEM((rows, width), F32),
                        pltpu.VMEM((rows, width), F32),
                        pltpu.VMEM((n_sub, 2, nh, dh, dh), F32), pltpu.VMEM((n_sub, 2, nh, 1, dh), F32),
                        pltpu.VMEM((n_sub, 2, nh, 1, 128), F32)],
        compiler_params=_cp(("parallel",)),
        name="mlstm",
    )(*args)


def _top_values(xs, count, with_rank=None):
    with_rank = with_rank or [False] * len(xs)
    curs = list(xs)
    vals = [[] for _ in xs]
    ranks = [jnp.full(x.shape, float(count), F32) if wr else None for x, wr in zip(xs, with_rank)]
    for r in range(count):
        ms = [jnp.max(c, axis=0, keepdims=True) for c in curs]
        for v, m in zip(vals, ms):
            v.append(m)
        need = [wr or r + 1 < count for wr in with_rank]
        tops = [c == m if nd else None for c, m, nd in zip(curs, ms, need)]
        curs = [jnp.where(t, -jnp.inf, c) if nd else c for c, t, nd in zip(curs, tops, need)]
        ranks = [jnp.where(t, float(r), rk) if wr else rk for rk, t, wr in zip(ranks, tops, with_rank)]
    return vals, ranks


def _peer_topk_kernel(q_ref, k_ref, nb_ref, rk_ref, e1_ref, e2_ref):
    dk = PEER_NKEYS
    s1 = _dot3(k_ref[0], q_ref[:, 0:dk], dot=_dot_nt)
    s2 = _dot3(k_ref[1], q_ref[:, dk:2 * dk], dot=_dot_nt)
    blocks = [slice(b * 128, (b + 1) * 128) for b in range(s1.shape[1] // 128)]
    nblk = len(blocks)
    s1s = [s1[:, lanes] for lanes in blocks]
    s2s = [s2[:, lanes] for lanes in blocks]
    vals, ranks = _top_values(s1s + s2s, PEER_TOPK, [False] * nblk + [True] * nblk)
    v1s, v2s, rank2s = vals[:nblk], vals[nblk:], ranks[nblk:]
    cands = []
    for v1, v2 in zip(v1s, v2s):
        rows = [v1[a] + v2[b] for a in range(PEER_TOPK) for b in range(PEER_TOPK // (a + 1))]
        rows += [jnp.full_like(rows[0], -jnp.inf)] * (-len(rows) % 8)
        cands.append(jnp.concatenate(rows, axis=0))
    bests, _ = _top_values(cands, PEER_TOPK)
    dense_b = 4
    for lanes, s1b, s2b, v1, v2, rank2, best in zip(blocks, s1s, s2s, v1s, v2s, rank2s, bests):
        zsum = jnp.ones_like(best[0])
        for r in range(1, PEER_TOPK):
            zsum = zsum + jnp.exp(best[r] - best[0])
        theta = best[PEER_TOPK - 1]
        nb = jnp.zeros_like(s1b)
        for b in range(dense_b):
            nb = nb + jnp.where(s1b + v2[b] >= theta, 1.0, 0.0)
        for a in range(PEER_TOPK // (dense_b + 1)):
            extra = jnp.zeros_like(theta)
            for b in range(dense_b, PEER_TOPK // (a + 1)):
                extra = extra + jnp.where(v1[a] + v2[b] >= theta, 1.0, 0.0)
            nb = nb + jnp.where(s1b == v1[a], extra, 0.0)
        nb_ref[:, lanes] = nb
        rk_ref[:, lanes] = pltpu.bitcast(rank2.astype(BF16), jnp.uint32)
        e1_ref[:, lanes] = jnp.exp(s1b - v1[0]) / zsum
        e2_ref[:, lanes] = pltpu.bitcast(jnp.exp(s2b - v2[0]).astype(BF16), jnp.uint32)


def peer_topk(q, keys, tt=1024):
    n = q.shape[0]
    nk = PEER_NKEYS
    spec = pl.BlockSpec((None, nk, tt), lambda i, h: (h, 0, i))
    pspec = pl.BlockSpec((None, nk // 2, tt), lambda i, h: (h, 0, i))
    full = jax.ShapeDtypeStruct((PEER_HEADS, nk, n), F32)
    packed = jax.ShapeDtypeStruct((PEER_HEADS, nk // 2, n), jnp.uint32)
    return pl.pallas_call(
        _peer_topk_kernel,
        grid=(n // tt, PEER_HEADS),
        in_specs=[pl.BlockSpec((tt, 2 * nk), lambda i, h: (i, h)),
                  pl.BlockSpec((None, 2, nk, nk), lambda i, h: (h, 0, 0, 0))],
        out_specs=[spec, pspec, spec, pspec],
        out_shape=[full, packed, full, packed],
        compiler_params=_cp(("parallel", "parallel")),
        name="peer_topk",
    )(q, keys)


def _row_bcast(row, n):
    t = jnp.broadcast_to(row, (16, 128)).astype(BF16)
    return jnp.broadcast_to(t[None], (n // 16, 16, 128)).reshape(n, 128)


def _gelu_tanh(x):
    return 0.5 * x * (1.0 + jnp.tanh(math.sqrt(2.0 / math.pi) * (x + 0.044715 * (x * x * x))))


def _peer_expert_kernel(ht_ref, u_ref, vt_ref, nb_ref, rk_ref, e1_ref, e2_ref, x_ref, gt_ref,
                        o_ref, acc_ref, w_ref, *, n_i1, tt):
    j = pl.program_id(1)
    nk = PEER_NKEYS

    @pl.when(j == 0)
    def _():
        acc_ref[...] = jnp.zeros_like(acc_ref)

    act = _dot(u_ref[...], ht_ref[...])
    zero = jnp.zeros((), BF16)
    for ts in range(tt // 128):
        lanes = slice(ts * 128, (ts + 1) * 128)
        for r in range(n_i1):
            w = None
            for h in range(PEER_HEADS):
                nbr = _row_bcast(nb_ref[h, r:r + 1, lanes], nk)
                e1r = _row_bcast(e1_ref[h, r:r + 1, lanes], nk)
                rk = pltpu.bitcast(rk_ref[h, :, lanes], BF16)
                e2 = pltpu.bitcast(e2_ref[h, :, lanes], BF16)
                term = jnp.where(rk < nbr, e2 * e1r, zero)
                w = term if w is None else w + term
            w_ref[r * nk:(r + 1) * nk, lanes] = w
    p = w_ref[...] * _gelu_tanh(act).astype(BF16)
    acc_ref[...] += _dot(vt_ref[...], p)

    @pl.when(j == pl.num_programs(1) - 1)
    def _():
        o_ref[...] = x_ref[...] + gt_ref[...] * acc_ref[...].T


def _xpose_cast_kernel(x_ref, o_ref):
    o_ref[...] = x_ref[...].T.astype(BF16)


def transpose_cast_chunks(tab, ec):
    depth, e, d = tab.shape
    return pl.pallas_call(
        _xpose_cast_kernel,
        grid=(depth, e // ec),
        in_specs=[pl.BlockSpec((None, ec, d), lambda l, c: (l, c, 0))],
        out_specs=pl.BlockSpec((None, None, d, ec), lambda l, c: (l, c, 0, 0)),
        out_shape=jax.ShapeDtypeStruct((depth, e // ec, d, ec), BF16),
        compiler_params=_cp(("parallel", "parallel")),
        name="transpose_cast_chunks",
    )(tab)


def peer_experts(h2t, u_tab, vt_tab, layer, nb, rk, e1, e2, x, mod3, gt_chunk, tt=512):
    ec = PEER_CHUNK
    n = x.shape[0]
    nk = PEER_NKEYS
    n_i1 = ec // nk
    once = pl.Buffered(1)
    sspec = pl.BlockSpec((PEER_HEADS, nk // 2, tt), lambda i, j: (0, 0, i), pipeline_mode=once)
    rspec = pl.BlockSpec((PEER_HEADS, n_i1, tt), lambda i, j: (0, j, i))
    return pl.pallas_call(
        functools.partial(_peer_expert_kernel, n_i1=n_i1, tt=tt),
        grid=(n // tt, PEER_EXPERTS // ec),
        in_specs=[pl.BlockSpec((D_MODEL, tt), lambda i, j: (0, i)),
                  pl.BlockSpec((None, ec, D_MODEL), lambda i, j: (layer, j, 0)),
                  pl.BlockSpec((None, None, D_MODEL, ec), lambda i, j: (layer, j, 0, 0)),
                  rspec, sspec, rspec, sspec,
                  pl.BlockSpec((tt, D_MODEL), lambda i, j: (i, 0), pipeline_mode=once),
                  pl.BlockSpec((None, 1, D_MODEL), lambda i, j: (_mod_row(i, tt), 0, gt_chunk))],
        out_specs=pl.BlockSpec((tt, D_MODEL), lambda i, j: (i, 0)),
        out_shape=jax.ShapeDtypeStruct((n, D_MODEL), F32),
        scratch_shapes=[pltpu.VMEM((D_MODEL, tt), F32), pltpu.VMEM((ec, tt), BF16)],
        compiler_params=_cp(("parallel", "arbitrary")),
        name="peer_experts",
    )(h2t, u_tab, vt_tab, nb, rk, e1, e2, x, mod3)


def _reorder_in_proj(w_in, b_in):
    splits = (1024, 1024, 1024, 512, 512, 1024, 1024, 16, 16, 1024, 1024, 1024, 1024, 8, 8, 2048, 2048, 2048)
    offs = np.concatenate([[0], np.cumsum(splits)])
    seg = lambda a, i: a[..., offs[i]:offs[i + 1]]
    main_ids = (0, 1, 2, 3, 4, 5, 6, 9, 10, 11, 12, 15, 16, 17)
    small_ids = (7, 8, 13, 14)
    w_main = jnp.concatenate([seg(w_in, i).astype(BF16) for i in main_ids], axis=-1)
    b_main = jnp.concatenate([seg(b_in, i) for i in main_ids], axis=-1)
    w_small = jnp.concatenate([seg(w_in, i) for i in small_ids], axis=-1)
    b_small = jnp.concatenate([seg(b_in, i) for i in small_ids], axis=-1)
    pad = N_SMALL - w_small.shape[-1]
    w_small = jnp.pad(w_small, ((0, 0), (0, 0), (0, pad)))
    b_small = jnp.pad(b_small, ((0, 0), (0, pad)))
    return w_main, b_main, w_small, b_small


def kernel(x_prompt, x_sample, c, state_gla, state_mlstm_C, state_mlstm_n, state_mlstm_m, c_ctx,
           mod_w, mod_b, norm1_g, norm2_g, final_g, w_in, b_in, hy_conv, hy_w1, hy_b1, hy_w2, hy_b2,
           hy_w3, hy_freq, hy_bias, gla_a2_w, gla_a2_b, gla_norm_g, ml_conv, ml_gate_b, ml_norm_g,
           w_branch, w_out, peer_wq, peer_keys, peer_u, peer_v):
    w_main, b_main, w_small, b_small = _reorder_in_proj(w_in, b_in)
    w_small_hi = w_small.astype(BF16)
    w_small_lo = (w_small - w_small_hi.astype(F32)).astype(BF16)
    w_branch_b = w_branch.astype(BF16)
    w_out_b = w_out.astype(BF16)
    wq_hi = peer_wq.astype(BF16)
    wq_lo = (peer_wq - wq_hi.astype(F32)).astype(BF16)
    u_b = peer_u.astype(BF16)
    vt_b = transpose_cast_chunks(peer_v, PEER_CHUNK)
    zero_b = jnp.zeros((1, D_MODEL), F32)
    a2f = jnp.pad(gla_a2_w[:, 0], ((0, 0), (0, N_SMALL - GLA_RANK), (0, 0)))
    a2b = jnp.pad(gla_a2_w[:, 1], ((0, 0), (GLA_RANK, N_SMALL - 2 * GLA_RANK), (0, 0)))
    a2bias = gla_a2_b.reshape(DEPTH, 2, 1, GLA_HEADS * GLA_DK)
    n0_all = state_mlstm_n.reshape(DEC_BATCH, DEPTH, 2, ML_HEADS, 1, ML_DH)

    tabs = {}
    for length in (SEQ, DEC_SEQ):
        fre, fim, g_mat, sgn = _dft_tables(length)
        fre_h, fre_l = _split2(fre)
        fim_h, fim_l = _split2(fim)
        tabs[length] = dict(spec=(fre_h, fre_l, fim_h, fim_l, sgn),
                            f=jnp.concatenate([fre_h, fim_h], axis=0), g=g_mat.astype(BF16))

    cvec = jnp.zeros((MOD_ROWS, D_MODEL), F32).at[:DEC_BATCH].set(c).at[CTX_ROW].set(c_ctx)
    mod_all = mod_table(cvec, mod_w, mod_b)

    x = jnp.concatenate([x_prompt.reshape(N_PROMPT, D_MODEL), x_sample.reshape(N_SAMPLE, D_MODEL)], axis=0)
    groups = ((SEQ, BATCH, 0, SEQ), (DEC_SEQ, DEC_BATCH, N_PROMPT // DEC_SEQ, GRID_W))
    new_gla, new_c, new_n, new_m = [], [], [], []
    for l in range(DEPTH):
        mod3 = mod_all[l].reshape(MOD_ROWS, 1, 6 * D_MODEL)
        h_hi, zsmall = normmod_gates(x, norm1_g[l], mod3, 0, 1, w_small_hi, w_small_lo, l,
                                     b_small[l].reshape(1, N_SMALL))
        zmain = mm_bias(h_hi, w_main, l, b_main[l].reshape(1, N_MAIN), tm=1024, tn=2048)
        mi = zsmall[:, 32:40].reshape(N_ROWS, 2, ML_HEADS)
        mf = zsmall[:, 40:48].reshape(N_ROWS, 2, ML_HEADS)
        gcol = jnp.concatenate([mi, mf, jnp.zeros((N_ROWS, 4, ML_HEADS), F32)], axis=1)
        gates_col = jnp.transpose(gcol, (2, 0, 1))
        gates_row = jnp.transpose(gcol.reshape(N_ROWS // CHUNK, CHUNK, 8, ML_HEADS), (3, 0, 2, 1))
        gate_b = ml_gate_b[l].reshape(-1)
        y_hy, y_gla, y_ml = [], [], []
        for gi, (length, n_seq, blk0, seg) in enumerate(groups):
            t = tabs[length]
            sre, sim = hyena_spectrum(length, hy_w1[l], hy_b1[l], hy_w2[l], hy_b2[l], hy_w3[l],
                                      hy_freq[l], t["spec"])
            y_hy.append(hyena(zmain, hy_conv[l], sre, sim, hy_bias[l], t["f"], t["g"],
                              length, n_seq, blk0, seg))
            gla_args = (zmain, zsmall, a2f[l], a2b[l], a2bias[l], gla_norm_g[l].reshape(1, GLA_DV),
                        length, n_seq, blk0)
            ml_args = (zmain, gates_col, gates_row, gate_b, ml_conv[l], ml_norm_g[l].reshape(1, ML_DH),
                       length, n_seq, blk0, seg)
            if gi == 0:
                yg, s_fin = gla(*gla_args, None, l, n_sub=PROMPT_SEQS_PER_STEP)
                ym, c_fin, n_fin, m_fin = mlstm(*ml_args, None, l, n_sub=PROMPT_SEQS_PER_STEP)
                new_gla.append(s_fin)
                new_c.append(c_fin)
                new_n.append(n_fin[:, :, :, 0, :])
                new_m.append(m_fin[:, :, :, 0, 0])
            else:
                yg = gla(*gla_args, state_gla, l)
                ym = mlstm(*ml_args, (state_mlstm_C, n0_all, state_mlstm_m[:, l].reshape(-1)), l)
            y_gla.append(yg)
            y_ml.append(ym)
        merged = merge_branches(y_hy, y_gla, y_ml, w_branch_b, l, zmain)
        x = mm_residual(merged, w_out_b, l, x, mod3, 2)
        h2, h2_lo, h2t = normmod_peer(x, norm2_g[l], mod3, 3, 4)
        q = mm3_bias(h2, h2_lo, wq_hi, wq_lo, l, zero_b)
        nb, rk, e1, e2 = peer_topk(q, peer_keys[l])
        x = peer_experts(h2t, u_b, vt_b, l, nb, rk, e1, e2, x, mod3, 5)

    y_prompt = final_norm(x, final_g, 0, N_PROMPT).reshape(BATCH, SEQ, D_MODEL)
    y_sample = final_norm(x, final_g, N_PROMPT, N_SAMPLE).reshape(DEC_BATCH, DEC_SEQ, D_MODEL)
    return (y_prompt, y_sample, jnp.stack(new_gla, axis=1), jnp.stack(new_c, axis=1),
            jnp.stack(new_n, axis=1), jnp.stack(new_m, axis=1))
```
